```python
import jax, jax.numpy as jnp
from jax import lax
import numpy as np

D_MODEL = 1024
BATCH = 8
SEQ = 16384
DEPTH = 1

CONV_WIDTH = D_MODEL
CONV_KERNEL = 31
N_RET_HEADS = 8
RET_QK_DIM = D_MODEL // N_RET_HEADS
RET_V_DIM = 2 * RET_QK_DIM
RET_QK_WIDTH = N_RET_HEADS * RET_QK_DIM
RET_V_WIDTH = N_RET_HEADS * RET_V_DIM
RET_CHUNK = 128
ROPE_BASE = 10000.0
D_FF = 2816
FFN_CONV_KERNEL = 3
LN_EPS = 1e-5
DEEPNORM_ALPHA = (2.0 * DEPTH) ** 0.25
DEEPNORM_BETA = (8.0 * DEPTH) ** -0.25
N_MOD = 6
IN_SIZES = (RET_QK_WIDTH, RET_QK_WIDTH, RET_V_WIDTH, RET_V_WIDTH,
            CONV_WIDTH, CONV_WIDTH, D_MODEL, D_MODEL)
IN_WIDTH = sum(IN_SIZES)
IN_SPLITS = tuple(int(s) for s in np.cumsum(IN_SIZES)[:-1])

kernel_name = "hybrid_conformer_retention_deepnorm_block"


def layer_norm(x, g=None, b=None):
    xf = x.astype(jnp.float32)
    mu = jnp.mean(xf, axis=-1, keepdims=True)
    var = jnp.mean(jnp.square(xf - mu), axis=-1, keepdims=True)
    y = ((xf - mu) * lax.rsqrt(var + LN_EPS)).astype(x.dtype)
    if g is not None:
        y = y * g + b
    return y


def causal_depthwise_conv(x, w, b):
    k_width, ch = w.shape
    y = lax.conv_general_dilated(
        x, w[:, None, :].astype(x.dtype), window_strides=(1,),
        padding=[(k_width - 1, 0)], dimension_numbers=("NWC", "WIO", "NWC"),
        feature_group_count=ch)
    return y + b


def rotary(x, positions):
    half = x.shape[-1] // 2
    inv_freq = ROPE_BASE ** (-jnp.arange(half, dtype=jnp.float32) / half)
    ang = positions.astype(jnp.float32)[..., None] * inv_freq
    cos = jnp.cos(ang)[:, :, None, :]
    sin = jnp.sin(ang)[:, :, None, :]
    x1, x2 = x[..., :half], x[..., half:]
    return jnp.concatenate([x1 * cos - x2 * sin, x2 * cos + x1 * sin], axis=-1)


def chunkwise_retention(q, k, v):
    bsz, seq, heads, dk = q.shape
    dv = v.shape[-1]
    n_chunks = seq // RET_CHUNK
    log_gamma = jnp.log(1.0 - 2.0 ** (-5.0 - jnp.arange(heads, dtype=jnp.float32)))
    q = q.reshape(bsz, n_chunks, RET_CHUNK, heads, dk)
    k = k.reshape(bsz, n_chunks, RET_CHUNK, heads, dk)
    v = v.reshape(bsz, n_chunks, RET_CHUNK, heads, dv)
    idx = jnp.arange(RET_CHUNK, dtype=jnp.float32)
    rel = idx[:, None] - idx[None, :]
    decay = jnp.where(rel[None] >= 0,
                      jnp.exp(log_gamma[:, None, None] * jnp.maximum(rel, 0.0)[None]),
                      0.0)
    scores = jnp.einsum("bnihd,bnjhd->bnhij", q, k) * decay
    inner = jnp.einsum("bnhij,bnjhe->bnihe", scores, v)
    zeta = jnp.exp(log_gamma[:, None] * (RET_CHUNK - 1.0 - idx)[None])
    kv = jnp.einsum("bnjhd,bnjhe->bnhde", k * zeta.T[None, None, :, :, None], v)
    chunk_decay = jnp.exp(log_gamma * RET_CHUNK)[None, :, None, None]

    def step(state, kv_c):
        return state * chunk_decay + kv_c, state

    _, r_prev = lax.scan(step, jnp.zeros((bsz, heads, dk, dv), jnp.float32),
                         jnp.moveaxis(kv, 1, 0))
    r_prev = jnp.moveaxis(r_prev, 0, 1)
    xi = jnp.exp(log_gamma[:, None] * (idx + 1.0)[None])
    cross = jnp.einsum("bnihd,bnhde->bnihe", q * xi.T[None, None, :, :, None], r_prev)
    return (inner + cross).reshape(bsz, seq, heads, dv)


def token_mixer(h, positions, w_in, b_in, conv_dw_w, conv_dw_b, conv_ln_g, conv_ln_b,
                w_conv_out, ret_gn_g, ret_gn_b, w_ret_out, w_out):
    bsz, seq, _ = h.shape
    proj = jnp.einsum("bsd,de->bse", h, w_in) + b_in
    q, k, v, g_ret, c_val, c_gate, gate_a, gate_b = jnp.split(proj, IN_SPLITS, axis=-1)

    a = c_val * jax.nn.sigmoid(c_gate)
    a = causal_depthwise_conv(a, conv_dw_w, conv_dw_b)
    a = jax.nn.silu(layer_norm(a, conv_ln_g, conv_ln_b))
    y_a = jnp.einsum("bsc,cd->bsd", a, w_conv_out)

    qf = rotary(q.reshape(bsz, seq, N_RET_HEADS, RET_QK_DIM).astype(jnp.float32), positions)
    kf = rotary(k.reshape(bsz, seq, N_RET_HEADS, RET_QK_DIM).astype(jnp.float32), positions)
    kf = kf * (RET_QK_DIM ** -0.5)
    vf = v.reshape(bsz, seq, N_RET_HEADS, RET_V_DIM).astype(jnp.float32)
    r = chunkwise_retention(qf, kf, vf)
    mu = jnp.mean(r, axis=-1, keepdims=True)
    var = jnp.mean(jnp.square(r - mu), axis=-1, keepdims=True)
    r = ((r - mu) * lax.rsqrt(var + LN_EPS)).reshape(bsz, seq, RET_V_WIDTH).astype(h.dtype)
    r = (r * ret_gn_g + ret_gn_b) * jax.nn.silu(g_ret)
    y_b = jnp.einsum("bse,ed->bsd", r, w_ret_out)

    m = jax.nn.sigmoid(gate_a) * y_a + jax.nn.sigmoid(gate_b) * y_b
    return jnp.einsum("bsd,de->bse", m, w_out)


def channel_mixer(h, w_up, ffn_dw_w, ffn_dw_b, w_down):
    u = jnp.einsum("bsd,df->bsf", h, w_up)
    u = causal_depthwise_conv(u, ffn_dw_w, ffn_dw_b)
    val, gate = jnp.split(u, 2, axis=-1)
    return jnp.einsum("bsf,fd->bsd", val * jax.nn.silu(gate), w_down)


def _fwd_setup_inputs(seed: int = 0) -> dict:
    key = jax.random.key(seed)
    ks = jax.random.split(key, 32)
    f32 = jnp.float32

    def nrm(k, shape, scale):
        return jax.random.normal(k, shape, f32) * scale

    L, D = DEPTH, D_MODEL
    x = jax.random.normal(ks[0], (BATCH, SEQ, D), f32)
    c = jax.random.normal(ks[1], (BATCH, D), f32)
    offset = jax.random.randint(ks[2], (BATCH, 1), 0, 1024, dtype=jnp.int32)
    positions = offset + jnp.arange(SEQ, dtype=jnp.int32)[None, :]
    return {
        "x": x,
        "c": c,
        "positions": positions,
        "w_ada": nrm(ks[3], (L, D, N_MOD * D), 0.5 * D ** -0.5),
        "b_ada": nrm(ks[4], (L, N_MOD * D), 0.02),
        "w_in": nrm(ks[5], (L, D, IN_WIDTH), D ** -0.5),
        "b_in": nrm(ks[6], (L, IN_WIDTH), 0.02),
        "conv_dw_w": nrm(ks[7], (L, CONV_KERNEL, CONV_WIDTH), CONV_KERNEL ** -0.5),
        "conv_dw_b": nrm(ks[8], (L, CONV_WIDTH), 0.02),
        "conv_ln_g": 1.0 + nrm(ks[9], (L, CONV_WIDTH), 0.02),
        "conv_ln_b": nrm(ks[10], (L, CONV_WIDTH), 0.02),
        "w_conv_out": nrm(ks[11], (L, CONV_WIDTH, D), CONV_WIDTH ** -0.5),
        "ret_gn_g": 1.0 + nrm(ks[12], (L, RET_V_WIDTH), 0.02),
        "ret_gn_b": nrm(ks[13], (L, RET_V_WIDTH), 0.02),
        "w_ret_out": nrm(ks[14], (L, RET_V_WIDTH, D), RET_V_WIDTH ** -0.5),
        "w_out": nrm(ks[15], (L, D, D), DEEPNORM_BETA * D ** -0.5),
        "ln1_g": 1.0 + nrm(ks[16], (L, D), 0.02),
        "ln1_b": nrm(ks[17], (L, D), 0.02),
        "w_up": nrm(ks[18], (L, D, 2 * D_FF), D ** -0.5),
        "ffn_dw_w": nrm(ks[19], (L, FFN_CONV_KERNEL, 2 * D_FF), FFN_CONV_KERNEL ** -0.5),
        "ffn_dw_b": nrm(ks[20], (L, 2 * D_FF), 0.02),
        "w_down": nrm(ks[21], (L, D_FF, D), DEEPNORM_BETA * D_FF ** -0.5),
        "ln2_g": 1.0 + nrm(ks[22], (L, D), 0.02),
        "ln2_b": nrm(ks[23], (L, D), 0.02),
    }


def _fwd_reference(x, c, positions, w_ada, b_ada, w_in, b_in, conv_dw_w, conv_dw_b, conv_ln_g,
              conv_ln_b, w_conv_out, ret_gn_g, ret_gn_b, w_ret_out, w_out, ln1_g, ln1_b,
              w_up, ffn_dw_w, ffn_dw_b, w_down, ln2_g, ln2_b):
    for l in range(DEPTH):
        mod = jnp.einsum("bd,de->be", jax.nn.silu(c), w_ada[l]) + b_ada[l]
        shift1, scale1, gate1, shift2, scale2, gate2 = jnp.split(mod[:, None, :], N_MOD, axis=-1)

        h = layer_norm(x) * (1.0 + scale1) + shift1
        t = token_mixer(h, positions, w_in[l], b_in[l], conv_dw_w[l], conv_dw_b[l],
                        conv_ln_g[l], conv_ln_b[l], w_conv_out[l], ret_gn_g[l], ret_gn_b[l],
                        w_ret_out[l], w_out[l])
        x = layer_norm(DEEPNORM_ALPHA * x + gate1 * t, ln1_g[l], ln1_b[l])

        h = layer_norm(x) * (1.0 + scale2) + shift2
        f = channel_mixer(h, w_up[l], ffn_dw_w[l], ffn_dw_b[l], w_down[l])
        x = layer_norm(DEEPNORM_ALPHA * x + gate2 * f, ln2_g[l], ln2_b[l])
    return x


import jax as _jax
import jax.numpy as _jnp

TWIN_FORMAT = 'train_step'
FWD_PARAMS = ['x', 'c', 'positions', 'w_ada', 'b_ada', 'w_in', 'b_in', 'conv_dw_w', 'conv_dw_b', 'conv_ln_g', 'conv_ln_b', 'w_conv_out', 'ret_gn_g', 'ret_gn_b', 'w_ret_out', 'w_out', 'ln1_g', 'ln1_b', 'w_up', 'ffn_dw_w', 'ffn_dw_b', 'w_down', 'ln2_g', 'ln2_b']
TWIN_WEIGHTS = ['w_ada', 'b_ada', 'w_in', 'b_in', 'conv_dw_w', 'conv_dw_b', 'conv_ln_g', 'conv_ln_b', 'w_conv_out', 'ret_gn_g', 'ret_gn_b', 'w_ret_out', 'w_out', 'ln1_g', 'ln1_b', 'w_up', 'ffn_dw_w', 'ffn_dw_b', 'w_down', 'ln2_g', 'ln2_b']
TWIN_DIFF_INPUT = 'x'
TWIN_INPUTS = ['x', 'c', 'positions', 'w_ada', 'b_ada', 'w_in', 'b_in', 'conv_dw_w', 'conv_dw_b', 'conv_ln_g', 'conv_ln_b', 'w_conv_out', 'ret_gn_g', 'ret_gn_b', 'w_ret_out', 'w_out', 'ln1_g', 'ln1_b', 'w_up', 'ffn_dw_w', 'ffn_dw_b', 'w_down', 'ln2_g', 'ln2_b', 'loss_target', 'm_w_ada', 'm_b_ada', 'm_w_in', 'm_b_in', 'm_conv_dw_w', 'm_conv_dw_b', 'm_conv_ln_g', 'm_conv_ln_b', 'm_w_conv_out', 'm_ret_gn_g', 'm_ret_gn_b', 'm_w_ret_out', 'm_w_out', 'm_ln1_g', 'm_ln1_b', 'm_w_up', 'm_ffn_dw_w', 'm_ffn_dw_b', 'm_w_down', 'm_ln2_g', 'm_ln2_b', 'v_w_ada', 'v_b_ada', 'v_w_in', 'v_b_in', 'v_conv_dw_w', 'v_conv_dw_b', 'v_conv_ln_g', 'v_conv_ln_b', 'v_w_conv_out', 'v_ret_gn_g', 'v_ret_gn_b', 'v_w_ret_out', 'v_w_out', 'v_ln1_g', 'v_ln1_b', 'v_w_up', 'v_ffn_dw_w', 'v_ffn_dw_b', 'v_w_down', 'v_ln2_g', 'v_ln2_b']
TWIN_OUTPUTS = ['loss', 'grad_x', 'grad_w_ada', 'grad_b_ada', 'grad_w_in', 'grad_b_in', 'grad_conv_dw_w', 'grad_conv_dw_b', 'grad_conv_ln_g', 'grad_conv_ln_b', 'grad_w_conv_out', 'grad_ret_gn_g', 'grad_ret_gn_b', 'grad_w_ret_out', 'grad_w_out', 'grad_ln1_g', 'grad_ln1_b', 'grad_w_up', 'grad_ffn_dw_w', 'grad_ffn_dw_b', 'grad_w_down', 'grad_ln2_g', 'grad_ln2_b', 'delta_w_ada', 'delta_b_ada', 'delta_w_in', 'delta_b_in', 'delta_conv_dw_w', 'delta_conv_dw_b', 'delta_conv_ln_g', 'delta_conv_ln_b', 'delta_w_conv_out', 'delta_ret_gn_g', 'delta_ret_gn_b', 'delta_w_ret_out', 'delta_w_out', 'delta_ln1_g', 'delta_ln1_b', 'delta_w_up', 'delta_ffn_dw_w', 'delta_ffn_dw_b', 'delta_w_down', 'delta_ln2_g', 'delta_ln2_b', 'new_m_w_ada', 'new_m_b_ada', 'new_m_w_in', 'new_m_b_in', 'new_m_conv_dw_w', 'new_m_conv_dw_b', 'new_m_conv_ln_g', 'new_m_conv_ln_b', 'new_m_w_conv_out', 'new_m_ret_gn_g', 'new_m_ret_gn_b', 'new_m_w_ret_out', 'new_m_w_out', 'new_m_ln1_g', 'new_m_ln1_b', 'new_m_w_up', 'new_m_ffn_dw_w', 'new_m_ffn_dw_b', 'new_m_w_down', 'new_m_ln2_g', 'new_m_ln2_b', 'new_v_w_ada', 'new_v_b_ada', 'new_v_w_in', 'new_v_b_in', 'new_v_conv_dw_w', 'new_v_conv_dw_b', 'new_v_conv_ln_g', 'new_v_conv_ln_b', 'new_v_w_conv_out', 'new_v_ret_gn_g', 'new_v_ret_gn_b', 'new_v_w_ret_out', 'new_v_w_out', 'new_v_ln1_g', 'new_v_ln1_b', 'new_v_w_up', 'new_v_ffn_dw_w', 'new_v_ffn_dw_b', 'new_v_w_down', 'new_v_ln2_g', 'new_v_ln2_b']
TWIN_LEAF_KINDS = {'loss': 'loss', 'grad_x': 'grad_x', 'grad_w_ada': 'grad_w', 'grad_b_ada': 'grad_w', 'grad_w_in': 'grad_w', 'grad_b_in': 'grad_w', 'grad_conv_dw_w': 'grad_w', 'grad_conv_dw_b': 'grad_w', 'grad_conv_ln_g': 'grad_w', 'grad_conv_ln_b': 'grad_w', 'grad_w_conv_out': 'grad_w', 'grad_ret_gn_g': 'grad_w', 'grad_ret_gn_b': 'grad_w', 'grad_w_ret_out': 'grad_w', 'grad_w_out': 'grad_w', 'grad_ln1_g': 'grad_w', 'grad_ln1_b': 'grad_w', 'grad_w_up': 'grad_w', 'grad_ffn_dw_w': 'grad_w', 'grad_ffn_dw_b': 'grad_w', 'grad_w_down': 'grad_w', 'grad_ln2_g': 'grad_w', 'grad_ln2_b': 'grad_w', 'delta_w_ada': 'delta_w', 'delta_b_ada': 'delta_w', 'delta_w_in': 'delta_w', 'delta_b_in': 'delta_w', 'delta_conv_dw_w': 'delta_w', 'delta_conv_dw_b': 'delta_w', 'delta_conv_ln_g': 'delta_w', 'delta_conv_ln_b': 'delta_w', 'delta_w_conv_out': 'delta_w', 'delta_ret_gn_g': 'delta_w', 'delta_ret_gn_b': 'delta_w', 'delta_w_ret_out': 'delta_w', 'delta_w_out': 'delta_w', 'delta_ln1_g': 'delta_w', 'delta_ln1_b': 'delta_w', 'delta_w_up': 'delta_w', 'delta_ffn_dw_w': 'delta_w', 'delta_ffn_dw_b': 'delta_w', 'delta_w_down': 'delta_w', 'delta_ln2_g': 'delta_w', 'delta_ln2_b': 'delta_w', 'new_m_w_ada': 'new_m', 'new_m_b_ada': 'new_m', 'new_m_w_in': 'new_m', 'new_m_b_in': 'new_m', 'new_m_conv_dw_w': 'new_m', 'new_m_conv_dw_b': 'new_m', 'new_m_conv_ln_g': 'new_m', 'new_m_conv_ln_b': 'new_m', 'new_m_w_conv_out': 'new_m', 'new_m_ret_gn_g': 'new_m', 'new_m_ret_gn_b': 'new_m', 'new_m_w_ret_out': 'new_m', 'new_m_w_out': 'new_m', 'new_m_ln1_g': 'new_m', 'new_m_ln1_b': 'new_m', 'new_m_w_up': 'new_m', 'new_m_ffn_dw_w': 'new_m', 'new_m_ffn_dw_b': 'new_m', 'new_m_w_down': 'new_m', 'new_m_ln2_g': 'new_m', 'new_m_ln2_b': 'new_m', 'new_v_w_ada': 'new_v', 'new_v_b_ada': 'new_v', 'new_v_w_in': 'new_v', 'new_v_b_in': 'new_v', 'new_v_conv_dw_w': 'new_v', 'new_v_conv_dw_b': 'new_v', 'new_v_conv_ln_g': 'new_v', 'new_v_conv_ln_b': 'new_v', 'new_v_w_conv_out': 'new_v', 'new_v_ret_gn_g': 'new_v', 'new_v_ret_gn_b': 'new_v', 'new_v_w_ret_out': 'new_v', 'new_v_w_out': 'new_v', 'new_v_ln1_g': 'new_v', 'new_v_ln1_b': 'new_v', 'new_v_w_up': 'new_v', 'new_v_ffn_dw_w': 'new_v', 'new_v_ffn_dw_b': 'new_v', 'new_v_w_down': 'new_v', 'new_v_ln2_g': 'new_v', 'new_v_ln2_b': 'new_v'}


def _forward(args):
    return _fwd_reference(*[args[k] for k in FWD_PARAMS])


def _output_shape():
    def fwd():
        inp = _fwd_setup_inputs(0)
        return _fwd_reference(*[inp[k] for k in FWD_PARAMS])
    out = _jax.eval_shape(fwd)
    return out.shape, out.dtype

N_MICROBATCH = 1
ADAM_LR = 0.001
ADAM_B1 = 0.9
ADAM_B2 = 0.999
ADAM_EPS = 1e-08
ADAM_WD = 0.01
ADAM_STEP = 10
PER_EXAMPLE_BATCH_AXIS = {'x': 0, 'c': 0, 'positions': 0, 'loss_target': 0}
SHARED_INPUTS = []
_WEIGHT_DTYPES = {'w_ada': _jnp.float32, 'b_ada': _jnp.float32, 'w_in': _jnp.float32, 'b_in': _jnp.float32, 'conv_dw_w': _jnp.float32, 'conv_dw_b': _jnp.float32, 'conv_ln_g': _jnp.float32, 'conv_ln_b': _jnp.float32, 'w_conv_out': _jnp.float32, 'ret_gn_g': _jnp.float32, 'ret_gn_b': _jnp.float32, 'w_ret_out': _jnp.float32, 'w_out': _jnp.float32, 'ln1_g': _jnp.float32, 'ln1_b': _jnp.float32, 'w_up': _jnp.float32, 'ffn_dw_w': _jnp.float32, 'ffn_dw_b': _jnp.float32, 'w_down': _jnp.float32, 'ln2_g': _jnp.float32, 'ln2_b': _jnp.float32}
MOMENT_SCALE = {'w_ada': 4.878728e-02, 'b_ada': 8.625718e-02, 'w_in': 1.507304e-02, 'b_in': 2.711416e-02, 'conv_dw_w': 1.866412e-02, 'conv_dw_b': 4.048269e-02, 'conv_ln_g': 2.726317e-02, 'conv_ln_b': 2.816508e-02, 'w_conv_out': 1.944578e-02, 'ret_gn_g': 1.489937e-02, 'ret_gn_b': 1.469236e-02, 'w_ret_out': 1.960559e-02, 'w_out': 4.662608e-02, 'ln1_g': 4.536990e+00, 'ln1_b': 1.731362e+00, 'w_up': 2.399376e-02, 'ffn_dw_w': 2.516839e-02, 'ffn_dw_b': 2.368971e-02, 'w_down': 6.615411e-02, 'ln2_g': 1.280696e+02, 'ln2_b': 2.566428e+00}


def _to_microbatches(a, axis):
    t = _jnp.moveaxis(a, axis, 0)
    t = t.reshape((N_MICROBATCH, t.shape[0] // N_MICROBATCH) + t.shape[1:])
    return _jnp.moveaxis(t, 1, axis + 1)


def setup_inputs(seed: int = 0) -> dict:
    inp = _fwd_setup_inputs(seed)
    key = _jax.random.fold_in(_jax.random.key(seed), 7919)
    shape, _ = _output_shape()
    out = dict(inp)
    out["loss_target"] = _jax.random.normal(_jax.random.fold_in(key, 0), shape, _jnp.float32)
    for i, name in enumerate(TWIN_WEIGHTS):
        w = inp[name].astype(_jnp.float32)
        if MOMENT_SCALE is None:
            s = _jnp.sqrt(_jnp.mean(_jnp.square(w)) + 1e-30)
        else:
            s = MOMENT_SCALE[name]
        km, kv = _jax.random.split(_jax.random.fold_in(key, i + 1))
        out[name] = w
        out["m_" + name] = s * _jax.random.normal(km, w.shape, _jnp.float32)
        out["v_" + name] = (s * s) * _jax.random.uniform(kv, w.shape, _jnp.float32, 0.5, 1.5)
    if N_MICROBATCH > 1:
        for name, axis in PER_EXAMPLE_BATCH_AXIS.items():
            out[name] = _to_microbatches(out[name], axis)
    return {'x': out['x'], 'c': out['c'], 'positions': out['positions'], 'w_ada': out['w_ada'], 'b_ada': out['b_ada'], 'w_in': out['w_in'], 'b_in': out['b_in'], 'conv_dw_w': out['conv_dw_w'], 'conv_dw_b': out['conv_dw_b'], 'conv_ln_g': out['conv_ln_g'], 'conv_ln_b': out['conv_ln_b'], 'w_conv_out': out['w_conv_out'], 'ret_gn_g': out['ret_gn_g'], 'ret_gn_b': out['ret_gn_b'], 'w_ret_out': out['w_ret_out'], 'w_out': out['w_out'], 'ln1_g': out['ln1_g'], 'ln1_b': out['ln1_b'], 'w_up': out['w_up'], 'ffn_dw_w': out['ffn_dw_w'], 'ffn_dw_b': out['ffn_dw_b'], 'w_down': out['w_down'], 'ln2_g': out['ln2_g'], 'ln2_b': out['ln2_b'], 'loss_target': out['loss_target'], 'm_w_ada': out['m_w_ada'], 'm_b_ada': out['m_b_ada'], 'm_w_in': out['m_w_in'], 'm_b_in': out['m_b_in'], 'm_conv_dw_w': out['m_conv_dw_w'], 'm_conv_dw_b': out['m_conv_dw_b'], 'm_conv_ln_g': out['m_conv_ln_g'], 'm_conv_ln_b': out['m_conv_ln_b'], 'm_w_conv_out': out['m_w_conv_out'], 'm_ret_gn_g': out['m_ret_gn_g'], 'm_ret_gn_b': out['m_ret_gn_b'], 'm_w_ret_out': out['m_w_ret_out'], 'm_w_out': out['m_w_out'], 'm_ln1_g': out['m_ln1_g'], 'm_ln1_b': out['m_ln1_b'], 'm_w_up': out['m_w_up'], 'm_ffn_dw_w': out['m_ffn_dw_w'], 'm_ffn_dw_b': out['m_ffn_dw_b'], 'm_w_down': out['m_w_down'], 'm_ln2_g': out['m_ln2_g'], 'm_ln2_b': out['m_ln2_b'], 'v_w_ada': out['v_w_ada'], 'v_b_ada': out['v_b_ada'], 'v_w_in': out['v_w_in'], 'v_b_in': out['v_b_in'], 'v_conv_dw_w': out['v_conv_dw_w'], 'v_conv_dw_b': out['v_conv_dw_b'], 'v_conv_ln_g': out['v_conv_ln_g'], 'v_conv_ln_b': out['v_conv_ln_b'], 'v_w_conv_out': out['v_w_conv_out'], 'v_ret_gn_g': out['v_ret_gn_g'], 'v_ret_gn_b': out['v_ret_gn_b'], 'v_w_ret_out': out['v_w_ret_out'], 'v_w_out': out['v_w_out'], 'v_ln1_g': out['v_ln1_g'], 'v_ln1_b': out['v_ln1_b'], 'v_w_up': out['v_w_up'], 'v_ffn_dw_w': out['v_ffn_dw_w'], 'v_ffn_dw_b': out['v_ffn_dw_b'], 'v_w_down': out['v_w_down'], 'v_ln2_g': out['v_ln2_g'], 'v_ln2_b': out['v_ln2_b']}


def _loss(weights, diff, rest, loss_target):
    with _jax.named_scope("forward"):
        args = {**rest, TWIN_DIFF_INPUT: diff, **{k: w.astype(_WEIGHT_DTYPES[k]) for k, w in weights.items()}}
        y = _forward(args)
    with _jax.named_scope("loss_head"):
        err = _jnp.square(y.astype(_jnp.float32) - loss_target)
        return 0.5 * _jnp.sum(_jnp.mean(err, axis=-1)) if err.ndim else 0.5 * err


def _adamw(w, g, m, v):
    m = ADAM_B1 * m + (1.0 - ADAM_B1) * g
    v = ADAM_B2 * v + (1.0 - ADAM_B2) * _jnp.square(g)
    m_hat = m / (1.0 - ADAM_B1 ** ADAM_STEP)
    v_hat = v / (1.0 - ADAM_B2 ** ADAM_STEP)
    delta = -ADAM_LR * (m_hat / (_jnp.sqrt(v_hat) + ADAM_EPS) + ADAM_WD * w)
    return delta, m, v


def reference(x, c, positions, w_ada, b_ada, w_in, b_in, conv_dw_w, conv_dw_b, conv_ln_g, conv_ln_b, w_conv_out, ret_gn_g, ret_gn_b, w_ret_out, w_out, ln1_g, ln1_b, w_up, ffn_dw_w, ffn_dw_b, w_down, ln2_g, ln2_b, loss_target, m_w_ada, m_b_ada, m_w_in, m_b_in, m_conv_dw_w, m_conv_dw_b, m_conv_ln_g, m_conv_ln_b, m_w_conv_out, m_ret_gn_g, m_ret_gn_b, m_w_ret_out, m_w_out, m_ln1_g, m_ln1_b, m_w_up, m_ffn_dw_w, m_ffn_dw_b, m_w_down, m_ln2_g, m_ln2_b, v_w_ada, v_b_ada, v_w_in, v_b_in, v_conv_dw_w, v_conv_dw_b, v_conv_ln_g, v_conv_ln_b, v_w_conv_out, v_ret_gn_g, v_ret_gn_b, v_w_ret_out, v_w_out, v_ln1_g, v_ln1_b, v_w_up, v_ffn_dw_w, v_ffn_dw_b, v_w_down, v_ln2_g, v_ln2_b):
    given = dict(x=x, c=c, positions=positions, w_ada=w_ada, b_ada=b_ada, w_in=w_in, b_in=b_in, conv_dw_w=conv_dw_w, conv_dw_b=conv_dw_b, conv_ln_g=conv_ln_g, conv_ln_b=conv_ln_b, w_conv_out=w_conv_out, ret_gn_g=ret_gn_g, ret_gn_b=ret_gn_b, w_ret_out=w_ret_out, w_out=w_out, ln1_g=ln1_g, ln1_b=ln1_b, w_up=w_up, ffn_dw_w=ffn_dw_w, ffn_dw_b=ffn_dw_b, w_down=w_down, ln2_g=ln2_g, ln2_b=ln2_b, loss_target=loss_target, m_w_ada=m_w_ada, m_b_ada=m_b_ada, m_w_in=m_w_in, m_b_in=m_b_in, m_conv_dw_w=m_conv_dw_w, m_conv_dw_b=m_conv_dw_b, m_conv_ln_g=m_conv_ln_g, m_conv_ln_b=m_conv_ln_b, m_w_conv_out=m_w_conv_out, m_ret_gn_g=m_ret_gn_g, m_ret_gn_b=m_ret_gn_b, m_w_ret_out=m_w_ret_out, m_w_out=m_w_out, m_ln1_g=m_ln1_g, m_ln1_b=m_ln1_b, m_w_up=m_w_up, m_ffn_dw_w=m_ffn_dw_w, m_ffn_dw_b=m_ffn_dw_b, m_w_down=m_w_down, m_ln2_g=m_ln2_g, m_ln2_b=m_ln2_b, v_w_ada=v_w_ada, v_b_ada=v_b_ada, v_w_in=v_w_in, v_b_in=v_b_in, v_conv_dw_w=v_conv_dw_w, v_conv_dw_b=v_conv_dw_b, v_conv_ln_g=v_conv_ln_g, v_conv_ln_b=v_conv_ln_b, v_w_conv_out=v_w_conv_out, v_ret_gn_g=v_ret_gn_g, v_ret_gn_b=v_ret_gn_b, v_w_ret_out=v_w_ret_out, v_w_out=v_w_out, v_ln1_g=v_ln1_g, v_ln1_b=v_ln1_b, v_w_up=v_w_up, v_ffn_dw_w=v_ffn_dw_w, v_ffn_dw_b=v_ffn_dw_b, v_w_down=v_w_down, v_ln2_g=v_ln2_g, v_ln2_b=v_ln2_b)
    weights = {n: given[n] for n in TWIN_WEIGHTS}
    shared = {n: given[n] for n in SHARED_INPUTS}
    per_example = {n: given[n] for n in ['x', 'c', 'positions']}
    grad_fn = _jax.value_and_grad(_loss, argnums=(0, 1))

    def one_microbatch(ex, loss_target):
        ex = dict(ex)
        diff = ex.pop(TWIN_DIFF_INPUT)
        return grad_fn(weights, diff, {**shared, **ex}, loss_target)

    if N_MICROBATCH == 1:
        loss, (grad_w, grad_x) = one_microbatch(per_example, given["loss_target"])
    else:
        def body(carry, xs):
            loss_sum, grad_sum = carry
            l_k, (gw_k, gx_k) = one_microbatch(xs[0], xs[1])
            with _jax.named_scope("update"):
                return (loss_sum + l_k, _jax.tree.map(_jnp.add, grad_sum, gw_k)), gx_k

        init = (_jnp.zeros((), _jnp.float32), _jax.tree.map(_jnp.zeros_like, weights))
        (loss, grad_w), grad_x = _jax.lax.scan(body, init, (per_example, given["loss_target"]))
    with _jax.named_scope("update"):
        delta_w, new_m, new_v = {}, {}, {}
        for n in TWIN_WEIGHTS:
            delta_w[n], new_m[n], new_v[n] = _adamw(weights[n], grad_w[n], given["m_" + n], given["v_" + n])
    return (loss, grad_x, *[grad_w[n] for n in TWIN_WEIGHTS], *[delta_w[n] for n in TWIN_WEIGHTS],
            *[new_m[n] for n in TWIN_WEIGHTS], *[new_v[n] for n in TWIN_WEIGHTS])
```

```python
import functools

import jax
import jax.numpy as jnp
from jax import lax
from jax.experimental import pallas as pl
from jax.experimental.pallas import tpu as pltpu

F32 = jnp.float32
BF16 = jnp.bfloat16
MESH = pl.DeviceIdType.MESH

D_MODEL = 1024
N_HEADS = 8
DK = 128
DV = 256
CHUNK = 128
ROPE_BASE = 10000.0
D_FF = 2816
CONV_K = 31
FFN_K = 3
LN_EPS = 1e-5
ALPHA = (2.0 * 1) ** 0.25
ADAM_LR = 0.001
ADAM_B1 = 0.9
ADAM_B2 = 0.999
ADAM_EPS = 1e-08
ADAM_WD = 0.01
ADAM_STEP = 10

V7X_VMEM_BYTES = 64 * 1024 * 1024
VMEM_LIMIT = V7X_VMEM_BYTES - 8 * 1024 * 1024
ROW_TILE = 256
MM_TILE = 512
N_CHIPS = 4
N_DEV = 8

PACK_ROWS = (("w_in", 2560), ("w_up", 1408), ("w_conv_out", 256), ("w_ret_out", 512), ("w_out", 256), ("w_down", 704))
PACK_TOTAL = sum(r for _, r in PACK_ROWS)


def _params(n_grid):
    return pltpu.CompilerParams(dimension_semantics=("arbitrary",) * n_grid, vmem_limit_bytes=VMEM_LIMIT)


def _rowwise(name, fn, rows, vecs, outs, reds, n_rows, tile=ROW_TILE, ncol=1, with_col=False):
    tile = _fit_tile(n_rows, tile)
    n_in = len(rows) + len(vecs)
    n_out = len(outs)

    def col_map(off, row):
        def index(j, i):
            return (i if row else 0, off(j) if callable(off) else off + j)
        return index

    def body(*refs):
        i = pl.program_id(1)
        vals = [r[...] for r in refs[:n_in]]
        res = fn(pl.program_id(0), *vals) if with_col else fn(*vals)
        for k in range(n_out):
            refs[n_in + k][...] = res[k].astype(refs[n_in + k].dtype)
        for k in range(len(reds)):
            o = refs[n_in + n_out + k]

            @pl.when(i == 0)
            def _():
                o[...] = jnp.zeros_like(o)

            o[...] += res[n_out + k]

    in_specs = [pl.BlockSpec((tile, w), col_map(off, True)) for _, w, off in rows]
    in_specs += [pl.BlockSpec((1, w), col_map(off, False)) for _, w, off in vecs]
    out_specs = [pl.BlockSpec((tile, w), lambda j, i: (i, j)) for w, _ in outs]
    out_specs += [pl.BlockSpec((1, w), lambda j, i: (0, j)) for w in reds]
    out_shape = [jax.ShapeDtypeStruct((n_rows, w * ncol), dt) for w, dt in outs]
    out_shape += [jax.ShapeDtypeStruct((1, w * ncol), F32) for w in reds]
    return pl.pallas_call(
        body, name=name, grid=(ncol, n_rows // tile), in_specs=in_specs, out_specs=out_specs, out_shape=out_shape,
        compiler_params=_params(2),
    )(*[a for a, _, _ in rows], *[a for a, _, _ in vecs])


def _vjp_rows(fn, n_row_in, n_ct):
    def bwd(*args):
        prim = [a.astype(F32) for a in args[:n_row_in] + args[n_row_in + n_ct:]]
        cts = tuple(a.astype(F32) for a in args[n_row_in:n_row_in + n_ct])
        _, pull = jax.vjp(fn, *prim)
        return pull(cts if n_ct > 1 else cts[0])

    return bwd


def _fit_tile(n, pref):
    if n <= pref:
        return n
    t = pref - pref % 16
    while n % t:
        t -= 16
    return t


def _col_tile(n):
    return n if n <= 1536 else n // 2


def _mm_nn(name, a, w, bias, out_dtype):
    s, k = a.shape
    g, _, n = w.shape
    tm, tn = min(MM_TILE, s), _col_tile(n)
    nt = n // tn

    def body(*refs):
        a_ref, w_ref = refs[0], refs[1]
        o_ref = refs[-1]
        acc = jnp.dot(a_ref[...].astype(BF16), w_ref[...], preferred_element_type=F32)
        if bias is not None:
            acc = acc + refs[2][...]
        o_ref[...] = acc.astype(o_ref.dtype)

    in_specs = [pl.BlockSpec((tm, k), lambda c, i: (i, 0)), pl.BlockSpec((None, k, tn), lambda c, i: (c // nt, 0, c % nt))]
    args = [a, w]
    if bias is not None:
        in_specs.append(pl.BlockSpec((1, tn), lambda c, i: (0, c)))
        args.append(bias)
    return pl.pallas_call(
        body, name=name, grid=(g * nt, s // tm), in_specs=in_specs, out_specs=pl.BlockSpec((tm, tn), lambda c, i: (i, c)),
        out_shape=jax.ShapeDtypeStruct((s, g * n), out_dtype), compiler_params=_params(2),
    )(*args)


def _mm_nt(name, dy, w, out_dtype):
    s = dy.shape[0]
    g, k, n = w.shape
    tm, tn = min(MM_TILE, s), _col_tile(n)
    nt = n // tn
    steps = g * nt

    def body(dy_ref, w_ref, o_ref, acc_ref):
        r = pl.program_id(1)
        part = lax.dot_general(dy_ref[...].astype(BF16), w_ref[...], (((1,), (1,)), ((), ())), preferred_element_type=F32)

        @pl.when(r == 0)
        def _():
            acc_ref[...] = part

        @pl.when(r > 0)
        def _():
            acc_ref[...] += part

        @pl.when(r == steps - 1)
        def _():
            o_ref[...] = acc_ref[...].astype(o_ref.dtype)

    return pl.pallas_call(
        body, name=name, grid=(s // tm, steps),
        in_specs=[pl.BlockSpec((tm, tn), lambda i, r: (i, r)), pl.BlockSpec((None, k, tn), lambda i, r: (r // nt, 0, r % nt))],
        out_specs=pl.BlockSpec((tm, k), lambda i, r: (i, 0)), out_shape=jax.ShapeDtypeStruct((s, k), out_dtype),
        scratch_shapes=[pltpu.VMEM((tm, k), F32)], compiler_params=_params(2),
    )(dy, w)


def _mm_tn(name, a, dy, g):
    s, k = a.shape
    n = dy.shape[1] // g
    ts = min(MM_TILE, s)
    tn = n if k * n <= 1024 * 1536 else (n // 2 if (n // 2) % 128 == 0 else n)
    nt = n // tn

    def body(a_ref, dy_ref, o_ref):
        t = pl.program_id(1)
        part = lax.dot_general(a_ref[...].astype(BF16), dy_ref[...].astype(BF16), (((0,), (0,)), ((), ())),
                               preferred_element_type=F32)

        @pl.when(t == 0)
        def _():
            o_ref[...] = part

        @pl.when(t > 0)
        def _():
            o_ref[...] += part

    return pl.pallas_call(
        body, name=name, grid=(g * nt, s // ts),
        in_specs=[pl.BlockSpec((ts, k), lambda c, t: (t, 0)), pl.BlockSpec((ts, tn), lambda c, t: (t, c))],
        out_specs=pl.BlockSpec((None, k, tn), lambda c, t: (c // nt, 0, c % nt)),
        out_shape=jax.ShapeDtypeStruct((g, k, n), F32), compiler_params=_params(2),
    )(a, dy)


def _conv_geometry(n_rows, kw):
    halo = 32 if kw > 8 else 8
    tile = min(ROW_TILE, n_rows)
    return halo, tile


def _glu(cv, cg):
    return cv * jax.nn.sigmoid(cg)


def _dwconv_fwd(name, xs, w, b, n_rows, n_ch, tc, glu):
    kw = w.shape[0]
    halo, tile = _conv_geometry(n_rows, kw)
    per = tile // halo
    n_x = len(xs)

    def body(*refs):
        mains, halos = refs[:n_x], refs[n_x:2 * n_x]
        w_ref, b_ref, o_ref, buf = refs[2 * n_x:]
        i = pl.program_id(1)
        if glu:
            main = _glu(mains[0][...], mains[1][...])
            prev = _glu(halos[0][...], halos[1][...])
        else:
            main, prev = mains[0][...], halos[0][...]
        buf[0:halo, :] = jnp.where(i == 0, 0.0, prev)
        buf[halo:halo + tile, :] = main
        acc = jnp.broadcast_to(b_ref[...], (tile, tc))
        for k in range(kw):
            acc = acc + w_ref[k:k + 1, :] * buf[pl.ds(halo - (kw - 1) + k, tile), :]
        o_ref[...] = acc

    in_specs = [pl.BlockSpec((tile, tc), functools.partial(lambda j, i, off: (i, off + j), off=off)) for _, off in xs]
    in_specs += [pl.BlockSpec((halo, tc), functools.partial(lambda j, i, off: (jnp.maximum(i * per - 1, 0), off + j), off=off))
                 for _, off in xs]
    in_specs += [pl.BlockSpec((kw, tc), lambda j, i: (0, j)), pl.BlockSpec((1, tc), lambda j, i: (0, j))]
    return pl.pallas_call(
        body, name=name, grid=(n_ch // tc, n_rows // tile), in_specs=in_specs,
        out_specs=pl.BlockSpec((tile, tc), lambda j, i: (i, j)), out_shape=jax.ShapeDtypeStruct((n_rows, n_ch), F32),
        scratch_shapes=[pltpu.VMEM((halo + tile, tc), F32)], compiler_params=_params(2),
    )(*[a for a, _ in xs], *[a for a, _ in xs], w, b)


def _dwconv_bwd(name, xs, dy, w, n_rows, n_ch, tc, glu):
    kw = w.shape[0]
    halo, tile = _conv_geometry(n_rows, kw)
    per = tile // halo
    n_x = len(xs)
    n_tiles = n_rows // tile

    def body(*refs):
        mains, halos = refs[:n_x], refs[n_x:2 * n_x]
        dy_ref, dyn_ref, w_ref = refs[2 * n_x:2 * n_x + 3]
        dxs = refs[2 * n_x + 3:3 * n_x + 3]
        dw_ref, db_ref, buf, dbuf = refs[3 * n_x + 3:]
        i = pl.program_id(1)
        if glu:
            main = _glu(mains[0][...], mains[1][...])
            prev = _glu(halos[0][...], halos[1][...])
        else:
            main, prev = mains[0][...], halos[0][...]
        buf[0:halo, :] = jnp.where(i == 0, 0.0, prev)
        buf[halo:halo + tile, :] = main
        dyv = dy_ref[...]
        dbuf[0:tile, :] = dyv
        dbuf[tile:tile + halo, :] = jnp.where(i == n_tiles - 1, 0.0, dyn_ref[...])

        @pl.when(i == 0)
        def _():
            dw_ref[...] = jnp.zeros_like(dw_ref)
            db_ref[...] = jnp.zeros_like(db_ref)

        db_ref[...] += jnp.sum(dyv, axis=0, keepdims=True)
        dx = jnp.zeros((tile, tc), F32)
        for k in range(kw):
            dx = dx + w_ref[k:k + 1, :] * dbuf[pl.ds(kw - 1 - k, tile), :]
            dw_ref[k:k + 1, :] += jnp.sum(dyv * buf[pl.ds(halo - (kw - 1) + k, tile), :], axis=0, keepdims=True)
        if glu:
            cv, cg = mains[0][...], mains[1][...]
            sig = jax.nn.sigmoid(cg)
            dxs[0][...] = (dx * sig).astype(dxs[0].dtype)
            dxs[1][...] = (dx * cv * sig * (1.0 - sig)).astype(dxs[1].dtype)
        else:
            dxs[0][...] = dx.astype(dxs[0].dtype)

    last_halo = n_rows // halo - 1
    in_specs = [pl.BlockSpec((tile, tc), functools.partial(lambda j, i, off: (i, off + j), off=off)) for _, off in xs]
    in_specs += [pl.BlockSpec((halo, tc), functools.partial(lambda j, i, off: (jnp.maximum(i * per - 1, 0), off + j), off=off))
                 for _, off in xs]
    in_specs += [pl.BlockSpec((tile, tc), lambda j, i: (i, j)),
                 pl.BlockSpec((halo, tc), lambda j, i: (jnp.minimum((i + 1) * per, last_halo), j)),
                 pl.BlockSpec((kw, tc), lambda j, i: (0, j))]
    out_specs = [pl.BlockSpec((tile, tc), lambda j, i: (i, j)) for _ in xs]
    out_specs += [pl.BlockSpec((kw, tc), lambda j, i: (0, j)), pl.BlockSpec((1, tc), lambda j, i: (0, j))]
    out_shape = [jax.ShapeDtypeStruct((n_rows, n_ch), BF16) for _ in xs]
    out_shape += [jax.ShapeDtypeStruct((kw, n_ch), F32), jax.ShapeDtypeStruct((1, n_ch), F32)]
    return pl.pallas_call(
        body, name=name, grid=(n_ch // tc, n_tiles), in_specs=in_specs, out_specs=out_specs, out_shape=out_shape,
        scratch_shapes=[pltpu.VMEM((halo + tile, tc), F32), pltpu.VMEM((tile + halo, tc), F32)], compiler_params=_params(2),
    )(*[a for a, _ in xs], *[a for a, _ in xs], dy, dy, w)


def _retention_consts():
    log_gamma = jnp.log(1.0 - 2.0 ** (-5.0 - jnp.arange(N_HEADS, dtype=F32)))
    idx = jnp.arange(CHUNK, dtype=F32)
    rel = idx[:, None] - idx[None, :]
    decay = jnp.where(rel[None] >= 0, jnp.exp(log_gamma[:, None, None] * jnp.maximum(rel, 0.0)[None]), 0.0)
    zeta = jnp.exp(log_gamma[:, None] * (CHUNK - 1.0 - idx)[None])
    xi = jnp.exp(log_gamma[:, None] * (idx + 1.0)[None])
    chunk_decay = jnp.exp(log_gamma * CHUNK)
    xi_b = jnp.broadcast_to(xi[:, :, None], (N_HEADS, CHUNK, DK))
    zeta_b = jnp.broadcast_to(zeta[:, :, None], (N_HEADS, CHUNK, DK))
    cd_b = jnp.broadcast_to(chunk_decay[:, None, None], (N_HEADS, 8, DV))
    return decay, xi_b, zeta_b, cd_b


def _rope_tables(positions):
    half = DK // 2
    inv_freq = ROPE_BASE ** (-jnp.arange(half, dtype=F32) / half)
    ang = positions.astype(F32)[:, None] * inv_freq
    cos, sin = jnp.cos(ang), jnp.sin(ang)
    return jnp.concatenate([cos, cos], axis=-1), jnp.concatenate([-sin, sin], axis=-1)


def _swap_halves(v):
    return pltpu.roll(v, DK // 2, 1)


def _dot(a, b):
    return jnp.dot(a, b, preferred_element_type=F32)


def _dot_nt(a, b):
    return lax.dot_general(a, b, (((1,), (1,)), ((), ())), preferred_element_type=F32)


def _dot_tn(a, b):
    return lax.dot_general(a, b, (((0,), (0,)), ((), ())), preferred_element_type=F32)


def _const_specs():
    return [pl.BlockSpec((N_HEADS, CHUNK, CHUNK), lambda n: (0, 0, 0)), pl.BlockSpec((N_HEADS, CHUNK, DK), lambda n: (0, 0, 0)),
            pl.BlockSpec((N_HEADS, CHUNK, DK), lambda n: (0, 0, 0)), pl.BlockSpec((N_HEADS, 8, DV), lambda n: (0, 0, 0))]


def _retention_fwd(proj, cos_t, sin_t, consts, n_rows):
    n_chunks = n_rows // CHUNK
    scale = DK ** -0.5

    def body(q_ref, k_ref, v_ref, cf_ref, ss_ref, d_ref, xi_ref, zt_ref, cd_ref, r_ref, st_ref, state):
        @pl.when(pl.program_id(0) == 0)
        def _():
            state[...] = jnp.zeros_like(state)

        cf, ss = cf_ref[...], ss_ref[...]
        for h in range(N_HEADS):
            qh = q_ref[:, h * DK:(h + 1) * DK]
            kh = k_ref[:, h * DK:(h + 1) * DK]
            qh = qh * cf + _swap_halves(qh) * ss
            kh = (kh * cf + _swap_halves(kh) * ss) * scale
            vh = v_ref[:, h * DV:(h + 1) * DV].astype(BF16)
            st = state[h]
            st_ref[0, h] = st
            sd = _dot_nt(qh.astype(BF16), kh.astype(BF16)) * d_ref[h]
            inner = _dot(sd.astype(BF16), vh)
            cross = _dot((qh * xi_ref[h]).astype(BF16), st.astype(BF16))
            kv = _dot_tn((kh * zt_ref[h]).astype(BF16), vh)
            state[h] = st * cd_ref[h, 0:1, :] + kv
            r_ref[:, h * DV:(h + 1) * DV] = inner + cross

    qk = N_HEADS * DK
    vw = N_HEADS * DV
    return pl.pallas_call(
        body, name="retention_fwd", grid=(n_chunks,),
        in_specs=[pl.BlockSpec((CHUNK, qk), lambda n: (n, 0)), pl.BlockSpec((CHUNK, qk), lambda n: (n, 1)),
                  pl.BlockSpec((CHUNK, vw), lambda n: (n, 1)), pl.BlockSpec((CHUNK, DK), lambda n: (n, 0)),
                  pl.BlockSpec((CHUNK, DK), lambda n: (n, 0))] + _const_specs(),
        out_specs=[pl.BlockSpec((CHUNK, vw), lambda n: (n, 0)), pl.BlockSpec((1, N_HEADS, DK, DV), lambda n: (n, 0, 0, 0))],
        out_shape=[jax.ShapeDtypeStruct((n_rows, vw), F32), jax.ShapeDtypeStruct((n_chunks, N_HEADS, DK, DV), F32)],
        scratch_shapes=[pltpu.VMEM((N_HEADS, DK, DV), F32)], compiler_params=_params(1),
    )(proj, proj, proj, cos_t, sin_t, *consts)


def _retention_bwd(proj, cos_t, sin_t, states, dr, consts, n_rows):
    n_chunks = n_rows // CHUNK
    scale = DK ** -0.5

    def body(q_ref, k_ref, v_ref, cf_ref, ss_ref, st_ref, dr_ref, d_ref, xi_ref, zt_ref, cd_ref, dq_ref, dk_ref, dv_ref, g_ref):
        @pl.when(pl.program_id(0) == 0)
        def _():
            g_ref[...] = jnp.zeros_like(g_ref)

        cf, ss = cf_ref[...], ss_ref[...]
        for h in range(N_HEADS):
            qh = q_ref[:, h * DK:(h + 1) * DK]
            kh = k_ref[:, h * DK:(h + 1) * DK]
            qh = qh * cf + _swap_halves(qh) * ss
            kh = (kh * cf + _swap_halves(kh) * ss) * scale
            qb, kb = qh.astype(BF16), kh.astype(BF16)
            vh = v_ref[:, h * DV:(h + 1) * DV].astype(BF16)
            do = dr_ref[:, h * DV:(h + 1) * DV].astype(BF16)
            rb = st_ref[0, h].astype(BF16)
            g = g_ref[h]
            gb = g.astype(BF16)
            dec, xi, zt = d_ref[h], xi_ref[h], zt_ref[h]
            sd = (_dot_nt(qb, kb) * dec).astype(BF16)
            ds = (_dot_nt(do, vh) * dec).astype(BF16)
            dqh = _dot(ds, kb) + _dot_nt(do, rb) * xi
            dkh = (_dot_tn(ds, qb) + _dot_nt(vh, gb) * zt) * scale
            dvh = _dot_tn(sd, do) + _dot((kh * zt).astype(BF16), gb)
            g_ref[h] = g * cd_ref[h, 0:1, :] + _dot_tn((qh * xi).astype(BF16), do)
            dq_ref[:, h * DK:(h + 1) * DK] = (dqh * cf + _swap_halves(dqh * ss)).astype(dq_ref.dtype)
            dk_ref[:, h * DK:(h + 1) * DK] = (dkh * cf + _swap_halves(dkh * ss)).astype(dk_ref.dtype)
            dv_ref[:, h * DV:(h + 1) * DV] = dvh.astype(dv_ref.dtype)

    qk = N_HEADS * DK
    vw = N_HEADS * DV
    last = n_chunks - 1
    return pl.pallas_call(
        body, name="retention_bwd", grid=(n_chunks,),
        in_specs=[pl.BlockSpec((CHUNK, qk), lambda n: (last - n, 0)), pl.BlockSpec((CHUNK, qk), lambda n: (last - n, 1)),
                  pl.BlockSpec((CHUNK, vw), lambda n: (last - n, 1)), pl.BlockSpec((CHUNK, DK), lambda n: (last - n, 0)),
                  pl.BlockSpec((CHUNK, DK), lambda n: (last - n, 0)),
                  pl.BlockSpec((1, N_HEADS, DK, DV), lambda n: (last - n, 0, 0, 0)),
                  pl.BlockSpec((CHUNK, vw), lambda n: (last - n, 0))] + _const_specs(),
        out_specs=[pl.BlockSpec((CHUNK, qk), lambda n: (last - n, 0)), pl.BlockSpec((CHUNK, qk), lambda n: (last - n, 0)),
                   pl.BlockSpec((CHUNK, vw), lambda n: (last - n, 0))],
        out_shape=[jax.ShapeDtypeStruct((n_rows, qk), BF16), jax.ShapeDtypeStruct((n_rows, qk), BF16),
                   jax.ShapeDtypeStruct((n_rows, vw), BF16)],
        scratch_shapes=[pltpu.VMEM((N_HEADS, DK, DV), F32)], compiler_params=_params(1),
    )(proj, proj, proj, cos_t, sin_t, states, dr, *consts)


def _ln(v):
    mu = jnp.mean(v, axis=-1, keepdims=True)
    var = jnp.mean(jnp.square(v - mu), axis=-1, keepdims=True)
    return (v - mu) * lax.rsqrt(var + LN_EPS)


def _f_modulate(x, scale, shift):
    return _ln(x) * (1.0 + scale) + shift


def _f_conv_norm(a1, g, b):
    return jax.nn.silu(_ln(a1) * g + b)


def _f_group_norm_gate(r, gate, g, b):
    return (_ln(r) * g + b) * jax.nn.silu(gate)


def _f_merge(ga, gb, ya, yb):
    return jax.nn.sigmoid(ga) * ya + jax.nn.sigmoid(gb) * yb


def _f_post1(x, t, gate1, g1, b1, scale2, shift2):
    x1 = _ln(ALPHA * x + gate1 * t) * g1 + b1
    return x1, _ln(x1) * (1.0 + scale2) + shift2


def _f_loss(x1, f, gate2, g2, b2, target):
    y = _ln(ALPHA * x1 + gate2 * f) * g2 + b2
    return 0.5 * jnp.sum(jnp.mean(jnp.square(y - target), axis=-1))


ANY = pl.BlockSpec(memory_space=pl.ANY)


def _allgather8(name, blk):
    r, c_ = blk.shape

    def body(x_ref, out_ref, send_sems, recv_sems, local_sem):
        x, y, c = lax.axis_index("x"), lax.axis_index("y"), lax.axis_index("c")
        me, sibling = (x, y, c), (x, y, 1 - c)
        chips = [(1 - x, y), (x, 1 - y), (1 - x, 1 - y)]

        def slot(px, py, pc):
            return out_ref.at[4 * px + 2 * py + pc]

        def copy(k, block, to, src=None):
            return pltpu.make_async_remote_copy(
                src_ref=slot(*block) if src is None else src, dst_ref=slot(*block), send_sem=send_sems.at[k],
                recv_sem=recv_sems.at[k], device_id=to, device_id_type=MESH)

        mine = pltpu.make_async_copy(x_ref, slot(*me), local_sem)
        mine.start()
        first = [copy(0, me, sibling, src=x_ref)]
        first += [copy(1 + j, me, (*chip, c), src=x_ref) for j, chip in enumerate(chips)]
        for cp in first:
            cp.start()
        passed = [copy(4 + j, (*chip, c), sibling) for j, chip in enumerate(chips)]
        for j, chip in enumerate(chips):
            copy(1 + j, (*chip, c), me).wait_recv()
            passed[j].start()
        copy(0, sibling, me).wait_recv()
        for j, chip in enumerate(chips):
            copy(4 + j, (*chip, 1 - c), me).wait_recv()
        for cp in first + passed:
            cp.wait_send()
        mine.wait()

    return pl.pallas_call(
        body, name=name, in_specs=[ANY], out_specs=ANY, out_shape=jax.ShapeDtypeStruct((N_DEV, r, c_), blk.dtype),
        scratch_shapes=[pltpu.SemaphoreType.DMA((7,)), pltpu.SemaphoreType.DMA((7,)), pltpu.SemaphoreType.DMA],
    )(blk)


def _sibling_swap(name, g_full, half_rows):
    n, _, c_ = g_full.shape

    def body(g_ref, out_ref, send_sem, recv_sem):
        x, y, c = lax.axis_index("x"), lax.axis_index("y"), lax.axis_index("c")
        cp = pltpu.make_async_remote_copy(
            src_ref=g_ref.at[:, pl.ds((1 - c) * half_rows, half_rows), :], dst_ref=out_ref, send_sem=send_sem,
            recv_sem=recv_sem, device_id=(x, y, 1 - c), device_id_type=MESH)
        cp.start()
        cp.wait_recv()
        cp.wait_send()

    return pl.pallas_call(
        body, name=name, in_specs=[ANY], out_specs=ANY, out_shape=jax.ShapeDtypeStruct((n, half_rows, c_), g_full.dtype),
        scratch_shapes=[pltpu.SemaphoreType.DMA, pltpu.SemaphoreType.DMA],
    )(g_full)


def _chip_alltoall(name, p):
    def body(p_ref, out_ref, send_sems, recv_sems, local_sem):
        x, y, c = lax.axis_index("x"), lax.axis_index("y"), lax.axis_index("c")
        mine = 2 * x + y
        own = pltpu.make_async_copy(p_ref.at[mine], out_ref.at[mine], local_sem)
        own.start()
        copies = []
        for k, (px, py) in enumerate([(1 - x, y), (x, 1 - y), (1 - x, 1 - y)]):
            cp = pltpu.make_async_remote_copy(
                src_ref=p_ref.at[2 * px + py], dst_ref=out_ref.at[mine], send_sem=send_sems.at[k], recv_sem=recv_sems.at[k],
                device_id=(px, py, c), device_id_type=MESH)
            cp.start()
            copies.append(cp)
        for cp in copies:
            cp.wait_recv()
        for cp in copies:
            cp.wait_send()
        own.wait()

    return pl.pallas_call(
        body, name=name, in_specs=[ANY], out_specs=ANY, out_shape=jax.ShapeDtypeStruct(p.shape, p.dtype),
        scratch_shapes=[pltpu.SemaphoreType.DMA((3,)), pltpu.SemaphoreType.DMA((3,)), pltpu.SemaphoreType.DMA],
    )(p)


def _sibling_gather(name, half):
    r, c_ = half.shape

    def body(h_ref, out_ref, send_sem, recv_sem, local_sem):
        x, y, c = lax.axis_index("x"), lax.axis_index("y"), lax.axis_index("c")
        own = pltpu.make_async_copy(h_ref, out_ref.at[c], local_sem)
        own.start()
        cp = pltpu.make_async_remote_copy(
            src_ref=h_ref, dst_ref=out_ref.at[c], send_sem=send_sem, recv_sem=recv_sem, device_id=(x, y, 1 - c),
            device_id_type=MESH)
        cp.start()
        cp.wait_recv()
        cp.wait_send()
        own.wait()

    return pl.pallas_call(
        body, name=name, in_specs=[ANY], out_specs=ANY, out_shape=jax.ShapeDtypeStruct((2, r, c_), half.dtype),
        scratch_shapes=[pltpu.SemaphoreType.DMA, pltpu.SemaphoreType.DMA, pltpu.SemaphoreType.DMA],
    )(half)


def _sum_arrays(name, arrays):
    r, c_ = arrays[0].shape

    def add_all(*vals):
        acc = vals[0]
        for v in vals[1:]:
            acc = acc + v
        return (acc,)

    return _rowwise(name, add_all, [(a, c_, 0) for a in arrays], [], [(c_, F32)], [], r)[0]


def _adamw_fn(w, g, m, v):
    m = ADAM_B1 * m + (1.0 - ADAM_B1) * g
    v = ADAM_B2 * v + (1.0 - ADAM_B2) * jnp.square(g)
    m_hat = m / (1.0 - ADAM_B1 ** ADAM_STEP)
    v_hat = v / (1.0 - ADAM_B2 ** ADAM_STEP)
    delta = -ADAM_LR * (m_hat / (jnp.sqrt(v_hat) + ADAM_EPS) + ADAM_WD * w)
    return delta, m, v


def _adamw(name, w, g, m, v):
    r, c_ = w.shape
    return _rowwise(name, _adamw_fn, [(w, c_, 0), (g, c_, 0), (m, c_, 0), (v, c_, 0)], [], [(c_, F32)] * 3, [], r)


def _ada_fwd(c_all, w_ada, b_ada):
    n = w_ada.shape[1]

    def body(c_ref, w_ref, b_ref, o_ref):
        o_ref[...] = jnp.dot(jax.nn.silu(c_ref[...]), w_ref[...], preferred_element_type=F32,
                             precision=lax.Precision.HIGHEST) + b_ref[...]

    return pl.pallas_call(body, name="ada_fwd", out_shape=jax.ShapeDtypeStruct((N_DEV, n), F32),
                          compiler_params=pltpu.CompilerParams(vmem_limit_bytes=VMEM_LIMIT))(c_all, w_ada, b_ada)


def _ada_bwd(c_all, dmod_cols):
    d = c_all.shape[1]
    n = dmod_cols.shape[1]

    def body(c_ref, dm_ref, gw_ref):
        gw_ref[...] = lax.dot_general(jax.nn.silu(c_ref[...]), dm_ref[...], (((0,), (0,)), ((), ())),
                                      preferred_element_type=F32, precision=lax.Precision.HIGHEST)

    return pl.pallas_call(body, name="ada_bwd", out_shape=jax.ShapeDtypeStruct((d, n), F32),
                          compiler_params=pltpu.CompilerParams(vmem_limit_bytes=VMEM_LIMIT))(c_all, dmod_cols)


def _split_conv(res, n_x):
    return tuple(res[:n_x]), res[n_x], res[n_x + 1]


def _cast_bf16(name, a):
    r, c_ = a.shape
    return _rowwise(name, lambda v: (v,), [(a, c_, 0)], [], [(c_, BF16)], [], r)[0]


def _pad_rows(vec, mult):
    n = vec.shape[0]
    return jnp.pad(vec, (0, (-n) % mult))


def kernel(x, c, positions, w_ada, b_ada, w_in, b_in, conv_dw_w, conv_dw_b, conv_ln_g, conv_ln_b, w_conv_out, ret_gn_g, ret_gn_b, w_ret_out, w_out, ln1_g, ln1_b, w_up, ffn_dw_w, ffn_dw_b, w_down, ln2_g, ln2_b, loss_target, m_w_ada, m_b_ada, m_w_in, m_b_in, m_conv_dw_w, m_conv_dw_b, m_conv_ln_g, m_conv_ln_b, m_w_conv_out, m_ret_gn_g, m_ret_gn_b, m_w_ret_out, m_w_out, m_ln1_g, m_ln1_b, m_w_up, m_ffn_dw_w, m_ffn_dw_b, m_w_down, m_ln2_g, m_ln2_b, v_w_ada, v_b_ada, v_w_in, v_b_in, v_conv_dw_w, v_conv_dw_b, v_conv_ln_g, v_conv_ln_b, v_w_conv_out, v_ret_gn_g, v_ret_gn_b, v_w_ret_out, v_w_out, v_ln1_g, v_ln1_b, v_w_up, v_ffn_dw_w, v_ffn_dw_b, v_w_down, v_ln2_g, v_ln2_b):
    given = dict(locals())
    n_rows = x.shape[1]
    d = D_MODEL
    my_c = lax.axis_index("c")
    chip = 2 * lax.axis_index("x") + lax.axis_index("y")
    dev = 2 * chip + my_c
    xr = x[0]
    target = loss_target[0]
    vw = N_HEADS * DV
    ffw = 2 * D_FF

    def pack(prefix):
        return jnp.concatenate([given[prefix + n][0].reshape(r, d) for n, r in PACK_ROWS], axis=0)

    def unpack(packed):
        out, o = {}, 0
        for n, r in PACK_ROWS:
            out[n] = packed[..., o:o + r, :]
            o += r
        return out

    def flat_rows(arrays, mult=8):
        v = jnp.concatenate([a.reshape(-1) for a in arrays])
        return _pad_rows(v, mult * d).reshape(-1, d)

    def unflatten(flat2d, shapes):
        v, out, o = flat2d.reshape(-1), [], 0
        for shp in shapes:
            size = 1
            for e in shp:
                size *= e
            out.append(v[o:o + size].reshape(shp))
            o += size
        return out

    w_pack = pack("")
    w_bf = _cast_bf16("cast_weights", w_pack)
    half = PACK_TOTAL // 2
    w_all = _allgather8("gather_weights", lax.dynamic_slice_in_dim(w_bf, my_c * half, half, 0))
    w_all = unpack(w_all.reshape(N_CHIPS, PACK_TOTAL, d))
    wg_in = w_all["w_in"].reshape(N_CHIPS, d, w_in.shape[2])
    wg_up = w_all["w_up"].reshape(N_CHIPS, d, w_up.shape[2])
    wg_conv_out = w_all["w_conv_out"].reshape(1, d, d)
    wg_ret_out = w_all["w_ret_out"].reshape(1, vw, d)
    wg_out = w_all["w_out"].reshape(1, d, d)
    wg_down = w_all["w_down"].reshape(1, D_FF, d)

    kc, kf = conv_dw_w.shape[2], ffn_dw_w.shape[2]
    small_all = _allgather8("gather_small", flat_rows([c, conv_dw_w, ffn_dw_w])).reshape(N_DEV, -1)
    c_all = small_all[:, :d]
    per_chip = small_all[0::2]
    conv_w = per_chip[:, d:d + CONV_K * kc].reshape(N_CHIPS, CONV_K, kc).transpose(1, 0, 2).reshape(CONV_K, N_CHIPS * kc)
    o_f = d + CONV_K * kc
    ffn_w = per_chip[:, o_f:o_f + FFN_K * kf].reshape(N_CHIPS, FFN_K, kf).transpose(1, 0, 2).reshape(FFN_K, N_CHIPS * kf)

    n_ada = w_ada.shape[2]
    b_ada_cols = lax.dynamic_slice_in_dim(b_ada, chip * n_ada, n_ada, 1)
    mod_cols = _ada_fwd(c_all, w_ada[0], b_ada_cols)
    mod_all = _allgather8("gather_mod", mod_cols)
    mod = lax.dynamic_index_in_dim(mod_all[0::2], dev, 1, keepdims=False).reshape(1, N_CHIPS * n_ada)
    shift1, scale1, gate1, shift2, scale2, gate2 = [mod[:, k * d:(k + 1) * d] for k in range(6)]

    h1 = _rowwise("ln_mod1", lambda a, s, t: (_f_modulate(a, s, t),), [(xr, d, 0)], [(scale1, d, 0), (shift1, d, 0)],
                  [(d, BF16)], [], n_rows)[0]
    proj = _mm_nn("mm_in", h1, wg_in, b_in, F32)
    conv_in = [(proj, 6144 // 256), (proj, 7168 // 256)]
    a1 = _dwconv_fwd("conv_fwd", conv_in, conv_w, conv_dw_b, n_rows, d, 256, True)
    a2 = _rowwise("conv_norm", lambda a, g, b: (_f_conv_norm(a, g, b),), [(a1, d, 0)], [(conv_ln_g, d, 0), (conv_ln_b, d, 0)],
                  [(d, BF16)], [], n_rows)[0]
    y_a = _mm_nn("mm_conv_out", a2, wg_conv_out, None, F32)
    cos_t, sin_t = _rope_tables(positions[0])
    consts = _retention_consts()
    r, states = _retention_fwd(proj, cos_t, sin_t, consts, n_rows)
    ret_rows = [(r, DV, 0), (proj, DV, 4096 // DV)]
    ret_vecs = [(ret_gn_g, DV, 0), (ret_gn_b, DV, 0)]
    r2 = _rowwise("ret_norm", lambda a, gt, g, b: (_f_group_norm_gate(a, gt, g, b),), ret_rows, ret_vecs, [(DV, BF16)], [],
                  n_rows, ncol=N_HEADS)[0]
    y_b = _mm_nn("mm_ret_out", r2, wg_ret_out, None, F32)
    merge_rows = [(proj, d, 8), (proj, d, 9), (y_a, d, 0), (y_b, d, 0)]
    m = _rowwise("merge", lambda *a: (_f_merge(*a),), merge_rows, [], [(d, BF16)], [], n_rows)[0]
    t = _mm_nn("mm_out", m, wg_out, None, F32)
    post1_vecs = [(gate1, d, 0), (ln1_g, d, 0), (ln1_b, d, 0), (scale2, d, 0), (shift2, d, 0)]
    x1, h2 = _rowwise("post1", _f_post1, [(xr, d, 0), (t, d, 0)], post1_vecs, [(d, F32), (d, BF16)], [], n_rows)
    u = _mm_nn("mm_up", h2, wg_up, None, F32)
    u2 = _dwconv_fwd("ffn_conv_fwd", [(u, 0)], ffn_w, ffn_dw_b, n_rows, ffw, kf, False)
    p = _rowwise("ffn_gate", lambda a, b: (a * jax.nn.silu(b),), [(u2, kf, 0), (u2, kf, 2)], [], [(kf, BF16)], [], n_rows,
                 ncol=2)[0]
    f = _mm_nn("mm_down", p, wg_down, None, F32)

    def loss_rows(x1v, fv, tv, g2v, lg, lb):
        loss, pull = jax.vjp(lambda a, b, c_, e, h: _f_loss(a, b, c_, e, h, tv), x1v, fv, g2v, lg, lb)
        return (*pull(jnp.ones((), F32)), jnp.full((1, 128), loss, F32))

    dx1_a, df, dgate2, dln2_g, dln2_b, loss_v = _rowwise(
        "loss", loss_rows, [(x1, d, 0), (f, d, 0), (target, d, 0)], [(gate2, d, 0), (ln2_g, d, 0), (ln2_b, d, 0)],
        [(d, F32), (d, BF16)], [d, d, d, 128], n_rows)
    loss = lax.psum(loss_v[0, 0], ("x", "y", "c"))

    dp = _mm_nt("mm_down_dx", df, wg_down, F32)
    gw_down = _mm_tn("mm_down_dw", p, df, 1)

    def gate_bwd(j, own, other, dpv):
        is_val = j < 2
        val = jnp.where(is_val, own, other)
        gate = jnp.where(is_val, other, own)
        sig = jax.nn.sigmoid(gate)
        d_val = dpv * gate * sig
        d_gate = dpv * val * sig * (1.0 + gate * (1.0 - sig))
        return (jnp.where(is_val, d_val, d_gate),)

    du2 = _rowwise("ffn_gate_bwd", gate_bwd, [(u2, kf, 0), (u2, kf, lambda j: (j + 2) % 4), (dp, kf, lambda j: j % 2)], [],
                   [(kf, F32)], [], n_rows, ncol=4, with_col=True)[0]
    (du,), g_ffn_w, g_ffn_b = _split_conv(_dwconv_bwd("ffn_conv_bwd", [(u, 0)], du2, ffn_w, n_rows, ffw, kf, False), 1)
    dh2 = _mm_nt("mm_up_dx", du, wg_up, F32)
    gw_up = _mm_tn("mm_up_dw", h2, du, N_CHIPS)

    dx_a, dt, dgate1, dln1_g, dln1_b, dscale2, dshift2 = _rowwise(
        "post1_bwd", _vjp_rows(_f_post1, 2, 2), [(xr, d, 0), (t, d, 0), (dx1_a, d, 0), (dh2, d, 0)], post1_vecs,
        [(d, F32), (d, BF16)], [d] * 5, n_rows)
    dm = _mm_nt("mm_out_dx", dt, wg_out, F32)
    gw_out = _mm_tn("mm_out_dw", m, dt, 1)
    dga, dgb, dya, dyb = _rowwise("merge_bwd", _vjp_rows(_f_merge, 4, 1), merge_rows + [(dm, d, 0)], [], [(d, BF16)] * 4, [],
                                  n_rows)

    da2 = _mm_nt("mm_conv_out_dx", dya, wg_conv_out, F32)
    gw_conv_out = _mm_tn("mm_conv_out_dw", a2, dya, 1)
    da1, dcl_g, dcl_b = _rowwise("conv_norm_bwd", _vjp_rows(_f_conv_norm, 1, 1), [(a1, d, 0), (da2, d, 0)],
                                 [(conv_ln_g, d, 0), (conv_ln_b, d, 0)], [(d, F32)], [d, d], n_rows)
    (dcv, dcg), g_conv_w, g_conv_b = _split_conv(_dwconv_bwd("conv_bwd", conv_in, da1, conv_w, n_rows, d, 256, True), 2)

    dr2 = _mm_nt("mm_ret_out_dx", dyb, wg_ret_out, F32)
    gw_ret_out = _mm_tn("mm_ret_out_dw", r2, dyb, 1)
    dr, dgr, dgn_g, dgn_b = _rowwise("ret_norm_bwd", _vjp_rows(_f_group_norm_gate, 2, 1), ret_rows + [(dr2, DV, 0)], ret_vecs,
                                     [(DV, F32), (DV, BF16)], [DV, DV], n_rows, ncol=N_HEADS)
    dq, dk, dv = _retention_bwd(proj, cos_t, sin_t, states, dr, consts, n_rows)

    dproj = jnp.concatenate([dq, dk, dv, dgr, dcv, dcg, dga, dgb], axis=1)
    gb_in = _rowwise("bias_in_grad", lambda a: (jnp.sum(a.astype(F32), axis=0, keepdims=True),), [(dproj, d, 0)], [], [], [d],
                     n_rows, ncol=dproj.shape[1] // d)[0]
    dh1 = _mm_nt("mm_in_dx", dproj, wg_in, F32)
    gw_in = _mm_tn("mm_in_dw", h1, dproj, N_CHIPS)
    mod_bwd = _vjp_rows(_f_modulate, 1, 1)

    def mod1_bwd(xv, dhv, dxav, sv, tv):
        dx, ds, dsh = mod_bwd(xv, dhv, sv, tv)
        return dx + dxav, ds, dsh

    grad_x, dscale1, dshift1 = _rowwise("ln_mod1_bwd", mod1_bwd, [(xr, d, 0), (dh1, d, 0), (dx_a, d, 0)],
                                        [(scale1, d, 0), (shift1, d, 0)], [(d, F32)], [d, d], n_rows)

    dmod = jnp.concatenate([dshift1, dscale1, dgate1, dshift2, dscale2, dgate2], axis=1)
    small_names = ["b_in", "conv_dw_w", "conv_dw_b", "conv_ln_g", "conv_ln_b", "ret_gn_g", "ret_gn_b", "ln1_g", "ln1_b",
                   "ffn_dw_w", "ffn_dw_b", "ln2_g", "ln2_b"]
    small_parts = [gb_in, g_conv_w, g_conv_b, dcl_g, dcl_b, dgn_g, dgn_b, dln1_g, dln1_b, g_ffn_w, g_ffn_b, dln2_g, dln2_b, dmod]
    small_shapes = [a.shape for a in small_parts]
    parts_all = _allgather8("gather_small_grads", flat_rows(small_parts))
    summed = _sum_arrays("sum_small_grads", [parts_all[k] for k in range(N_DEV)])
    small_sum = unflatten(summed, small_shapes)
    grads = dict(zip(small_names, small_sum[:-1]))
    grads["b_ada"] = small_sum[-1]
    grads["conv_dw_w"] = lax.dynamic_slice_in_dim(grads["conv_dw_w"], chip * kc, kc, 1)
    grads["ffn_dw_w"] = lax.dynamic_slice_in_dim(grads["ffn_dw_w"], chip * kf, kf, 1)
    o_mod = sum(a.size for a in small_parts[:-1])
    dmod_all = parts_all.reshape(N_DEV, -1)[:, o_mod:o_mod + dmod.shape[1]]
    grads["w_ada"] = _ada_bwd(c_all, lax.dynamic_slice_in_dim(dmod_all, chip * n_ada, n_ada, 1))

    g_pack = jnp.concatenate([gw_in.reshape(N_CHIPS, -1, d), gw_up.reshape(N_CHIPS, -1, d), gw_conv_out.reshape(N_CHIPS, -1, d),
                              gw_ret_out.reshape(N_CHIPS, -1, d), gw_out.reshape(N_CHIPS, -1, d),
                              gw_down.reshape(N_CHIPS, -1, d)], axis=1)
    theirs = _sibling_swap("grad_swap", g_pack, half)
    mine = lax.dynamic_slice_in_dim(g_pack, my_c * half, half, 1)
    pair = _sum_arrays("grad_pair_sum", [mine.reshape(N_CHIPS * half, d), theirs.reshape(N_CHIPS * half, d)])
    arrived = _chip_alltoall("grad_alltoall", pair.reshape(N_CHIPS, half, d))
    reduced = _sum_arrays("grad_chip_sum", [arrived[k] for k in range(N_CHIPS)])
    g_shard = _sibling_gather("grad_gather", reduced).reshape(PACK_TOTAL, d)

    delta_p, m_p, v_p = _adamw("adamw_big", w_pack, g_shard, pack("m_"), pack("v_"))
    outs = {}
    for prefix, packed in (("grad_", g_shard), ("delta_", delta_p), ("new_m_", m_p), ("new_v_", v_p)):
        for n, rows in unpack(packed).items():
            outs[prefix + n] = rows.reshape(given[n].shape)
    ada = _adamw("adamw_ada", w_ada[0], grads["w_ada"], m_w_ada[0], v_w_ada[0])
    for prefix, val in zip(("grad_", "delta_", "new_m_", "new_v_"), (grads["w_ada"], *ada)):
        outs[prefix + "w_ada"] = val.reshape(w_ada.shape)
    small_all_names = ["b_ada"] + small_names
    small_w_shapes = [given[n].shape for n in small_all_names]
    g_small = flat_rows([grads[n] for n in small_all_names])
    small_upd = _adamw("adamw_small", flat_rows([given[n] for n in small_all_names]), g_small,
                       flat_rows([given["m_" + n] for n in small_all_names]), flat_rows([given["v_" + n] for n in small_all_names]))
    for prefix, packed in zip(("grad_", "delta_", "new_m_", "new_v_"), (g_small, *small_upd)):
        for n, val in zip(small_all_names, unflatten(packed, small_w_shapes)):
            outs[prefix + n] = val

    weights = ["w_ada", "b_ada", "w_in", "b_in", "conv_dw_w", "conv_dw_b", "conv_ln_g", "conv_ln_b", "w_conv_out", "ret_gn_g",
               "ret_gn_b", "w_ret_out", "w_out", "ln1_g", "ln1_b", "w_up", "ffn_dw_w", "ffn_dw_b", "w_down", "ln2_g", "ln2_b"]
    result = [loss, grad_x.reshape(x.shape)]
    for prefix in ("grad_", "delta_", "new_m_", "new_v_"):
        result += [outs[prefix + n] for n in weights]
    return tuple(result)
```

```python
import jax
import jax.numpy as jnp
from jax import lax
from jax.experimental import pallas as pl
from jax.experimental.pallas import tpu as pltpu

F32 = jnp.float32
BF16 = jnp.bfloat16
MESH = pl.DeviceIdType.MESH

D_MODEL = 1024
N_HEADS = 8
DK = 128
DV = 256
CHUNK = 128
ROPE_BASE = 10000.0
D_FF = 2816
CONV_K = 31
FFN_K = 3
LN_EPS = 1e-5
ALPHA = (2.0 * 1) ** 0.25
ADAM_LR = 0.001
ADAM_B1 = 0.9
ADAM_B2 = 0.999
ADAM_EPS = 1e-08
ADAM_WD = 0.01
ADAM_STEP = 10

V7X_VMEM_BYTES = 64 * 1024 * 1024
VMEM_LIMIT = V7X_VMEM_BYTES - 8 * 1024 * 1024
ROW_TILE = 256
MM_TILE = 512
N_CHIPS = 4
N_DEV = 8

PACK_ROWS = (("w_in", 2560), ("w_up", 1408), ("w_conv_out", 256), ("w_ret_out", 512), ("w_out", 256), ("w_down", 704))
PACK_TOTAL = sum(r for _, r in PACK_ROWS)


def _params(n_grid):
    return pltpu.CompilerParams(dimension_semantics=("arbitrary",) * n_grid, vmem_limit_bytes=VMEM_LIMIT)


def _rowwise(name, fn, rows, vecs, outs, reds, n_rows, tile=ROW_TILE, ncol=1, with_col=False, into=None):
    tile = _fit_tile(n_rows, tile)
    n_in = len(rows) + len(vecs)
    n_ref_in = n_in + (into is not None)
    n_out = len(outs)
    outs = [o if len(o) == 4 else (o[0], o[1], o[0] * ncol, 0) for o in outs]

    def col_map(off, row):
        def index(j, i):
            return (i if row else 0, off(j) if callable(off) else off + j)
        return index

    def body(*refs):
        i = pl.program_id(1)
        vals = [r[...].astype(F32) for r in refs[:n_in]]
        res = fn(pl.program_id(0), *vals) if with_col else fn(*vals)
        for k in range(n_out):
            refs[n_ref_in + k][...] = res[k].astype(refs[n_ref_in + k].dtype)
        for k in range(len(reds)):
            o = refs[n_ref_in + n_out + k]

            @pl.when(i == 0)
            def _():
                o[...] = jnp.zeros_like(o)

            o[...] += res[n_out + k]

    in_specs = [pl.BlockSpec((tile, w), col_map(off, True)) for _, w, off in rows]
    in_specs += [pl.BlockSpec((1, w), col_map(off, False)) for _, w, off in vecs]
    args = [a for a, _, _ in rows] + [a for a, _, _ in vecs]
    aliases = {}
    if into is not None:
        in_specs.append(pl.BlockSpec(memory_space=pl.ANY))
        args.append(into[0])
        aliases = {n_in: into[1]}
    out_specs = [pl.BlockSpec((tile, w), col_map(off, True)) for w, _, _, off in outs]
    out_specs += [pl.BlockSpec((1, w), lambda j, i: (0, j)) for w in reds]
    out_shape = [jax.ShapeDtypeStruct((n_rows, total), dt) for _, dt, total, _ in outs]
    out_shape += [jax.ShapeDtypeStruct((1, w * ncol), F32) for w in reds]
    return pl.pallas_call(
        body, name=name, grid=(ncol, n_rows // tile), in_specs=in_specs, out_specs=out_specs, out_shape=out_shape,
        input_output_aliases=aliases, compiler_params=_params(2),
    )(*args)


def _vjp_rows(fn, n_row_in, n_ct):
    def bwd(*args):
        prim = [a.astype(F32) for a in args[:n_row_in] + args[n_row_in + n_ct:]]
        cts = tuple(a.astype(F32) for a in args[n_row_in:n_row_in + n_ct])
        _, pull = jax.vjp(fn, *prim)
        return pull(cts if n_ct > 1 else cts[0])

    return bwd


def _fit_tile(n, pref):
    if n <= pref:
        return n
    t = pref - pref % 16
    while n % t:
        t -= 16
    return t


def _col_tile(n):
    return n if n <= 1536 else n // 2


def _same_group(c):
    return c


def _mm_nn(name, a, w, bias, out_dtype, gmap=_same_group):
    s, k = a.shape
    g, _, n = w.shape
    tm, tn = min(MM_TILE, s), _col_tile(n)
    nt = n // tn

    def body(*refs):
        a_ref, w_ref = refs[0], refs[1]
        o_ref = refs[-1]
        acc = jnp.dot(a_ref[...].astype(BF16), w_ref[...], preferred_element_type=F32)
        if bias is not None:
            acc = acc + refs[2][...]
        o_ref[...] = acc.astype(o_ref.dtype)

    in_specs = [pl.BlockSpec((tm, k), lambda c, i: (i, 0)),
                pl.BlockSpec((None, k, tn), lambda c, i: (gmap(c // nt), 0, c % nt))]
    args = [a, w]
    if bias is not None:
        in_specs.append(pl.BlockSpec((1, tn), lambda c, i: (0, c)))
        args.append(bias)
    return pl.pallas_call(
        body, name=name, grid=(g * nt, s // tm), in_specs=in_specs, out_specs=pl.BlockSpec((tm, tn), lambda c, i: (i, c)),
        out_shape=jax.ShapeDtypeStruct((s, g * n), out_dtype), compiler_params=_params(2),
    )(*args)


def _mm_nt(name, dy, w, out_dtype, gmap=_same_group):
    s = dy.shape[0]
    g, k, n = w.shape
    tm, tn = min(MM_TILE, s), n
    nt = 1
    steps = g

    def body(dy_ref, w_ref, o_ref, acc_ref):
        r = pl.program_id(1)
        part = lax.dot_general(dy_ref[...].astype(BF16), w_ref[...], (((1,), (1,)), ((), ())), preferred_element_type=F32)

        @pl.when(r == 0)
        def _():
            acc_ref[...] = part

        @pl.when(r > 0)
        def _():
            acc_ref[...] += part

        @pl.when(r == steps - 1)
        def _():
            o_ref[...] = acc_ref[...].astype(o_ref.dtype)

    return pl.pallas_call(
        body, name=name, grid=(s // tm, steps),
        in_specs=[pl.BlockSpec((tm, tn), lambda i, r: (i, r)),
                  pl.BlockSpec((None, k, tn), lambda i, r: (gmap(r // nt), 0, r % nt))],
        out_specs=pl.BlockSpec((tm, k), lambda i, r: (i, 0)), out_shape=jax.ShapeDtypeStruct((s, k), out_dtype),
        scratch_shapes=[pltpu.VMEM((tm, k), F32)], compiler_params=_params(2),
    )(dy, w)


def _mm_tn(name, a, dy, g, gmap=_same_group, colsum=False):
    s, k = a.shape
    n = dy.shape[1] // g
    ts = min(2048 if k <= 1024 else 1024, s)
    tn = n if k * n <= 1024 * 1536 else (n // 2 if (n // 2) % 128 == 0 else n)
    nt = n // tn

    def body(a_ref, dy_ref, o_ref, *sum_ref):
        t = pl.program_id(1)
        dyv = dy_ref[...].astype(BF16)
        part = lax.dot_general(a_ref[...].astype(BF16), dyv, (((0,), (0,)), ((), ())), preferred_element_type=F32)

        @pl.when(t == 0)
        def _():
            o_ref[...] = part

        @pl.when(t > 0)
        def _():
            o_ref[...] += part

        if colsum:
            col = jnp.sum(dyv.astype(F32), axis=0, keepdims=True)

            @pl.when(t == 0)
            def _():
                sum_ref[0][...] = col

            @pl.when(t > 0)
            def _():
                sum_ref[0][...] += col

    out_specs = [pl.BlockSpec((None, k, tn), lambda c, t: (gmap(c // nt), 0, c % nt))]
    out_shape = [jax.ShapeDtypeStruct((g, k, n), F32)]
    if colsum:
        out_specs.append(pl.BlockSpec((1, tn), lambda c, t: (0, c)))
        out_shape.append(jax.ShapeDtypeStruct((1, g * n), F32))
    res = pl.pallas_call(
        body, name=name, grid=(g * nt, s // ts),
        in_specs=[pl.BlockSpec((ts, k), lambda c, t: (t, 0)), pl.BlockSpec((ts, tn), lambda c, t: (t, c))],
        out_specs=out_specs, out_shape=out_shape, compiler_params=_params(2),
    )(a, dy)
    return res if colsum else res[0]


SUBLANES = 8


def _tap_sum(read, w_row, offsets, tile):
    acc = None
    for b in range(SUBLANES):
        group = [(k, o) for k, o in offsets if o % SUBLANES == b]
        if not group:
            continue
        rows = tile if b == 0 else tile + SUBLANES
        z = None
        for k, o in group:
            term = w_row(k) * read(o - b, rows)
            z = term if z is None else z + term
        part = z if b == 0 else z[b:b + tile]
        acc = part if acc is None else acc + part
    return acc


def _tap_grads(read, dy, offsets, tile):
    padded = jnp.concatenate([dy, jnp.zeros((SUBLANES, dy.shape[1]), dy.dtype)], axis=0)
    out = {}
    for b in range(SUBLANES):
        group = [(k, o) for k, o in offsets if o % SUBLANES == b]
        if not group:
            continue
        shifted = dy if b == 0 else pltpu.roll(padded, b, 0)
        rows = tile if b == 0 else tile + SUBLANES
        for k, o in group:
            out[k] = jnp.sum(shifted * read(o - b, rows), axis=0, keepdims=True)
    return out


CONV_HALO = 32
CONV_TILE = 256
CONV_LANES = 256


def _glu_conv_fwd(proj, col_v, col_g, w, b, n_rows):
    kw, n_ch = w.shape
    tile = min(CONV_TILE, n_rows)
    per = tile // CONV_HALO
    offsets = [(k, CONV_HALO - (kw - 1) + k) for k in range(kw)]

    def body(v_ref, g_ref, vh_ref, gh_ref, w_ref, b_ref, o_ref, buf):
        i = pl.program_id(0)
        prev = vh_ref[...].astype(F32) * jax.nn.sigmoid(gh_ref[...].astype(F32))
        buf[0:CONV_HALO, :] = jnp.where(i == 0, 0.0, prev)
        buf[CONV_HALO:CONV_HALO + tile, :] = v_ref[...].astype(F32) * jax.nn.sigmoid(g_ref[...].astype(F32))
        for c0 in range(0, n_ch, CONV_LANES):
            cols = slice(c0, c0 + CONV_LANES)
            o_ref[:, cols] = b_ref[:, cols] + _tap_sum(lambda s, n: buf[pl.ds(s, n), cols], lambda k: w_ref[k:k + 1, cols],
                                                      offsets, tile)

    main = lambda col: pl.BlockSpec((tile, n_ch), lambda i: (i, col))
    halo = lambda col: pl.BlockSpec((CONV_HALO, n_ch), lambda i: (jnp.maximum(i * per - 1, 0), col))
    return pl.pallas_call(
        body, name="conv_fwd", grid=(n_rows // tile,),
        in_specs=[main(col_v), main(col_g), halo(col_v), halo(col_g), pl.BlockSpec((kw, n_ch), lambda i: (0, 0)),
                  pl.BlockSpec((1, n_ch), lambda i: (0, 0))],
        out_specs=pl.BlockSpec((tile, n_ch), lambda i: (i, 0)), out_shape=jax.ShapeDtypeStruct((n_rows, n_ch), F32),
        scratch_shapes=[pltpu.VMEM((CONV_HALO + tile, n_ch), F32)], compiler_params=_params(1),
    )(proj, proj, proj, proj, w, b)


def _glu_conv_bwd(proj, col_v, col_g, dy, w, dproj, col_out, n_rows):
    kw, n_ch = w.shape
    tile = min(CONV_TILE, n_rows)
    per = tile // CONV_HALO
    n_tiles = n_rows // tile
    last_halo = n_rows // CONV_HALO - 1
    offsets = [(k, CONV_HALO - (kw - 1) + k) for k in range(kw)]
    back = [(k, kw - 1 - k) for k in range(kw)]

    def body(v_ref, g_ref, vh_ref, gh_ref, dy_ref, dyn_ref, w_ref, dp_any, dx_ref, dw_ref, db_ref, buf, dbuf):
        i = pl.program_id(0)
        cv, cg = v_ref[...].astype(F32), g_ref[...].astype(F32)
        sig = jax.nn.sigmoid(cg)
        prev = vh_ref[...].astype(F32) * jax.nn.sigmoid(gh_ref[...].astype(F32))
        buf[0:CONV_HALO, :] = jnp.where(i == 0, 0.0, prev)
        buf[CONV_HALO:CONV_HALO + tile, :] = cv * sig
        dbuf[0:tile, :] = dy_ref[...]
        dbuf[tile:tile + CONV_HALO, :] = jnp.where(i == n_tiles - 1, 0.0, dyn_ref[...])

        @pl.when(i == 0)
        def _():
            dw_ref[...] = jnp.zeros_like(dw_ref)
            db_ref[...] = jnp.zeros_like(db_ref)

        db_ref[...] += jnp.sum(dy_ref[...], axis=0, keepdims=True)
        for c0 in range(0, n_ch, CONV_LANES):
            cols = slice(c0, c0 + CONV_LANES)
            w_row = lambda k: w_ref[k:k + 1, cols]
            dx = _tap_sum(lambda s, n: dbuf[pl.ds(s, n), cols], w_row, back, tile)
            grads = _tap_grads(lambda s, n: buf[pl.ds(s, n), cols], dy_ref[:, cols], offsets, tile)
            for k in range(kw):
                dw_ref[k:k + 1, cols] += grads[k]
            sg = sig[:, cols]
            dx_ref[:, c0:c0 + CONV_LANES] = (dx * sg).astype(dx_ref.dtype)
            dx_ref[:, n_ch + c0:n_ch + c0 + CONV_LANES] = (dx * cv[:, cols] * sg * (1.0 - sg)).astype(dx_ref.dtype)

    main = lambda col: pl.BlockSpec((tile, n_ch), lambda i: (i, col))
    halo = lambda col: pl.BlockSpec((CONV_HALO, n_ch), lambda i: (jnp.maximum(i * per - 1, 0), col))
    return pl.pallas_call(
        body, name="conv_bwd", grid=(n_tiles,),
        in_specs=[main(col_v), main(col_g), halo(col_v), halo(col_g), main(0),
                  pl.BlockSpec((CONV_HALO, n_ch), lambda i: (jnp.minimum((i + 1) * per, last_halo), 0)),
                  pl.BlockSpec((kw, n_ch), lambda i: (0, 0)), pl.BlockSpec(memory_space=pl.ANY)],
        out_specs=[pl.BlockSpec((tile, 2 * n_ch), lambda i: (i, col_out)), pl.BlockSpec((kw, n_ch), lambda i: (0, 0)),
                   pl.BlockSpec((1, n_ch), lambda i: (0, 0))],
        out_shape=[jax.ShapeDtypeStruct(dproj.shape, dproj.dtype), jax.ShapeDtypeStruct((kw, n_ch), F32),
                   jax.ShapeDtypeStruct((1, n_ch), F32)],
        input_output_aliases={7: 0},
        scratch_shapes=[pltpu.VMEM((CONV_HALO + tile, n_ch), F32), pltpu.VMEM((tile + CONV_HALO, n_ch), F32)],
        compiler_params=_params(1),
    )(proj, proj, proj, proj, dy, dy, w, dproj)


FFN_HALO = 8


def _ffn_fwd(u, w, b, n_rows):
    kw = w.shape[0]
    half = u.shape[1] // 4
    tile = min(ROW_TILE, n_rows)
    per = tile // FFN_HALO
    offsets = [(k, FFN_HALO - (kw - 1) + k) for k in range(kw)]

    def body(u_ref, uh_ref, w_ref, b_ref, p_ref, buf):
        i = pl.program_id(1)
        buf[0:FFN_HALO, :] = jnp.where(i == 0, 0.0, uh_ref[...].astype(F32))
        buf[FFN_HALO:FFN_HALO + tile, :] = u_ref[...].astype(F32)
        conv = []
        for c0 in (0, half):
            cols = slice(c0, c0 + half)
            conv.append(b_ref[:, cols] + _tap_sum(lambda s, n: buf[pl.ds(s, n), cols], lambda k: w_ref[k:k + 1, cols],
                                                  offsets, tile))
        p_ref[...] = (conv[0] * jax.nn.silu(conv[1])).astype(p_ref.dtype)

    return pl.pallas_call(
        body, name="ffn_fwd", grid=(2, n_rows // tile),
        in_specs=[pl.BlockSpec((tile, 2 * half), lambda j, i: (i, j)),
                  pl.BlockSpec((FFN_HALO, 2 * half), lambda j, i: (jnp.maximum(i * per - 1, 0), j)),
                  pl.BlockSpec((kw, 2 * half), lambda j, i: (0, j)), pl.BlockSpec((1, 2 * half), lambda j, i: (0, j))],
        out_specs=pl.BlockSpec((tile, half), lambda j, i: (i, j)), out_shape=jax.ShapeDtypeStruct((n_rows, 2 * half), BF16),
        scratch_shapes=[pltpu.VMEM((FFN_HALO + tile, 2 * half), F32)], compiler_params=_params(2),
    )(u, u, w, b)


def _ffn_bwd(u, dp, w, b, n_rows):
    kw = w.shape[0]
    half = u.shape[1] // 4
    tile = min(ROW_TILE, n_rows)
    per = tile // FFN_HALO
    n_tiles = n_rows // tile
    last_halo = n_rows // FFN_HALO - 1
    ext = tile + FFN_HALO
    offsets = [(k, FFN_HALO - (kw - 1) + k) for k in range(kw)]
    back = [(k, kw - 1 - k) for k in range(kw)]

    def body(u_ref, up_ref, un_ref, dp_ref, dpn_ref, w_ref, b_ref, du_ref, dw_ref, db_ref, buf, dbuf):
        i = pl.program_id(1)
        buf[0:FFN_HALO, :] = jnp.where(i == 0, 0.0, up_ref[...].astype(F32))
        buf[FFN_HALO:FFN_HALO + tile, :] = u_ref[...].astype(F32)
        buf[FFN_HALO + tile:FFN_HALO + ext, :] = un_ref[...].astype(F32)
        conv = []
        for c0 in (0, half):
            cols = slice(c0, c0 + half)
            conv.append(b_ref[:, cols] + _tap_sum(lambda s, n: buf[pl.ds(s, n), cols], lambda k: w_ref[k:k + 1, cols],
                                                  offsets, ext))
        val, gate = conv
        dpe = jnp.concatenate([dp_ref[...], jnp.where(i == n_tiles - 1, 0.0, dpn_ref[...])], axis=0)
        sig = jax.nn.sigmoid(gate)
        dbuf[:, 0:half] = dpe * gate * sig
        dbuf[:, half:2 * half] = dpe * val * sig * (1.0 + gate * (1.0 - sig))

        @pl.when(i == 0)
        def _():
            dw_ref[...] = jnp.zeros_like(dw_ref)
            db_ref[...] = jnp.zeros_like(db_ref)

        for c0 in (0, half):
            cols = slice(c0, c0 + half)
            du_ref[:, cols] = _tap_sum(lambda s, n: dbuf[pl.ds(s, n), cols], lambda k: w_ref[k:k + 1, cols], back,
                                       tile).astype(du_ref.dtype)
            d_main = dbuf[0:tile, cols]
            db_ref[:, cols] += jnp.sum(d_main, axis=0, keepdims=True)
            grads = _tap_grads(lambda s, n: buf[pl.ds(s, n), cols], d_main, offsets, tile)
            for k in range(kw):
                dw_ref[k:k + 1, cols] += grads[k]

    wide = 2 * half
    return pl.pallas_call(
        body, name="ffn_bwd", grid=(2, n_tiles),
        in_specs=[pl.BlockSpec((tile, wide), lambda j, i: (i, j)),
                  pl.BlockSpec((FFN_HALO, wide), lambda j, i: (jnp.maximum(i * per - 1, 0), j)),
                  pl.BlockSpec((FFN_HALO, wide), lambda j, i: (jnp.minimum((i + 1) * per, last_halo), j)),
                  pl.BlockSpec((tile, half), lambda j, i: (i, j)),
                  pl.BlockSpec((FFN_HALO, half), lambda j, i: (jnp.minimum((i + 1) * per, last_halo), j)),
                  pl.BlockSpec((kw, wide), lambda j, i: (0, j)), pl.BlockSpec((1, wide), lambda j, i: (0, j))],
        out_specs=[pl.BlockSpec((tile, wide), lambda j, i: (i, j)), pl.BlockSpec((kw, wide), lambda j, i: (0, j)),
                   pl.BlockSpec((1, wide), lambda j, i: (0, j))],
        out_shape=[jax.ShapeDtypeStruct(u.shape, BF16), jax.ShapeDtypeStruct((kw, u.shape[1]), F32),
                   jax.ShapeDtypeStruct((1, u.shape[1]), F32)],
        scratch_shapes=[pltpu.VMEM((FFN_HALO + ext, wide), F32), pltpu.VMEM((ext, wide), F32)], compiler_params=_params(2),
    )(u, u, u, dp, dp, w, b)


def _retention_consts():
    log_gamma = jnp.log(1.0 - 2.0 ** (-5.0 - jnp.arange(N_HEADS, dtype=F32)))
    idx = jnp.arange(CHUNK, dtype=F32)
    rel = idx[:, None] - idx[None, :]
    decay = jnp.where(rel[None] >= 0, jnp.exp(log_gamma[:, None, None] * jnp.maximum(rel, 0.0)[None]), 0.0)
    zeta = jnp.exp(log_gamma[:, None] * (CHUNK - 1.0 - idx)[None])
    xi = jnp.exp(log_gamma[:, None] * (idx + 1.0)[None])
    chunk_decay = jnp.exp(log_gamma * CHUNK)
    xi_b = jnp.broadcast_to(xi[:, :, None], (N_HEADS, CHUNK, DK))
    zeta_b = jnp.broadcast_to(zeta[:, :, None], (N_HEADS, CHUNK, DK))
    cd_b = jnp.broadcast_to(chunk_decay[:, None, None], (N_HEADS, 8, DV))
    return decay, xi_b, zeta_b, cd_b


def _rope_tables(positions):
    half = DK // 2
    inv_freq = ROPE_BASE ** (-jnp.arange(half, dtype=F32) / half)
    ang = positions.astype(F32)[:, None] * inv_freq
    cos, sin = jnp.cos(ang), jnp.sin(ang)
    return jnp.concatenate([cos, cos], axis=-1), jnp.concatenate([-sin, sin], axis=-1)


def _swap_halves(v):
    return pltpu.roll(v, DK // 2, 1)


def _dot(a, b):
    return jnp.dot(a, b, preferred_element_type=F32)


def _dot_nt(a, b):
    return lax.dot_general(a, b, (((1,), (1,)), ((), ())), preferred_element_type=F32)


def _dot_tn(a, b):
    return lax.dot_general(a, b, (((0,), (0,)), ((), ())), preferred_element_type=F32)


def _const_specs():
    return [pl.BlockSpec((N_HEADS, CHUNK, CHUNK), lambda n: (0, 0, 0)), pl.BlockSpec((N_HEADS, CHUNK, DK), lambda n: (0, 0, 0)),
            pl.BlockSpec((N_HEADS, CHUNK, DK), lambda n: (0, 0, 0)), pl.BlockSpec((N_HEADS, 8, DV), lambda n: (0, 0, 0))]


def _retention_fwd(proj, cos_t, sin_t, consts, n_rows):
    n_chunks = n_rows // CHUNK
    scale = DK ** -0.5

    def body(q_ref, k_ref, v_ref, cf_ref, ss_ref, d_ref, xi_ref, zt_ref, cd_ref, r_ref, st_ref, state):
        @pl.when(pl.program_id(0) == 0)
        def _():
            state[...] = jnp.zeros_like(state)

        cf, ss = cf_ref[...], ss_ref[...]
        for h in range(N_HEADS):
            qh = q_ref[:, h * DK:(h + 1) * DK].astype(F32)
            kh = k_ref[:, h * DK:(h + 1) * DK].astype(F32)
            qh = qh * cf + _swap_halves(qh) * ss
            kh = (kh * cf + _swap_halves(kh) * ss) * scale
            vh = v_ref[:, h * DV:(h + 1) * DV].astype(BF16)
            st = state[h]
            st_ref[0, h] = st
            sd = _dot_nt(qh.astype(BF16), kh.astype(BF16)) * d_ref[h]
            inner = _dot(sd.astype(BF16), vh)
            cross = _dot((qh * xi_ref[h]).astype(BF16), st.astype(BF16))
            kv = _dot_tn((kh * zt_ref[h]).astype(BF16), vh)
            state[h] = st * cd_ref[h, 0:1, :] + kv
            r_ref[:, h * DV:(h + 1) * DV] = inner + cross

    qk = N_HEADS * DK
    vw = N_HEADS * DV
    return pl.pallas_call(
        body, name="retention_fwd", grid=(n_chunks,),
        in_specs=[pl.BlockSpec((CHUNK, qk), lambda n: (n, 0)), pl.BlockSpec((CHUNK, qk), lambda n: (n, 1)),
                  pl.BlockSpec((CHUNK, vw), lambda n: (n, 1)), pl.BlockSpec((CHUNK, DK), lambda n: (n, 0)),
                  pl.BlockSpec((CHUNK, DK), lambda n: (n, 0))] + _const_specs(),
        out_specs=[pl.BlockSpec((CHUNK, vw), lambda n: (n, 0)), pl.BlockSpec((1, N_HEADS, DK, DV), lambda n: (n, 0, 0, 0))],
        out_shape=[jax.ShapeDtypeStruct((n_rows, vw), F32), jax.ShapeDtypeStruct((n_chunks, N_HEADS, DK, DV), F32)],
        scratch_shapes=[pltpu.VMEM((N_HEADS, DK, DV), F32)], compiler_params=_params(1),
    )(proj, proj, proj, cos_t, sin_t, *consts)


def _retention_bwd(proj, cos_t, sin_t, states, dr, consts, dproj, n_rows):
    n_chunks = n_rows // CHUNK
    scale = DK ** -0.5
    qk = N_HEADS * DK
    vw = N_HEADS * DV

    def body(q_ref, k_ref, v_ref, cf_ref, ss_ref, st_ref, dr_ref, d_ref, xi_ref, zt_ref, cd_ref, dp_any, dqkv_ref, g_ref):
        dq_ref, dk_ref, dv_ref = dqkv_ref.at[:, 0:qk], dqkv_ref.at[:, qk:2 * qk], dqkv_ref.at[:, 2 * qk:2 * qk + vw]

        @pl.when(pl.program_id(0) == 0)
        def _():
            g_ref[...] = jnp.zeros_like(g_ref)

        cf, ss = cf_ref[...], ss_ref[...]
        for h in range(N_HEADS):
            qh = q_ref[:, h * DK:(h + 1) * DK].astype(F32)
            kh = k_ref[:, h * DK:(h + 1) * DK].astype(F32)
            qh = qh * cf + _swap_halves(qh) * ss
            kh = (kh * cf + _swap_halves(kh) * ss) * scale
            qb, kb = qh.astype(BF16), kh.astype(BF16)
            vh = v_ref[:, h * DV:(h + 1) * DV].astype(BF16)
            do = dr_ref[:, h * DV:(h + 1) * DV].astype(BF16)
            rb = st_ref[0, h].astype(BF16)
            g = g_ref[h]
            gb = g.astype(BF16)
            dec, xi, zt = d_ref[h], xi_ref[h], zt_ref[h]
            sd = (_dot_nt(qb, kb) * dec).astype(BF16)
            ds = (_dot_nt(do, vh) * dec).astype(BF16)
            dqh = _dot(ds, kb) + _dot_nt(do, rb) * xi
            dkh = (_dot_tn(ds, qb) + _dot_nt(vh, gb) * zt) * scale
            dvh = _dot_tn(sd, do) + _dot((kh * zt).astype(BF16), gb)
            g_ref[h] = g * cd_ref[h, 0:1, :] + _dot_tn((qh * xi).astype(BF16), do)
            dq_ref[:, h * DK:(h + 1) * DK] = (dqh * cf + _swap_halves(dqh * ss)).astype(dq_ref.dtype)
            dk_ref[:, h * DK:(h + 1) * DK] = (dkh * cf + _swap_halves(dkh * ss)).astype(dk_ref.dtype)
            dv_ref[:, h * DV:(h + 1) * DV] = dvh.astype(dv_ref.dtype)

    last = n_chunks - 1
    return pl.pallas_call(
        body, name="retention_bwd", grid=(n_chunks,),
        in_specs=[pl.BlockSpec((CHUNK, qk), lambda n: (last - n, 0)), pl.BlockSpec((CHUNK, qk), lambda n: (last - n, 1)),
                  pl.BlockSpec((CHUNK, vw), lambda n: (last - n, 1)), pl.BlockSpec((CHUNK, DK), lambda n: (last - n, 0)),
                  pl.BlockSpec((CHUNK, DK), lambda n: (last - n, 0)),
                  pl.BlockSpec((1, N_HEADS, DK, DV), lambda n: (last - n, 0, 0, 0)),
                  pl.BlockSpec((CHUNK, vw), lambda n: (last - n, 0))] + _const_specs() + [pl.BlockSpec(memory_space=pl.ANY)],
        out_specs=pl.BlockSpec((CHUNK, 2 * qk + vw), lambda n: (last - n, 0)),
        out_shape=jax.ShapeDtypeStruct(dproj.shape, dproj.dtype), input_output_aliases={11: 0},
        scratch_shapes=[pltpu.VMEM((N_HEADS, DK, DV), F32)], compiler_params=_params(1),
    )(proj, proj, proj, cos_t, sin_t, states, dr, *consts, dproj)


def _ln(v):
    mu = jnp.mean(v, axis=-1, keepdims=True)
    var = jnp.mean(jnp.square(v - mu), axis=-1, keepdims=True)
    return (v - mu) * lax.rsqrt(var + LN_EPS)


def _f_modulate(x, scale, shift):
    return _ln(x) * (1.0 + scale) + shift


def _f_conv_norm(a1, g, b):
    return jax.nn.silu(_ln(a1) * g + b)


def _f_group_norm_gate(r, gate, g, b):
    return (_ln(r) * g + b) * jax.nn.silu(gate)


def _f_merge(ga, gb, ya, yb):
    return jax.nn.sigmoid(ga) * ya + jax.nn.sigmoid(gb) * yb


def _f_post1(x, t, gate1, g1, b1, scale2, shift2):
    x1 = _ln(ALPHA * x + gate1 * t) * g1 + b1
    return x1, _ln(x1) * (1.0 + scale2) + shift2


def _f_loss(x1, f, gate2, g2, b2, target):
    y = _ln(ALPHA * x1 + gate2 * f) * g2 + b2
    return 0.5 * jnp.sum(jnp.mean(jnp.square(y - target), axis=-1))


ANY = pl.BlockSpec(memory_space=pl.ANY)


def _allgather8(name, blk):
    r, c_ = blk.shape

    def body(x_ref, out_ref, send_sems, recv_sems, local_sem):
        x, y, c = lax.axis_index("x"), lax.axis_index("y"), lax.axis_index("c")
        me, sibling = (x, y, c), (x, y, 1 - c)
        chips = [(1 - x, y), (x, 1 - y), (1 - x, 1 - y)]

        def slot(px, py, pc):
            return out_ref.at[4 * px + 2 * py + pc]

        def copy(k, block, to, src=None):
            return pltpu.make_async_remote_copy(
                src_ref=slot(*block) if src is None else src, dst_ref=slot(*block), send_sem=send_sems.at[k],
                recv_sem=recv_sems.at[k], device_id=to, device_id_type=MESH)

        mine = pltpu.make_async_copy(x_ref, slot(*me), local_sem)
        mine.start()
        first = [copy(0, me, sibling, src=x_ref)]
        first += [copy(1 + j, me, (*chip, c), src=x_ref) for j, chip in enumerate(chips)]
        for cp in first:
            cp.start()
        passed = [copy(4 + j, (*chip, c), sibling) for j, chip in enumerate(chips)]
        for j, chip in enumerate(chips):
            copy(1 + j, (*chip, c), me).wait_recv()
            passed[j].start()
        copy(0, sibling, me).wait_recv()
        for j, chip in enumerate(chips):
            copy(4 + j, (*chip, 1 - c), me).wait_recv()
        for cp in first + passed:
            cp.wait_send()
        mine.wait()

    return pl.pallas_call(
        body, name=name, in_specs=[ANY], out_specs=ANY, out_shape=jax.ShapeDtypeStruct((N_DEV, r, c_), blk.dtype),
        scratch_shapes=[pltpu.SemaphoreType.DMA((7,)), pltpu.SemaphoreType.DMA((7,)), pltpu.SemaphoreType.DMA],
    )(blk)


def _sibling_swap(name, g_full, half_rows):
    n, _, c_ = g_full.shape

    def body(g_ref, out_ref, send_sem, recv_sem):
        x, y, c = lax.axis_index("x"), lax.axis_index("y"), lax.axis_index("c")
        cp = pltpu.make_async_remote_copy(
            src_ref=g_ref.at[:, pl.ds((1 - c) * half_rows, half_rows), :], dst_ref=out_ref, send_sem=send_sem,
            recv_sem=recv_sem, device_id=(x, y, 1 - c), device_id_type=MESH)
        cp.start()
        cp.wait_recv()
        cp.wait_send()

    return pl.pallas_call(
        body, name=name, in_specs=[ANY], out_specs=ANY, out_shape=jax.ShapeDtypeStruct((n, half_rows, c_), g_full.dtype),
        scratch_shapes=[pltpu.SemaphoreType.DMA, pltpu.SemaphoreType.DMA],
    )(g_full)


def _chip_alltoall(name, p):
    def body(p_ref, out_ref, send_sems, recv_sems, local_sem):
        x, y, c = lax.axis_index("x"), lax.axis_index("y"), lax.axis_index("c")
        mine = 2 * x + y
        own = pltpu.make_async_copy(p_ref.at[mine], out_ref.at[mine], local_sem)
        own.start()
        copies = []
        for k, (px, py) in enumerate([(1 - x, y), (x, 1 - y), (1 - x, 1 - y)]):
            cp = pltpu.make_async_remote_copy(
                src_ref=p_ref.at[2 * px + py], dst_ref=out_ref.at[mine], send_sem=send_sems.at[k], recv_sem=recv_sems.at[k],
                device_id=(px, py, c), device_id_type=MESH)
            cp.start()
            copies.append(cp)
        for cp in copies:
            cp.wait_recv()
        for cp in copies:
            cp.wait_send()
        own.wait()

    return pl.pallas_call(
        body, name=name, in_specs=[ANY], out_specs=ANY, out_shape=jax.ShapeDtypeStruct(p.shape, p.dtype),
        scratch_shapes=[pltpu.SemaphoreType.DMA((3,)), pltpu.SemaphoreType.DMA((3,)), pltpu.SemaphoreType.DMA],
    )(p)


def _sibling_gather(name, half):
    r, c_ = half.shape

    def body(h_ref, out_ref, send_sem, recv_sem, local_sem):
        x, y, c = lax.axis_index("x"), lax.axis_index("y"), lax.axis_index("c")
        own = pltpu.make_async_copy(h_ref, out_ref.at[c], local_sem)
        own.start()
        cp = pltpu.make_async_remote_copy(
            src_ref=h_ref, dst_ref=out_ref.at[c], send_sem=send_sem, recv_sem=recv_sem, device_id=(x, y, 1 - c),
            device_id_type=MESH)
        cp.start()
        cp.wait_recv()
        cp.wait_send()
        own.wait()

    return pl.pallas_call(
        body, name=name, in_specs=[ANY], out_specs=ANY, out_shape=jax.ShapeDtypeStruct((2, r, c_), half.dtype),
        scratch_shapes=[pltpu.SemaphoreType.DMA, pltpu.SemaphoreType.DMA, pltpu.SemaphoreType.DMA],
    )(half)


def _sum_arrays(name, arrays):
    r, c_ = arrays[0].shape

    def add_all(*vals):
        acc = vals[0]
        for v in vals[1:]:
            acc = acc + v
        return (acc,)

    return _rowwise(name, add_all, [(a, c_, 0) for a in arrays], [], [(c_, F32)], [], r)[0]


def _adamw_fn(w, g, m, v):
    m = ADAM_B1 * m + (1.0 - ADAM_B1) * g
    v = ADAM_B2 * v + (1.0 - ADAM_B2) * jnp.square(g)
    m_hat = m / (1.0 - ADAM_B1 ** ADAM_STEP)
    v_hat = v / (1.0 - ADAM_B2 ** ADAM_STEP)
    delta = -ADAM_LR * (m_hat / (jnp.sqrt(v_hat) + ADAM_EPS) + ADAM_WD * w)
    return delta, m, v


def _adamw(name, w, g, m, v):
    r, c_ = w.shape
    return _rowwise(name, _adamw_fn, [(w, c_, 0), (g, c_, 0), (m, c_, 0), (v, c_, 0)], [], [(c_, F32)] * 3, [], r)


def _ada_fwd(c_all, w_ada, b_ada):
    n = w_ada.shape[1]

    def body(c_ref, w_ref, b_ref, o_ref):
        o_ref[...] = jnp.dot(jax.nn.silu(c_ref[...]).astype(BF16), w_ref[...].astype(BF16),
                             preferred_element_type=F32) + b_ref[...]

    return pl.pallas_call(body, name="ada_fwd", out_shape=jax.ShapeDtypeStruct((N_DEV, n), F32),
                          compiler_params=pltpu.CompilerParams(vmem_limit_bytes=VMEM_LIMIT))(c_all, w_ada, b_ada)


def _ada_bwd(c_all, dmod_cols):
    d = c_all.shape[1]
    n = dmod_cols.shape[1]

    def body(c_ref, dm_ref, gw_ref):
        gw_ref[...] = lax.dot_general(jax.nn.silu(c_ref[...]).astype(BF16), dm_ref[...].astype(BF16),
                                      (((0,), (0,)), ((), ())), preferred_element_type=F32)

    return pl.pallas_call(body, name="ada_bwd", out_shape=jax.ShapeDtypeStruct((d, n), F32),
                          compiler_params=pltpu.CompilerParams(vmem_limit_bytes=VMEM_LIMIT))(c_all, dmod_cols)


def _cast_bf16(name, a):
    r, c_ = a.shape
    return _rowwise(name, lambda v: (v,), [(a, c_, 0)], [], [(c_, BF16)], [], r)[0]


def _pad_rows(vec, mult):
    n = vec.shape[0]
    return jnp.pad(vec, (0, (-n) % mult))


def kernel(x, c, positions, w_ada, b_ada, w_in, b_in, conv_dw_w, conv_dw_b, conv_ln_g, conv_ln_b, w_conv_out, ret_gn_g, ret_gn_b, w_ret_out, w_out, ln1_g, ln1_b, w_up, ffn_dw_w, ffn_dw_b, w_down, ln2_g, ln2_b, loss_target, m_w_ada, m_b_ada, m_w_in, m_b_in, m_conv_dw_w, m_conv_dw_b, m_conv_ln_g, m_conv_ln_b, m_w_conv_out, m_ret_gn_g, m_ret_gn_b, m_w_ret_out, m_w_out, m_ln1_g, m_ln1_b, m_w_up, m_ffn_dw_w, m_ffn_dw_b, m_w_down, m_ln2_g, m_ln2_b, v_w_ada, v_b_ada, v_w_in, v_b_in, v_conv_dw_w, v_conv_dw_b, v_conv_ln_g, v_conv_ln_b, v_w_conv_out, v_ret_gn_g, v_ret_gn_b, v_w_ret_out, v_w_out, v_ln1_g, v_ln1_b, v_w_up, v_ffn_dw_w, v_ffn_dw_b, v_w_down, v_ln2_g, v_ln2_b):
    given = dict(locals())
    n_rows = x.shape[1]
    d = D_MODEL
    my_c = lax.axis_index("c")
    chip = 2 * lax.axis_index("x") + lax.axis_index("y")
    dev = 2 * chip + my_c
    xr = x[0]
    target = loss_target[0]
    vw = N_HEADS * DV
    ffw = 2 * D_FF

    def pack(prefix):
        return jnp.concatenate([given[prefix + n][0].reshape(r, d) for n, r in PACK_ROWS], axis=0)

    def unpack(packed):
        out, o = {}, 0
        for n, r in PACK_ROWS:
            out[n] = packed[..., o:o + r, :]
            o += r
        return out

    def flat_rows(arrays, mult=8):
        v = jnp.concatenate([a.reshape(-1) for a in arrays])
        return _pad_rows(v, mult * d).reshape(-1, d)

    def unflatten(flat2d, shapes):
        v, out, o = flat2d.reshape(-1), [], 0
        for shp in shapes:
            size = 1
            for e in shp:
                size *= e
            out.append(v[o:o + size].reshape(shp))
            o += size
        return out

    w_pack = pack("")
    w_bf = _cast_bf16("cast_weights", w_pack)
    half = PACK_TOTAL // 2
    w_all = _allgather8("gather_weights", lax.dynamic_slice_in_dim(w_bf, my_c * half, half, 0))
    w_all = unpack(w_all.reshape(N_CHIPS, PACK_TOTAL, d))
    wg_in = w_all["w_in"].reshape(N_CHIPS, d, w_in.shape[2])
    wg_up = w_all["w_up"].reshape(N_CHIPS, d, w_up.shape[2])
    wg_conv_out = w_all["w_conv_out"].reshape(1, d, d)
    wg_ret_out = w_all["w_ret_out"].reshape(1, vw, d)
    wg_out = w_all["w_out"].reshape(1, d, d)
    wg_down = w_all["w_down"].reshape(1, D_FF, d)

    kc, kf = conv_dw_w.shape[2], ffn_dw_w.shape[2]
    small_all = _allgather8("gather_small", flat_rows([c, conv_dw_w, ffn_dw_w])).reshape(N_DEV, -1)
    c_all = small_all[:, :d]
    per_chip = small_all[0::2]
    conv_w = per_chip[:, d:d + CONV_K * kc].reshape(N_CHIPS, CONV_K, kc).transpose(1, 0, 2).reshape(CONV_K, N_CHIPS * kc)
    o_f = d + CONV_K * kc
    ffn_w = per_chip[:, o_f:o_f + FFN_K * kf].reshape(N_CHIPS, FFN_K, kf).transpose(1, 0, 2).reshape(FFN_K, N_CHIPS * kf)

    n_ada = w_ada.shape[2]
    b_ada_cols = lax.dynamic_slice_in_dim(b_ada, chip * n_ada, n_ada, 1)
    mod_cols = _ada_fwd(c_all, w_ada[0], b_ada_cols)
    mod_all = _allgather8("gather_mod", mod_cols)
    mod = lax.dynamic_index_in_dim(mod_all[0::2], dev, 1, keepdims=False).reshape(1, N_CHIPS * n_ada)
    shift1, scale1, gate1, shift2, scale2, gate2 = [mod[:, k * d:(k + 1) * d] for k in range(6)]

    h1 = _rowwise("ln_mod1", lambda a, s, t: (_f_modulate(a, s, t),), [(xr, d, 0)], [(scale1, d, 0), (shift1, d, 0)],
                  [(d, BF16)], [], n_rows)[0]
    proj = _mm_nn("mm_in", h1, wg_in, b_in, F32)
    a1 = _glu_conv_fwd(proj, 6, 7, conv_w, conv_dw_b, n_rows)
    a2 = _rowwise("conv_norm", lambda a, g, b: (_f_conv_norm(a, g, b),), [(a1, d, 0)], [(conv_ln_g, d, 0), (conv_ln_b, d, 0)],
                  [(d, BF16)], [], n_rows)[0]
    y_a = _mm_nn("mm_conv_out", a2, wg_conv_out, None, F32)
    cos_t, sin_t = _rope_tables(positions[0])
    consts = _retention_consts()
    r, states = _retention_fwd(proj, cos_t, sin_t, consts, n_rows)
    ret_rows = [(r, DV, 0), (proj, DV, 4096 // DV)]
    ret_vecs = [(ret_gn_g, DV, 0), (ret_gn_b, DV, 0)]
    r2 = _rowwise("ret_norm", lambda a, gt, g, b: (_f_group_norm_gate(a, gt, g, b),), ret_rows, ret_vecs, [(DV, BF16)], [],
                  n_rows, ncol=N_HEADS)[0]
    y_b = _mm_nn("mm_ret_out", r2, wg_ret_out, None, F32)
    merge_rows = [(proj, d, 8), (proj, d, 9), (y_a, d, 0), (y_b, d, 0)]
    m = _rowwise("merge", lambda *a: (_f_merge(*a),), merge_rows, [], [(d, BF16)], [], n_rows)[0]
    t = _mm_nn("mm_out", m, wg_out, None, F32)
    post1_vecs = [(gate1, d, 0), (ln1_g, d, 0), (ln1_b, d, 0), (scale2, d, 0), (shift2, d, 0)]
    x1, h2 = _rowwise("post1", _f_post1, [(xr, d, 0), (t, d, 0)], post1_vecs, [(d, F32), (d, BF16)], [], n_rows)
    pair_up = lambda c: (c % 2) * 2 + c // 2
    paired = lambda a: a.reshape(a.shape[0], N_CHIPS, kf)[:, jnp.array([0, 2, 1, 3])].reshape(a.shape[0], ffw)
    ffn_w_p, ffn_b_p = paired(ffn_w), paired(ffn_dw_b)
    u = _mm_nn("mm_up", h2, wg_up, None, BF16, gmap=pair_up)
    p = _ffn_fwd(u, ffn_w_p, ffn_b_p, n_rows)
    f = _mm_nn("mm_down", p, wg_down, None, F32)

    def loss_rows(x1v, fv, tv, g2v, lg, lb):
        loss, pull = jax.vjp(lambda a, b, c_, e, h: _f_loss(a, b, c_, e, h, tv), x1v, fv, g2v, lg, lb)
        return (*pull(jnp.ones((), F32)), jnp.full((1, 128), loss, F32))

    dx1_a, df, dgate2, dln2_g, dln2_b, loss_v = _rowwise(
        "loss", loss_rows, [(x1, d, 0), (f, d, 0), (target, d, 0)], [(gate2, d, 0), (ln2_g, d, 0), (ln2_b, d, 0)],
        [(d, F32), (d, BF16)], [d, d, d, 128], n_rows)
    loss = lax.psum(loss_v[0, 0], ("x", "y", "c"))

    dp = _mm_nt("mm_down_dx", df, wg_down, F32)
    gw_down = _mm_tn("mm_down_dw", p, df, 1)
    du, g_ffn_w, g_ffn_b = _ffn_bwd(u, dp, ffn_w_p, ffn_b_p, n_rows)
    g_ffn_w, g_ffn_b = paired(g_ffn_w), paired(g_ffn_b)
    dh2 = _mm_nt("mm_up_dx", du, wg_up, F32, gmap=pair_up)
    gw_up = _mm_tn("mm_up_dw", h2, du, N_CHIPS, gmap=pair_up)

    dx_a, dt, dgate1, dln1_g, dln1_b, dscale2, dshift2 = _rowwise(
        "post1_bwd", _vjp_rows(_f_post1, 2, 2), [(xr, d, 0), (t, d, 0), (dx1_a, d, 0), (dh2, d, 0)], post1_vecs,
        [(d, F32), (d, BF16)], [d] * 5, n_rows)
    dm = _mm_nt("mm_out_dx", dt, wg_out, F32)
    gw_out = _mm_tn("mm_out_dw", m, dt, 1)
    merge_vjp = _vjp_rows(_f_merge, 4, 1)

    def merge_bwd(*a):
        dga, dgb, dya, dyb = merge_vjp(*a)
        return jnp.concatenate([dga, dgb], axis=1), dya, dyb

    dproj, dya, dyb = _rowwise("merge_bwd", merge_bwd, merge_rows + [(dm, d, 0)], [],
                               [(2 * d, BF16, proj.shape[1], 8192 // (2 * d)), (d, BF16), (d, BF16)], [], n_rows)

    da2 = _mm_nt("mm_conv_out_dx", dya, wg_conv_out, F32)
    gw_conv_out = _mm_tn("mm_conv_out_dw", a2, dya, 1)
    da1, dcl_g, dcl_b = _rowwise("conv_norm_bwd", _vjp_rows(_f_conv_norm, 1, 1), [(a1, d, 0), (da2, d, 0)],
                                 [(conv_ln_g, d, 0), (conv_ln_b, d, 0)], [(d, F32)], [d, d], n_rows)
    dproj, g_conv_w, g_conv_b = _glu_conv_bwd(proj, 6, 7, da1, conv_w, dproj, 6144 // (2 * d), n_rows)

    dr2 = _mm_nt("mm_ret_out_dx", dyb, wg_ret_out, F32)
    gw_ret_out = _mm_tn("mm_ret_out_dw", r2, dyb, 1)
    dr, dproj, dgn_g, dgn_b = _rowwise(
        "ret_norm_bwd", _vjp_rows(_f_group_norm_gate, 2, 1), ret_rows + [(dr2, DV, 0)], ret_vecs,
        [(DV, F32), (DV, BF16, proj.shape[1], 4096 // DV)], [DV, DV], n_rows, ncol=N_HEADS, into=(dproj, 1))
    dproj = _retention_bwd(proj, cos_t, sin_t, states, dr, consts, dproj, n_rows)

    dh1 = _mm_nt("mm_in_dx", dproj, wg_in, F32)
    gw_in, gb_in = _mm_tn("mm_in_dw", h1, dproj, N_CHIPS, colsum=True)
    mod_bwd = _vjp_rows(_f_modulate, 1, 1)

    def mod1_bwd(xv, dhv, dxav, sv, tv):
        dx, ds, dsh = mod_bwd(xv, dhv, sv, tv)
        return dx + dxav, ds, dsh

    grad_x, dscale1, dshift1 = _rowwise("ln_mod1_bwd", mod1_bwd, [(xr, d, 0), (dh1, d, 0), (dx_a, d, 0)],
                                        [(scale1, d, 0), (shift1, d, 0)], [(d, F32)], [d, d], n_rows)

    dmod = jnp.concatenate([dshift1, dscale1, dgate1, dshift2, dscale2, dgate2], axis=1)
    small_names = ["b_in", "conv_dw_w", "conv_dw_b", "conv_ln_g", "conv_ln_b", "ret_gn_g", "ret_gn_b", "ln1_g", "ln1_b",
                   "ffn_dw_w", "ffn_dw_b", "ln2_g", "ln2_b"]
    small_parts = [gb_in, g_conv_w, g_conv_b, dcl_g, dcl_b, dgn_g, dgn_b, dln1_g, dln1_b, g_ffn_w, g_ffn_b, dln2_g, dln2_b, dmod]
    small_shapes = [a.shape for a in small_parts]
    parts_all = _allgather8("gather_small_grads", flat_rows(small_parts))
    summed = _sum_arrays("sum_small_grads", [parts_all[k] for k in range(N_DEV)])
    small_sum = unflatten(summed, small_shapes)
    grads = dict(zip(small_names, small_sum[:-1]))
    grads["b_ada"] = small_sum[-1]
    grads["conv_dw_w"] = lax.dynamic_slice_in_dim(grads["conv_dw_w"], chip * kc, kc, 1)
    grads["ffn_dw_w"] = lax.dynamic_slice_in_dim(grads["ffn_dw_w"], chip * kf, kf, 1)
    o_mod = sum(a.size for a in small_parts[:-1])
    dmod_all = parts_all.reshape(N_DEV, -1)[:, o_mod:o_mod + dmod.shape[1]]
    grads["w_ada"] = _ada_bwd(c_all, lax.dynamic_slice_in_dim(dmod_all, chip * n_ada, n_ada, 1))

    g_pack = jnp.concatenate([gw_in.reshape(N_CHIPS, -1, d), gw_up.reshape(N_CHIPS, -1, d), gw_conv_out.reshape(N_CHIPS, -1, d),
                              gw_ret_out.reshape(N_CHIPS, -1, d), gw_out.reshape(N_CHIPS, -1, d),
                              gw_down.reshape(N_CHIPS, -1, d)], axis=1)
    theirs = _sibling_swap("grad_swap", g_pack, half)
    mine = lax.dynamic_slice_in_dim(g_pack, my_c * half, half, 1)
    pair = _sum_arrays("grad_pair_sum", [mine.reshape(N_CHIPS * half, d), theirs.reshape(N_CHIPS * half, d)])
    arrived = _chip_alltoall("grad_alltoall", pair.reshape(N_CHIPS, half, d))
    reduced = _sum_arrays("grad_chip_sum", [arrived[k] for k in range(N_CHIPS)])
    g_shard = _sibling_gather("grad_gather", reduced).reshape(PACK_TOTAL, d)

    delta_p, m_p, v_p = _adamw("adamw_big", w_pack, g_shard, pack("m_"), pack("v_"))
    outs = {}
    for prefix, packed in (("grad_", g_shard), ("delta_", delta_p), ("new_m_", m_p), ("new_v_", v_p)):
        for n, rows in unpack(packed).items():
            outs[prefix + n] = rows.reshape(given[n].shape)
    ada = _adamw("adamw_ada", w_ada[0], grads["w_ada"], m_w_ada[0], v_w_ada[0])
    for prefix, val in zip(("grad_", "delta_", "new_m_", "new_v_"), (grads["w_ada"], *ada)):
        outs[prefix + "w_ada"] = val.reshape(w_ada.shape)
    small_all_names = ["b_ada"] + small_names
    small_w_shapes = [given[n].shape for n in small_all_names]
    g_small = flat_rows([grads[n] for n in small_all_names])
    small_upd = _adamw("adamw_small", flat_rows([given[n] for n in small_all_names]), g_small,
                       flat_rows([given["m_" + n] for n in small_all_names]), flat_rows([given["v_" + n] for n in small_all_names]))
    for prefix, packed in zip(("grad_", "delta_", "new_m_", "new_v_"), (g_small, *small_upd)):
        for n, val in zip(small_all_names, unflatten(packed, small_w_shapes)):
            outs[prefix + n] = val

    weights = ["w_ada", "b_ada", "w_in", "b_in", "conv_dw_w", "conv_dw_b", "conv_ln_g", "conv_ln_b", "w_conv_out", "ret_gn_g",
               "ret_gn_b", "w_ret_out", "w_out", "ln1_g", "ln1_b", "w_up", "ffn_dw_w", "ffn_dw_b", "w_down", "ln2_g", "ln2_b"]
    result = [loss, grad_x.reshape(x.shape)]
    for prefix in ("grad_", "delta_", "new_m_", "new_v_"):
        result += [outs[prefix + n] for n in weights]
    return tuple(result)
```

```python
import jax
import jax.numpy as jnp
from jax import lax
from jax.experimental import pallas as pl
from jax.experimental.pallas import tpu as pltpu

F32 = jnp.float32
BF16 = jnp.bfloat16
MESH = pl.DeviceIdType.MESH

D_MODEL = 1024
N_HEADS = 8
DK = 128
DV = 256
CHUNK = 128
ROPE_BASE = 10000.0
D_FF = 2816
CONV_K = 31
FFN_K = 3
LN_EPS = 1e-5
ALPHA = (2.0 * 1) ** 0.25
ADAM_LR = 0.001
ADAM_B1 = 0.9
ADAM_B2 = 0.999
ADAM_EPS = 1e-08
ADAM_WD = 0.01
ADAM_STEP = 10

V7X_VMEM_BYTES = 64 * 1024 * 1024
VMEM_LIMIT = V7X_VMEM_BYTES - 8 * 1024 * 1024
ROW_TILE = 256
MM_TILE = 512
N_CHIPS = 4
N_DEV = 8

BIG_WEIGHTS = ("w_in", "w_up", "w_conv_out", "w_ret_out", "w_out", "w_down")


def _params(n_grid):
    return pltpu.CompilerParams(dimension_semantics=("arbitrary",) * n_grid, vmem_limit_bytes=VMEM_LIMIT)


def _rowwise(name, fn, rows, vecs, outs, reds, n_rows, tile=ROW_TILE, ncol=1, with_col=False, into=None):
    tile = _fit_tile(n_rows, tile)
    n_in = len(rows) + len(vecs)
    n_ref_in = n_in + (into is not None)
    n_out = len(outs)
    outs = [o if len(o) == 4 else (o[0], o[1], o[0] * ncol, 0) for o in outs]

    def col_map(off, row, first_row=0):
        def index(j, i):
            return (i + first_row // tile if row else 0, off(j) if callable(off) else off + j)
        return index

    def body(*refs):
        i = pl.program_id(1)
        vals = [r[...].astype(F32) for r in refs[:n_in]]
        res = fn(pl.program_id(0), *vals) if with_col else fn(*vals)
        for k in range(n_out):
            refs[n_ref_in + k][...] = res[k].astype(refs[n_ref_in + k].dtype)
        for k in range(len(reds)):
            o = refs[n_ref_in + n_out + k]

            @pl.when(i == 0)
            def _():
                o[...] = jnp.zeros_like(o)

            o[...] += res[n_out + k]

    in_specs = [pl.BlockSpec((tile, e[1]), col_map(e[2], True, e[3] if len(e) > 3 else 0)) for e in rows]
    in_specs += [pl.BlockSpec((1, w), col_map(off, False)) for _, w, off in vecs]
    args = [e[0] for e in rows] + [a for a, _, _ in vecs]
    aliases = {}
    if into is not None:
        in_specs.append(pl.BlockSpec(memory_space=pl.ANY))
        args.append(into[0])
        aliases = {n_in: into[1]}
    out_specs = [pl.BlockSpec((tile, w), col_map(off, True)) for w, _, _, off in outs]
    out_specs += [pl.BlockSpec((1, w), lambda j, i: (0, j)) for w in reds]
    out_shape = [jax.ShapeDtypeStruct((n_rows, total), dt) for _, dt, total, _ in outs]
    out_shape += [jax.ShapeDtypeStruct((1, w * ncol), F32) for w in reds]
    return pl.pallas_call(
        body, name=name, grid=(ncol, n_rows // tile), in_specs=in_specs, out_specs=out_specs, out_shape=out_shape,
        input_output_aliases=aliases, compiler_params=_params(2),
    )(*args)


def _vjp_rows(fn, n_row_in, n_ct):
    def bwd(*args):
        prim = [a.astype(F32) for a in args[:n_row_in] + args[n_row_in + n_ct:]]
        cts = tuple(a.astype(F32) for a in args[n_row_in:n_row_in + n_ct])
        _, pull = jax.vjp(fn, *prim)
        return pull(cts if n_ct > 1 else cts[0])

    return bwd


def _fit_tile(n, pref):
    if n <= pref:
        return n
    t = pref - pref % 16
    while n % t:
        t -= 16
    return t


def _col_tile(n):
    return n if n <= 1536 else n // 2


def _same_group(c):
    return c


def _mm_nn(name, a, w, bias, out_dtype, gmap=_same_group):
    s, k = a.shape
    g, _, n = w.shape
    tm, tn = min(MM_TILE, s), _col_tile(n)
    nt = n // tn

    def body(*refs):
        a_ref, w_ref = refs[0], refs[1]
        o_ref = refs[-1]
        acc = jnp.dot(a_ref[...].astype(BF16), w_ref[...], preferred_element_type=F32)
        if bias is not None:
            acc = acc + refs[2][...]
        o_ref[...] = acc.astype(o_ref.dtype)

    in_specs = [pl.BlockSpec((tm, k), lambda c, i: (i, 0)),
                pl.BlockSpec((None, k, tn), lambda c, i: (gmap(c // nt), 0, c % nt))]
    args = [a, w]
    if bias is not None:
        in_specs.append(pl.BlockSpec((1, tn), lambda c, i: (0, c)))
        args.append(bias)
    return pl.pallas_call(
        body, name=name, grid=(g * nt, s // tm), in_specs=in_specs, out_specs=pl.BlockSpec((tm, tn), lambda c, i: (i, c)),
        out_shape=jax.ShapeDtypeStruct((s, g * n), out_dtype), compiler_params=_params(2),
    )(*args)


def _mm_nt(name, dy, w, out_dtype, gmap=_same_group):
    s = dy.shape[0]
    g, k, n = w.shape
    tm, tn = min(MM_TILE, s), n
    nt = 1
    steps = g

    def body(dy_ref, w_ref, o_ref, acc_ref):
        r = pl.program_id(1)
        part = lax.dot_general(dy_ref[...].astype(BF16), w_ref[...], (((1,), (1,)), ((), ())), preferred_element_type=F32)

        @pl.when(r == 0)
        def _():
            acc_ref[...] = part

        @pl.when(r > 0)
        def _():
            acc_ref[...] += part

        @pl.when(r == steps - 1)
        def _():
            o_ref[...] = acc_ref[...].astype(o_ref.dtype)

    return pl.pallas_call(
        body, name=name, grid=(s // tm, steps),
        in_specs=[pl.BlockSpec((tm, tn), lambda i, r: (i, r)),
                  pl.BlockSpec((None, k, tn), lambda i, r: (gmap(r // nt), 0, r % nt))],
        out_specs=pl.BlockSpec((tm, k), lambda i, r: (i, 0)), out_shape=jax.ShapeDtypeStruct((s, k), out_dtype),
        scratch_shapes=[pltpu.VMEM((tm, k), F32)], compiler_params=_params(2),
    )(dy, w)


def _mm_tn(name, a, dy, g, gmap=_same_group, colsum=False):
    s, k = a.shape
    n = dy.shape[1] // g
    ts = min(2048 if k <= 1024 else 1024, s)
    tn = n if k * n <= 1024 * 1536 else (n // 2 if (n // 2) % 128 == 0 else n)
    nt = n // tn

    def body(a_ref, dy_ref, o_ref, *sum_ref):
        t = pl.program_id(1)
        dyv = dy_ref[...].astype(BF16)
        part = lax.dot_general(a_ref[...].astype(BF16), dyv, (((0,), (0,)), ((), ())), preferred_element_type=F32)

        @pl.when(t == 0)
        def _():
            o_ref[...] = part

        @pl.when(t > 0)
        def _():
            o_ref[...] += part

        if colsum:
            col = jnp.sum(dyv.astype(F32), axis=0, keepdims=True)

            @pl.when(t == 0)
            def _():
                sum_ref[0][...] = col

            @pl.when(t > 0)
            def _():
                sum_ref[0][...] += col

    out_specs = [pl.BlockSpec((None, k, tn), lambda c, t: (gmap(c // nt), 0, c % nt))]
    out_shape = [jax.ShapeDtypeStruct((g, k, n), F32)]
    if colsum:
        out_specs.append(pl.BlockSpec((1, tn), lambda c, t: (0, c)))
        out_shape.append(jax.ShapeDtypeStruct((1, g * n), F32))
    res = pl.pallas_call(
        body, name=name, grid=(g * nt, s // ts),
        in_specs=[pl.BlockSpec((ts, k), lambda c, t: (t, 0)), pl.BlockSpec((ts, tn), lambda c, t: (t, c))],
        out_specs=out_specs, out_shape=out_shape, compiler_params=_params(2),
    )(a, dy)
    return res if colsum else res[0]


SUBLANES = 8


def _tap_sum(read, w_row, offsets, tile):
    acc = None
    for b in range(SUBLANES):
        group = [(k, o) for k, o in offsets if o % SUBLANES == b]
        if not group:
            continue
        rows = tile if b == 0 else tile + SUBLANES
        z = None
        for k, o in group:
            term = w_row(k) * read(o - b, rows)
            z = term if z is None else z + term
        part = z if b == 0 else z[b:b + tile]
        acc = part if acc is None else acc + part
    return acc


def _tap_grads(read, dy, offsets, tile):
    padded = jnp.concatenate([dy, jnp.zeros((SUBLANES, dy.shape[1]), dy.dtype)], axis=0)
    out = {}
    for b in range(SUBLANES):
        group = [(k, o) for k, o in offsets if o % SUBLANES == b]
        if not group:
            continue
        shifted = dy if b == 0 else pltpu.roll(padded, b, 0)
        rows = tile if b == 0 else tile + SUBLANES
        for k, o in group:
            out[k] = jnp.sum(shifted * read(o - b, rows), axis=0, keepdims=True)
    return out


CONV_HALO = 32
CONV_TILE = 256
CONV_LANES = 256


def _glu_conv_fwd(proj, col_v, col_g, w, b, n_rows):
    kw, n_ch = w.shape
    tile = min(CONV_TILE, n_rows)
    per = tile // CONV_HALO
    offsets = [(k, CONV_HALO - (kw - 1) + k) for k in range(kw)]

    def body(v_ref, g_ref, vh_ref, gh_ref, w_ref, b_ref, o_ref, buf):
        i = pl.program_id(0)
        prev = vh_ref[...].astype(F32) * jax.nn.sigmoid(gh_ref[...].astype(F32))
        buf[0:CONV_HALO, :] = jnp.where(i == 0, 0.0, prev)
        buf[CONV_HALO:CONV_HALO + tile, :] = v_ref[...].astype(F32) * jax.nn.sigmoid(g_ref[...].astype(F32))
        for c0 in range(0, n_ch, CONV_LANES):
            cols = slice(c0, c0 + CONV_LANES)
            o_ref[:, cols] = b_ref[:, cols] + _tap_sum(lambda s, n: buf[pl.ds(s, n), cols], lambda k: w_ref[k:k + 1, cols],
                                                      offsets, tile)

    main = lambda col: pl.BlockSpec((tile, n_ch), lambda i: (i, col))
    halo = lambda col: pl.BlockSpec((CONV_HALO, n_ch), lambda i: (jnp.maximum(i * per - 1, 0), col))
    return pl.pallas_call(
        body, name="conv_fwd", grid=(n_rows // tile,),
        in_specs=[main(col_v), main(col_g), halo(col_v), halo(col_g), pl.BlockSpec((kw, n_ch), lambda i: (0, 0)),
                  pl.BlockSpec((1, n_ch), lambda i: (0, 0))],
        out_specs=pl.BlockSpec((tile, n_ch), lambda i: (i, 0)), out_shape=jax.ShapeDtypeStruct((n_rows, n_ch), F32),
        scratch_shapes=[pltpu.VMEM((CONV_HALO + tile, n_ch), F32)], compiler_params=_params(1),
    )(proj, proj, proj, proj, w, b)


def _glu_conv_bwd(proj, col_v, col_g, dy, w, dproj, col_out, n_rows):
    kw, n_ch = w.shape
    tile = min(CONV_TILE, n_rows)
    per = tile // CONV_HALO
    n_tiles = n_rows // tile
    last_halo = n_rows // CONV_HALO - 1
    offsets = [(k, CONV_HALO - (kw - 1) + k) for k in range(kw)]
    back = [(k, kw - 1 - k) for k in range(kw)]

    def body(v_ref, g_ref, vh_ref, gh_ref, dy_ref, dyn_ref, w_ref, dp_any, dx_ref, dw_ref, db_ref, buf, dbuf):
        i = pl.program_id(0)
        cv, cg = v_ref[...].astype(F32), g_ref[...].astype(F32)
        sig = jax.nn.sigmoid(cg)
        prev = vh_ref[...].astype(F32) * jax.nn.sigmoid(gh_ref[...].astype(F32))
        buf[0:CONV_HALO, :] = jnp.where(i == 0, 0.0, prev)
        buf[CONV_HALO:CONV_HALO + tile, :] = cv * sig
        dbuf[0:tile, :] = dy_ref[...]
        dbuf[tile:tile + CONV_HALO, :] = jnp.where(i == n_tiles - 1, 0.0, dyn_ref[...])

        @pl.when(i == 0)
        def _():
            dw_ref[...] = jnp.zeros_like(dw_ref)
            db_ref[...] = jnp.zeros_like(db_ref)

        db_ref[...] += jnp.sum(dy_ref[...], axis=0, keepdims=True)
        for c0 in range(0, n_ch, CONV_LANES):
            cols = slice(c0, c0 + CONV_LANES)
            w_row = lambda k: w_ref[k:k + 1, cols]
            dx = _tap_sum(lambda s, n: dbuf[pl.ds(s, n), cols], w_row, back, tile)
            grads = _tap_grads(lambda s, n: buf[pl.ds(s, n), cols], dy_ref[:, cols], offsets, tile)
            for k in range(kw):
                dw_ref[k:k + 1, cols] += grads[k]
            sg = sig[:, cols]
            dx_ref[:, c0:c0 + CONV_LANES] = (dx * sg).astype(dx_ref.dtype)
            dx_ref[:, n_ch + c0:n_ch + c0 + CONV_LANES] = (dx * cv[:, cols] * sg * (1.0 - sg)).astype(dx_ref.dtype)

    main = lambda col: pl.BlockSpec((tile, n_ch), lambda i: (i, col))
    halo = lambda col: pl.BlockSpec((CONV_HALO, n_ch), lambda i: (jnp.maximum(i * per - 1, 0), col))
    return pl.pallas_call(
        body, name="conv_bwd", grid=(n_tiles,),
        in_specs=[main(col_v), main(col_g), halo(col_v), halo(col_g), main(0),
                  pl.BlockSpec((CONV_HALO, n_ch), lambda i: (jnp.minimum((i + 1) * per, last_halo), 0)),
                  pl.BlockSpec((kw, n_ch), lambda i: (0, 0)), pl.BlockSpec(memory_space=pl.ANY)],
        out_specs=[pl.BlockSpec((tile, 2 * n_ch), lambda i: (i, col_out)), pl.BlockSpec((kw, n_ch), lambda i: (0, 0)),
                   pl.BlockSpec((1, n_ch), lambda i: (0, 0))],
        out_shape=[jax.ShapeDtypeStruct(dproj.shape, dproj.dtype), jax.ShapeDtypeStruct((kw, n_ch), F32),
                   jax.ShapeDtypeStruct((1, n_ch), F32)],
        input_output_aliases={7: 0},
        scratch_shapes=[pltpu.VMEM((CONV_HALO + tile, n_ch), F32), pltpu.VMEM((tile + CONV_HALO, n_ch), F32)],
        compiler_params=_params(1),
    )(proj, proj, proj, proj, dy, dy, w, dproj)


FFN_HALO = 8


def _ffn_fwd(u, w, b, n_rows):
    kw = w.shape[0]
    half = u.shape[1] // 4
    tile = min(ROW_TILE, n_rows)
    per = tile // FFN_HALO
    offsets = [(k, FFN_HALO - (kw - 1) + k) for k in range(kw)]

    def body(u_ref, uh_ref, w_ref, b_ref, p_ref, buf):
        i = pl.program_id(1)
        buf[0:FFN_HALO, :] = jnp.where(i == 0, 0.0, uh_ref[...].astype(F32))
        buf[FFN_HALO:FFN_HALO + tile, :] = u_ref[...].astype(F32)
        conv = []
        for c0 in (0, half):
            cols = slice(c0, c0 + half)
            conv.append(b_ref[:, cols] + _tap_sum(lambda s, n: buf[pl.ds(s, n), cols], lambda k: w_ref[k:k + 1, cols],
                                                  offsets, tile))
        p_ref[...] = (conv[0] * jax.nn.silu(conv[1])).astype(p_ref.dtype)

    return pl.pallas_call(
        body, name="ffn_fwd", grid=(2, n_rows // tile),
        in_specs=[pl.BlockSpec((tile, 2 * half), lambda j, i: (i, j)),
                  pl.BlockSpec((FFN_HALO, 2 * half), lambda j, i: (jnp.maximum(i * per - 1, 0), j)),
                  pl.BlockSpec((kw, 2 * half), lambda j, i: (0, j)), pl.BlockSpec((1, 2 * half), lambda j, i: (0, j))],
        out_specs=pl.BlockSpec((tile, half), lambda j, i: (i, j)), out_shape=jax.ShapeDtypeStruct((n_rows, 2 * half), BF16),
        scratch_shapes=[pltpu.VMEM((FFN_HALO + tile, 2 * half), F32)], compiler_params=_params(2),
    )(u, u, w, b)


def _ffn_bwd(u, dp, w, b, n_rows):
    kw = w.shape[0]
    half = u.shape[1] // 4
    tile = min(ROW_TILE, n_rows)
    per = tile // FFN_HALO
    n_tiles = n_rows // tile
    last_halo = n_rows // FFN_HALO - 1
    ext = tile + FFN_HALO
    offsets = [(k, FFN_HALO - (kw - 1) + k) for k in range(kw)]
    back = [(k, kw - 1 - k) for k in range(kw)]

    def body(u_ref, up_ref, un_ref, dp_ref, dpn_ref, w_ref, b_ref, du_ref, dw_ref, db_ref, buf, dbuf):
        i = pl.program_id(1)
        buf[0:FFN_HALO, :] = jnp.where(i == 0, 0.0, up_ref[...].astype(F32))
        buf[FFN_HALO:FFN_HALO + tile, :] = u_ref[...].astype(F32)
        buf[FFN_HALO + tile:FFN_HALO + ext, :] = un_ref[...].astype(F32)
        conv = []
        for c0 in (0, half):
            cols = slice(c0, c0 + half)
            conv.append(b_ref[:, cols] + _tap_sum(lambda s, n: buf[pl.ds(s, n), cols], lambda k: w_ref[k:k + 1, cols],
                                                  offsets, ext))
        val, gate = conv
        dpe = jnp.concatenate([dp_ref[...], jnp.where(i == n_tiles - 1, 0.0, dpn_ref[...])], axis=0)
        sig = jax.nn.sigmoid(gate)
        dbuf[:, 0:half] = dpe * gate * sig
        dbuf[:, half:2 * half] = dpe * val * sig * (1.0 + gate * (1.0 - sig))

        @pl.when(i == 0)
        def _():
            dw_ref[...] = jnp.zeros_like(dw_ref)
            db_ref[...] = jnp.zeros_like(db_ref)

        for c0 in (0, half):
            cols = slice(c0, c0 + half)
            du_ref[:, cols] = _tap_sum(lambda s, n: dbuf[pl.ds(s, n), cols], lambda k: w_ref[k:k + 1, cols], back,
                                       tile).astype(du_ref.dtype)
            d_main = dbuf[0:tile, cols]
            db_ref[:, cols] += jnp.sum(d_main, axis=0, keepdims=True)
            grads = _tap_grads(lambda s, n: buf[pl.ds(s, n), cols], d_main, offsets, tile)
            for k in range(kw):
                dw_ref[k:k + 1, cols] += grads[k]

    wide = 2 * half
    return pl.pallas_call(
        body, name="ffn_bwd", grid=(2, n_tiles),
        in_specs=[pl.BlockSpec((tile, wide), lambda j, i: (i, j)),
                  pl.BlockSpec((FFN_HALO, wide), lambda j, i: (jnp.maximum(i * per - 1, 0), j)),
                  pl.BlockSpec((FFN_HALO, wide), lambda j, i: (jnp.minimum((i + 1) * per, last_halo), j)),
                  pl.BlockSpec((tile, half), lambda j, i: (i, j)),
                  pl.BlockSpec((FFN_HALO, half), lambda j, i: (jnp.minimum((i + 1) * per, last_halo), j)),
                  pl.BlockSpec((kw, wide), lambda j, i: (0, j)), pl.BlockSpec((1, wide), lambda j, i: (0, j))],
        out_specs=[pl.BlockSpec((tile, wide), lambda j, i: (i, j)), pl.BlockSpec((kw, wide), lambda j, i: (0, j)),
                   pl.BlockSpec((1, wide), lambda j, i: (0, j))],
        out_shape=[jax.ShapeDtypeStruct(u.shape, BF16), jax.ShapeDtypeStruct((kw, u.shape[1]), F32),
                   jax.ShapeDtypeStruct((1, u.shape[1]), F32)],
        scratch_shapes=[pltpu.VMEM((FFN_HALO + ext, wide), F32), pltpu.VMEM((ext, wide), F32)], compiler_params=_params(2),
    )(u, u, u, dp, dp, w, b)


def _retention_consts():
    log_gamma = jnp.log(1.0 - 2.0 ** (-5.0 - jnp.arange(N_HEADS, dtype=F32)))
    idx = jnp.arange(CHUNK, dtype=F32)
    rel = idx[:, None] - idx[None, :]
    decay = jnp.where(rel[None] >= 0, jnp.exp(log_gamma[:, None, None] * jnp.maximum(rel, 0.0)[None]), 0.0)
    zeta = jnp.exp(log_gamma[:, None] * (CHUNK - 1.0 - idx)[None])
    xi = jnp.exp(log_gamma[:, None] * (idx + 1.0)[None])
    chunk_decay = jnp.exp(log_gamma * CHUNK)
    xi_b = jnp.broadcast_to(xi[:, :, None], (N_HEADS, CHUNK, DK))
    zeta_b = jnp.broadcast_to(zeta[:, :, None], (N_HEADS, CHUNK, DK))
    cd_b = jnp.broadcast_to(chunk_decay[:, None, None], (N_HEADS, 8, DV))
    return decay, xi_b, zeta_b, cd_b


def _rope_tables(positions):
    half = DK // 2
    inv_freq = ROPE_BASE ** (-jnp.arange(half, dtype=F32) / half)
    ang = positions.astype(F32)[:, None] * inv_freq
    cos, sin = jnp.cos(ang), jnp.sin(ang)
    return jnp.concatenate([cos, cos], axis=-1), jnp.concatenate([-sin, sin], axis=-1)


def _swap_halves(v):
    return pltpu.roll(v, DK // 2, 1)


def _dot(a, b):
    return jnp.dot(a, b, preferred_element_type=F32)


def _dot_nt(a, b):
    return lax.dot_general(a, b, (((1,), (1,)), ((), ())), preferred_element_type=F32)


def _dot_tn(a, b):
    return lax.dot_general(a, b, (((0,), (0,)), ((), ())), preferred_element_type=F32)


def _const_specs():
    return [pl.BlockSpec((N_HEADS, CHUNK, CHUNK), lambda n: (0, 0, 0)), pl.BlockSpec((N_HEADS, CHUNK, DK), lambda n: (0, 0, 0)),
            pl.BlockSpec((N_HEADS, CHUNK, DK), lambda n: (0, 0, 0)), pl.BlockSpec((N_HEADS, 8, DV), lambda n: (0, 0, 0))]


def _retention_fwd(proj, cos_t, sin_t, consts, n_rows):
    n_chunks = n_rows // CHUNK
    scale = DK ** -0.5

    def body(q_ref, k_ref, v_ref, cf_ref, ss_ref, d_ref, xi_ref, zt_ref, cd_ref, r_ref, st_ref, state):
        @pl.when(pl.program_id(0) == 0)
        def _():
            state[...] = jnp.zeros_like(state)

        cf, ss = cf_ref[...], ss_ref[...]
        for h in range(N_HEADS):
            qh = q_ref[:, h * DK:(h + 1) * DK].astype(F32)
            kh = k_ref[:, h * DK:(h + 1) * DK].astype(F32)
            qh = qh * cf + _swap_halves(qh) * ss
            kh = (kh * cf + _swap_halves(kh) * ss) * scale
            vh = v_ref[:, h * DV:(h + 1) * DV].astype(BF16)
            st = state[h]
            st_ref[0, h] = st
            sd = _dot_nt(qh.astype(BF16), kh.astype(BF16)) * d_ref[h]
            inner = _dot(sd.astype(BF16), vh)
            cross = _dot((qh * xi_ref[h]).astype(BF16), st.astype(BF16))
            kv = _dot_tn((kh * zt_ref[h]).astype(BF16), vh)
            state[h] = st * cd_ref[h, 0:1, :] + kv
            r_ref[:, h * DV:(h + 1) * DV] = inner + cross

    qk = N_HEADS * DK
    vw = N_HEADS * DV
    return pl.pallas_call(
        body, name="retention_fwd", grid=(n_chunks,),
        in_specs=[pl.BlockSpec((CHUNK, qk), lambda n: (n, 0)), pl.BlockSpec((CHUNK, qk), lambda n: (n, 1)),
                  pl.BlockSpec((CHUNK, vw), lambda n: (n, 1)), pl.BlockSpec((CHUNK, DK), lambda n: (n, 0)),
                  pl.BlockSpec((CHUNK, DK), lambda n: (n, 0))] + _const_specs(),
        out_specs=[pl.BlockSpec((CHUNK, vw), lambda n: (n, 0)), pl.BlockSpec((1, N_HEADS, DK, DV), lambda n: (n, 0, 0, 0))],
        out_shape=[jax.ShapeDtypeStruct((n_rows, vw), F32), jax.ShapeDtypeStruct((n_chunks, N_HEADS, DK, DV), F32)],
        scratch_shapes=[pltpu.VMEM((N_HEADS, DK, DV), F32)], compiler_params=_params(1),
    )(proj, proj, proj, cos_t, sin_t, *consts)


def _retention_bwd(proj, cos_t, sin_t, states, dr, consts, dproj, n_rows):
    n_chunks = n_rows // CHUNK
    scale = DK ** -0.5
    qk = N_HEADS * DK
    vw = N_HEADS * DV

    def body(q_ref, k_ref, v_ref, cf_ref, ss_ref, st_ref, dr_ref, d_ref, xi_ref, zt_ref, cd_ref, dp_any, dqkv_ref, g_ref):
        dq_ref, dk_ref, dv_ref = dqkv_ref.at[:, 0:qk], dqkv_ref.at[:, qk:2 * qk], dqkv_ref.at[:, 2 * qk:2 * qk + vw]

        @pl.when(pl.program_id(0) == 0)
        def _():
            g_ref[...] = jnp.zeros_like(g_ref)

        cf, ss = cf_ref[...], ss_ref[...]
        for h in range(N_HEADS):
            qh = q_ref[:, h * DK:(h + 1) * DK].astype(F32)
            kh = k_ref[:, h * DK:(h + 1) * DK].astype(F32)
            qh = qh * cf + _swap_halves(qh) * ss
            kh = (kh * cf + _swap_halves(kh) * ss) * scale
            qb, kb = qh.astype(BF16), kh.astype(BF16)
            vh = v_ref[:, h * DV:(h + 1) * DV].astype(BF16)
            do = dr_ref[:, h * DV:(h + 1) * DV].astype(BF16)
            rb = st_ref[0, h].astype(BF16)
            g = g_ref[h]
            gb = g.astype(BF16)
            dec, xi, zt = d_ref[h], xi_ref[h], zt_ref[h]
            sd = (_dot_nt(qb, kb) * dec).astype(BF16)
            ds = (_dot_nt(do, vh) * dec).astype(BF16)
            dqh = _dot(ds, kb) + _dot_nt(do, rb) * xi
            dkh = (_dot_tn(ds, qb) + _dot_nt(vh, gb) * zt) * scale
            dvh = _dot_tn(sd, do) + _dot((kh * zt).astype(BF16), gb)
            g_ref[h] = g * cd_ref[h, 0:1, :] + _dot_tn((qh * xi).astype(BF16), do)
            dq_ref[:, h * DK:(h + 1) * DK] = (dqh * cf + _swap_halves(dqh * ss)).astype(dq_ref.dtype)
            dk_ref[:, h * DK:(h + 1) * DK] = (dkh * cf + _swap_halves(dkh * ss)).astype(dk_ref.dtype)
            dv_ref[:, h * DV:(h + 1) * DV] = dvh.astype(dv_ref.dtype)

    last = n_chunks - 1
    return pl.pallas_call(
        body, name="retention_bwd", grid=(n_chunks,),
        in_specs=[pl.BlockSpec((CHUNK, qk), lambda n: (last - n, 0)), pl.BlockSpec((CHUNK, qk), lambda n: (last - n, 1)),
                  pl.BlockSpec((CHUNK, vw), lambda n: (last - n, 1)), pl.BlockSpec((CHUNK, DK), lambda n: (last - n, 0)),
                  pl.BlockSpec((CHUNK, DK), lambda n: (last - n, 0)),
                  pl.BlockSpec((1, N_HEADS, DK, DV), lambda n: (last - n, 0, 0, 0)),
                  pl.BlockSpec((CHUNK, vw), lambda n: (last - n, 0))] + _const_specs() + [pl.BlockSpec(memory_space=pl.ANY)],
        out_specs=pl.BlockSpec((CHUNK, 2 * qk + vw), lambda n: (last - n, 0)),
        out_shape=jax.ShapeDtypeStruct(dproj.shape, dproj.dtype), input_output_aliases={11: 0},
        scratch_shapes=[pltpu.VMEM((N_HEADS, DK, DV), F32)], compiler_params=_params(1),
    )(proj, proj, proj, cos_t, sin_t, states, dr, *consts, dproj)


def _ln(v):
    mu = jnp.mean(v, axis=-1, keepdims=True)
    var = jnp.mean(jnp.square(v - mu), axis=-1, keepdims=True)
    return (v - mu) * lax.rsqrt(var + LN_EPS)


def _f_modulate(x, scale, shift):
    return _ln(x) * (1.0 + scale) + shift


def _f_conv_norm(a1, g, b):
    return jax.nn.silu(_ln(a1) * g + b)


def _f_group_norm_gate(r, gate, g, b):
    return (_ln(r) * g + b) * jax.nn.silu(gate)


def _per_head(fn):
    def run(*arrays):
        parts = [fn(*[a[:, h * DV:(h + 1) * DV] for a in arrays]) for h in range(N_HEADS)]
        if not isinstance(parts[0], (tuple, list)):
            return (jnp.concatenate(parts, axis=1),)
        return tuple(jnp.concatenate([p[k] for p in parts], axis=1) for k in range(len(parts[0])))

    return run


def _f_merge(ga, gb, ya, yb):
    return jax.nn.sigmoid(ga) * ya + jax.nn.sigmoid(gb) * yb


def _f_post1(x, t, gate1, g1, b1, scale2, shift2):
    x1 = _ln(ALPHA * x + gate1 * t) * g1 + b1
    return x1, _ln(x1) * (1.0 + scale2) + shift2


def _f_loss(x1, f, gate2, g2, b2, target):
    y = _ln(ALPHA * x1 + gate2 * f) * g2 + b2
    return 0.5 * jnp.sum(jnp.mean(jnp.square(y - target), axis=-1))


ANY = pl.BlockSpec(memory_space=pl.ANY)


def _allgather8(name, blocks, own_half=False):
    n = len(blocks)
    rows = [b.shape[0] // 2 if own_half else b.shape[0] for b in blocks]

    def body(*refs):
        x_refs, out_refs = refs[:n], refs[n:2 * n]
        send_sems, recv_sems, local_sems = refs[2 * n:]
        x, y, c = lax.axis_index("x"), lax.axis_index("y"), lax.axis_index("c")
        me, sibling = (x, y, c), (x, y, 1 - c)
        chips = [(1 - x, y), (x, 1 - y), (1 - x, 1 - y)]
        every = range(n)

        def src(a):
            return x_refs[a].at[pl.ds(c * rows[a], rows[a])] if own_half else x_refs[a]

        def slot(a, px, py, pc):
            return out_refs[a].at[4 * px + 2 * py + pc]

        def copy(k, a, block, to, from_input=False):
            return pltpu.make_async_remote_copy(
                src_ref=src(a) if from_input else slot(a, *block), dst_ref=slot(a, *block), send_sem=send_sems.at[k, a],
                recv_sem=recv_sems.at[k, a], device_id=to, device_id_type=MESH)

        mine = [pltpu.make_async_copy(src(a), slot(a, *me), local_sems.at[a]) for a in every]
        for cp in mine:
            cp.start()
        first = [copy(0, a, me, sibling, True) for a in every]
        first += [copy(1 + j, a, me, (*chip, c), True) for j, chip in enumerate(chips) for a in every]
        for cp in first:
            cp.start()
        passed = [[copy(4 + j, a, (*chip, c), sibling) for a in every] for j, chip in enumerate(chips)]
        for j, chip in enumerate(chips):
            for a in every:
                copy(1 + j, a, (*chip, c), me).wait_recv()
            for cp in passed[j]:
                cp.start()
        for a in every:
            copy(0, a, sibling, me).wait_recv()
        for j, chip in enumerate(chips):
            for a in every:
                copy(4 + j, a, (*chip, 1 - c), me).wait_recv()
        for cp in first + [cp for group in passed for cp in group]:
            cp.wait_send()
        for cp in mine:
            cp.wait()

    return pl.pallas_call(
        body, name=name, in_specs=[ANY] * n, out_specs=[ANY] * n,
        out_shape=[jax.ShapeDtypeStruct((N_DEV, r, b.shape[1]), b.dtype) for r, b in zip(rows, blocks)],
        scratch_shapes=[pltpu.SemaphoreType.DMA((7, n)), pltpu.SemaphoreType.DMA((7, n)), pltpu.SemaphoreType.DMA((n,))],
    )(*blocks)


def _sibling_swap(name, arrays):
    n = len(arrays)
    halves = [a.shape[1] // 2 for a in arrays]

    def body(*refs):
        g_refs, theirs, mine = refs[:n], refs[n:2 * n], refs[2 * n:3 * n]
        send_sems, recv_sems, local_sems = refs[3 * n:]
        x, y, c = lax.axis_index("x"), lax.axis_index("y"), lax.axis_index("c")
        remote, local = [], []
        for a in range(n):
            h = halves[a]
            local.append(pltpu.make_async_copy(g_refs[a].at[:, pl.ds(c * h, h), :], mine[a], local_sems.at[a]))
            remote.append(pltpu.make_async_remote_copy(
                src_ref=g_refs[a].at[:, pl.ds((1 - c) * h, h), :], dst_ref=theirs[a], send_sem=send_sems.at[a],
                recv_sem=recv_sems.at[a], device_id=(x, y, 1 - c), device_id_type=MESH))
        for cp in remote + local:
            cp.start()
        for cp in remote:
            cp.wait_recv()
        for cp in remote:
            cp.wait_send()
        for cp in local:
            cp.wait()

    half_shapes = [jax.ShapeDtypeStruct((a.shape[0], h, a.shape[2]), a.dtype) for a, h in zip(arrays, halves)]
    res = pl.pallas_call(
        body, name=name, in_specs=[ANY] * n, out_specs=[ANY] * (2 * n), out_shape=half_shapes + half_shapes,
        scratch_shapes=[pltpu.SemaphoreType.DMA((n,)), pltpu.SemaphoreType.DMA((n,)), pltpu.SemaphoreType.DMA((n,))],
    )(*arrays)
    return res[:n], res[n:]


def _chip_alltoall(name, arrays):
    n = len(arrays)

    def body(*refs):
        p_refs, out_refs = refs[:n], refs[n:2 * n]
        send_sems, recv_sems, local_sems = refs[2 * n:]
        x, y, c = lax.axis_index("x"), lax.axis_index("y"), lax.axis_index("c")
        mine = 2 * x + y
        own = [pltpu.make_async_copy(p_refs[a].at[mine], out_refs[a].at[mine], local_sems.at[a]) for a in range(n)]
        copies = []
        for k, (px, py) in enumerate([(1 - x, y), (x, 1 - y), (1 - x, 1 - y)]):
            for a in range(n):
                copies.append(pltpu.make_async_remote_copy(
                    src_ref=p_refs[a].at[2 * px + py], dst_ref=out_refs[a].at[mine], send_sem=send_sems.at[k, a],
                    recv_sem=recv_sems.at[k, a], device_id=(px, py, c), device_id_type=MESH))
        for cp in own + copies:
            cp.start()
        for cp in copies:
            cp.wait_recv()
        for cp in copies:
            cp.wait_send()
        for cp in own:
            cp.wait()

    return pl.pallas_call(
        body, name=name, in_specs=[ANY] * n, out_specs=[ANY] * n,
        out_shape=[jax.ShapeDtypeStruct(a.shape, a.dtype) for a in arrays],
        scratch_shapes=[pltpu.SemaphoreType.DMA((3, n)), pltpu.SemaphoreType.DMA((3, n)), pltpu.SemaphoreType.DMA((n,))],
    )(*arrays)


def _sibling_gather(name, halves):
    n = len(halves)

    def body(*refs):
        h_refs, out_refs = refs[:n], refs[n:2 * n]
        send_sems, recv_sems, local_sems = refs[2 * n:]
        x, y, c = lax.axis_index("x"), lax.axis_index("y"), lax.axis_index("c")
        own = [pltpu.make_async_copy(h_refs[a], out_refs[a].at[c], local_sems.at[a]) for a in range(n)]
        remote = [pltpu.make_async_remote_copy(
            src_ref=h_refs[a], dst_ref=out_refs[a].at[c], send_sem=send_sems.at[a], recv_sem=recv_sems.at[a],
            device_id=(x, y, 1 - c), device_id_type=MESH) for a in range(n)]
        for cp in own + remote:
            cp.start()
        for cp in remote:
            cp.wait_recv()
        for cp in remote:
            cp.wait_send()
        for cp in own:
            cp.wait()

    return pl.pallas_call(
        body, name=name, in_specs=[ANY] * n, out_specs=[ANY] * n,
        out_shape=[jax.ShapeDtypeStruct((2,) + h.shape, h.dtype) for h in halves],
        scratch_shapes=[pltpu.SemaphoreType.DMA((n,)), pltpu.SemaphoreType.DMA((n,)), pltpu.SemaphoreType.DMA((n,))],
    )(*halves)


def _sum_arrays(name, terms, n_rows):
    c_ = terms[0][0].shape[1]

    def add_all(*vals):
        acc = vals[0]
        for v in vals[1:]:
            acc = acc + v
        return (acc,)

    return _rowwise(name, add_all, [(a, c_, 0, first) for a, first in terms], [], [(c_, F32)], [], n_rows, tile=128)[0]


def _adamw_fn(w, g, m, v):
    m = ADAM_B1 * m + (1.0 - ADAM_B1) * g
    v = ADAM_B2 * v + (1.0 - ADAM_B2) * jnp.square(g)
    m_hat = m / (1.0 - ADAM_B1 ** ADAM_STEP)
    v_hat = v / (1.0 - ADAM_B2 ** ADAM_STEP)
    delta = -ADAM_LR * (m_hat / (jnp.sqrt(v_hat) + ADAM_EPS) + ADAM_WD * w)
    return delta, m, v


def _adamw(name, w, g, m, v):
    r, c_ = w.shape
    return _rowwise(name, _adamw_fn, [(w, c_, 0), (g, c_, 0), (m, c_, 0), (v, c_, 0)], [], [(c_, F32)] * 3, [], r, tile=128)


def _ada_fwd(c_all, w_ada, b_ada):
    n = w_ada.shape[1]

    def body(c_ref, w_ref, b_ref, o_ref):
        o_ref[...] = jnp.dot(jax.nn.silu(c_ref[...]).astype(BF16), w_ref[...].astype(BF16),
                             preferred_element_type=F32) + b_ref[...]

    return pl.pallas_call(body, name="ada_fwd", out_shape=jax.ShapeDtypeStruct((N_DEV, n), F32),
                          compiler_params=pltpu.CompilerParams(vmem_limit_bytes=VMEM_LIMIT))(c_all, w_ada, b_ada)


def _ada_bwd(c_all, dmod_cols):
    d = c_all.shape[1]
    n = dmod_cols.shape[1]

    def body(c_ref, dm_ref, gw_ref):
        gw_ref[...] = lax.dot_general(jax.nn.silu(c_ref[...]).astype(BF16), dm_ref[...].astype(BF16),
                                      (((0,), (0,)), ((), ())), preferred_element_type=F32)

    return pl.pallas_call(body, name="ada_bwd", out_shape=jax.ShapeDtypeStruct((d, n), F32),
                          compiler_params=pltpu.CompilerParams(vmem_limit_bytes=VMEM_LIMIT))(c_all, dmod_cols)


def _cast_bf16(name, a):
    r, c_ = a.shape
    return _rowwise(name, lambda v: (v,), [(a, c_, 0)], [], [(c_, BF16)], [], r)[0]


def _pad_rows(vec, mult):
    n = vec.shape[0]
    return jnp.pad(vec, (0, (-n) % mult))


def kernel(x, c, positions, w_ada, b_ada, w_in, b_in, conv_dw_w, conv_dw_b, conv_ln_g, conv_ln_b, w_conv_out, ret_gn_g, ret_gn_b, w_ret_out, w_out, ln1_g, ln1_b, w_up, ffn_dw_w, ffn_dw_b, w_down, ln2_g, ln2_b, loss_target, m_w_ada, m_b_ada, m_w_in, m_b_in, m_conv_dw_w, m_conv_dw_b, m_conv_ln_g, m_conv_ln_b, m_w_conv_out, m_ret_gn_g, m_ret_gn_b, m_w_ret_out, m_w_out, m_ln1_g, m_ln1_b, m_w_up, m_ffn_dw_w, m_ffn_dw_b, m_w_down, m_ln2_g, m_ln2_b, v_w_ada, v_b_ada, v_w_in, v_b_in, v_conv_dw_w, v_conv_dw_b, v_conv_ln_g, v_conv_ln_b, v_w_conv_out, v_ret_gn_g, v_ret_gn_b, v_w_ret_out, v_w_out, v_ln1_g, v_ln1_b, v_w_up, v_ffn_dw_w, v_ffn_dw_b, v_w_down, v_ln2_g, v_ln2_b):
    given = dict(locals())
    n_rows = x.shape[1]
    d = D_MODEL
    my_c = lax.axis_index("c")
    chip = 2 * lax.axis_index("x") + lax.axis_index("y")
    dev = 2 * chip + my_c
    xr = x[0]
    target = loss_target[0]
    vw = N_HEADS * DV
    ffw = 2 * D_FF

    def flat_rows(arrays, mult=8):
        v = jnp.concatenate([a.reshape(-1) for a in arrays])
        return _pad_rows(v, mult * d).reshape(-1, d)

    def unflatten(flat2d, shapes):
        v, out, o = flat2d.reshape(-1), [], 0
        for shp in shapes:
            size = 1
            for e in shp:
                size *= e
            out.append(v[o:o + size].reshape(shp))
            o += size
        return out

    w_bf = [_cast_bf16("cast_" + n, given[n][0]) for n in BIG_WEIGHTS]
    w_all = dict(zip(BIG_WEIGHTS, _allgather8("gather_weights", w_bf, own_half=True)))
    wg_in = w_all["w_in"].reshape(N_CHIPS, d, w_in.shape[2])
    wg_up = w_all["w_up"].reshape(N_CHIPS, d, w_up.shape[2])
    wg_conv_out = w_all["w_conv_out"].reshape(1, d, d)
    wg_ret_out = w_all["w_ret_out"].reshape(1, vw, d)
    wg_out = w_all["w_out"].reshape(1, d, d)
    wg_down = w_all["w_down"].reshape(1, D_FF, d)

    kc, kf = conv_dw_w.shape[2], ffn_dw_w.shape[2]
    small_all = _allgather8("gather_small", [flat_rows([c, conv_dw_w, ffn_dw_w])])[0].reshape(N_DEV, -1)
    c_all = small_all[:, :d]
    per_chip = small_all[0::2]
    conv_w = per_chip[:, d:d + CONV_K * kc].reshape(N_CHIPS, CONV_K, kc).transpose(1, 0, 2).reshape(CONV_K, N_CHIPS * kc)
    o_f = d + CONV_K * kc
    ffn_w = per_chip[:, o_f:o_f + FFN_K * kf].reshape(N_CHIPS, FFN_K, kf).transpose(1, 0, 2).reshape(FFN_K, N_CHIPS * kf)

    n_ada = w_ada.shape[2]
    b_ada_cols = lax.dynamic_slice_in_dim(b_ada, chip * n_ada, n_ada, 1)
    mod_cols = _ada_fwd(c_all, w_ada[0], b_ada_cols)
    mod_all = _allgather8("gather_mod", [mod_cols])[0]
    mod = lax.dynamic_index_in_dim(mod_all[0::2], dev, 1, keepdims=False).reshape(1, N_CHIPS * n_ada)
    shift1, scale1, gate1, shift2, scale2, gate2 = [mod[:, k * d:(k + 1) * d] for k in range(6)]

    h1 = _rowwise("ln_mod1", lambda a, s, t: (_f_modulate(a, s, t),), [(xr, d, 0)], [(scale1, d, 0), (shift1, d, 0)],
                  [(d, BF16)], [], n_rows)[0]
    proj = _mm_nn("mm_in", h1, wg_in, b_in, F32)
    a1 = _glu_conv_fwd(proj, 6, 7, conv_w, conv_dw_b, n_rows)
    a2 = _rowwise("conv_norm", lambda a, g, b: (_f_conv_norm(a, g, b),), [(a1, d, 0)], [(conv_ln_g, d, 0), (conv_ln_b, d, 0)],
                  [(d, BF16)], [], n_rows)[0]
    y_a = _mm_nn("mm_conv_out", a2, wg_conv_out, None, F32)
    cos_t, sin_t = _rope_tables(positions[0])
    consts = _retention_consts()
    r, states = _retention_fwd(proj, cos_t, sin_t, consts, n_rows)
    ret_rows = [(r, vw, 0), (proj, vw, 4096 // vw)]
    ret_vecs = [(ret_gn_g, vw, 0), (ret_gn_b, vw, 0)]
    r2 = _rowwise("ret_norm", _per_head(_f_group_norm_gate), ret_rows, ret_vecs, [(vw, BF16)], [], n_rows)[0]
    y_b = _mm_nn("mm_ret_out", r2, wg_ret_out, None, F32)
    merge_rows = [(proj, d, 8), (proj, d, 9), (y_a, d, 0), (y_b, d, 0)]
    m = _rowwise("merge", lambda *a: (_f_merge(*a),), merge_rows, [], [(d, BF16)], [], n_rows)[0]
    t = _mm_nn("mm_out", m, wg_out, None, F32)
    post1_vecs = [(gate1, d, 0), (ln1_g, d, 0), (ln1_b, d, 0), (scale2, d, 0), (shift2, d, 0)]
    x1, h2 = _rowwise("post1", _f_post1, [(xr, d, 0), (t, d, 0)], post1_vecs, [(d, F32), (d, BF16)], [], n_rows)
    pair_up = lambda c: (c % 2) * 2 + c // 2
    paired = lambda a: a.reshape(a.shape[0], N_CHIPS, kf)[:, jnp.array([0, 2, 1, 3])].reshape(a.shape[0], ffw)
    ffn_w_p, ffn_b_p = paired(ffn_w), paired(ffn_dw_b)
    u = _mm_nn("mm_up", h2, wg_up, None, BF16, gmap=pair_up)
    p = _ffn_fwd(u, ffn_w_p, ffn_b_p, n_rows)
    f = _mm_nn("mm_down", p, wg_down, None, F32)

    def loss_rows(x1v, fv, tv, g2v, lg, lb):
        loss, pull = jax.vjp(lambda a, b, c_, e, h: _f_loss(a, b, c_, e, h, tv), x1v, fv, g2v, lg, lb)
        return (*pull(jnp.ones((), F32)), jnp.full((1, 128), loss, F32))

    dx1_a, df, dgate2, dln2_g, dln2_b, loss_v = _rowwise(
        "loss", loss_rows, [(x1, d, 0), (f, d, 0), (target, d, 0)], [(gate2, d, 0), (ln2_g, d, 0), (ln2_b, d, 0)],
        [(d, F32), (d, BF16)], [d, d, d, 128], n_rows)
    loss = lax.psum(loss_v[0, 0], ("x", "y", "c"))

    dp = _mm_nt("mm_down_dx", df, wg_down, F32)
    gw_down = _mm_tn("mm_down_dw", p, df, 1)
    du, g_ffn_w, g_ffn_b = _ffn_bwd(u, dp, ffn_w_p, ffn_b_p, n_rows)
    g_ffn_w, g_ffn_b = paired(g_ffn_w), paired(g_ffn_b)
    dh2 = _mm_nt("mm_up_dx", du, wg_up, F32, gmap=pair_up)
    gw_up = _mm_tn("mm_up_dw", h2, du, N_CHIPS, gmap=pair_up)

    dx_a, dt, dgate1, dln1_g, dln1_b, dscale2, dshift2 = _rowwise(
        "post1_bwd", _vjp_rows(_f_post1, 2, 2), [(xr, d, 0), (t, d, 0), (dx1_a, d, 0), (dh2, d, 0)], post1_vecs,
        [(d, F32), (d, BF16)], [d] * 5, n_rows)
    dm = _mm_nt("mm_out_dx", dt, wg_out, F32)
    gw_out = _mm_tn("mm_out_dw", m, dt, 1)
    merge_vjp = _vjp_rows(_f_merge, 4, 1)

    def merge_bwd(*a):
        dga, dgb, dya, dyb = merge_vjp(*a)
        return jnp.concatenate([dga, dgb], axis=1), dya, dyb

    dproj, dya, dyb = _rowwise("merge_bwd", merge_bwd, merge_rows + [(dm, d, 0)], [],
                               [(2 * d, BF16, proj.shape[1], 8192 // (2 * d)), (d, BF16), (d, BF16)], [], n_rows)

    da2 = _mm_nt("mm_conv_out_dx", dya, wg_conv_out, F32)
    gw_conv_out = _mm_tn("mm_conv_out_dw", a2, dya, 1)
    da1, dcl_g, dcl_b = _rowwise("conv_norm_bwd", _vjp_rows(_f_conv_norm, 1, 1), [(a1, d, 0), (da2, d, 0)],
                                 [(conv_ln_g, d, 0), (conv_ln_b, d, 0)], [(d, F32)], [d, d], n_rows)
    dproj, g_conv_w, g_conv_b = _glu_conv_bwd(proj, 6, 7, da1, conv_w, dproj, 6144 // (2 * d), n_rows)

    dr2 = _mm_nt("mm_ret_out_dx", dyb, wg_ret_out, F32)
    gw_ret_out = _mm_tn("mm_ret_out_dw", r2, dyb, 1)
    dr, dproj, dgn_g, dgn_b = _rowwise(
        "ret_norm_bwd", _per_head(_vjp_rows(_f_group_norm_gate, 2, 1)), ret_rows + [(dr2, vw, 0)], ret_vecs,
        [(vw, F32), (vw, BF16, proj.shape[1], 4096 // vw)], [vw, vw], n_rows, into=(dproj, 1))
    dproj = _retention_bwd(proj, cos_t, sin_t, states, dr, consts, dproj, n_rows)

    dh1 = _mm_nt("mm_in_dx", dproj, wg_in, F32)
    gw_in, gb_in = _mm_tn("mm_in_dw", h1, dproj, N_CHIPS, colsum=True)
    mod_bwd = _vjp_rows(_f_modulate, 1, 1)

    def mod1_bwd(xv, dhv, dxav, sv, tv):
        dx, ds, dsh = mod_bwd(xv, dhv, sv, tv)
        return dx + dxav, ds, dsh

    grad_x, dscale1, dshift1 = _rowwise("ln_mod1_bwd", mod1_bwd, [(xr, d, 0), (dh1, d, 0), (dx_a, d, 0)],
                                        [(scale1, d, 0), (shift1, d, 0)], [(d, F32)], [d, d], n_rows)

    dmod = jnp.concatenate([dshift1, dscale1, dgate1, dshift2, dscale2, dgate2], axis=1)
    small_names = ["b_in", "conv_dw_w", "conv_dw_b", "conv_ln_g", "conv_ln_b", "ret_gn_g", "ret_gn_b", "ln1_g", "ln1_b",
                   "ffn_dw_w", "ffn_dw_b", "ln2_g", "ln2_b"]
    small_parts = [gb_in, g_conv_w, g_conv_b, dcl_g, dcl_b, dgn_g, dgn_b, dln1_g, dln1_b, g_ffn_w, g_ffn_b, dln2_g, dln2_b, dmod]
    small_shapes = [a.shape for a in small_parts]
    parts_all = _allgather8("gather_small_grads", [flat_rows(small_parts)])[0]
    part_rows = parts_all.shape[1]
    summed = _sum_arrays("sum_small_grads", [(parts_all.reshape(N_DEV * part_rows, d), k * part_rows) for k in range(N_DEV)],
                         part_rows)
    small_sum = unflatten(summed, small_shapes)
    grads = dict(zip(small_names, small_sum[:-1]))
    grads["b_ada"] = small_sum[-1]
    grads["conv_dw_w"] = lax.dynamic_slice_in_dim(grads["conv_dw_w"], chip * kc, kc, 1)
    grads["ffn_dw_w"] = lax.dynamic_slice_in_dim(grads["ffn_dw_w"], chip * kf, kf, 1)
    o_mod = sum(a.size for a in small_parts[:-1])
    dmod_all = parts_all.reshape(N_DEV, -1)[:, o_mod:o_mod + dmod.shape[1]]
    grads["w_ada"] = _ada_bwd(c_all, lax.dynamic_slice_in_dim(dmod_all, chip * n_ada, n_ada, 1))

    g_full = dict(w_in=gw_in, w_up=gw_up, w_conv_out=gw_conv_out, w_ret_out=gw_ret_out, w_out=gw_out, w_down=gw_down)
    shard_shapes = [given[n].shape[1:] for n in BIG_WEIGHTS]
    theirs, mine = _sibling_swap("grad_swap", [g_full[n].reshape(N_CHIPS, r, c_) for n, (r, c_) in zip(BIG_WEIGHTS, shard_shapes)])
    pair = [_sum_arrays("grad_pair_sum_" + n, [(m_.reshape(-1, c_), 0), (t_.reshape(-1, c_), 0)], N_CHIPS * (r // 2))
            .reshape(N_CHIPS, r // 2, c_) for n, (r, c_), m_, t_ in zip(BIG_WEIGHTS, shard_shapes, mine, theirs)]
    arrived = _chip_alltoall("grad_alltoall", pair)
    reduced = [_sum_arrays("grad_chip_sum_" + n, [(a_.reshape(-1, c_), k * (r // 2)) for k in range(N_CHIPS)], r // 2)
               for n, (r, c_), a_ in zip(BIG_WEIGHTS, shard_shapes, arrived)]
    g_shards = [g_.reshape(r, c_) for g_, (r, c_) in zip(_sibling_gather("grad_gather", reduced), shard_shapes)]

    outs = {}
    for n, g_ in zip(BIG_WEIGHTS, g_shards):
        upd = _adamw("adamw_" + n, given[n][0], g_, given["m_" + n][0], given["v_" + n][0])
        for prefix, val in zip(("grad_", "delta_", "new_m_", "new_v_"), (g_, *upd)):
            outs[prefix + n] = val.reshape(given[n].shape)
    ada = _adamw("adamw_ada", w_ada[0], grads["w_ada"], m_w_ada[0], v_w_ada[0])
    for prefix, val in zip(("grad_", "delta_", "new_m_", "new_v_"), (grads["w_ada"], *ada)):
        outs[prefix + "w_ada"] = val.reshape(w_ada.shape)
    small_all_names = ["b_ada"] + small_names
    small_w_shapes = [given[n].shape for n in small_all_names]
    g_small = flat_rows([grads[n] for n in small_all_names])
    small_upd = _adamw("adamw_small", flat_rows([given[n] for n in small_all_names]), g_small,
                       flat_rows([given["m_" + n] for n in small_all_names]), flat_rows([given["v_" + n] for n in small_all_names]))
    for prefix, packed in zip(("grad_", "delta_", "new_m_", "new_v_"), (g_small, *small_upd)):
        for n, val in zip(small_all_names, unflatten(packed, small_w_shapes)):
            outs[prefix + n] = val

    weights = ["w_ada", "b_ada", "w_in", "b_in", "conv_dw_w", "conv_dw_b", "conv_ln_g", "conv_ln_b", "w_conv_out", "ret_gn_g",
               "ret_gn_b", "w_ret_out", "w_out", "ln1_g", "ln1_b", "w_up", "ffn_dw_w", "ffn_dw_b", "w_down", "ln2_g", "ln2_b"]
    result = [loss, grad_x.reshape(x.shape)]
    for prefix in ("grad_", "delta_", "new_m_", "new_v_"):
        result += [outs[prefix + n] for n in weights]
    return tuple(result)
```

```python
import jax
import jax.numpy as jnp
from jax import lax
from jax.experimental import pallas as pl
from jax.experimental.pallas import tpu as pltpu

F32 = jnp.float32
BF16 = jnp.bfloat16
MESH = pl.DeviceIdType.MESH

D_MODEL = 1024
N_HEADS = 8
DK = 128
DV = 256
CHUNK = 128
ROPE_BASE = 10000.0
D_FF = 2816
CONV_K = 31
FFN_K = 3
LN_EPS = 1e-5
ALPHA = (2.0 * 1) ** 0.25
ADAM_LR = 0.001
ADAM_B1 = 0.9
ADAM_B2 = 0.999
ADAM_EPS = 1e-08
ADAM_WD = 0.01
ADAM_STEP = 10

V7X_VMEM_BYTES = 64 * 1024 * 1024
VMEM_LIMIT = V7X_VMEM_BYTES - 8 * 1024 * 1024
ROW_TILE = 256
MM_TILE = 512
N_CHIPS = 4
N_DEV = 8

BIG_WEIGHTS = ("w_in", "w_up", "w_conv_out", "w_ret_out", "w_out", "w_down")


def _params(n_grid):
    return pltpu.CompilerParams(dimension_semantics=("arbitrary",) * n_grid, vmem_limit_bytes=VMEM_LIMIT)


def _rowwise(name, fn, rows, vecs, outs, reds, n_rows, tile=ROW_TILE, ncol=1, with_col=False, into=None):
    tile = _fit_tile(n_rows, tile)
    n_in = len(rows) + len(vecs)
    n_ref_in = n_in + (into is not None)
    n_out = len(outs)
    outs = [o if len(o) == 4 else (o[0], o[1], o[0] * ncol, 0) for o in outs]

    def col_map(off, row, first_row=0):
        def index(j, i):
            return (i + first_row // tile if row else 0, off(j) if callable(off) else off + j)
        return index

    def body(*refs):
        i = pl.program_id(1)
        vals = [r[...].astype(F32) for r in refs[:n_in]]
        res = fn(pl.program_id(0), *vals) if with_col else fn(*vals)
        for k in range(n_out):
            refs[n_ref_in + k][...] = res[k].astype(refs[n_ref_in + k].dtype)
        for k in range(len(reds)):
            o = refs[n_ref_in + n_out + k]

            @pl.when(i == 0)
            def _():
                o[...] = jnp.zeros_like(o)

            o[...] += res[n_out + k]

    in_specs = [pl.BlockSpec((tile, e[1]), col_map(e[2], True, e[3] if len(e) > 3 else 0)) for e in rows]
    in_specs += [pl.BlockSpec((1, w), col_map(off, False)) for _, w, off in vecs]
    args = [e[0] for e in rows] + [a for a, _, _ in vecs]
    aliases = {}
    if into is not None:
        in_specs.append(pl.BlockSpec(memory_space=pl.ANY))
        args.append(into[0])
        aliases = {n_in: into[1]}
    out_specs = [pl.BlockSpec((tile, w), col_map(off, True)) for w, _, _, off in outs]
    out_specs += [pl.BlockSpec((1, w), lambda j, i: (0, j)) for w in reds]
    out_shape = [jax.ShapeDtypeStruct((n_rows, total), dt) for _, dt, total, _ in outs]
    out_shape += [jax.ShapeDtypeStruct((1, w * ncol), F32) for w in reds]
    return pl.pallas_call(
        body, name=name, grid=(ncol, n_rows // tile), in_specs=in_specs, out_specs=out_specs, out_shape=out_shape,
        input_output_aliases=aliases, compiler_params=_params(2),
    )(*args)


def _vjp_rows(fn, n_row_in, n_ct):
    def bwd(*args):
        prim = [a.astype(F32) for a in args[:n_row_in] + args[n_row_in + n_ct:]]
        cts = tuple(a.astype(F32) for a in args[n_row_in:n_row_in + n_ct])
        _, pull = jax.vjp(fn, *prim)
        return pull(cts if n_ct > 1 else cts[0])

    return bwd


def _fit_tile(n, pref):
    if n <= pref:
        return n
    t = pref - pref % 16
    while n % t:
        t -= 16
    return t


def _col_tile(n):
    return n if n <= 1536 else n // 2


def _same_group(c):
    return c


def _mm_nn(name, a, w, bias, out_dtype, gmap=_same_group):
    s, k = a.shape
    g, _, n = w.shape
    tm, tn = min(MM_TILE, s), _col_tile(n)
    nt = n // tn

    def body(*refs):
        a_ref, w_ref = refs[0], refs[1]
        o_ref = refs[-1]
        acc = jnp.dot(a_ref[...].astype(BF16), w_ref[...], preferred_element_type=F32)
        if bias is not None:
            acc = acc + refs[2][...]
        o_ref[...] = acc.astype(o_ref.dtype)

    in_specs = [pl.BlockSpec((tm, k), lambda c, i: (i, 0)),
                pl.BlockSpec((None, k, tn), lambda c, i: (gmap(c // nt), 0, c % nt))]
    args = [a, w]
    if bias is not None:
        in_specs.append(pl.BlockSpec((1, tn), lambda c, i: (0, c)))
        args.append(bias)
    return pl.pallas_call(
        body, name=name, grid=(g * nt, s // tm), in_specs=in_specs, out_specs=pl.BlockSpec((tm, tn), lambda c, i: (i, c)),
        out_shape=jax.ShapeDtypeStruct((s, g * n), out_dtype), compiler_params=_params(2),
    )(*args)


def _mm_nt(name, dy, w, out_dtype, gmap=_same_group):
    s = dy.shape[0]
    g, k, n = w.shape
    tm, tn = min(MM_TILE, s), n
    nt = 1
    steps = g

    def body(dy_ref, w_ref, o_ref, acc_ref):
        r = pl.program_id(1)
        part = lax.dot_general(dy_ref[...].astype(BF16), w_ref[...], (((1,), (1,)), ((), ())), preferred_element_type=F32)

        @pl.when(r == 0)
        def _():
            acc_ref[...] = part

        @pl.when(r > 0)
        def _():
            acc_ref[...] += part

        @pl.when(r == steps - 1)
        def _():
            o_ref[...] = acc_ref[...].astype(o_ref.dtype)

    return pl.pallas_call(
        body, name=name, grid=(s // tm, steps),
        in_specs=[pl.BlockSpec((tm, tn), lambda i, r: (i, r)),
                  pl.BlockSpec((None, k, tn), lambda i, r: (gmap(r // nt), 0, r % nt))],
        out_specs=pl.BlockSpec((tm, k), lambda i, r: (i, 0)), out_shape=jax.ShapeDtypeStruct((s, k), out_dtype),
        scratch_shapes=[pltpu.VMEM((tm, k), F32)], compiler_params=_params(2),
    )(dy, w)


def _mm_tn(name, a, dy, g, gmap=_same_group, colsum=False):
    s, k = a.shape
    n = dy.shape[1] // g
    ts = min(2048 if k <= 1024 else 1024, s)
    tn = n if k * n <= 1024 * 1536 else (n // 2 if (n // 2) % 128 == 0 else n)
    nt = n // tn

    def body(a_ref, dy_ref, o_ref, *sum_ref):
        t = pl.program_id(1)
        dyv = dy_ref[...].astype(BF16)
        part = lax.dot_general(a_ref[...].astype(BF16), dyv, (((0,), (0,)), ((), ())), preferred_element_type=F32)

        @pl.when(t == 0)
        def _():
            o_ref[...] = part

        @pl.when(t > 0)
        def _():
            o_ref[...] += part

        if colsum:
            col = jnp.sum(dyv.astype(F32), axis=0, keepdims=True)

            @pl.when(t == 0)
            def _():
                sum_ref[0][...] = col

            @pl.when(t > 0)
            def _():
                sum_ref[0][...] += col

    out_specs = [pl.BlockSpec((None, k, tn), lambda c, t: (gmap(c // nt), 0, c % nt))]
    out_shape = [jax.ShapeDtypeStruct((g, k, n), F32)]
    if colsum:
        out_specs.append(pl.BlockSpec((1, tn), lambda c, t: (0, c)))
        out_shape.append(jax.ShapeDtypeStruct((1, g * n), F32))
    res = pl.pallas_call(
        body, name=name, grid=(g * nt, s // ts),
        in_specs=[pl.BlockSpec((ts, k), lambda c, t: (t, 0)), pl.BlockSpec((ts, tn), lambda c, t: (t, c))],
        out_specs=out_specs, out_shape=out_shape, compiler_params=_params(2),
    )(a, dy)
    return res if colsum else res[0]


SUBLANES = 8


def _tap_sum(read, w_row, offsets, tile):
    acc = None
    for b in range(SUBLANES):
        group = [(k, o) for k, o in offsets if o % SUBLANES == b]
        if not group:
            continue
        rows = tile if b == 0 else tile + SUBLANES
        z = None
        for k, o in group:
            term = w_row(k) * read(o - b, rows)
            z = term if z is None else z + term
        part = z if b == 0 else z[b:b + tile]
        acc = part if acc is None else acc + part
    return acc


def _tap_grads(read, dy, offsets, tile):
    padded = jnp.concatenate([dy, jnp.zeros((SUBLANES, dy.shape[1]), dy.dtype)], axis=0)
    out = {}
    for b in range(SUBLANES):
        group = [(k, o) for k, o in offsets if o % SUBLANES == b]
        if not group:
            continue
        shifted = dy if b == 0 else pltpu.roll(padded, b, 0)
        rows = tile if b == 0 else tile + SUBLANES
        for k, o in group:
            out[k] = jnp.sum(shifted * read(o - b, rows), axis=0, keepdims=True)
    return out


CONV_HALO = 32
CONV_TILE = 256
CONV_LANES = 256


def _glu_conv_fwd(proj, col_v, col_g, w, b, n_rows):
    kw, n_ch = w.shape
    tile = min(CONV_TILE, n_rows)
    per = tile // CONV_HALO
    offsets = [(k, CONV_HALO - (kw - 1) + k) for k in range(kw)]

    def body(v_ref, g_ref, vh_ref, gh_ref, w_ref, b_ref, o_ref, buf):
        i = pl.program_id(0)
        prev = vh_ref[...].astype(F32) * jax.nn.sigmoid(gh_ref[...].astype(F32))
        buf[0:CONV_HALO, :] = jnp.where(i == 0, 0.0, prev)
        buf[CONV_HALO:CONV_HALO + tile, :] = v_ref[...].astype(F32) * jax.nn.sigmoid(g_ref[...].astype(F32))
        for c0 in range(0, n_ch, CONV_LANES):
            cols = slice(c0, c0 + CONV_LANES)
            o_ref[:, cols] = b_ref[:, cols] + _tap_sum(lambda s, n: buf[pl.ds(s, n), cols], lambda k: w_ref[k:k + 1, cols],
                                                      offsets, tile)

    main = lambda col: pl.BlockSpec((tile, n_ch), lambda i: (i, col))
    halo = lambda col: pl.BlockSpec((CONV_HALO, n_ch), lambda i: (jnp.maximum(i * per - 1, 0), col))
    return pl.pallas_call(
        body, name="conv_fwd", grid=(n_rows // tile,),
        in_specs=[main(col_v), main(col_g), halo(col_v), halo(col_g), pl.BlockSpec((kw, n_ch), lambda i: (0, 0)),
                  pl.BlockSpec((1, n_ch), lambda i: (0, 0))],
        out_specs=pl.BlockSpec((tile, n_ch), lambda i: (i, 0)), out_shape=jax.ShapeDtypeStruct((n_rows, n_ch), F32),
        scratch_shapes=[pltpu.VMEM((CONV_HALO + tile, n_ch), F32)], compiler_params=_params(1),
    )(proj, proj, proj, proj, w, b)


def _glu_conv_bwd(proj, col_v, col_g, dy, w, dproj, col_out, n_rows):
    kw, n_ch = w.shape
    tile = min(CONV_TILE, n_rows)
    per = tile // CONV_HALO
    n_tiles = n_rows // tile
    last_halo = n_rows // CONV_HALO - 1
    offsets = [(k, CONV_HALO - (kw - 1) + k) for k in range(kw)]
    back = [(k, kw - 1 - k) for k in range(kw)]

    def body(v_ref, g_ref, vh_ref, gh_ref, dy_ref, dyn_ref, w_ref, dp_any, dx_ref, dw_ref, db_ref, buf, dbuf):
        i = pl.program_id(0)
        cv, cg = v_ref[...].astype(F32), g_ref[...].astype(F32)
        sig = jax.nn.sigmoid(cg)
        prev = vh_ref[...].astype(F32) * jax.nn.sigmoid(gh_ref[...].astype(F32))
        buf[0:CONV_HALO, :] = jnp.where(i == 0, 0.0, prev)
        buf[CONV_HALO:CONV_HALO + tile, :] = cv * sig
        dbuf[0:tile, :] = dy_ref[...]
        dbuf[tile:tile + CONV_HALO, :] = jnp.where(i == n_tiles - 1, 0.0, dyn_ref[...])

        @pl.when(i == 0)
        def _():
            dw_ref[...] = jnp.zeros_like(dw_ref)
            db_ref[...] = jnp.zeros_like(db_ref)

        db_ref[...] += jnp.sum(dy_ref[...], axis=0, keepdims=True)
        for c0 in range(0, n_ch, CONV_LANES):
            cols = slice(c0, c0 + CONV_LANES)
            w_row = lambda k: w_ref[k:k + 1, cols]
            dx = _tap_sum(lambda s, n: dbuf[pl.ds(s, n), cols], w_row, back, tile)
            grads = _tap_grads(lambda s, n: buf[pl.ds(s, n), cols], dy_ref[:, cols], offsets, tile)
            for k in range(kw):
                dw_ref[k:k + 1, cols] += grads[k]
            sg = sig[:, cols]
            dx_ref[:, c0:c0 + CONV_LANES] = (dx * sg).astype(dx_ref.dtype)
            dx_ref[:, n_ch + c0:n_ch + c0 + CONV_LANES] = (dx * cv[:, cols] * sg * (1.0 - sg)).astype(dx_ref.dtype)

    main = lambda col: pl.BlockSpec((tile, n_ch), lambda i: (i, col))
    halo = lambda col: pl.BlockSpec((CONV_HALO, n_ch), lambda i: (jnp.maximum(i * per - 1, 0), col))
    return pl.pallas_call(
        body, name="conv_bwd", grid=(n_tiles,),
        in_specs=[main(col_v), main(col_g), halo(col_v), halo(col_g), main(0),
                  pl.BlockSpec((CONV_HALO, n_ch), lambda i: (jnp.minimum((i + 1) * per, last_halo), 0)),
                  pl.BlockSpec((kw, n_ch), lambda i: (0, 0)), pl.BlockSpec(memory_space=pl.ANY)],
        out_specs=[pl.BlockSpec((tile, 2 * n_ch), lambda i: (i, col_out)), pl.BlockSpec((kw, n_ch), lambda i: (0, 0)),
                   pl.BlockSpec((1, n_ch), lambda i: (0, 0))],
        out_shape=[jax.ShapeDtypeStruct(dproj.shape, dproj.dtype), jax.ShapeDtypeStruct((kw, n_ch), F32),
                   jax.ShapeDtypeStruct((1, n_ch), F32)],
        input_output_aliases={7: 0},
        scratch_shapes=[pltpu.VMEM((CONV_HALO + tile, n_ch), F32), pltpu.VMEM((tile + CONV_HALO, n_ch), F32)],
        compiler_params=_params(1),
    )(proj, proj, proj, proj, dy, dy, w, dproj)


FFN_HALO = 8


def _ffn_fwd(u, w, b, n_rows):
    kw = w.shape[0]
    half = u.shape[1] // 4
    tile = min(ROW_TILE, n_rows)
    per = tile // FFN_HALO
    offsets = [(k, FFN_HALO - (kw - 1) + k) for k in range(kw)]

    def body(u_ref, uh_ref, w_ref, b_ref, p_ref, buf):
        i = pl.program_id(1)
        buf[0:FFN_HALO, :] = jnp.where(i == 0, 0.0, uh_ref[...].astype(F32))
        buf[FFN_HALO:FFN_HALO + tile, :] = u_ref[...].astype(F32)
        conv = []
        for c0 in (0, half):
            cols = slice(c0, c0 + half)
            conv.append(b_ref[:, cols] + _tap_sum(lambda s, n: buf[pl.ds(s, n), cols], lambda k: w_ref[k:k + 1, cols],
                                                  offsets, tile))
        p_ref[...] = (conv[0] * jax.nn.silu(conv[1])).astype(p_ref.dtype)

    return pl.pallas_call(
        body, name="ffn_fwd", grid=(2, n_rows // tile),
        in_specs=[pl.BlockSpec((tile, 2 * half), lambda j, i: (i, j)),
                  pl.BlockSpec((FFN_HALO, 2 * half), lambda j, i: (jnp.maximum(i * per - 1, 0), j)),
                  pl.BlockSpec((kw, 2 * half), lambda j, i: (0, j)), pl.BlockSpec((1, 2 * half), lambda j, i: (0, j))],
        out_specs=pl.BlockSpec((tile, half), lambda j, i: (i, j)), out_shape=jax.ShapeDtypeStruct((n_rows, 2 * half), BF16),
        scratch_shapes=[pltpu.VMEM((FFN_HALO + tile, 2 * half), F32)], compiler_params=_params(2),
    )(u, u, w, b)


def _ffn_bwd(u, dp, w, b, n_rows):
    kw = w.shape[0]
    half = u.shape[1] // 4
    tile = min(ROW_TILE, n_rows)
    per = tile // FFN_HALO
    n_tiles = n_rows // tile
    last_halo = n_rows // FFN_HALO - 1
    ext = tile + FFN_HALO
    offsets = [(k, FFN_HALO - (kw - 1) + k) for k in range(kw)]
    back = [(k, kw - 1 - k) for k in range(kw)]

    def body(u_ref, up_ref, un_ref, dp_ref, dpn_ref, w_ref, b_ref, du_ref, dw_ref, db_ref, buf, dbuf):
        i = pl.program_id(1)
        buf[0:FFN_HALO, :] = jnp.where(i == 0, 0.0, up_ref[...].astype(F32))
        buf[FFN_HALO:FFN_HALO + tile, :] = u_ref[...].astype(F32)
        buf[FFN_HALO + tile:FFN_HALO + ext, :] = un_ref[...].astype(F32)
        conv = []
        for c0 in (0, half):
            cols = slice(c0, c0 + half)
            conv.append(b_ref[:, cols] + _tap_sum(lambda s, n: buf[pl.ds(s, n), cols], lambda k: w_ref[k:k + 1, cols],
                                                  offsets, ext))
        val, gate = conv
        dpe = jnp.concatenate([dp_ref[...], jnp.where(i == n_tiles - 1, 0.0, dpn_ref[...])], axis=0)
        sig = jax.nn.sigmoid(gate)
        dbuf[:, 0:half] = dpe * gate * sig
        dbuf[:, half:2 * half] = dpe * val * sig * (1.0 + gate * (1.0 - sig))

        @pl.when(i == 0)
        def _():
            dw_ref[...] = jnp.zeros_like(dw_ref)
            db_ref[...] = jnp.zeros_like(db_ref)

        for c0 in (0, half):
            cols = slice(c0, c0 + half)
            du_ref[:, cols] = _tap_sum(lambda s, n: dbuf[pl.ds(s, n), cols], lambda k: w_ref[k:k + 1, cols], back,
                                       tile).astype(du_ref.dtype)
            d_main = dbuf[0:tile, cols]
            db_ref[:, cols] += jnp.sum(d_main, axis=0, keepdims=True)
            grads = _tap_grads(lambda s, n: buf[pl.ds(s, n), cols], d_main, offsets, tile)
            for k in range(kw):
                dw_ref[k:k + 1, cols] += grads[k]

    wide = 2 * half
    return pl.pallas_call(
        body, name="ffn_bwd", grid=(2, n_tiles),
        in_specs=[pl.BlockSpec((tile, wide), lambda j, i: (i, j)),
                  pl.BlockSpec((FFN_HALO, wide), lambda j, i: (jnp.maximum(i * per - 1, 0), j)),
                  pl.BlockSpec((FFN_HALO, wide), lambda j, i: (jnp.minimum((i + 1) * per, last_halo), j)),
                  pl.BlockSpec((tile, half), lambda j, i: (i, j)),
                  pl.BlockSpec((FFN_HALO, half), lambda j, i: (jnp.minimum((i + 1) * per, last_halo), j)),
                  pl.BlockSpec((kw, wide), lambda j, i: (0, j)), pl.BlockSpec((1, wide), lambda j, i: (0, j))],
        out_specs=[pl.BlockSpec((tile, wide), lambda j, i: (i, j)), pl.BlockSpec((kw, wide), lambda j, i: (0, j)),
                   pl.BlockSpec((1, wide), lambda j, i: (0, j))],
        out_shape=[jax.ShapeDtypeStruct(u.shape, BF16), jax.ShapeDtypeStruct((kw, u.shape[1]), F32),
                   jax.ShapeDtypeStruct((1, u.shape[1]), F32)],
        scratch_shapes=[pltpu.VMEM((FFN_HALO + ext, wide), F32), pltpu.VMEM((ext, wide), F32)], compiler_params=_params(2),
    )(u, u, u, dp, dp, w, b)


def _retention_consts():
    log_gamma = jnp.log(1.0 - 2.0 ** (-5.0 - jnp.arange(N_HEADS, dtype=F32)))
    idx = jnp.arange(CHUNK, dtype=F32)
    rel = idx[:, None] - idx[None, :]
    decay = jnp.where(rel[None] >= 0, jnp.exp(log_gamma[:, None, None] * jnp.maximum(rel, 0.0)[None]), 0.0)
    zeta = jnp.exp(log_gamma[:, None] * (CHUNK - 1.0 - idx)[None])
    xi = jnp.exp(log_gamma[:, None] * (idx + 1.0)[None])
    chunk_decay = jnp.exp(log_gamma * CHUNK)
    xi_b = jnp.broadcast_to(xi[:, :, None], (N_HEADS, CHUNK, DK))
    zeta_b = jnp.broadcast_to(zeta[:, :, None], (N_HEADS, CHUNK, DK))
    cd_b = jnp.broadcast_to(chunk_decay[:, None, None], (N_HEADS, 8, DV))
    return decay, xi_b, zeta_b, cd_b


def _rope_tables(positions):
    half = DK // 2
    inv_freq = ROPE_BASE ** (-jnp.arange(half, dtype=F32) / half)
    ang = positions.astype(F32)[:, None] * inv_freq
    cos, sin = jnp.cos(ang), jnp.sin(ang)
    return jnp.concatenate([cos, cos], axis=-1), jnp.concatenate([-sin, sin], axis=-1)


def _swap_halves(v):
    return pltpu.roll(v, DK // 2, 1)


def _dot(a, b):
    return jnp.dot(a, b, preferred_element_type=F32)


def _dot_nt(a, b):
    return lax.dot_general(a, b, (((1,), (1,)), ((), ())), preferred_element_type=F32)


def _dot_tn(a, b):
    return lax.dot_general(a, b, (((0,), (0,)), ((), ())), preferred_element_type=F32)


def _const_specs():
    return [pl.BlockSpec((N_HEADS, CHUNK, CHUNK), lambda n: (0, 0, 0)), pl.BlockSpec((N_HEADS, CHUNK, DK), lambda n: (0, 0, 0)),
            pl.BlockSpec((N_HEADS, CHUNK, DK), lambda n: (0, 0, 0)), pl.BlockSpec((N_HEADS, 8, DV), lambda n: (0, 0, 0))]


def _retention_fwd(proj, cos_t, sin_t, consts, n_rows):
    n_chunks = n_rows // CHUNK
    scale = DK ** -0.5

    def body(q_ref, k_ref, v_ref, cf_ref, ss_ref, d_ref, xi_ref, zt_ref, cd_ref, r_ref, st_ref, state):
        @pl.when(pl.program_id(0) == 0)
        def _():
            state[...] = jnp.zeros_like(state)

        cf, ss = cf_ref[...], ss_ref[...]
        for h in range(N_HEADS):
            qh = q_ref[:, h * DK:(h + 1) * DK].astype(F32)
            kh = k_ref[:, h * DK:(h + 1) * DK].astype(F32)
            qh = qh * cf + _swap_halves(qh) * ss
            kh = (kh * cf + _swap_halves(kh) * ss) * scale
            vh = v_ref[:, h * DV:(h + 1) * DV].astype(BF16)
            st = state[h]
            st_ref[0, h] = st
            sd = _dot_nt(qh.astype(BF16), kh.astype(BF16)) * d_ref[h]
            inner = _dot(sd.astype(BF16), vh)
            cross = _dot((qh * xi_ref[h]).astype(BF16), st.astype(BF16))
            kv = _dot_tn((kh * zt_ref[h]).astype(BF16), vh)
            state[h] = st * cd_ref[h, 0:1, :] + kv
            r_ref[:, h * DV:(h + 1) * DV] = inner + cross

    qk = N_HEADS * DK
    vw = N_HEADS * DV
    return pl.pallas_call(
        body, name="retention_fwd", grid=(n_chunks,),
        in_specs=[pl.BlockSpec((CHUNK, qk), lambda n: (n, 0)), pl.BlockSpec((CHUNK, qk), lambda n: (n, 1)),
                  pl.BlockSpec((CHUNK, vw), lambda n: (n, 1)), pl.BlockSpec((CHUNK, DK), lambda n: (n, 0)),
                  pl.BlockSpec((CHUNK, DK), lambda n: (n, 0))] + _const_specs(),
        out_specs=[pl.BlockSpec((CHUNK, vw), lambda n: (n, 0)), pl.BlockSpec((1, N_HEADS, DK, DV), lambda n: (n, 0, 0, 0))],
        out_shape=[jax.ShapeDtypeStruct((n_rows, vw), F32), jax.ShapeDtypeStruct((n_chunks, N_HEADS, DK, DV), F32)],
        scratch_shapes=[pltpu.VMEM((N_HEADS, DK, DV), F32)], compiler_params=_params(1),
    )(proj, proj, proj, cos_t, sin_t, *consts)


def _retention_bwd(proj, cos_t, sin_t, states, dr, consts, dproj, n_rows):
    n_chunks = n_rows // CHUNK
    scale = DK ** -0.5
    qk = N_HEADS * DK
    vw = N_HEADS * DV

    def body(q_ref, k_ref, v_ref, cf_ref, ss_ref, st_ref, dr_ref, d_ref, xi_ref, zt_ref, cd_ref, dp_any, dqkv_ref, g_ref):
        dq_ref, dk_ref, dv_ref = dqkv_ref.at[:, 0:qk], dqkv_ref.at[:, qk:2 * qk], dqkv_ref.at[:, 2 * qk:2 * qk + vw]

        @pl.when(pl.program_id(0) == 0)
        def _():
            g_ref[...] = jnp.zeros_like(g_ref)

        cf, ss = cf_ref[...], ss_ref[...]
        for h in range(N_HEADS):
            qh = q_ref[:, h * DK:(h + 1) * DK].astype(F32)
            kh = k_ref[:, h * DK:(h + 1) * DK].astype(F32)
            qh = qh * cf + _swap_halves(qh) * ss
            kh = (kh * cf + _swap_halves(kh) * ss) * scale
            qb, kb = qh.astype(BF16), kh.astype(BF16)
            vh = v_ref[:, h * DV:(h + 1) * DV].astype(BF16)
            do = dr_ref[:, h * DV:(h + 1) * DV].astype(BF16)
            rb = st_ref[0, h].astype(BF16)
            g = g_ref[h]
            gb = g.astype(BF16)
            dec, xi, zt = d_ref[h], xi_ref[h], zt_ref[h]
            sd = (_dot_nt(qb, kb) * dec).astype(BF16)
            ds = (_dot_nt(do, vh) * dec).astype(BF16)
            dqh = _dot(ds, kb) + _dot_nt(do, rb) * xi
            dkh = (_dot_tn(ds, qb) + _dot_nt(vh, gb) * zt) * scale
            dvh = _dot_tn(sd, do) + _dot((kh * zt).astype(BF16), gb)
            g_ref[h] = g * cd_ref[h, 0:1, :] + _dot_tn((qh * xi).astype(BF16), do)
            dq_ref[:, h * DK:(h + 1) * DK] = (dqh * cf + _swap_halves(dqh * ss)).astype(dq_ref.dtype)
            dk_ref[:, h * DK:(h + 1) * DK] = (dkh * cf + _swap_halves(dkh * ss)).astype(dk_ref.dtype)
            dv_ref[:, h * DV:(h + 1) * DV] = dvh.astype(dv_ref.dtype)

    last = n_chunks - 1
    return pl.pallas_call(
        body, name="retention_bwd", grid=(n_chunks,),
        in_specs=[pl.BlockSpec((CHUNK, qk), lambda n: (last - n, 0)), pl.BlockSpec((CHUNK, qk), lambda n: (last - n, 1)),
                  pl.BlockSpec((CHUNK, vw), lambda n: (last - n, 1)), pl.BlockSpec((CHUNK, DK), lambda n: (last - n, 0)),
                  pl.BlockSpec((CHUNK, DK), lambda n: (last - n, 0)),
                  pl.BlockSpec((1, N_HEADS, DK, DV), lambda n: (last - n, 0, 0, 0)),
                  pl.BlockSpec((CHUNK, vw), lambda n: (last - n, 0))] + _const_specs() + [pl.BlockSpec(memory_space=pl.ANY)],
        out_specs=pl.BlockSpec((CHUNK, 2 * qk + vw), lambda n: (last - n, 0)),
        out_shape=jax.ShapeDtypeStruct(dproj.shape, dproj.dtype), input_output_aliases={11: 0},
        scratch_shapes=[pltpu.VMEM((N_HEADS, DK, DV), F32)], compiler_params=_params(1),
    )(proj, proj, proj, cos_t, sin_t, states, dr, *consts, dproj)


def _ln(v):
    mu = jnp.mean(v, axis=-1, keepdims=True)
    var = jnp.mean(jnp.square(v - mu), axis=-1, keepdims=True)
    return (v - mu) * lax.rsqrt(var + LN_EPS)


def _f_modulate(x, scale, shift):
    return _ln(x) * (1.0 + scale) + shift


def _f_conv_norm(a1, g, b):
    return jax.nn.silu(_ln(a1) * g + b)


def _f_group_norm_gate(r, gate, g, b):
    return (_ln(r) * g + b) * jax.nn.silu(gate)


def _per_head(fn):
    def run(*arrays):
        parts = [fn(*[a[:, h * DV:(h + 1) * DV] for a in arrays]) for h in range(N_HEADS)]
        if not isinstance(parts[0], (tuple, list)):
            return (jnp.concatenate(parts, axis=1),)
        return tuple(jnp.concatenate([p[k] for p in parts], axis=1) for k in range(len(parts[0])))

    return run


def _f_merge(ga, gb, ya, yb):
    return jax.nn.sigmoid(ga) * ya + jax.nn.sigmoid(gb) * yb


def _f_post1(x, t, gate1, g1, b1, scale2, shift2):
    x1 = _ln(ALPHA * x + gate1 * t) * g1 + b1
    return x1, _ln(x1) * (1.0 + scale2) + shift2


def _f_loss(x1, f, gate2, g2, b2, target):
    y = _ln(ALPHA * x1 + gate2 * f) * g2 + b2
    return 0.5 * jnp.sum(jnp.mean(jnp.square(y - target), axis=-1))


ANY = pl.BlockSpec(memory_space=pl.ANY)


def _allgather8(name, blocks, own_half=False):
    n = len(blocks)
    rows = [b.shape[0] // 2 if own_half else b.shape[0] for b in blocks]

    def body(*refs):
        x_refs, out_refs = refs[:n], refs[n:2 * n]
        send_sems, recv_sems, local_sems = refs[2 * n:]
        x, y, c = lax.axis_index("x"), lax.axis_index("y"), lax.axis_index("c")
        me, sibling = (x, y, c), (x, y, 1 - c)
        chips = [(1 - x, y), (x, 1 - y), (1 - x, 1 - y)]
        every = range(n)

        def src(a):
            return x_refs[a].at[pl.ds(c * rows[a], rows[a])] if own_half else x_refs[a]

        def slot(a, px, py, pc):
            return out_refs[a].at[4 * px + 2 * py + pc]

        def copy(k, a, block, to, from_input=False):
            return pltpu.make_async_remote_copy(
                src_ref=src(a) if from_input else slot(a, *block), dst_ref=slot(a, *block), send_sem=send_sems.at[k, a],
                recv_sem=recv_sems.at[k, a], device_id=to, device_id_type=MESH)

        mine = [pltpu.make_async_copy(src(a), slot(a, *me), local_sems.at[a]) for a in every]
        for cp in mine:
            cp.start()
        first = [copy(0, a, me, sibling, True) for a in every]
        first += [copy(1 + j, a, me, (*chip, c), True) for j, chip in enumerate(chips) for a in every]
        for cp in first:
            cp.start()
        passed = [[copy(4 + j, a, (*chip, c), sibling) for a in every] for j, chip in enumerate(chips)]
        for j, chip in enumerate(chips):
            for a in every:
                copy(1 + j, a, (*chip, c), me).wait_recv()
            for cp in passed[j]:
                cp.start()
        for a in every:
            copy(0, a, sibling, me).wait_recv()
        for j, chip in enumerate(chips):
            for a in every:
                copy(4 + j, a, (*chip, 1 - c), me).wait_recv()
        for cp in first + [cp for group in passed for cp in group]:
            cp.wait_send()
        for cp in mine:
            cp.wait()

    return pl.pallas_call(
        body, name=name, in_specs=[ANY] * n, out_specs=[ANY] * n,
        out_shape=[jax.ShapeDtypeStruct((N_DEV, r, b.shape[1]), b.dtype) for r, b in zip(rows, blocks)],
        scratch_shapes=[pltpu.SemaphoreType.DMA((7, n)), pltpu.SemaphoreType.DMA((7, n)), pltpu.SemaphoreType.DMA((n,))],
    )(*blocks)


def _sibling_swap(name, arrays):
    n = len(arrays)
    halves = [a.shape[1] // 2 for a in arrays]

    def body(*refs):
        g_refs, theirs = refs[:n], refs[n:2 * n]
        send_sems, recv_sems = refs[2 * n:]
        x, y, c = lax.axis_index("x"), lax.axis_index("y"), lax.axis_index("c")
        remote = [pltpu.make_async_remote_copy(
            src_ref=g_refs[a].at[:, pl.ds((1 - c) * halves[a], halves[a]), :], dst_ref=theirs[a], send_sem=send_sems.at[a],
            recv_sem=recv_sems.at[a], device_id=(x, y, 1 - c), device_id_type=MESH) for a in range(n)]
        for cp in remote:
            cp.start()
        for cp in remote:
            cp.wait_recv()
        for cp in remote:
            cp.wait_send()

    return pl.pallas_call(
        body, name=name, in_specs=[ANY] * n, out_specs=[ANY] * n,
        out_shape=[jax.ShapeDtypeStruct((a.shape[0], h, a.shape[2]), a.dtype) for a, h in zip(arrays, halves)],
        scratch_shapes=[pltpu.SemaphoreType.DMA((n,)), pltpu.SemaphoreType.DMA((n,))],
    )(*arrays)


HBM = pl.BlockSpec(memory_space=pltpu.HBM)
SEM = pl.BlockSpec(memory_space=pltpu.SEMAPHORE)
DATAFLOW = pltpu.SideEffectType.DATAFLOW_SIDE_EFFECTING


def _chip_peers(x, y):
    return [(1 - x, y), (x, 1 - y), (1 - x, 1 - y)]


def _alltoall_start(name, arrays):
    n = len(arrays)
    lands = [lax.empty((3,) + a.shape[1:], a.dtype) for a in arrays]

    def body(*refs):
        p_refs, land_refs = refs[:n], refs[n:2 * n]
        send_sems, recv_sems = refs[2 * n], refs[2 * n + 1]
        token = refs[-1]
        x, y, c = lax.axis_index("x"), lax.axis_index("y"), lax.axis_index("c")
        for k, (px, py) in enumerate(_chip_peers(x, y)):
            for a in range(n):
                pltpu.make_async_remote_copy(
                    src_ref=p_refs[a].at[2 * px + py], dst_ref=land_refs[a].at[k], send_sem=send_sems.at[k * n + a],
                    recv_sem=recv_sems.at[k * n + a], device_id=(px, py, c), device_id_type=MESH).start()
        token[...] = jnp.zeros_like(token)

    thru = [pltpu.HBM(a.shape, a.dtype) for a in arrays] + [pltpu.HBM(l.shape, l.dtype) for l in lands]
    res = pl.pallas_call(
        body, name=name, in_specs=[HBM] * (2 * n),
        out_specs=(SEM, SEM, *[HBM] * (2 * n), pl.BlockSpec(memory_space=pltpu.VMEM)),
        out_shape=(pltpu.SemaphoreType.DMA((3 * n,)), pltpu.SemaphoreType.DMA((3 * n,)), *thru, jax.ShapeDtypeStruct((8, 128), F32)),
        input_output_aliases={i: 2 + i for i in range(2 * n)},
        compiler_params=pltpu.CompilerParams(has_side_effects=DATAFLOW),
    )(*[pltpu.with_memory_space_constraint(a, pltpu.HBM) for a in arrays],
      *[pltpu.with_memory_space_constraint(l, pltpu.HBM) for l in lands])
    return (res[0], res[1], list(res[2:2 + n]), list(res[2 + n:2 + 2 * n])), res[-1]


def _alltoall_wait(name, handle, after):
    send_sems, recv_sems, sources, lands = handle
    n = len(sources)

    def body(*refs):
        p_refs, land_refs = refs[:n], refs[n:2 * n]
        send_sems, recv_sems = refs[2 * n], refs[2 * n + 1]
        x, y, c = lax.axis_index("x"), lax.axis_index("y"), lax.axis_index("c")
        for k, (px, py) in enumerate(_chip_peers(x, y)):
            for a in range(n):
                cp = pltpu.make_async_remote_copy(
                    src_ref=p_refs[a].at[2 * px + py], dst_ref=land_refs[a].at[k], send_sem=send_sems.at[k * n + a],
                    recv_sem=recv_sems.at[k * n + a], device_id=(px, py, c), device_id_type=MESH)
                cp.wait_send()
                cp.wait_recv()

    res = pl.pallas_call(
        body, name=name, in_specs=[HBM] * (2 * n) + [SEM, SEM, ANY], out_specs=[HBM] * (2 * n),
        out_shape=[pltpu.HBM(a.shape, a.dtype) for a in sources + lands],
        input_output_aliases={i: i for i in range(2 * n)}, compiler_params=pltpu.CompilerParams(has_side_effects=DATAFLOW),
    )(*sources, *lands, send_sems, recv_sems, after)
    return list(res[n:])


def _sibling_gather(name, halves):
    n = len(halves)

    def body(*refs):
        h_refs, out_refs = refs[:n], refs[n:2 * n]
        send_sems, recv_sems, local_sems = refs[2 * n:]
        x, y, c = lax.axis_index("x"), lax.axis_index("y"), lax.axis_index("c")
        own = [pltpu.make_async_copy(h_refs[a], out_refs[a].at[c], local_sems.at[a]) for a in range(n)]
        remote = [pltpu.make_async_remote_copy(
            src_ref=h_refs[a], dst_ref=out_refs[a].at[c], send_sem=send_sems.at[a], recv_sem=recv_sems.at[a],
            device_id=(x, y, 1 - c), device_id_type=MESH) for a in range(n)]
        for cp in own + remote:
            cp.start()
        for cp in remote:
            cp.wait_recv()
        for cp in remote:
            cp.wait_send()
        for cp in own:
            cp.wait()

    return pl.pallas_call(
        body, name=name, in_specs=[ANY] * n, out_specs=[ANY] * n,
        out_shape=[jax.ShapeDtypeStruct((2,) + h.shape, h.dtype) for h in halves],
        scratch_shapes=[pltpu.SemaphoreType.DMA((n,)), pltpu.SemaphoreType.DMA((n,)), pltpu.SemaphoreType.DMA((n,))],
    )(*halves)


def _sum_arrays(name, terms, n_rows):
    c_ = terms[0][0].shape[1]

    def add_all(*vals):
        acc = vals[0]
        for v in vals[1:]:
            acc = acc + v
        return (acc,)

    return _rowwise(name, add_all, [(a, c_, 0, first) for a, first in terms], [], [(c_, F32)], [], n_rows, tile=128)[0]


def _adamw_fn(w, g, m, v):
    m = ADAM_B1 * m + (1.0 - ADAM_B1) * g
    v = ADAM_B2 * v + (1.0 - ADAM_B2) * jnp.square(g)
    m_hat = m / (1.0 - ADAM_B1 ** ADAM_STEP)
    v_hat = v / (1.0 - ADAM_B2 ** ADAM_STEP)
    delta = -ADAM_LR * (m_hat / (jnp.sqrt(v_hat) + ADAM_EPS) + ADAM_WD * w)
    return delta, m, v


def _adamw(name, w, g, m, v):
    r, c_ = w.shape
    return _rowwise(name, _adamw_fn, [(w, c_, 0), (g, c_, 0), (m, c_, 0), (v, c_, 0)], [], [(c_, F32)] * 3, [], r, tile=128)


def _ada_fwd(c_all, w_ada, b_ada):
    n = w_ada.shape[1]

    def body(c_ref, w_ref, b_ref, o_ref):
        o_ref[...] = jnp.dot(jax.nn.silu(c_ref[...]).astype(BF16), w_ref[...].astype(BF16),
                             preferred_element_type=F32) + b_ref[...]

    return pl.pallas_call(body, name="ada_fwd", out_shape=jax.ShapeDtypeStruct((N_DEV, n), F32),
                          compiler_params=pltpu.CompilerParams(vmem_limit_bytes=VMEM_LIMIT))(c_all, w_ada, b_ada)


def _ada_bwd(c_all, dmod_cols):
    d = c_all.shape[1]
    n = dmod_cols.shape[1]

    def body(c_ref, dm_ref, gw_ref):
        gw_ref[...] = lax.dot_general(jax.nn.silu(c_ref[...]).astype(BF16), dm_ref[...].astype(BF16),
                                      (((0,), (0,)), ((), ())), preferred_element_type=F32)

    return pl.pallas_call(body, name="ada_bwd", out_shape=jax.ShapeDtypeStruct((d, n), F32),
                          compiler_params=pltpu.CompilerParams(vmem_limit_bytes=VMEM_LIMIT))(c_all, dmod_cols)


def _cast_bf16(name, a):
    r, c_ = a.shape
    return _rowwise(name, lambda v: (v,), [(a, c_, 0)], [], [(c_, BF16)], [], r)[0]


def _pad_rows(vec, mult):
    n = vec.shape[0]
    return jnp.pad(vec, (0, (-n) % mult))


def kernel(x, c, positions, w_ada, b_ada, w_in, b_in, conv_dw_w, conv_dw_b, conv_ln_g, conv_ln_b, w_conv_out, ret_gn_g, ret_gn_b, w_ret_out, w_out, ln1_g, ln1_b, w_up, ffn_dw_w, ffn_dw_b, w_down, ln2_g, ln2_b, loss_target, m_w_ada, m_b_ada, m_w_in, m_b_in, m_conv_dw_w, m_conv_dw_b, m_conv_ln_g, m_conv_ln_b, m_w_conv_out, m_ret_gn_g, m_ret_gn_b, m_w_ret_out, m_w_out, m_ln1_g, m_ln1_b, m_w_up, m_ffn_dw_w, m_ffn_dw_b, m_w_down, m_ln2_g, m_ln2_b, v_w_ada, v_b_ada, v_w_in, v_b_in, v_conv_dw_w, v_conv_dw_b, v_conv_ln_g, v_conv_ln_b, v_w_conv_out, v_ret_gn_g, v_ret_gn_b, v_w_ret_out, v_w_out, v_ln1_g, v_ln1_b, v_w_up, v_ffn_dw_w, v_ffn_dw_b, v_w_down, v_ln2_g, v_ln2_b):
    given = dict(locals())
    n_rows = x.shape[1]
    d = D_MODEL
    my_c = lax.axis_index("c")
    chip = 2 * lax.axis_index("x") + lax.axis_index("y")
    dev = 2 * chip + my_c
    xr = x[0]
    target = loss_target[0]
    vw = N_HEADS * DV
    ffw = 2 * D_FF

    def flat_rows(arrays, mult=8):
        v = jnp.concatenate([a.reshape(-1) for a in arrays])
        return _pad_rows(v, mult * d).reshape(-1, d)

    def unflatten(flat2d, shapes):
        v, out, o = flat2d.reshape(-1), [], 0
        for shp in shapes:
            size = 1
            for e in shp:
                size *= e
            out.append(v[o:o + size].reshape(shp))
            o += size
        return out

    w_bf = [_cast_bf16("cast_" + n, given[n][0]) for n in BIG_WEIGHTS]
    w_all = dict(zip(BIG_WEIGHTS, _allgather8("gather_weights", w_bf, own_half=True)))
    wg_in = w_all["w_in"].reshape(N_CHIPS, d, w_in.shape[2])
    wg_up = w_all["w_up"].reshape(N_CHIPS, d, w_up.shape[2])
    wg_conv_out = w_all["w_conv_out"].reshape(1, d, d)
    wg_ret_out = w_all["w_ret_out"].reshape(1, vw, d)
    wg_out = w_all["w_out"].reshape(1, d, d)
    wg_down = w_all["w_down"].reshape(1, D_FF, d)

    kc, kf = conv_dw_w.shape[2], ffn_dw_w.shape[2]
    small_all = _allgather8("gather_small", [flat_rows([c, conv_dw_w, ffn_dw_w])])[0].reshape(N_DEV, -1)
    c_all = small_all[:, :d]
    per_chip = small_all[0::2]
    conv_w = per_chip[:, d:d + CONV_K * kc].reshape(N_CHIPS, CONV_K, kc).transpose(1, 0, 2).reshape(CONV_K, N_CHIPS * kc)
    o_f = d + CONV_K * kc
    ffn_w = per_chip[:, o_f:o_f + FFN_K * kf].reshape(N_CHIPS, FFN_K, kf).transpose(1, 0, 2).reshape(FFN_K, N_CHIPS * kf)

    n_ada = w_ada.shape[2]
    b_ada_cols = lax.dynamic_slice_in_dim(b_ada, chip * n_ada, n_ada, 1)
    mod_cols = _ada_fwd(c_all, w_ada[0], b_ada_cols)
    mod_all = _allgather8("gather_mod", [mod_cols])[0]
    mod = lax.dynamic_index_in_dim(mod_all[0::2], dev, 1, keepdims=False).reshape(1, N_CHIPS * n_ada)
    shift1, scale1, gate1, shift2, scale2, gate2 = [mod[:, k * d:(k + 1) * d] for k in range(6)]

    h1 = _rowwise("ln_mod1", lambda a, s, t: (_f_modulate(a, s, t),), [(xr, d, 0)], [(scale1, d, 0), (shift1, d, 0)],
                  [(d, BF16)], [], n_rows)[0]
    proj = _mm_nn("mm_in", h1, wg_in, b_in, F32)
    a1 = _glu_conv_fwd(proj, 6, 7, conv_w, conv_dw_b, n_rows)
    a2 = _rowwise("conv_norm", lambda a, g, b: (_f_conv_norm(a, g, b),), [(a1, d, 0)], [(conv_ln_g, d, 0), (conv_ln_b, d, 0)],
                  [(d, BF16)], [], n_rows)[0]
    y_a = _mm_nn("mm_conv_out", a2, wg_conv_out, None, F32)
    cos_t, sin_t = _rope_tables(positions[0])
    consts = _retention_consts()
    r, states = _retention_fwd(proj, cos_t, sin_t, consts, n_rows)
    ret_rows = [(r, vw, 0), (proj, vw, 4096 // vw)]
    ret_vecs = [(ret_gn_g, vw, 0), (ret_gn_b, vw, 0)]
    r2 = _rowwise("ret_norm", _per_head(_f_group_norm_gate), ret_rows, ret_vecs, [(vw, BF16)], [], n_rows)[0]
    y_b = _mm_nn("mm_ret_out", r2, wg_ret_out, None, F32)
    merge_rows = [(proj, d, 8), (proj, d, 9), (y_a, d, 0), (y_b, d, 0)]
    m = _rowwise("merge", lambda *a: (_f_merge(*a),), merge_rows, [], [(d, BF16)], [], n_rows)[0]
    t = _mm_nn("mm_out", m, wg_out, None, F32)
    post1_vecs = [(gate1, d, 0), (ln1_g, d, 0), (ln1_b, d, 0), (scale2, d, 0), (shift2, d, 0)]
    x1, h2 = _rowwise("post1", _f_post1, [(xr, d, 0), (t, d, 0)], post1_vecs, [(d, F32), (d, BF16)], [], n_rows)
    pair_up = lambda c: (c % 2) * 2 + c // 2
    paired = lambda a: a.reshape(a.shape[0], N_CHIPS, kf)[:, jnp.array([0, 2, 1, 3])].reshape(a.shape[0], ffw)
    ffn_w_p, ffn_b_p = paired(ffn_w), paired(ffn_dw_b)
    u = _mm_nn("mm_up", h2, wg_up, None, BF16, gmap=pair_up)
    p = _ffn_fwd(u, ffn_w_p, ffn_b_p, n_rows)
    f = _mm_nn("mm_down", p, wg_down, None, F32)

    def loss_rows(x1v, fv, tv, g2v, lg, lb):
        loss, pull = jax.vjp(lambda a, b, c_, e, h: _f_loss(a, b, c_, e, h, tv), x1v, fv, g2v, lg, lb)
        return (*pull(jnp.ones((), F32)), jnp.full((1, 128), loss, F32))

    dx1_a, df, dgate2, dln2_g, dln2_b, loss_v = _rowwise(
        "loss", loss_rows, [(x1, d, 0), (f, d, 0), (target, d, 0)], [(gate2, d, 0), (ln2_g, d, 0), (ln2_b, d, 0)],
        [(d, F32), (d, BF16)], [d, d, d, 128], n_rows)
    loss = lax.psum(loss_v[0, 0], ("x", "y", "c"))

    dp = _mm_nt("mm_down_dx", df, wg_down, F32)
    gw_down = _mm_tn("mm_down_dw", p, df, 1)
    du, g_ffn_w, g_ffn_b = _ffn_bwd(u, dp, ffn_w_p, ffn_b_p, n_rows)
    g_ffn_w, g_ffn_b = paired(g_ffn_w), paired(g_ffn_b)
    dh2 = _mm_nt("mm_up_dx", du, wg_up, F32, gmap=pair_up)
    gw_up = _mm_tn("mm_up_dw", h2, du, N_CHIPS, gmap=pair_up)

    dx_a, dt, dgate1, dln1_g, dln1_b, dscale2, dshift2 = _rowwise(
        "post1_bwd", _vjp_rows(_f_post1, 2, 2), [(xr, d, 0), (t, d, 0), (dx1_a, d, 0), (dh2, d, 0)], post1_vecs,
        [(d, F32), (d, BF16)], [d] * 5, n_rows)
    dm = _mm_nt("mm_out_dx", dt, wg_out, F32)
    gw_out = _mm_tn("mm_out_dw", m, dt, 1)
    merge_vjp = _vjp_rows(_f_merge, 4, 1)

    def merge_bwd(*a):
        dga, dgb, dya, dyb = merge_vjp(*a)
        return jnp.concatenate([dga, dgb], axis=1), dya, dyb

    dproj, dya, dyb = _rowwise("merge_bwd", merge_bwd, merge_rows + [(dm, d, 0)], [],
                               [(2 * d, BF16, proj.shape[1], 8192 // (2 * d)), (d, BF16), (d, BF16)], [], n_rows)

    da2 = _mm_nt("mm_conv_out_dx", dya, wg_conv_out, F32)
    gw_conv_out = _mm_tn("mm_conv_out_dw", a2, dya, 1)
    da1, dcl_g, dcl_b = _rowwise("conv_norm_bwd", _vjp_rows(_f_conv_norm, 1, 1), [(a1, d, 0), (da2, d, 0)],
                                 [(conv_ln_g, d, 0), (conv_ln_b, d, 0)], [(d, F32)], [d, d], n_rows)
    dproj, g_conv_w, g_conv_b = _glu_conv_bwd(proj, 6, 7, da1, conv_w, dproj, 6144 // (2 * d), n_rows)

    dr2 = _mm_nt("mm_ret_out_dx", dyb, wg_ret_out, F32)
    gw_ret_out = _mm_tn("mm_ret_out_dw", r2, dyb, 1)

    def reduce_start(tag, names, fulls):
        shapes = [given[n].shape[1:] for n in names]
        fulls = [g_.reshape(N_CHIPS, r, c_) for g_, (r, c_) in zip(fulls, shapes)]
        theirs = _sibling_swap("grad_swap_" + tag, fulls)
        pair = []
        for n, (r, c_), g_, t_ in zip(names, shapes, fulls, theirs):
            mine = lax.dynamic_slice_in_dim(g_, my_c * (r // 2), r // 2, 1)
            pair.append(_sum_arrays("grad_pair_sum_" + n, [(mine.reshape(-1, c_), 0), (t_.reshape(-1, c_), 0)],
                                    N_CHIPS * (r // 2)).reshape(N_CHIPS, r // 2, c_))
        handle, token = _alltoall_start("grad_alltoall_start_" + tag, pair)
        own = [lax.dynamic_index_in_dim(p_, chip, 0, keepdims=False) for p_ in pair]
        return (names, shapes, handle, own), token[0:1, 0:1]

    def reduce_finish(tag, started, after):
        names, shapes, handle, own = started
        lands = _alltoall_wait("grad_alltoall_wait_" + tag, handle, after)
        return [_sum_arrays("grad_chip_sum_" + n, [(o_, 0)] + [(l_.reshape(-1, c_), k * (r // 2)) for k in range(3)], r // 2)
                for n, (r, c_), o_, l_ in zip(names, shapes, own, lands)]

    late = ("w_up", "w_down", "w_out", "w_conv_out", "w_ret_out")
    started_late, token_late = reduce_start("late", late, [gw_up, gw_down, gw_out, gw_conv_out, gw_ret_out])
    ret_vecs_bwd = [(ret_gn_g + token_late, vw, 0), (ret_gn_b, vw, 0)]
    dr, dproj, dgn_g, dgn_b = _rowwise(
        "ret_norm_bwd", _per_head(_vjp_rows(_f_group_norm_gate, 2, 1)), ret_rows + [(dr2, vw, 0)], ret_vecs_bwd,
        [(vw, F32), (vw, BF16, proj.shape[1], 4096 // vw)], [vw, vw], n_rows, into=(dproj, 1))
    dproj = _retention_bwd(proj, cos_t, sin_t, states, dr, consts, dproj, n_rows)

    gw_in, gb_in = _mm_tn("mm_in_dw", h1, dproj, N_CHIPS, colsum=True)
    started_in, token_in = reduce_start("in", ("w_in",), [gw_in])
    dh1 = _mm_nt("mm_in_dx", dproj, wg_in, F32)
    mod_bwd = _vjp_rows(_f_modulate, 1, 1)

    def mod1_bwd(xv, dhv, dxav, sv, tv):
        dx, ds, dsh = mod_bwd(xv, dhv, sv, tv)
        return dx + dxav, ds, dsh

    grad_x, dscale1, dshift1 = _rowwise("ln_mod1_bwd", mod1_bwd, [(xr, d, 0), (dh1, d, 0), (dx_a, d, 0)],
                                        [(scale1 + token_in, d, 0), (shift1, d, 0)], [(d, F32)], [d, d], n_rows)

    dmod = jnp.concatenate([dshift1, dscale1, dgate1, dshift2, dscale2, dgate2], axis=1)
    small_names = ["b_in", "conv_dw_w", "conv_dw_b", "conv_ln_g", "conv_ln_b", "ret_gn_g", "ret_gn_b", "ln1_g", "ln1_b",
                   "ffn_dw_w", "ffn_dw_b", "ln2_g", "ln2_b"]
    small_parts = [gb_in, g_conv_w, g_conv_b, dcl_g, dcl_b, dgn_g, dgn_b, dln1_g, dln1_b, g_ffn_w, g_ffn_b, dln2_g, dln2_b, dmod]
    small_shapes = [a.shape for a in small_parts]
    parts_all = _allgather8("gather_small_grads", [flat_rows(small_parts)])[0]
    part_rows = parts_all.shape[1]
    summed = _sum_arrays("sum_small_grads", [(parts_all.reshape(N_DEV * part_rows, d), k * part_rows) for k in range(N_DEV)],
                         part_rows)
    small_sum = unflatten(summed, small_shapes)
    grads = dict(zip(small_names, small_sum[:-1]))
    grads["b_ada"] = small_sum[-1]
    grads["conv_dw_w"] = lax.dynamic_slice_in_dim(grads["conv_dw_w"], chip * kc, kc, 1)
    grads["ffn_dw_w"] = lax.dynamic_slice_in_dim(grads["ffn_dw_w"], chip * kf, kf, 1)
    o_mod = sum(a.size for a in small_parts[:-1])
    dmod_all = parts_all.reshape(N_DEV, -1)[:, o_mod:o_mod + dmod.shape[1]]
    grads["w_ada"] = _ada_bwd(c_all, lax.dynamic_slice_in_dim(dmod_all, chip * n_ada, n_ada, 1))

    reduced = reduce_finish("late", started_late, grad_x) + reduce_finish("in", started_in, grad_x)
    big_order = late + ("w_in",)
    g_shards = [g_.reshape(given[n].shape[1:]) for n, g_ in zip(big_order, _sibling_gather("grad_gather", reduced))]

    outs = {}
    for n, g_ in zip(big_order, g_shards):
        upd = _adamw("adamw_" + n, given[n][0], g_, given["m_" + n][0], given["v_" + n][0])
        for prefix, val in zip(("grad_", "delta_", "new_m_", "new_v_"), (g_, *upd)):
            outs[prefix + n] = val.reshape(given[n].shape)
    ada = _adamw("adamw_ada", w_ada[0], grads["w_ada"], m_w_ada[0], v_w_ada[0])
    for prefix, val in zip(("grad_", "delta_", "new_m_", "new_v_"), (grads["w_ada"], *ada)):
        outs[prefix + "w_ada"] = val.reshape(w_ada.shape)
    small_all_names = ["b_ada"] + small_names
    small_w_shapes = [given[n].shape for n in small_all_names]
    g_small = flat_rows([grads[n] for n in small_all_names])
    small_upd = _adamw("adamw_small", flat_rows([given[n] for n in small_all_names]), g_small,
                       flat_rows([given["m_" + n] for n in small_all_names]), flat_rows([given["v_" + n] for n in small_all_names]))
    for prefix, packed in zip(("grad_", "delta_", "new_m_", "new_v_"), (g_small, *small_upd)):
        for n, val in zip(small_all_names, unflatten(packed, small_w_shapes)):
            outs[prefix + n] = val

    weights = ["w_ada", "b_ada", "w_in", "b_in", "conv_dw_w", "conv_dw_b", "conv_ln_g", "conv_ln_b", "w_conv_out", "ret_gn_g",
               "ret_gn_b", "w_ret_out", "w_out", "ln1_g", "ln1_b", "w_up", "ffn_dw_w", "ffn_dw_b", "w_down", "ln2_g", "ln2_b"]
    result = [loss, grad_x.reshape(x.shape)]
    for prefix in ("grad_", "delta_", "new_m_", "new_v_"):
        result += [outs[prefix + n] for n in weights]
    return tuple(result)
```

```python
import jax
import jax.numpy as jnp
from jax import lax
from jax.experimental import pallas as pl
from jax.experimental.pallas import tpu as pltpu

F32 = jnp.float32
BF16 = jnp.bfloat16
MESH = pl.DeviceIdType.MESH

D_MODEL = 1024
N_HEADS = 8
DK = 128
DV = 256
CHUNK = 128
ROPE_BASE = 10000.0
D_FF = 2816
CONV_K = 31
FFN_K = 3
LN_EPS = 1e-5
ALPHA = (2.0 * 1) ** 0.25
ADAM_LR = 0.001
ADAM_B1 = 0.9
ADAM_B2 = 0.999
ADAM_EPS = 1e-08
ADAM_WD = 0.01
ADAM_STEP = 10

V7X_VMEM_BYTES = 64 * 1024 * 1024
VMEM_LIMIT = V7X_VMEM_BYTES - 8 * 1024 * 1024
ROW_TILE = 256
MM_TILE = 512
N_CHIPS = 4
N_DEV = 8

BIG_WEIGHTS = ("w_in", "w_up", "w_conv_out", "w_ret_out", "w_out", "w_down")


def _params(n_grid):
    return pltpu.CompilerParams(dimension_semantics=("arbitrary",) * n_grid, vmem_limit_bytes=VMEM_LIMIT)


def _rowwise(name, fn, rows, vecs, outs, reds, n_rows, tile=ROW_TILE, ncol=1, with_col=False, into=None):
    tile = _fit_tile(n_rows, tile)
    n_in = len(rows) + len(vecs)
    n_ref_in = n_in + (into is not None)
    n_out = len(outs)
    outs = [o if len(o) == 4 else (o[0], o[1], o[0] * ncol, 0) for o in outs]

    def col_map(off, row, first_row=0):
        def index(j, i):
            return (i + first_row // tile if row else 0, off(j) if callable(off) else off + j)
        return index

    def body(*refs):
        i = pl.program_id(1)
        vals = [r[...].astype(F32) for r in refs[:n_in]]
        res = fn(pl.program_id(0), *vals) if with_col else fn(*vals)
        for k in range(n_out):
            refs[n_ref_in + k][...] = res[k].astype(refs[n_ref_in + k].dtype)
        for k in range(len(reds)):
            o = refs[n_ref_in + n_out + k]

            @pl.when(i == 0)
            def _():
                o[...] = jnp.zeros_like(o)

            o[...] += res[n_out + k]

    in_specs = [pl.BlockSpec((tile, e[1]), col_map(e[2], True, e[3] if len(e) > 3 else 0)) for e in rows]
    in_specs += [pl.BlockSpec((1, w), col_map(off, False)) for _, w, off in vecs]
    args = [e[0] for e in rows] + [a for a, _, _ in vecs]
    aliases = {}
    if into is not None:
        in_specs.append(pl.BlockSpec(memory_space=pl.ANY))
        args.append(into[0])
        aliases = {n_in: into[1]}
    out_specs = [pl.BlockSpec((tile, w), col_map(off, True)) for w, _, _, off in outs]
    out_specs += [pl.BlockSpec((1, w), lambda j, i: (0, j)) for w in reds]
    out_shape = [jax.ShapeDtypeStruct((n_rows, total), dt) for _, dt, total, _ in outs]
    out_shape += [jax.ShapeDtypeStruct((1, w * ncol), F32) for w in reds]
    return pl.pallas_call(
        body, name=name, grid=(ncol, n_rows // tile), in_specs=in_specs, out_specs=out_specs, out_shape=out_shape,
        input_output_aliases=aliases, compiler_params=_params(2),
    )(*args)


def _vjp_rows(fn, n_row_in, n_ct):
    def bwd(*args):
        prim = [a.astype(F32) for a in args[:n_row_in] + args[n_row_in + n_ct:]]
        cts = tuple(a.astype(F32) for a in args[n_row_in:n_row_in + n_ct])
        _, pull = jax.vjp(fn, *prim)
        return pull(cts if n_ct > 1 else cts[0])

    return bwd


def _fit_tile(n, pref):
    if n <= pref:
        return n
    t = pref - pref % 16
    while n % t:
        t -= 16
    return t


def _col_tile(n):
    return n if n <= 1536 else n // 2


def _same_group(c):
    return c


def _mm_nn(name, a, w, bias, out_dtype, gmap=_same_group, tile=None, window=None):
    s, k = a.shape
    g, _, n = w.shape
    tm, tn = (min(MM_TILE, s), _col_tile(n)) if tile is None else (min(tile[0], s), tile[1])
    nt = n // tn
    first, n_cols = (0, g * n) if window is None else window
    t0 = first // tn

    def body(*refs):
        a_ref, w_ref = refs[0], refs[1]
        o_ref = refs[-1]
        acc = jnp.dot(a_ref[...].astype(BF16), w_ref[...], preferred_element_type=F32)
        if bias is not None:
            acc = acc + refs[2][...]
        o_ref[...] = acc.astype(o_ref.dtype)

    in_specs = [pl.BlockSpec((tm, k), lambda c, i: (i, 0)),
                pl.BlockSpec((None, k, tn), lambda c, i: (gmap((t0 + c) // nt), 0, (t0 + c) % nt))]
    args = [a, w]
    if bias is not None:
        in_specs.append(pl.BlockSpec((1, tn), lambda c, i: (0, t0 + c)))
        args.append(bias)
    return pl.pallas_call(
        body, name=name, grid=(n_cols // tn, s // tm), in_specs=in_specs, out_specs=pl.BlockSpec((tm, tn), lambda c, i: (i, c)),
        out_shape=jax.ShapeDtypeStruct((s, n_cols), out_dtype), compiler_params=_params(2),
    )(*args)


def _mm_nt(name, dy, w, out_dtype, gmap=_same_group):
    s = dy.shape[0]
    g, k, n = w.shape
    tm, tn = min(MM_TILE, s), n
    nt = 1
    steps = g

    def body(dy_ref, w_ref, o_ref, acc_ref):
        r = pl.program_id(1)
        part = lax.dot_general(dy_ref[...].astype(BF16), w_ref[...], (((1,), (1,)), ((), ())), preferred_element_type=F32)

        @pl.when(r == 0)
        def _():
            acc_ref[...] = part

        @pl.when(r > 0)
        def _():
            acc_ref[...] += part

        @pl.when(r == steps - 1)
        def _():
            o_ref[...] = acc_ref[...].astype(o_ref.dtype)

    return pl.pallas_call(
        body, name=name, grid=(s // tm, steps),
        in_specs=[pl.BlockSpec((tm, tn), lambda i, r: (i, r)),
                  pl.BlockSpec((None, k, tn), lambda i, r: (gmap(r // nt), 0, r % nt))],
        out_specs=pl.BlockSpec((tm, k), lambda i, r: (i, 0)), out_shape=jax.ShapeDtypeStruct((s, k), out_dtype),
        scratch_shapes=[pltpu.VMEM((tm, k), F32)], compiler_params=_params(2),
    )(dy, w)


def _mm_tn(name, a, dy, g, gmap=_same_group, colsum=False):
    s, k = a.shape
    n = dy.shape[1] // g
    ts = min(2048 if k <= 1024 else 1024, s)
    tn = n if k * n <= 1024 * 1536 else (n // 2 if (n // 2) % 128 == 0 else n)
    nt = n // tn

    def body(a_ref, dy_ref, o_ref, *sum_ref):
        t = pl.program_id(1)
        dyv = dy_ref[...].astype(BF16)
        part = lax.dot_general(a_ref[...].astype(BF16), dyv, (((0,), (0,)), ((), ())), preferred_element_type=F32)

        @pl.when(t == 0)
        def _():
            o_ref[...] = part

        @pl.when(t > 0)
        def _():
            o_ref[...] += part

        if colsum:
            col = jnp.sum(dyv.astype(F32), axis=0, keepdims=True)

            @pl.when(t == 0)
            def _():
                sum_ref[0][...] = col

            @pl.when(t > 0)
            def _():
                sum_ref[0][...] += col

    out_specs = [pl.BlockSpec((None, k, tn), lambda c, t: (gmap(c // nt), 0, c % nt))]
    out_shape = [jax.ShapeDtypeStruct((g, k, n), F32)]
    if colsum:
        out_specs.append(pl.BlockSpec((1, tn), lambda c, t: (0, c)))
        out_shape.append(jax.ShapeDtypeStruct((1, g * n), F32))
    res = pl.pallas_call(
        body, name=name, grid=(g * nt, s // ts),
        in_specs=[pl.BlockSpec((ts, k), lambda c, t: (t, 0)), pl.BlockSpec((ts, tn), lambda c, t: (t, c))],
        out_specs=out_specs, out_shape=out_shape, compiler_params=_params(2),
    )(a, dy)
    return res if colsum else res[0]


SUBLANES = 8


def _tap_sum(read, w_row, offsets, tile):
    acc = None
    for b in range(SUBLANES):
        group = [(k, o) for k, o in offsets if o % SUBLANES == b]
        if not group:
            continue
        rows = tile if b == 0 else tile + SUBLANES
        z = None
        for k, o in group:
            term = w_row(k) * read(o - b, rows)
            z = term if z is None else z + term
        part = z if b == 0 else z[b:b + tile]
        acc = part if acc is None else acc + part
    return acc


def _tap_grads(read, dy, offsets, tile):
    padded = jnp.concatenate([dy, jnp.zeros((SUBLANES, dy.shape[1]), dy.dtype)], axis=0)
    out = {}
    for b in range(SUBLANES):
        group = [(k, o) for k, o in offsets if o % SUBLANES == b]
        if not group:
            continue
        shifted = dy if b == 0 else pltpu.roll(padded, b, 0)
        rows = tile if b == 0 else tile + SUBLANES
        for k, o in group:
            out[k] = jnp.sum(shifted * read(o - b, rows), axis=0, keepdims=True)
    return out


CONV_HALO = 32
CONV_TILE = 256
CONV_LANES = 256


def _glu_conv_fwd(proj, col_v, col_g, w, b, n_rows):
    kw, n_ch = w.shape
    tile = min(CONV_TILE, n_rows)
    per = tile // CONV_HALO
    offsets = [(k, CONV_HALO - (kw - 1) + k) for k in range(kw)]

    def body(v_ref, g_ref, vh_ref, gh_ref, w_ref, b_ref, o_ref, buf):
        i = pl.program_id(0)
        prev = vh_ref[...].astype(F32) * jax.nn.sigmoid(gh_ref[...].astype(F32))
        buf[0:CONV_HALO, :] = jnp.where(i == 0, 0.0, prev)
        buf[CONV_HALO:CONV_HALO + tile, :] = v_ref[...].astype(F32) * jax.nn.sigmoid(g_ref[...].astype(F32))
        for c0 in range(0, n_ch, CONV_LANES):
            cols = slice(c0, c0 + CONV_LANES)
            o_ref[:, cols] = b_ref[:, cols] + _tap_sum(lambda s, n: buf[pl.ds(s, n), cols], lambda k: w_ref[k:k + 1, cols],
                                                      offsets, tile)

    main = lambda col: pl.BlockSpec((tile, n_ch), lambda i: (i, col))
    halo = lambda col: pl.BlockSpec((CONV_HALO, n_ch), lambda i: (jnp.maximum(i * per - 1, 0), col))
    return pl.pallas_call(
        body, name="conv_fwd", grid=(n_rows // tile,),
        in_specs=[main(col_v), main(col_g), halo(col_v), halo(col_g), pl.BlockSpec((kw, n_ch), lambda i: (0, 0)),
                  pl.BlockSpec((1, n_ch), lambda i: (0, 0))],
        out_specs=pl.BlockSpec((tile, n_ch), lambda i: (i, 0)), out_shape=jax.ShapeDtypeStruct((n_rows, n_ch), F32),
        scratch_shapes=[pltpu.VMEM((CONV_HALO + tile, n_ch), F32)], compiler_params=_params(1),
    )(proj, proj, proj, proj, w, b)


def _glu_conv_bwd(proj, col_v, col_g, dy, w, dproj, col_out, n_rows):
    kw, n_ch = w.shape
    tile = min(CONV_TILE, n_rows)
    per = tile // CONV_HALO
    n_tiles = n_rows // tile
    last_halo = n_rows // CONV_HALO - 1
    offsets = [(k, CONV_HALO - (kw - 1) + k) for k in range(kw)]
    back = [(k, kw - 1 - k) for k in range(kw)]

    def body(v_ref, g_ref, vh_ref, gh_ref, dy_ref, dyn_ref, w_ref, dp_any, dx_ref, dw_ref, db_ref, buf, dbuf):
        i = pl.program_id(0)
        cv, cg = v_ref[...].astype(F32), g_ref[...].astype(F32)
        sig = jax.nn.sigmoid(cg)
        prev = vh_ref[...].astype(F32) * jax.nn.sigmoid(gh_ref[...].astype(F32))
        buf[0:CONV_HALO, :] = jnp.where(i == 0, 0.0, prev)
        buf[CONV_HALO:CONV_HALO + tile, :] = cv * sig
        dbuf[0:tile, :] = dy_ref[...]
        dbuf[tile:tile + CONV_HALO, :] = jnp.where(i == n_tiles - 1, 0.0, dyn_ref[...])

        @pl.when(i == 0)
        def _():
            dw_ref[...] = jnp.zeros_like(dw_ref)
            db_ref[...] = jnp.zeros_like(db_ref)

        db_ref[...] += jnp.sum(dy_ref[...], axis=0, keepdims=True)
        for c0 in range(0, n_ch, CONV_LANES):
            cols = slice(c0, c0 + CONV_LANES)
            w_row = lambda k: w_ref[k:k + 1, cols]
            dx = _tap_sum(lambda s, n: dbuf[pl.ds(s, n), cols], w_row, back, tile)
            grads = _tap_grads(lambda s, n: buf[pl.ds(s, n), cols], dy_ref[:, cols], offsets, tile)
            for k in range(kw):
                dw_ref[k:k + 1, cols] += grads[k]
            sg = sig[:, cols]
            dx_ref[:, c0:c0 + CONV_LANES] = (dx * sg).astype(dx_ref.dtype)
            dx_ref[:, n_ch + c0:n_ch + c0 + CONV_LANES] = (dx * cv[:, cols] * sg * (1.0 - sg)).astype(dx_ref.dtype)

    main = lambda col: pl.BlockSpec((tile, n_ch), lambda i: (i, col))
    halo = lambda col: pl.BlockSpec((CONV_HALO, n_ch), lambda i: (jnp.maximum(i * per - 1, 0), col))
    return pl.pallas_call(
        body, name="conv_bwd", grid=(n_tiles,),
        in_specs=[main(col_v), main(col_g), halo(col_v), halo(col_g), main(0),
                  pl.BlockSpec((CONV_HALO, n_ch), lambda i: (jnp.minimum((i + 1) * per, last_halo), 0)),
                  pl.BlockSpec((kw, n_ch), lambda i: (0, 0)), pl.BlockSpec(memory_space=pl.ANY)],
        out_specs=[pl.BlockSpec((tile, 2 * n_ch), lambda i: (i, col_out)), pl.BlockSpec((kw, n_ch), lambda i: (0, 0)),
                   pl.BlockSpec((1, n_ch), lambda i: (0, 0))],
        out_shape=[jax.ShapeDtypeStruct(dproj.shape, dproj.dtype), jax.ShapeDtypeStruct((kw, n_ch), F32),
                   jax.ShapeDtypeStruct((1, n_ch), F32)],
        input_output_aliases={7: 0},
        scratch_shapes=[pltpu.VMEM((CONV_HALO + tile, n_ch), F32), pltpu.VMEM((tile + CONV_HALO, n_ch), F32)],
        compiler_params=_params(1),
    )(proj, proj, proj, proj, dy, dy, w, dproj)


FFN_HALO = 8


def _ffn_fwd(u, w, b, n_rows):
    kw = w.shape[0]
    half = u.shape[1] // 4
    tile = min(ROW_TILE, n_rows)
    per = tile // FFN_HALO
    offsets = [(k, FFN_HALO - (kw - 1) + k) for k in range(kw)]

    def body(u_ref, uh_ref, w_ref, b_ref, p_ref, buf):
        i = pl.program_id(1)
        buf[0:FFN_HALO, :] = jnp.where(i == 0, 0.0, uh_ref[...].astype(F32))
        buf[FFN_HALO:FFN_HALO + tile, :] = u_ref[...].astype(F32)
        conv = []
        for c0 in (0, half):
            cols = slice(c0, c0 + half)
            conv.append(b_ref[:, cols] + _tap_sum(lambda s, n: buf[pl.ds(s, n), cols], lambda k: w_ref[k:k + 1, cols],
                                                  offsets, tile))
        p_ref[...] = (conv[0] * jax.nn.silu(conv[1])).astype(p_ref.dtype)

    return pl.pallas_call(
        body, name="ffn_fwd", grid=(2, n_rows // tile),
        in_specs=[pl.BlockSpec((tile, 2 * half), lambda j, i: (i, j)),
                  pl.BlockSpec((FFN_HALO, 2 * half), lambda j, i: (jnp.maximum(i * per - 1, 0), j)),
                  pl.BlockSpec((kw, 2 * half), lambda j, i: (0, j)), pl.BlockSpec((1, 2 * half), lambda j, i: (0, j))],
        out_specs=pl.BlockSpec((tile, half), lambda j, i: (i, j)), out_shape=jax.ShapeDtypeStruct((n_rows, 2 * half), BF16),
        scratch_shapes=[pltpu.VMEM((FFN_HALO + tile, 2 * half), F32)], compiler_params=_params(2),
    )(u, u, w, b)


def _ffn_bwd(u, dp, w, b, n_rows):
    kw = w.shape[0]
    half = u.shape[1] // 4
    tile = min(ROW_TILE, n_rows)
    per = tile // FFN_HALO
    n_tiles = n_rows // tile
    last_halo = n_rows // FFN_HALO - 1
    ext = tile + FFN_HALO
    offsets = [(k, FFN_HALO - (kw - 1) + k) for k in range(kw)]
    back = [(k, kw - 1 - k) for k in range(kw)]

    def body(u_ref, up_ref, un_ref, dp_ref, dpn_ref, w_ref, b_ref, du_ref, dw_ref, db_ref, buf, dbuf):
        i = pl.program_id(1)
        buf[0:FFN_HALO, :] = jnp.where(i == 0, 0.0, up_ref[...].astype(F32))
        buf[FFN_HALO:FFN_HALO + tile, :] = u_ref[...].astype(F32)
        buf[FFN_HALO + tile:FFN_HALO + ext, :] = un_ref[...].astype(F32)
        conv = []
        for c0 in (0, half):
            cols = slice(c0, c0 + half)
            conv.append(b_ref[:, cols] + _tap_sum(lambda s, n: buf[pl.ds(s, n), cols], lambda k: w_ref[k:k + 1, cols],
                                                  offsets, ext))
        val, gate = conv
        dpe = jnp.concatenate([dp_ref[...], jnp.where(i == n_tiles - 1, 0.0, dpn_ref[...])], axis=0)
        sig = jax.nn.sigmoid(gate)
        dbuf[:, 0:half] = dpe * gate * sig
        dbuf[:, half:2 * half] = dpe * val * sig * (1.0 + gate * (1.0 - sig))

        @pl.when(i == 0)
        def _():
            dw_ref[...] = jnp.zeros_like(dw_ref)
            db_ref[...] = jnp.zeros_like(db_ref)

        for c0 in (0, half):
            cols = slice(c0, c0 + half)
            du_ref[:, cols] = _tap_sum(lambda s, n: dbuf[pl.ds(s, n), cols], lambda k: w_ref[k:k + 1, cols], back,
                                       tile).astype(du_ref.dtype)
            d_main = dbuf[0:tile, cols]
            db_ref[:, cols] += jnp.sum(d_main, axis=0, keepdims=True)
            grads = _tap_grads(lambda s, n: buf[pl.ds(s, n), cols], d_main, offsets, tile)
            for k in range(kw):
                dw_ref[k:k + 1, cols] += grads[k]

    wide = 2 * half
    return pl.pallas_call(
        body, name="ffn_bwd", grid=(2, n_tiles),
        in_specs=[pl.BlockSpec((tile, wide), lambda j, i: (i, j)),
                  pl.BlockSpec((FFN_HALO, wide), lambda j, i: (jnp.maximum(i * per - 1, 0), j)),
                  pl.BlockSpec((FFN_HALO, wide), lambda j, i: (jnp.minimum((i + 1) * per, last_halo), j)),
                  pl.BlockSpec((tile, half), lambda j, i: (i, j)),
                  pl.BlockSpec((FFN_HALO, half), lambda j, i: (jnp.minimum((i + 1) * per, last_halo), j)),
                  pl.BlockSpec((kw, wide), lambda j, i: (0, j)), pl.BlockSpec((1, wide), lambda j, i: (0, j))],
        out_specs=[pl.BlockSpec((tile, wide), lambda j, i: (i, j)), pl.BlockSpec((kw, wide), lambda j, i: (0, j)),
                   pl.BlockSpec((1, wide), lambda j, i: (0, j))],
        out_shape=[jax.ShapeDtypeStruct(u.shape, BF16), jax.ShapeDtypeStruct((kw, u.shape[1]), F32),
                   jax.ShapeDtypeStruct((1, u.shape[1]), F32)],
        scratch_shapes=[pltpu.VMEM((FFN_HALO + ext, wide), F32), pltpu.VMEM((ext, wide), F32)], compiler_params=_params(2),
    )(u, u, u, dp, dp, w, b)


def _retention_consts():
    log_gamma = jnp.log(1.0 - 2.0 ** (-5.0 - jnp.arange(N_HEADS, dtype=F32)))
    idx = jnp.arange(CHUNK, dtype=F32)
    rel = idx[:, None] - idx[None, :]
    decay = jnp.where(rel[None] >= 0, jnp.exp(log_gamma[:, None, None] * jnp.maximum(rel, 0.0)[None]), 0.0)
    zeta = jnp.exp(log_gamma[:, None] * (CHUNK - 1.0 - idx)[None])
    xi = jnp.exp(log_gamma[:, None] * (idx + 1.0)[None])
    chunk_decay = jnp.exp(log_gamma * CHUNK)
    xi_b = jnp.broadcast_to(xi[:, :, None], (N_HEADS, CHUNK, DK))
    zeta_b = jnp.broadcast_to(zeta[:, :, None], (N_HEADS, CHUNK, DK))
    cd_b = jnp.broadcast_to(chunk_decay[:, None, None], (N_HEADS, 8, DV))
    return decay, xi_b, zeta_b, cd_b


def _rope_tables(positions):
    half = DK // 2
    inv_freq = ROPE_BASE ** (-jnp.arange(half, dtype=F32) / half)
    ang = positions.astype(F32)[:, None] * inv_freq
    cos, sin = jnp.cos(ang), jnp.sin(ang)
    return jnp.concatenate([cos, cos], axis=-1), jnp.concatenate([-sin, sin], axis=-1)


def _swap_halves(v):
    return pltpu.roll(v, DK // 2, 1)


def _dot(a, b):
    return jnp.dot(a, b, preferred_element_type=F32)


def _dot_nt(a, b):
    return lax.dot_general(a, b, (((1,), (1,)), ((), ())), preferred_element_type=F32)


def _dot_tn(a, b):
    return lax.dot_general(a, b, (((0,), (0,)), ((), ())), preferred_element_type=F32)


def _const_specs():
    return [pl.BlockSpec((N_HEADS, CHUNK, CHUNK), lambda n: (0, 0, 0)), pl.BlockSpec((N_HEADS, CHUNK, DK), lambda n: (0, 0, 0)),
            pl.BlockSpec((N_HEADS, CHUNK, DK), lambda n: (0, 0, 0)), pl.BlockSpec((N_HEADS, 8, DV), lambda n: (0, 0, 0))]


def _retention_fwd(proj_qk, proj_rest, cos_t, sin_t, consts, n_rows):
    n_chunks = n_rows // CHUNK
    scale = DK ** -0.5

    def body(q_ref, k_ref, v_ref, cf_ref, ss_ref, d_ref, xi_ref, zt_ref, cd_ref, r_ref, st_ref, state):
        @pl.when(pl.program_id(0) == 0)
        def _():
            state[...] = jnp.zeros_like(state)

        cf, ss = cf_ref[...], ss_ref[...]
        for h in range(N_HEADS):
            qh = q_ref[:, h * DK:(h + 1) * DK].astype(F32)
            kh = k_ref[:, h * DK:(h + 1) * DK].astype(F32)
            qh = qh * cf + _swap_halves(qh) * ss
            kh = (kh * cf + _swap_halves(kh) * ss) * scale
            vh = v_ref[:, h * DV:(h + 1) * DV].astype(BF16)
            st = state[h]
            st_ref[0, h] = st
            sd = _dot_nt(qh.astype(BF16), kh.astype(BF16)) * d_ref[h]
            inner = _dot(sd.astype(BF16), vh)
            cross = _dot((qh * xi_ref[h]).astype(BF16), st.astype(BF16))
            kv = _dot_tn((kh * zt_ref[h]).astype(BF16), vh)
            state[h] = st * cd_ref[h, 0:1, :] + kv
            r_ref[:, h * DV:(h + 1) * DV] = inner + cross

    qk = N_HEADS * DK
    vw = N_HEADS * DV
    return pl.pallas_call(
        body, name="retention_fwd", grid=(n_chunks,),
        in_specs=[pl.BlockSpec((CHUNK, qk), lambda n: (n, 0)), pl.BlockSpec((CHUNK, qk), lambda n: (n, 1)),
                  pl.BlockSpec((CHUNK, vw), lambda n: (n, 0)), pl.BlockSpec((CHUNK, DK), lambda n: (n, 0)),
                  pl.BlockSpec((CHUNK, DK), lambda n: (n, 0))] + _const_specs(),
        out_specs=[pl.BlockSpec((CHUNK, vw), lambda n: (n, 0)), pl.BlockSpec((1, N_HEADS, DK, DV), lambda n: (n, 0, 0, 0))],
        out_shape=[jax.ShapeDtypeStruct((n_rows, vw), F32), jax.ShapeDtypeStruct((n_chunks, N_HEADS, DK, DV), F32)],
        scratch_shapes=[pltpu.VMEM((N_HEADS, DK, DV), F32)], compiler_params=_params(1),
    )(proj_qk, proj_qk, proj_rest, cos_t, sin_t, *consts)


def _retention_bwd(proj_qk, proj_rest, cos_t, sin_t, states, dr, consts, dproj, n_rows):
    n_chunks = n_rows // CHUNK
    scale = DK ** -0.5
    qk = N_HEADS * DK
    vw = N_HEADS * DV

    def body(q_ref, k_ref, v_ref, cf_ref, ss_ref, st_ref, dr_ref, d_ref, xi_ref, zt_ref, cd_ref, dp_any, dqkv_ref, g_ref):
        dq_ref, dk_ref, dv_ref = dqkv_ref.at[:, 0:qk], dqkv_ref.at[:, qk:2 * qk], dqkv_ref.at[:, 2 * qk:2 * qk + vw]

        @pl.when(pl.program_id(0) == 0)
        def _():
            g_ref[...] = jnp.zeros_like(g_ref)

        cf, ss = cf_ref[...], ss_ref[...]
        for h in range(N_HEADS):
            qh = q_ref[:, h * DK:(h + 1) * DK].astype(F32)
            kh = k_ref[:, h * DK:(h + 1) * DK].astype(F32)
            qh = qh * cf + _swap_halves(qh) * ss
            kh = (kh * cf + _swap_halves(kh) * ss) * scale
            qb, kb = qh.astype(BF16), kh.astype(BF16)
            vh = v_ref[:, h * DV:(h + 1) * DV].astype(BF16)
            do = dr_ref[:, h * DV:(h + 1) * DV].astype(BF16)
            rb = st_ref[0, h].astype(BF16)
            g = g_ref[h]
            gb = g.astype(BF16)
            dec, xi, zt = d_ref[h], xi_ref[h], zt_ref[h]
            sd = (_dot_nt(qb, kb) * dec).astype(BF16)
            ds = (_dot_nt(do, vh) * dec).astype(BF16)
            dqh = _dot(ds, kb) + _dot_nt(do, rb) * xi
            dkh = (_dot_tn(ds, qb) + _dot_nt(vh, gb) * zt) * scale
            dvh = _dot_tn(sd, do) + _dot((kh * zt).astype(BF16), gb)
            g_ref[h] = g * cd_ref[h, 0:1, :] + _dot_tn((qh * xi).astype(BF16), do)
            dq_ref[:, h * DK:(h + 1) * DK] = (dqh * cf + _swap_halves(dqh * ss)).astype(dq_ref.dtype)
            dk_ref[:, h * DK:(h + 1) * DK] = (dkh * cf + _swap_halves(dkh * ss)).astype(dk_ref.dtype)
            dv_ref[:, h * DV:(h + 1) * DV] = dvh.astype(dv_ref.dtype)

    last = n_chunks - 1
    return pl.pallas_call(
        body, name="retention_bwd", grid=(n_chunks,),
        in_specs=[pl.BlockSpec((CHUNK, qk), lambda n: (last - n, 0)), pl.BlockSpec((CHUNK, qk), lambda n: (last - n, 1)),
                  pl.BlockSpec((CHUNK, vw), lambda n: (last - n, 0)), pl.BlockSpec((CHUNK, DK), lambda n: (last - n, 0)),
                  pl.BlockSpec((CHUNK, DK), lambda n: (last - n, 0)),
                  pl.BlockSpec((1, N_HEADS, DK, DV), lambda n: (last - n, 0, 0, 0)),
                  pl.BlockSpec((CHUNK, vw), lambda n: (last - n, 0))] + _const_specs() + [pl.BlockSpec(memory_space=pl.ANY)],
        out_specs=pl.BlockSpec((CHUNK, 2 * qk + vw), lambda n: (last - n, 0)),
        out_shape=jax.ShapeDtypeStruct(dproj.shape, dproj.dtype), input_output_aliases={11: 0},
        scratch_shapes=[pltpu.VMEM((N_HEADS, DK, DV), F32)], compiler_params=_params(1),
    )(proj_qk, proj_qk, proj_rest, cos_t, sin_t, states, dr, *consts, dproj)


def _ln(v):
    mu = jnp.mean(v, axis=-1, keepdims=True)
    var = jnp.mean(jnp.square(v - mu), axis=-1, keepdims=True)
    return (v - mu) * lax.rsqrt(var + LN_EPS)


def _f_modulate(x, scale, shift):
    return _ln(x) * (1.0 + scale) + shift


def _f_conv_norm(a1, g, b):
    return jax.nn.silu(_ln(a1) * g + b)


def _f_group_norm_gate(r, gate, g, b):
    return (_ln(r) * g + b) * jax.nn.silu(gate)


def _per_head(fn):
    def run(*arrays):
        parts = [fn(*[a[:, h * DV:(h + 1) * DV] for a in arrays]) for h in range(N_HEADS)]
        if not isinstance(parts[0], (tuple, list)):
            return (jnp.concatenate(parts, axis=1),)
        return tuple(jnp.concatenate([p[k] for p in parts], axis=1) for k in range(len(parts[0])))

    return run


def _f_merge(ga, gb, ya, yb):
    return jax.nn.sigmoid(ga) * ya + jax.nn.sigmoid(gb) * yb


def _f_post1(x, t, gate1, g1, b1, scale2, shift2):
    x1 = _ln(ALPHA * x + gate1 * t) * g1 + b1
    return x1, _ln(x1) * (1.0 + scale2) + shift2


def _f_loss(x1, f, gate2, g2, b2, target):
    y = _ln(ALPHA * x1 + gate2 * f) * g2 + b2
    return 0.5 * jnp.sum(jnp.mean(jnp.square(y - target), axis=-1))


ANY = pl.BlockSpec(memory_space=pl.ANY)


def _allgather8(name, blocks, own_half=False):
    n = len(blocks)
    rows = [b.shape[0] // 2 if own_half else b.shape[0] for b in blocks]

    def body(*refs):
        x_refs, out_refs = refs[:n], refs[n:2 * n]
        send_sems, recv_sems = refs[2 * n:]
        x, y, c = lax.axis_index("x"), lax.axis_index("y"), lax.axis_index("c")
        me, sibling = (x, y, c), (x, y, 1 - c)
        chips = [(1 - x, y), (x, 1 - y), (1 - x, 1 - y)]
        every = range(n)

        def src(a):
            return x_refs[a].at[pl.ds(c * rows[a], rows[a])] if own_half else x_refs[a]

        def slot(a, px, py, pc):
            return out_refs[a].at[4 * px + 2 * py + pc]

        def copy(k, a, block, to, from_input=False):
            return pltpu.make_async_remote_copy(
                src_ref=src(a) if from_input else slot(a, *block), dst_ref=slot(a, *block), send_sem=send_sems.at[k, a],
                recv_sem=recv_sems.at[k, a], device_id=to, device_id_type=MESH)

        first = [copy(0, a, me, sibling, True) for a in every]
        first += [copy(1 + j, a, me, (*chip, c), True) for j, chip in enumerate(chips) for a in every]
        for cp in first:
            cp.start()
        passed = [[copy(4 + j, a, (*chip, c), sibling) for a in every] for j, chip in enumerate(chips)]
        for j, chip in enumerate(chips):
            for a in every:
                copy(1 + j, a, (*chip, c), me).wait_recv()
            for cp in passed[j]:
                cp.start()
        for a in every:
            copy(0, a, sibling, me).wait_recv()
        for j, chip in enumerate(chips):
            for a in every:
                copy(4 + j, a, (*chip, 1 - c), me).wait_recv()
        for cp in first + [cp for group in passed for cp in group]:
            cp.wait_send()

    gathered = pl.pallas_call(
        body, name=name, in_specs=[ANY] * n, out_specs=[ANY] * n,
        out_shape=[jax.ShapeDtypeStruct((N_DEV, r, b.shape[1]), b.dtype) for r, b in zip(rows, blocks)],
        scratch_shapes=[pltpu.SemaphoreType.DMA((7, n)), pltpu.SemaphoreType.DMA((7, n))],
    )(*blocks)
    c = lax.axis_index("c")
    me = 4 * lax.axis_index("x") + 2 * lax.axis_index("y") + c
    own = [lax.dynamic_slice_in_dim(b, c * r, r, 0) if own_half else b for r, b in zip(rows, blocks)]
    return [lax.dynamic_update_slice_in_dim(g_, o_[None], me, 0) for g_, o_ in zip(gathered, own)]


def _sibling_swap(name, arrays):
    n = len(arrays)
    halves = [a.shape[1] // 2 for a in arrays]

    def body(*refs):
        g_refs, theirs = refs[:n], refs[n:2 * n]
        send_sems, recv_sems = refs[2 * n:]
        x, y, c = lax.axis_index("x"), lax.axis_index("y"), lax.axis_index("c")
        remote = [pltpu.make_async_remote_copy(
            src_ref=g_refs[a].at[:, pl.ds((1 - c) * halves[a], halves[a]), :], dst_ref=theirs[a], send_sem=send_sems.at[a],
            recv_sem=recv_sems.at[a], device_id=(x, y, 1 - c), device_id_type=MESH) for a in range(n)]
        for cp in remote:
            cp.start()
        for cp in remote:
            cp.wait_recv()
        for cp in remote:
            cp.wait_send()

    return pl.pallas_call(
        body, name=name, in_specs=[ANY] * n, out_specs=[ANY] * n,
        out_shape=[jax.ShapeDtypeStruct((a.shape[0], h, a.shape[2]), a.dtype) for a, h in zip(arrays, halves)],
        scratch_shapes=[pltpu.SemaphoreType.DMA((n,)), pltpu.SemaphoreType.DMA((n,))],
    )(*arrays)


HBM = pl.BlockSpec(memory_space=pltpu.HBM)
SEM = pl.BlockSpec(memory_space=pltpu.SEMAPHORE)
DATAFLOW = pltpu.SideEffectType.DATAFLOW_SIDE_EFFECTING


def _chip_peers(x, y):
    return [(1 - x, y), (x, 1 - y), (1 - x, 1 - y)]


def _alltoall_start(name, arrays):
    n = len(arrays)
    lands = [lax.empty((3,) + a.shape[1:], a.dtype) for a in arrays]

    def body(*refs):
        p_refs, land_refs = refs[:n], refs[n:2 * n]
        send_sems, recv_sems = refs[2 * n], refs[2 * n + 1]
        token = refs[-1]
        x, y, c = lax.axis_index("x"), lax.axis_index("y"), lax.axis_index("c")
        for k, (px, py) in enumerate(_chip_peers(x, y)):
            for a in range(n):
                pltpu.make_async_remote_copy(
                    src_ref=p_refs[a].at[2 * px + py], dst_ref=land_refs[a].at[k], send_sem=send_sems.at[k * n + a],
                    recv_sem=recv_sems.at[k * n + a], device_id=(px, py, c), device_id_type=MESH).start()
        token[...] = jnp.zeros_like(token)

    thru = [pltpu.HBM(a.shape, a.dtype) for a in arrays] + [pltpu.HBM(l.shape, l.dtype) for l in lands]
    res = pl.pallas_call(
        body, name=name, in_specs=[HBM] * (2 * n),
        out_specs=(SEM, SEM, *[HBM] * (2 * n), pl.BlockSpec(memory_space=pltpu.VMEM)),
        out_shape=(pltpu.SemaphoreType.DMA((3 * n,)), pltpu.SemaphoreType.DMA((3 * n,)), *thru, jax.ShapeDtypeStruct((8, 128), F32)),
        input_output_aliases={i: 2 + i for i in range(2 * n)},
        compiler_params=pltpu.CompilerParams(has_side_effects=DATAFLOW),
    )(*[pltpu.with_memory_space_constraint(a, pltpu.HBM) for a in arrays],
      *[pltpu.with_memory_space_constraint(l, pltpu.HBM) for l in lands])
    return (res[0], res[1], list(res[2:2 + n]), list(res[2 + n:2 + 2 * n])), res[-1]


def _alltoall_wait(name, handle, after):
    send_sems, recv_sems, sources, lands = handle
    n = len(sources)

    def body(*refs):
        p_refs, land_refs = refs[:n], refs[n:2 * n]
        send_sems, recv_sems = refs[2 * n], refs[2 * n + 1]
        x, y, c = lax.axis_index("x"), lax.axis_index("y"), lax.axis_index("c")
        for k, (px, py) in enumerate(_chip_peers(x, y)):
            for a in range(n):
                cp = pltpu.make_async_remote_copy(
                    src_ref=p_refs[a].at[2 * px + py], dst_ref=land_refs[a].at[k], send_sem=send_sems.at[k * n + a],
                    recv_sem=recv_sems.at[k * n + a], device_id=(px, py, c), device_id_type=MESH)
                cp.wait_send()
                cp.wait_recv()

    res = pl.pallas_call(
        body, name=name, in_specs=[HBM] * (2 * n) + [SEM, SEM, ANY], out_specs=[HBM] * (2 * n),
        out_shape=[pltpu.HBM(a.shape, a.dtype) for a in sources + lands],
        input_output_aliases={i: i for i in range(2 * n)}, compiler_params=pltpu.CompilerParams(has_side_effects=DATAFLOW),
    )(*sources, *lands, send_sems, recv_sems, after)
    return list(res[n:])


def _sibling_exchange(name, halves):
    n = len(halves)

    def body(*refs):
        h_refs, out_refs = refs[:n], refs[n:2 * n]
        send_sems, recv_sems = refs[2 * n:]
        x, y, c = lax.axis_index("x"), lax.axis_index("y"), lax.axis_index("c")
        remote = [pltpu.make_async_remote_copy(
            src_ref=h_refs[a], dst_ref=out_refs[a], send_sem=send_sems.at[a], recv_sem=recv_sems.at[a],
            device_id=(x, y, 1 - c), device_id_type=MESH) for a in range(n)]
        for cp in remote:
            cp.start()
        for cp in remote:
            cp.wait_recv()
        for cp in remote:
            cp.wait_send()

    return pl.pallas_call(
        body, name=name, in_specs=[ANY] * n, out_specs=[ANY] * n,
        out_shape=[jax.ShapeDtypeStruct(h.shape, h.dtype) for h in halves],
        scratch_shapes=[pltpu.SemaphoreType.DMA((n,)), pltpu.SemaphoreType.DMA((n,))],
    )(*halves)


def _sum_arrays(name, terms, n_rows):
    c_ = terms[0][0].shape[1]

    def add_all(*vals):
        acc = vals[0]
        for v in vals[1:]:
            acc = acc + v
        return (acc,)

    return _rowwise(name, add_all, [(a, c_, 0, first) for a, first in terms], [], [(c_, F32)], [], n_rows, tile=128)[0]


def _adamw_fn(w, g, m, v):
    m = ADAM_B1 * m + (1.0 - ADAM_B1) * g
    v = ADAM_B2 * v + (1.0 - ADAM_B2) * jnp.square(g)
    m_hat = m / (1.0 - ADAM_B1 ** ADAM_STEP)
    v_hat = v / (1.0 - ADAM_B2 ** ADAM_STEP)
    delta = -ADAM_LR * (m_hat / (jnp.sqrt(v_hat) + ADAM_EPS) + ADAM_WD * w)
    return delta, m, v


def _adamw(name, w, g, m, v):
    r, c_ = w.shape
    return _rowwise(name, _adamw_fn, [(w, c_, 0), (g, c_, 0), (m, c_, 0), (v, c_, 0)], [], [(c_, F32)] * 3, [], r, tile=128)


def _ada_fwd(c_all, w_ada, b_ada):
    n = w_ada.shape[1]

    def body(c_ref, w_ref, b_ref, o_ref):
        o_ref[...] = jnp.dot(jax.nn.silu(c_ref[...]).astype(BF16), w_ref[...].astype(BF16),
                             preferred_element_type=F32) + b_ref[...]

    return pl.pallas_call(body, name="ada_fwd", out_shape=jax.ShapeDtypeStruct((N_DEV, n), F32),
                          compiler_params=pltpu.CompilerParams(vmem_limit_bytes=VMEM_LIMIT))(c_all, w_ada, b_ada)


def _ada_bwd(c_all, dmod_cols):
    d = c_all.shape[1]
    n = dmod_cols.shape[1]

    def body(c_ref, dm_ref, gw_ref):
        gw_ref[...] = lax.dot_general(jax.nn.silu(c_ref[...]).astype(BF16), dm_ref[...].astype(BF16),
                                      (((0,), (0,)), ((), ())), preferred_element_type=F32)

    return pl.pallas_call(body, name="ada_bwd", out_shape=jax.ShapeDtypeStruct((d, n), F32),
                          compiler_params=pltpu.CompilerParams(vmem_limit_bytes=VMEM_LIMIT))(c_all, dmod_cols)


def _cast_bf16(name, a):
    r, c_ = a.shape
    return _rowwise(name, lambda v: (v,), [(a, c_, 0)], [], [(c_, BF16)], [], r)[0]


def _pad_rows(vec, mult):
    n = vec.shape[0]
    return jnp.pad(vec, (0, (-n) % mult))


def kernel(x, c, positions, w_ada, b_ada, w_in, b_in, conv_dw_w, conv_dw_b, conv_ln_g, conv_ln_b, w_conv_out, ret_gn_g, ret_gn_b, w_ret_out, w_out, ln1_g, ln1_b, w_up, ffn_dw_w, ffn_dw_b, w_down, ln2_g, ln2_b, loss_target, m_w_ada, m_b_ada, m_w_in, m_b_in, m_conv_dw_w, m_conv_dw_b, m_conv_ln_g, m_conv_ln_b, m_w_conv_out, m_ret_gn_g, m_ret_gn_b, m_w_ret_out, m_w_out, m_ln1_g, m_ln1_b, m_w_up, m_ffn_dw_w, m_ffn_dw_b, m_w_down, m_ln2_g, m_ln2_b, v_w_ada, v_b_ada, v_w_in, v_b_in, v_conv_dw_w, v_conv_dw_b, v_conv_ln_g, v_conv_ln_b, v_w_conv_out, v_ret_gn_g, v_ret_gn_b, v_w_ret_out, v_w_out, v_ln1_g, v_ln1_b, v_w_up, v_ffn_dw_w, v_ffn_dw_b, v_w_down, v_ln2_g, v_ln2_b):
    given = dict(locals())
    n_rows = x.shape[1]
    d = D_MODEL
    my_c = lax.axis_index("c")
    chip = 2 * lax.axis_index("x") + lax.axis_index("y")
    dev = 2 * chip + my_c
    xr = x[0]
    target = loss_target[0]
    vw = N_HEADS * DV
    ffw = 2 * D_FF

    def flat_rows(arrays, mult=8):
        v = jnp.concatenate([a.reshape(-1) for a in arrays])
        return _pad_rows(v, mult * d).reshape(-1, d)

    def unflatten(flat2d, shapes):
        v, out, o = flat2d.reshape(-1), [], 0
        for shp in shapes:
            size = 1
            for e in shp:
                size *= e
            out.append(v[o:o + size].reshape(shp))
            o += size
        return out

    w_bf = [_cast_bf16("cast_" + n, given[n][0]) for n in BIG_WEIGHTS]
    w_all = dict(zip(BIG_WEIGHTS, _allgather8("gather_weights", w_bf, own_half=True)))
    wg_in = w_all["w_in"].reshape(N_CHIPS, d, w_in.shape[2])
    wg_up = w_all["w_up"].reshape(N_CHIPS, d, w_up.shape[2])
    wg_conv_out = w_all["w_conv_out"].reshape(1, d, d)
    wg_ret_out = w_all["w_ret_out"].reshape(1, vw, d)
    wg_out = w_all["w_out"].reshape(1, d, d)
    wg_down = w_all["w_down"].reshape(1, D_FF, d)

    kc, kf = conv_dw_w.shape[2], ffn_dw_w.shape[2]
    small_all = _allgather8("gather_small", [flat_rows([c, conv_dw_w, ffn_dw_w])])[0].reshape(N_DEV, -1)
    c_all = small_all[:, :d]
    per_chip = small_all[0::2]
    conv_w = per_chip[:, d:d + CONV_K * kc].reshape(N_CHIPS, CONV_K, kc).transpose(1, 0, 2).reshape(CONV_K, N_CHIPS * kc)
    o_f = d + CONV_K * kc
    ffn_w = per_chip[:, o_f:o_f + FFN_K * kf].reshape(N_CHIPS, FFN_K, kf).transpose(1, 0, 2).reshape(FFN_K, N_CHIPS * kf)

    n_ada = w_ada.shape[2]
    b_ada_cols = lax.dynamic_slice_in_dim(b_ada, chip * n_ada, n_ada, 1)
    mod_cols = _ada_fwd(c_all, w_ada[0], b_ada_cols)
    mod_all = _allgather8("gather_mod", [mod_cols])[0]
    mod = lax.dynamic_index_in_dim(mod_all[0::2], dev, 1, keepdims=False).reshape(1, N_CHIPS * n_ada)
    shift1, scale1, gate1, shift2, scale2, gate2 = [mod[:, k * d:(k + 1) * d] for k in range(6)]

    h1 = _rowwise("ln_mod1", lambda a, s, t: (_f_modulate(a, s, t),), [(xr, d, 0)], [(scale1, d, 0), (shift1, d, 0)],
                  [(d, BF16)], [], n_rows)[0]
    n_proj = N_CHIPS * w_in.shape[2]
    proj_qk = _mm_nn("mm_in_qk", h1, wg_in, b_in, F32, tile=(1024, 512), window=(0, 2 * d))
    proj_rest = _mm_nn("mm_in_rest", h1, wg_in, b_in, BF16, tile=(1024, 512), window=(2 * d, n_proj - 2 * d))
    a1 = _glu_conv_fwd(proj_rest, 4, 5, conv_w, conv_dw_b, n_rows)
    a2 = _rowwise("conv_norm", lambda a, g, b: (_f_conv_norm(a, g, b),), [(a1, d, 0)], [(conv_ln_g, d, 0), (conv_ln_b, d, 0)],
                  [(d, BF16)], [], n_rows)[0]
    y_a = _mm_nn("mm_conv_out", a2, wg_conv_out, None, F32)
    cos_t, sin_t = _rope_tables(positions[0])
    consts = _retention_consts()
    r, states = _retention_fwd(proj_qk, proj_rest, cos_t, sin_t, consts, n_rows)
    ret_rows = [(r, vw, 0), (proj_rest, vw, 1)]
    ret_vecs = [(ret_gn_g, vw, 0), (ret_gn_b, vw, 0)]
    r2 = _rowwise("ret_norm", _per_head(_f_group_norm_gate), ret_rows, ret_vecs, [(vw, BF16)], [], n_rows)[0]
    y_b = _mm_nn("mm_ret_out", r2, wg_ret_out, None, F32)
    merge_rows = [(proj_rest, d, 6), (proj_rest, d, 7), (y_a, d, 0), (y_b, d, 0)]
    m = _rowwise("merge", lambda *a: (_f_merge(*a),), merge_rows, [], [(d, BF16)], [], n_rows)[0]
    t = _mm_nn("mm_out", m, wg_out, None, F32)
    post1_vecs = [(gate1, d, 0), (ln1_g, d, 0), (ln1_b, d, 0), (scale2, d, 0), (shift2, d, 0)]
    x1, h2 = _rowwise("post1", _f_post1, [(xr, d, 0), (t, d, 0)], post1_vecs, [(d, F32), (d, BF16)], [], n_rows)
    pair_up = lambda c: (c % 2) * 2 + c // 2
    paired = lambda a: a.reshape(a.shape[0], N_CHIPS, kf)[:, jnp.array([0, 2, 1, 3])].reshape(a.shape[0], ffw)
    ffn_w_p, ffn_b_p = paired(ffn_w), paired(ffn_dw_b)
    u = _mm_nn("mm_up", h2, wg_up, None, BF16, gmap=pair_up)
    p = _ffn_fwd(u, ffn_w_p, ffn_b_p, n_rows)
    f = _mm_nn("mm_down", p, wg_down, None, F32)

    def loss_rows(x1v, fv, tv, g2v, lg, lb):
        loss, pull = jax.vjp(lambda a, b, c_, e, h: _f_loss(a, b, c_, e, h, tv), x1v, fv, g2v, lg, lb)
        return (*pull(jnp.ones((), F32)), jnp.full((1, 128), loss, F32))

    dx1_a, df, dgate2, dln2_g, dln2_b, loss_v = _rowwise(
        "loss", loss_rows, [(x1, d, 0), (f, d, 0), (target, d, 0)], [(gate2, d, 0), (ln2_g, d, 0), (ln2_b, d, 0)],
        [(d, F32), (d, BF16)], [d, d, d, 128], n_rows)
    loss = lax.psum(loss_v[0, 0], ("x", "y", "c"))

    dp = _mm_nt("mm_down_dx", df, wg_down, F32)
    gw_down = _mm_tn("mm_down_dw", p, df, 1)
    du, g_ffn_w, g_ffn_b = _ffn_bwd(u, dp, ffn_w_p, ffn_b_p, n_rows)
    g_ffn_w, g_ffn_b = paired(g_ffn_w), paired(g_ffn_b)
    dh2 = _mm_nt("mm_up_dx", du, wg_up, F32, gmap=pair_up)
    gw_up = _mm_tn("mm_up_dw", h2, du, N_CHIPS, gmap=pair_up)

    dx_a, dt, dgate1, dln1_g, dln1_b, dscale2, dshift2 = _rowwise(
        "post1_bwd", _vjp_rows(_f_post1, 2, 2), [(xr, d, 0), (t, d, 0), (dx1_a, d, 0), (dh2, d, 0)], post1_vecs,
        [(d, F32), (d, BF16)], [d] * 5, n_rows)
    dm = _mm_nt("mm_out_dx", dt, wg_out, F32)
    gw_out = _mm_tn("mm_out_dw", m, dt, 1)
    merge_vjp = _vjp_rows(_f_merge, 4, 1)

    def merge_bwd(*a):
        dga, dgb, dya, dyb = merge_vjp(*a)
        return jnp.concatenate([dga, dgb], axis=1), dya, dyb

    dproj, dya, dyb = _rowwise("merge_bwd", merge_bwd, merge_rows + [(dm, d, 0)], [],
                               [(2 * d, BF16, n_proj, 8192 // (2 * d)), (d, BF16), (d, BF16)], [], n_rows)

    da2 = _mm_nt("mm_conv_out_dx", dya, wg_conv_out, F32)
    gw_conv_out = _mm_tn("mm_conv_out_dw", a2, dya, 1)
    da1, dcl_g, dcl_b = _rowwise("conv_norm_bwd", _vjp_rows(_f_conv_norm, 1, 1), [(a1, d, 0), (da2, d, 0)],
                                 [(conv_ln_g, d, 0), (conv_ln_b, d, 0)], [(d, F32)], [d, d], n_rows)
    dproj, g_conv_w, g_conv_b = _glu_conv_bwd(proj_rest, 4, 5, da1, conv_w, dproj, 6144 // (2 * d), n_rows)

    dr2 = _mm_nt("mm_ret_out_dx", dyb, wg_ret_out, F32)
    gw_ret_out = _mm_tn("mm_ret_out_dw", r2, dyb, 1)

    def reduce_start(tag, names, fulls):
        shapes = [given[n].shape[1:] for n in names]
        fulls = [g_.reshape(N_CHIPS, r, c_) for g_, (r, c_) in zip(fulls, shapes)]
        theirs = _sibling_swap("grad_swap_" + tag, fulls)
        pair = []
        for n, (r, c_), g_, t_ in zip(names, shapes, fulls, theirs):
            mine = lax.dynamic_slice_in_dim(g_, my_c * (r // 2), r // 2, 1)
            pair.append(_sum_arrays("grad_pair_sum_" + n, [(mine.reshape(-1, c_), 0), (t_.reshape(-1, c_), 0)],
                                    N_CHIPS * (r // 2)).reshape(N_CHIPS, r // 2, c_))
        handle, token = _alltoall_start("grad_alltoall_start_" + tag, pair)
        own = [lax.dynamic_index_in_dim(p_, chip, 0, keepdims=False) for p_ in pair]
        return (names, shapes, handle, own), token[0:1, 0:1]

    def reduce_finish(tag, started, after):
        names, shapes, handle, own = started
        lands = _alltoall_wait("grad_alltoall_wait_" + tag, handle, after)
        return [_sum_arrays("grad_chip_sum_" + n, [(o_, 0)] + [(l_.reshape(-1, c_), k * (r // 2)) for k in range(3)], r // 2)
                for n, (r, c_), o_, l_ in zip(names, shapes, own, lands)]

    late = ("w_up", "w_down", "w_out", "w_conv_out", "w_ret_out")
    started_late, token_late = reduce_start("late", late, [gw_up, gw_down, gw_out, gw_conv_out, gw_ret_out])
    ret_vecs_bwd = [(ret_gn_g + token_late, vw, 0), (ret_gn_b, vw, 0)]
    dr, dproj, dgn_g, dgn_b = _rowwise(
        "ret_norm_bwd", _per_head(_vjp_rows(_f_group_norm_gate, 2, 1)), ret_rows + [(dr2, vw, 0)], ret_vecs_bwd,
        [(vw, F32), (vw, BF16, n_proj, 4096 // vw)], [vw, vw], n_rows, into=(dproj, 1))
    dproj = _retention_bwd(proj_qk, proj_rest, cos_t, sin_t, states, dr, consts, dproj, n_rows)

    gw_in, gb_in = _mm_tn("mm_in_dw", h1, dproj, N_CHIPS, colsum=True)
    started_in, token_in = reduce_start("in", ("w_in",), [gw_in])
    dh1 = _mm_nt("mm_in_dx", dproj, wg_in, F32)
    mod_bwd = _vjp_rows(_f_modulate, 1, 1)

    def mod1_bwd(xv, dhv, dxav, sv, tv):
        dx, ds, dsh = mod_bwd(xv, dhv, sv, tv)
        return dx + dxav, ds, dsh

    grad_x, dscale1, dshift1 = _rowwise("ln_mod1_bwd", mod1_bwd, [(xr, d, 0), (dh1, d, 0), (dx_a, d, 0)],
                                        [(scale1 + token_in, d, 0), (shift1, d, 0)], [(d, F32)], [d, d], n_rows)

    dmod = jnp.concatenate([dshift1, dscale1, dgate1, dshift2, dscale2, dgate2], axis=1)
    small_names = ["b_in", "conv_dw_w", "conv_dw_b", "conv_ln_g", "conv_ln_b", "ret_gn_g", "ret_gn_b", "ln1_g", "ln1_b",
                   "ffn_dw_w", "ffn_dw_b", "ln2_g", "ln2_b"]
    small_parts = [gb_in, g_conv_w, g_conv_b, dcl_g, dcl_b, dgn_g, dgn_b, dln1_g, dln1_b, g_ffn_w, g_ffn_b, dln2_g, dln2_b, dmod]
    small_shapes = [a.shape for a in small_parts]
    parts_all = _allgather8("gather_small_grads", [flat_rows(small_parts)])[0]
    part_rows = parts_all.shape[1]
    summed = _sum_arrays("sum_small_grads", [(parts_all.reshape(N_DEV * part_rows, d), k * part_rows) for k in range(N_DEV)],
                         part_rows)
    small_sum = unflatten(summed, small_shapes)
    grads = dict(zip(small_names, small_sum[:-1]))
    grads["b_ada"] = small_sum[-1]
    grads["conv_dw_w"] = lax.dynamic_slice_in_dim(grads["conv_dw_w"], chip * kc, kc, 1)
    grads["ffn_dw_w"] = lax.dynamic_slice_in_dim(grads["ffn_dw_w"], chip * kf, kf, 1)
    o_mod = sum(a.size for a in small_parts[:-1])
    dmod_all = parts_all.reshape(N_DEV, -1)[:, o_mod:o_mod + dmod.shape[1]]
    grads["w_ada"] = _ada_bwd(c_all, lax.dynamic_slice_in_dim(dmod_all, chip * n_ada, n_ada, 1))

    reduced = reduce_finish("late", started_late, grad_x) + reduce_finish("in", started_in, grad_x)
    big_order = late + ("w_in",)
    g_shards = []
    for n, mine_, theirs_ in zip(big_order, reduced, _sibling_exchange("grad_exchange", reduced)):
        both = lax.dynamic_update_slice_in_dim(jnp.stack([theirs_, theirs_]), mine_[None], my_c, 0)
        g_shards.append(both.reshape(given[n].shape[1:]))

    outs = {}
    for n, g_ in zip(big_order, g_shards):
        upd = _adamw("adamw_" + n, given[n][0], g_, given["m_" + n][0], given["v_" + n][0])
        for prefix, val in zip(("grad_", "delta_", "new_m_", "new_v_"), (g_, *upd)):
            outs[prefix + n] = val.reshape(given[n].shape)
    ada = _adamw("adamw_ada", w_ada[0], grads["w_ada"], m_w_ada[0], v_w_ada[0])
    for prefix, val in zip(("grad_", "delta_", "new_m_", "new_v_"), (grads["w_ada"], *ada)):
        outs[prefix + "w_ada"] = val.reshape(w_ada.shape)
    small_all_names = ["b_ada"] + small_names
    small_w_shapes = [given[n].shape for n in small_all_names]
    g_small = flat_rows([grads[n] for n in small_all_names])
    small_upd = _adamw("adamw_small", flat_rows([given[n] for n in small_all_names]), g_small,
                       flat_rows([given["m_" + n] for n in small_all_names]), flat_rows([given["v_" + n] for n in small_all_names]))
    for prefix, packed in zip(("grad_", "delta_", "new_m_", "new_v_"), (g_small, *small_upd)):
        for n, val in zip(small_all_names, unflatten(packed, small_w_shapes)):
            outs[prefix + n] = val

    weights = ["w_ada", "b_ada", "w_in", "b_in", "conv_dw_w", "conv_dw_b", "conv_ln_g", "conv_ln_b", "w_conv_out", "ret_gn_g",
               "ret_gn_b", "w_ret_out", "w_out", "ln1_g", "ln1_b", "w_up", "ffn_dw_w", "ffn_dw_b", "w_down", "ln2_g", "ln2_b"]
    result = [loss, grad_x.reshape(x.shape)]
    for prefix in ("grad_", "delta_", "new_m_", "new_v_"):
        result += [outs[prefix + n] for n in weights]
    return tuple(result)
```

```python
import jax
import jax.numpy as jnp
from jax import lax
from jax.experimental import pallas as pl
from jax.experimental.pallas import tpu as pltpu

F32 = jnp.float32
BF16 = jnp.bfloat16
MESH = pl.DeviceIdType.MESH

D_MODEL = 1024
N_HEADS = 8
DK = 128
DV = 256
CHUNK = 128
ROPE_BASE = 10000.0
D_FF = 2816
CONV_K = 31
FFN_K = 3
LN_EPS = 1e-5
ALPHA = (2.0 * 1) ** 0.25
ADAM_LR = 0.001
ADAM_B1 = 0.9
ADAM_B2 = 0.999
ADAM_EPS = 1e-08
ADAM_WD = 0.01
ADAM_STEP = 10

V7X_VMEM_BYTES = 64 * 1024 * 1024
VMEM_LIMIT = V7X_VMEM_BYTES - 8 * 1024 * 1024
ROW_TILE = 256
MM_TILE = 512
N_CHIPS = 4
N_DEV = 8

BIG_WEIGHTS = ("w_in", "w_up", "w_conv_out", "w_ret_out", "w_out", "w_down")


def _params(n_grid):
    return pltpu.CompilerParams(dimension_semantics=("arbitrary",) * n_grid, vmem_limit_bytes=VMEM_LIMIT)


def _rowwise(name, fn, rows, vecs, outs, reds, n_rows, tile=ROW_TILE, ncol=1, with_col=False, into=None):
    tile = _fit_tile(n_rows, tile)
    n_in = len(rows) + len(vecs)
    n_ref_in = n_in + (into is not None)
    n_out = len(outs)
    outs = [o if len(o) == 4 else (o[0], o[1], o[0] * ncol, 0) for o in outs]

    def col_map(off, row, first_row=0):
        def index(j, i):
            return (i + first_row // tile if row else 0, off(j) if callable(off) else off + j)
        return index

    def body(*refs):
        i = pl.program_id(1)
        vals = [r[...].astype(F32) for r in refs[:n_in]]
        res = fn(pl.program_id(0), *vals) if with_col else fn(*vals)
        for k in range(n_out):
            refs[n_ref_in + k][...] = res[k].astype(refs[n_ref_in + k].dtype)
        for k in range(len(reds)):
            o = refs[n_ref_in + n_out + k]

            @pl.when(i == 0)
            def _():
                o[...] = jnp.zeros_like(o)

            o[...] += res[n_out + k]

    in_specs = [pl.BlockSpec((tile, e[1]), col_map(e[2], True, e[3] if len(e) > 3 else 0)) for e in rows]
    in_specs += [pl.BlockSpec((1, w), col_map(off, False)) for _, w, off in vecs]
    args = [e[0] for e in rows] + [a for a, _, _ in vecs]
    aliases = {}
    if into is not None:
        in_specs.append(pl.BlockSpec(memory_space=pl.ANY))
        args.append(into[0])
        aliases = {n_in: into[1]}
    out_specs = [pl.BlockSpec((tile, w), col_map(off, True)) for w, _, _, off in outs]
    out_specs += [pl.BlockSpec((1, w), lambda j, i: (0, j)) for w in reds]
    out_shape = [jax.ShapeDtypeStruct((n_rows, total), dt) for _, dt, total, _ in outs]
    out_shape += [jax.ShapeDtypeStruct((1, w * ncol), F32) for w in reds]
    return pl.pallas_call(
        body, name=name, grid=(ncol, n_rows // tile), in_specs=in_specs, out_specs=out_specs, out_shape=out_shape,
        input_output_aliases=aliases, compiler_params=_params(2),
    )(*args)


def _vjp_rows(fn, n_row_in, n_ct):
    def bwd(*args):
        prim = [a.astype(F32) for a in args[:n_row_in] + args[n_row_in + n_ct:]]
        cts = tuple(a.astype(F32) for a in args[n_row_in:n_row_in + n_ct])
        _, pull = jax.vjp(fn, *prim)
        return pull(cts if n_ct > 1 else cts[0])

    return bwd


def _fit_tile(n, pref):
    if n <= pref:
        return n
    t = pref - pref % 16
    while n % t:
        t -= 16
    return t


def _col_tile(n):
    return n if n <= 1536 else n // 2


def _same_group(c):
    return c


def _mm_nn(name, a, w, bias, out_dtype, gmap=_same_group, tile=None, window=None, rows_outer=False):
    s, k = a.shape
    g, _, n = w.shape
    tm, tn = (min(MM_TILE, s), _col_tile(n)) if tile is None else (min(tile[0], s), tile[1])
    nt = n // tn
    first, n_cols = (0, g * n) if window is None else window
    t0 = first // tn

    def body(*refs):
        a_ref, w_ref = refs[0], refs[1]
        o_ref = refs[-1]
        acc = jnp.dot(a_ref[...].astype(BF16), w_ref[...], preferred_element_type=F32)
        if bias is not None:
            acc = acc + refs[2][...]
        o_ref[...] = acc.astype(o_ref.dtype)

    order = (lambda i, c: (c, i)) if rows_outer else (lambda c, i: (c, i))
    a_map = lambda *g_: (order(*g_)[1], 0)
    w_map = lambda *g_: (gmap((t0 + order(*g_)[0]) // nt), 0, (t0 + order(*g_)[0]) % nt)
    in_specs = [pl.BlockSpec((tm, k), a_map), pl.BlockSpec((None, k, tn), w_map)]
    args = [a, w]
    if bias is not None:
        in_specs.append(pl.BlockSpec((1, tn), lambda *g_: (0, t0 + order(*g_)[0])))
        args.append(bias)
    grid = (s // tm, n_cols // tn) if rows_outer else (n_cols // tn, s // tm)
    return pl.pallas_call(
        body, name=name, grid=grid, in_specs=in_specs, out_specs=pl.BlockSpec((tm, tn), lambda *g_: order(*g_)[::-1]),
        out_shape=jax.ShapeDtypeStruct((s, n_cols), out_dtype), compiler_params=_params(2),
    )(*args)


def _mm_nt(name, dy, w, out_dtype, gmap=_same_group):
    s = dy.shape[0]
    g, k, n = w.shape
    tm, tn = min(2 * MM_TILE if k <= 1024 else MM_TILE, s), n
    nt = 1
    steps = g

    def body(dy_ref, w_ref, o_ref, acc_ref):
        r = pl.program_id(1)
        part = lax.dot_general(dy_ref[...].astype(BF16), w_ref[...], (((1,), (1,)), ((), ())), preferred_element_type=F32)

        @pl.when(r == 0)
        def _():
            acc_ref[...] = part

        @pl.when(r > 0)
        def _():
            acc_ref[...] += part

        @pl.when(r == steps - 1)
        def _():
            o_ref[...] = acc_ref[...].astype(o_ref.dtype)

    return pl.pallas_call(
        body, name=name, grid=(s // tm, steps),
        in_specs=[pl.BlockSpec((tm, tn), lambda i, r: (i, r)),
                  pl.BlockSpec((None, k, tn), lambda i, r: (gmap(r // nt), 0, r % nt))],
        out_specs=pl.BlockSpec((tm, k), lambda i, r: (i, 0)), out_shape=jax.ShapeDtypeStruct((s, k), out_dtype),
        scratch_shapes=[pltpu.VMEM((tm, k), F32)], compiler_params=_params(2),
    )(dy, w)


def _mm_tn(name, a, dy, g, gmap=_same_group, colsum=False):
    s, k = a.shape
    n = dy.shape[1] // g
    ts = min(2048 if k <= 1024 else 1024, s)
    tn = n if k * n <= 1024 * 1536 else (n // 2 if (n // 2) % 128 == 0 else n)
    nt = n // tn

    def body(a_ref, dy_ref, o_ref, *sum_ref):
        t = pl.program_id(1)
        dyv = dy_ref[...].astype(BF16)
        part = lax.dot_general(a_ref[...].astype(BF16), dyv, (((0,), (0,)), ((), ())), preferred_element_type=F32)

        @pl.when(t == 0)
        def _():
            o_ref[...] = part

        @pl.when(t > 0)
        def _():
            o_ref[...] += part

        if colsum:
            col = jnp.sum(dyv.astype(F32), axis=0, keepdims=True)

            @pl.when(t == 0)
            def _():
                sum_ref[0][...] = col

            @pl.when(t > 0)
            def _():
                sum_ref[0][...] += col

    out_specs = [pl.BlockSpec((None, k, tn), lambda c, t: (gmap(c // nt), 0, c % nt))]
    out_shape = [jax.ShapeDtypeStruct((g, k, n), F32)]
    if colsum:
        out_specs.append(pl.BlockSpec((1, tn), lambda c, t: (0, c)))
        out_shape.append(jax.ShapeDtypeStruct((1, g * n), F32))
    res = pl.pallas_call(
        body, name=name, grid=(g * nt, s // ts),
        in_specs=[pl.BlockSpec((ts, k), lambda c, t: (t, 0)), pl.BlockSpec((ts, tn), lambda c, t: (t, c))],
        out_specs=out_specs, out_shape=out_shape, compiler_params=_params(2),
    )(a, dy)
    return res if colsum else res[0]


SUBLANES = 8


def _tap_sum(read, w_row, offsets, tile):
    acc = None
    for b in range(SUBLANES):
        group = [(k, o) for k, o in offsets if o % SUBLANES == b]
        if not group:
            continue
        rows = tile if b == 0 else tile + SUBLANES
        z = None
        for k, o in group:
            term = w_row(k) * read(o - b, rows)
            z = term if z is None else z + term
        part = z if b == 0 else z[b:b + tile]
        acc = part if acc is None else acc + part
    return acc


def _tap_grads(read, dy, offsets, tile):
    padded = jnp.concatenate([dy, jnp.zeros((SUBLANES, dy.shape[1]), dy.dtype)], axis=0)
    out = {}
    for b in range(SUBLANES):
        group = [(k, o) for k, o in offsets if o % SUBLANES == b]
        if not group:
            continue
        shifted = dy if b == 0 else pltpu.roll(padded, b, 0)
        rows = tile if b == 0 else tile + SUBLANES
        for k, o in group:
            out[k] = jnp.sum(shifted * read(o - b, rows), axis=0, keepdims=True)
    return out


CONV_HALO = 32
CONV_TILE = 256
CONV_LANES = 256


def _glu_conv_fwd(proj, col_v, col_g, w, b, n_rows):
    kw, n_ch = w.shape
    tile = min(CONV_TILE, n_rows)
    per = tile // CONV_HALO
    offsets = [(k, CONV_HALO - (kw - 1) + k) for k in range(kw)]

    def body(v_ref, g_ref, vh_ref, gh_ref, w_ref, b_ref, o_ref, buf):
        i = pl.program_id(0)
        prev = vh_ref[...].astype(F32) * jax.nn.sigmoid(gh_ref[...].astype(F32))
        buf[0:CONV_HALO, :] = jnp.where(i == 0, 0.0, prev)
        buf[CONV_HALO:CONV_HALO + tile, :] = v_ref[...].astype(F32) * jax.nn.sigmoid(g_ref[...].astype(F32))
        for c0 in range(0, n_ch, CONV_LANES):
            cols = slice(c0, c0 + CONV_LANES)
            o_ref[:, cols] = b_ref[:, cols] + _tap_sum(lambda s, n: buf[pl.ds(s, n), cols], lambda k: w_ref[k:k + 1, cols],
                                                      offsets, tile)

    main = lambda col: pl.BlockSpec((tile, n_ch), lambda i: (i, col))
    halo = lambda col: pl.BlockSpec((CONV_HALO, n_ch), lambda i: (jnp.maximum(i * per - 1, 0), col))
    return pl.pallas_call(
        body, name="conv_fwd", grid=(n_rows // tile,),
        in_specs=[main(col_v), main(col_g), halo(col_v), halo(col_g), pl.BlockSpec((kw, n_ch), lambda i: (0, 0)),
                  pl.BlockSpec((1, n_ch), lambda i: (0, 0))],
        out_specs=pl.BlockSpec((tile, n_ch), lambda i: (i, 0)), out_shape=jax.ShapeDtypeStruct((n_rows, n_ch), F32),
        scratch_shapes=[pltpu.VMEM((CONV_HALO + tile, n_ch), F32)], compiler_params=_params(1),
    )(proj, proj, proj, proj, w, b)


def _glu_conv_bwd(proj, col_v, col_g, dy, w, dproj, col_out, n_rows):
    kw, n_ch = w.shape
    tile = min(CONV_TILE, n_rows)
    per = tile // CONV_HALO
    n_tiles = n_rows // tile
    last_halo = n_rows // CONV_HALO - 1
    offsets = [(k, CONV_HALO - (kw - 1) + k) for k in range(kw)]
    back = [(k, kw - 1 - k) for k in range(kw)]

    def body(v_ref, g_ref, vh_ref, gh_ref, dy_ref, dyn_ref, w_ref, dp_any, dx_ref, dw_ref, db_ref, buf, dbuf):
        i = pl.program_id(0)
        cv, cg = v_ref[...].astype(F32), g_ref[...].astype(F32)
        sig = jax.nn.sigmoid(cg)
        prev = vh_ref[...].astype(F32) * jax.nn.sigmoid(gh_ref[...].astype(F32))
        buf[0:CONV_HALO, :] = jnp.where(i == 0, 0.0, prev)
        buf[CONV_HALO:CONV_HALO + tile, :] = cv * sig
        dbuf[0:tile, :] = dy_ref[...]
        dbuf[tile:tile + CONV_HALO, :] = jnp.where(i == n_tiles - 1, 0.0, dyn_ref[...])

        @pl.when(i == 0)
        def _():
            dw_ref[...] = jnp.zeros_like(dw_ref)
            db_ref[...] = jnp.zeros_like(db_ref)

        db_ref[...] += jnp.sum(dy_ref[...], axis=0, keepdims=True)
        for c0 in range(0, n_ch, CONV_LANES):
            cols = slice(c0, c0 + CONV_LANES)
            w_row = lambda k: w_ref[k:k + 1, cols]
            dx = _tap_sum(lambda s, n: dbuf[pl.ds(s, n), cols], w_row, back, tile)
            grads = _tap_grads(lambda s, n: buf[pl.ds(s, n), cols], dy_ref[:, cols], offsets, tile)
            for k in range(kw):
                dw_ref[k:k + 1, cols] += grads[k]
            sg = sig[:, cols]
            dx_ref[:, c0:c0 + CONV_LANES] = (dx * sg).astype(dx_ref.dtype)
            dx_ref[:, n_ch + c0:n_ch + c0 + CONV_LANES] = (dx * cv[:, cols] * sg * (1.0 - sg)).astype(dx_ref.dtype)

    main = lambda col: pl.BlockSpec((tile, n_ch), lambda i: (i, col))
    halo = lambda col: pl.BlockSpec((CONV_HALO, n_ch), lambda i: (jnp.maximum(i * per - 1, 0), col))
    return pl.pallas_call(
        body, name="conv_bwd", grid=(n_tiles,),
        in_specs=[main(col_v), main(col_g), halo(col_v), halo(col_g), main(0),
                  pl.BlockSpec((CONV_HALO, n_ch), lambda i: (jnp.minimum((i + 1) * per, last_halo), 0)),
                  pl.BlockSpec((kw, n_ch), lambda i: (0, 0)), pl.BlockSpec(memory_space=pl.ANY)],
        out_specs=[pl.BlockSpec((tile, 2 * n_ch), lambda i: (i, col_out)), pl.BlockSpec((kw, n_ch), lambda i: (0, 0)),
                   pl.BlockSpec((1, n_ch), lambda i: (0, 0))],
        out_shape=[jax.ShapeDtypeStruct(dproj.shape, dproj.dtype), jax.ShapeDtypeStruct((kw, n_ch), F32),
                   jax.ShapeDtypeStruct((1, n_ch), F32)],
        input_output_aliases={7: 0},
        scratch_shapes=[pltpu.VMEM((CONV_HALO + tile, n_ch), F32), pltpu.VMEM((tile + CONV_HALO, n_ch), F32)],
        compiler_params=_params(1),
    )(proj, proj, proj, proj, dy, dy, w, dproj)


FFN_HALO = 8


def _ffn_fwd(u, w, b, n_rows):
    kw = w.shape[0]
    half = u.shape[1] // 4
    tile = min(ROW_TILE, n_rows)
    per = tile // FFN_HALO
    offsets = [(k, FFN_HALO - (kw - 1) + k) for k in range(kw)]

    def body(u_ref, uh_ref, w_ref, b_ref, p_ref, buf):
        i = pl.program_id(1)
        buf[0:FFN_HALO, :] = jnp.where(i == 0, 0.0, uh_ref[...].astype(F32))
        buf[FFN_HALO:FFN_HALO + tile, :] = u_ref[...].astype(F32)
        conv = []
        for c0 in (0, half):
            cols = slice(c0, c0 + half)
            conv.append(b_ref[:, cols] + _tap_sum(lambda s, n: buf[pl.ds(s, n), cols], lambda k: w_ref[k:k + 1, cols],
                                                  offsets, tile))
        p_ref[...] = (conv[0] * jax.nn.silu(conv[1])).astype(p_ref.dtype)

    return pl.pallas_call(
        body, name="ffn_fwd", grid=(2, n_rows // tile),
        in_specs=[pl.BlockSpec((tile, 2 * half), lambda j, i: (i, j)),
                  pl.BlockSpec((FFN_HALO, 2 * half), lambda j, i: (jnp.maximum(i * per - 1, 0), j)),
                  pl.BlockSpec((kw, 2 * half), lambda j, i: (0, j)), pl.BlockSpec((1, 2 * half), lambda j, i: (0, j))],
        out_specs=pl.BlockSpec((tile, half), lambda j, i: (i, j)), out_shape=jax.ShapeDtypeStruct((n_rows, 2 * half), BF16),
        scratch_shapes=[pltpu.VMEM((FFN_HALO + tile, 2 * half), F32)], compiler_params=_params(2),
    )(u, u, w, b)


def _ffn_bwd(u, dp, w, b, n_rows):
    kw = w.shape[0]
    half = u.shape[1] // 4
    tile = min(ROW_TILE, n_rows)
    per = tile // FFN_HALO
    n_tiles = n_rows // tile
    last_halo = n_rows // FFN_HALO - 1
    ext = tile + FFN_HALO
    offsets = [(k, FFN_HALO - (kw - 1) + k) for k in range(kw)]
    back = [(k, kw - 1 - k) for k in range(kw)]

    def body(u_ref, up_ref, un_ref, dp_ref, dpn_ref, w_ref, b_ref, du_ref, dw_ref, db_ref, buf, dbuf):
        i = pl.program_id(1)
        buf[0:FFN_HALO, :] = jnp.where(i == 0, 0.0, up_ref[...].astype(F32))
        buf[FFN_HALO:FFN_HALO + tile, :] = u_ref[...].astype(F32)
        buf[FFN_HALO + tile:FFN_HALO + ext, :] = un_ref[...].astype(F32)
        conv = []
        for c0 in (0, half):
            cols = slice(c0, c0 + half)
            conv.append(b_ref[:, cols] + _tap_sum(lambda s, n: buf[pl.ds(s, n), cols], lambda k: w_ref[k:k + 1, cols],
                                                  offsets, ext))
        val, gate = conv
        dpe = jnp.concatenate([dp_ref[...], jnp.where(i == n_tiles - 1, 0.0, dpn_ref[...])], axis=0)
        sig = jax.nn.sigmoid(gate)
        dbuf[:, 0:half] = dpe * gate * sig
        dbuf[:, half:2 * half] = dpe * val * sig * (1.0 + gate * (1.0 - sig))

        @pl.when(i == 0)
        def _():
            dw_ref[...] = jnp.zeros_like(dw_ref)
            db_ref[...] = jnp.zeros_like(db_ref)

        for c0 in (0, half):
            cols = slice(c0, c0 + half)
            du_ref[:, cols] = _tap_sum(lambda s, n: dbuf[pl.ds(s, n), cols], lambda k: w_ref[k:k + 1, cols], back,
                                       tile).astype(du_ref.dtype)
            d_main = dbuf[0:tile, cols]
            db_ref[:, cols] += jnp.sum(d_main, axis=0, keepdims=True)
            grads = _tap_grads(lambda s, n: buf[pl.ds(s, n), cols], d_main, offsets, tile)
            for k in range(kw):
                dw_ref[k:k + 1, cols] += grads[k]

    wide = 2 * half
    return pl.pallas_call(
        body, name="ffn_bwd", grid=(2, n_tiles),
        in_specs=[pl.BlockSpec((tile, wide), lambda j, i: (i, j)),
                  pl.BlockSpec((FFN_HALO, wide), lambda j, i: (jnp.maximum(i * per - 1, 0), j)),
                  pl.BlockSpec((FFN_HALO, wide), lambda j, i: (jnp.minimum((i + 1) * per, last_halo), j)),
                  pl.BlockSpec((tile, half), lambda j, i: (i, j)),
                  pl.BlockSpec((FFN_HALO, half), lambda j, i: (jnp.minimum((i + 1) * per, last_halo), j)),
                  pl.BlockSpec((kw, wide), lambda j, i: (0, j)), pl.BlockSpec((1, wide), lambda j, i: (0, j))],
        out_specs=[pl.BlockSpec((tile, wide), lambda j, i: (i, j)), pl.BlockSpec((kw, wide), lambda j, i: (0, j)),
                   pl.BlockSpec((1, wide), lambda j, i: (0, j))],
        out_shape=[jax.ShapeDtypeStruct(u.shape, BF16), jax.ShapeDtypeStruct((kw, u.shape[1]), F32),
                   jax.ShapeDtypeStruct((1, u.shape[1]), F32)],
        scratch_shapes=[pltpu.VMEM((FFN_HALO + ext, wide), F32), pltpu.VMEM((ext, wide), F32)], compiler_params=_params(2),
    )(u, u, u, dp, dp, w, b)


def _retention_consts():
    log_gamma = jnp.log(1.0 - 2.0 ** (-5.0 - jnp.arange(N_HEADS, dtype=F32)))
    idx = jnp.arange(CHUNK, dtype=F32)
    rel = idx[:, None] - idx[None, :]
    decay = jnp.where(rel[None] >= 0, jnp.exp(log_gamma[:, None, None] * jnp.maximum(rel, 0.0)[None]), 0.0)
    zeta = jnp.exp(log_gamma[:, None] * (CHUNK - 1.0 - idx)[None])
    xi = jnp.exp(log_gamma[:, None] * (idx + 1.0)[None])
    chunk_decay = jnp.exp(log_gamma * CHUNK)
    xi_b = jnp.broadcast_to(xi[:, :, None], (N_HEADS, CHUNK, DK))
    zeta_b = jnp.broadcast_to(zeta[:, :, None], (N_HEADS, CHUNK, DK))
    cd_b = jnp.broadcast_to(chunk_decay[:, None, None], (N_HEADS, 8, DV))
    return decay, xi_b, zeta_b, cd_b


def _rope_tables(positions):
    half = DK // 2
    inv_freq = ROPE_BASE ** (-jnp.arange(half, dtype=F32) / half)
    ang = positions.astype(F32)[:, None] * inv_freq
    cos, sin = jnp.cos(ang), jnp.sin(ang)
    return jnp.concatenate([cos, cos], axis=-1), jnp.concatenate([-sin, sin], axis=-1)


def _swap_halves(v):
    return pltpu.roll(v, DK // 2, 1)


def _dot(a, b):
    return jnp.dot(a, b, preferred_element_type=F32)


def _dot_nt(a, b):
    return lax.dot_general(a, b, (((1,), (1,)), ((), ())), preferred_element_type=F32)


def _dot_tn(a, b):
    return lax.dot_general(a, b, (((0,), (0,)), ((), ())), preferred_element_type=F32)


def _const_specs():
    return [pl.BlockSpec((N_HEADS, CHUNK, CHUNK), lambda n: (0, 0, 0)), pl.BlockSpec((N_HEADS, CHUNK, DK), lambda n: (0, 0, 0)),
            pl.BlockSpec((N_HEADS, CHUNK, DK), lambda n: (0, 0, 0)), pl.BlockSpec((N_HEADS, 8, DV), lambda n: (0, 0, 0))]


def _retention_fwd(proj_qk, proj_rest, cos_t, sin_t, consts, n_rows):
    n_chunks = n_rows // CHUNK
    scale = DK ** -0.5

    def body(q_ref, k_ref, v_ref, cf_ref, ss_ref, d_ref, xi_ref, zt_ref, cd_ref, r_ref, st_ref, state):
        @pl.when(pl.program_id(0) == 0)
        def _():
            state[...] = jnp.zeros_like(state)

        cf, ss = cf_ref[...], ss_ref[...]
        for h in range(N_HEADS):
            qh = q_ref[:, h * DK:(h + 1) * DK].astype(F32)
            kh = k_ref[:, h * DK:(h + 1) * DK].astype(F32)
            qh = qh * cf + _swap_halves(qh) * ss
            kh = (kh * cf + _swap_halves(kh) * ss) * scale
            vh = v_ref[:, h * DV:(h + 1) * DV].astype(BF16)
            st = state[h]
            st_ref[0, h] = st
            sd = _dot_nt(qh.astype(BF16), kh.astype(BF16)) * d_ref[h]
            inner = _dot(sd.astype(BF16), vh)
            cross = _dot((qh * xi_ref[h]).astype(BF16), st.astype(BF16))
            kv = _dot_tn((kh * zt_ref[h]).astype(BF16), vh)
            state[h] = st * cd_ref[h, 0:1, :] + kv
            r_ref[:, h * DV:(h + 1) * DV] = inner + cross

    qk = N_HEADS * DK
    vw = N_HEADS * DV
    return pl.pallas_call(
        body, name="retention_fwd", grid=(n_chunks,),
        in_specs=[pl.BlockSpec((CHUNK, qk), lambda n: (n, 0)), pl.BlockSpec((CHUNK, qk), lambda n: (n, 1)),
                  pl.BlockSpec((CHUNK, vw), lambda n: (n, 0)), pl.BlockSpec((CHUNK, DK), lambda n: (n, 0)),
                  pl.BlockSpec((CHUNK, DK), lambda n: (n, 0))] + _const_specs(),
        out_specs=[pl.BlockSpec((CHUNK, vw), lambda n: (n, 0)), pl.BlockSpec((1, N_HEADS, DK, DV), lambda n: (n, 0, 0, 0))],
        out_shape=[jax.ShapeDtypeStruct((n_rows, vw), F32), jax.ShapeDtypeStruct((n_chunks, N_HEADS, DK, DV), F32)],
        scratch_shapes=[pltpu.VMEM((N_HEADS, DK, DV), F32)], compiler_params=_params(1),
    )(proj_qk, proj_qk, proj_rest, cos_t, sin_t, *consts)


def _retention_bwd(proj_qk, proj_rest, cos_t, sin_t, states, dr, consts, dproj, n_rows):
    n_chunks = n_rows // CHUNK
    scale = DK ** -0.5
    qk = N_HEADS * DK
    vw = N_HEADS * DV

    def body(q_ref, k_ref, v_ref, cf_ref, ss_ref, st_ref, dr_ref, d_ref, xi_ref, zt_ref, cd_ref, dp_any, dqkv_ref, g_ref):
        dq_ref, dk_ref, dv_ref = dqkv_ref.at[:, 0:qk], dqkv_ref.at[:, qk:2 * qk], dqkv_ref.at[:, 2 * qk:2 * qk + vw]

        @pl.when(pl.program_id(0) == 0)
        def _():
            g_ref[...] = jnp.zeros_like(g_ref)

        cf, ss = cf_ref[...], ss_ref[...]
        for h in range(N_HEADS):
            qh = q_ref[:, h * DK:(h + 1) * DK].astype(F32)
            kh = k_ref[:, h * DK:(h + 1) * DK].astype(F32)
            qh = qh * cf + _swap_halves(qh) * ss
            kh = (kh * cf + _swap_halves(kh) * ss) * scale
            qb, kb = qh.astype(BF16), kh.astype(BF16)
            vh = v_ref[:, h * DV:(h + 1) * DV].astype(BF16)
            do = dr_ref[:, h * DV:(h + 1) * DV].astype(BF16)
            rb = st_ref[0, h].astype(BF16)
            g = g_ref[h]
            gb = g.astype(BF16)
            dec, xi, zt = d_ref[h], xi_ref[h], zt_ref[h]
            sd = (_dot_nt(qb, kb) * dec).astype(BF16)
            ds = (_dot_nt(do, vh) * dec).astype(BF16)
            dqh = _dot(ds, kb) + _dot_nt(do, rb) * xi
            dkh = (_dot_tn(ds, qb) + _dot_nt(vh, gb) * zt) * scale
            dvh = _dot_tn(sd, do) + _dot((kh * zt).astype(BF16), gb)
            g_ref[h] = g * cd_ref[h, 0:1, :] + _dot_tn((qh * xi).astype(BF16), do)
            dq_ref[:, h * DK:(h + 1) * DK] = (dqh * cf + _swap_halves(dqh * ss)).astype(dq_ref.dtype)
            dk_ref[:, h * DK:(h + 1) * DK] = (dkh * cf + _swap_halves(dkh * ss)).astype(dk_ref.dtype)
            dv_ref[:, h * DV:(h + 1) * DV] = dvh.astype(dv_ref.dtype)

    last = n_chunks - 1
    return pl.pallas_call(
        body, name="retention_bwd", grid=(n_chunks,),
        in_specs=[pl.BlockSpec((CHUNK, qk), lambda n: (last - n, 0)), pl.BlockSpec((CHUNK, qk), lambda n: (last - n, 1)),
                  pl.BlockSpec((CHUNK, vw), lambda n: (last - n, 0)), pl.BlockSpec((CHUNK, DK), lambda n: (last - n, 0)),
                  pl.BlockSpec((CHUNK, DK), lambda n: (last - n, 0)),
                  pl.BlockSpec((1, N_HEADS, DK, DV), lambda n: (last - n, 0, 0, 0)),
                  pl.BlockSpec((CHUNK, vw), lambda n: (last - n, 0))] + _const_specs() + [pl.BlockSpec(memory_space=pl.ANY)],
        out_specs=pl.BlockSpec((CHUNK, 2 * qk + vw), lambda n: (last - n, 0)),
        out_shape=jax.ShapeDtypeStruct(dproj.shape, dproj.dtype), input_output_aliases={11: 0},
        scratch_shapes=[pltpu.VMEM((N_HEADS, DK, DV), F32)], compiler_params=_params(1),
    )(proj_qk, proj_qk, proj_rest, cos_t, sin_t, states, dr, *consts, dproj)


def _ln(v):
    mu = jnp.mean(v, axis=-1, keepdims=True)
    var = jnp.mean(jnp.square(v - mu), axis=-1, keepdims=True)
    return (v - mu) * lax.rsqrt(var + LN_EPS)


def _f_modulate(x, scale, shift):
    return _ln(x) * (1.0 + scale) + shift


def _f_conv_norm(a1, g, b):
    return jax.nn.silu(_ln(a1) * g + b)


def _f_group_norm_gate(r, gate, g, b):
    return (_ln(r) * g + b) * jax.nn.silu(gate)


def _per_head(fn):
    def run(*arrays):
        parts = [fn(*[a[:, h * DV:(h + 1) * DV] for a in arrays]) for h in range(N_HEADS)]
        if not isinstance(parts[0], (tuple, list)):
            return (jnp.concatenate(parts, axis=1),)
        return tuple(jnp.concatenate([p[k] for p in parts], axis=1) for k in range(len(parts[0])))

    return run


def _f_merge(ga, gb, ya, yb):
    return jax.nn.sigmoid(ga) * ya + jax.nn.sigmoid(gb) * yb


def _f_post1(x, t, gate1, g1, b1, scale2, shift2):
    x1 = _ln(ALPHA * x + gate1 * t) * g1 + b1
    return x1, _ln(x1) * (1.0 + scale2) + shift2


def _f_loss(x1, f, gate2, g2, b2, target):
    y = _ln(ALPHA * x1 + gate2 * f) * g2 + b2
    return 0.5 * jnp.sum(jnp.mean(jnp.square(y - target), axis=-1))


ANY = pl.BlockSpec(memory_space=pl.ANY)


def _allgather8(name, blocks, own_half=False):
    n = len(blocks)
    rows = [b.shape[0] // 2 if own_half else b.shape[0] for b in blocks]

    def body(*refs):
        x_refs, out_refs = refs[:n], refs[n:2 * n]
        send_sems, recv_sems = refs[2 * n:]
        x, y, c = lax.axis_index("x"), lax.axis_index("y"), lax.axis_index("c")
        me, sibling = (x, y, c), (x, y, 1 - c)
        chips = [(1 - x, y), (x, 1 - y), (1 - x, 1 - y)]
        every = range(n)

        def src(a):
            return x_refs[a].at[pl.ds(c * rows[a], rows[a])] if own_half else x_refs[a]

        def slot(a, px, py, pc):
            return out_refs[a].at[4 * px + 2 * py + pc]

        def copy(k, a, block, to, from_input=False):
            return pltpu.make_async_remote_copy(
                src_ref=src(a) if from_input else slot(a, *block), dst_ref=slot(a, *block), send_sem=send_sems.at[k, a],
                recv_sem=recv_sems.at[k, a], device_id=to, device_id_type=MESH)

        first = [copy(0, a, me, sibling, True) for a in every]
        first += [copy(1 + j, a, me, (*chip, c), True) for j, chip in enumerate(chips) for a in every]
        for cp in first:
            cp.start()
        passed = [[copy(4 + j, a, (*chip, c), sibling) for a in every] for j, chip in enumerate(chips)]
        for j, chip in enumerate(chips):
            for a in every:
                copy(1 + j, a, (*chip, c), me).wait_recv()
            for cp in passed[j]:
                cp.start()
        for a in every:
            copy(0, a, sibling, me).wait_recv()
        for j, chip in enumerate(chips):
            for a in every:
                copy(4 + j, a, (*chip, 1 - c), me).wait_recv()
        for cp in first + [cp for group in passed for cp in group]:
            cp.wait_send()

    gathered = pl.pallas_call(
        body, name=name, in_specs=[ANY] * n, out_specs=[ANY] * n,
        out_shape=[jax.ShapeDtypeStruct((N_DEV, r, b.shape[1]), b.dtype) for r, b in zip(rows, blocks)],
        scratch_shapes=[pltpu.SemaphoreType.DMA((7, n)), pltpu.SemaphoreType.DMA((7, n))],
    )(*blocks)
    c = lax.axis_index("c")
    me = 4 * lax.axis_index("x") + 2 * lax.axis_index("y") + c
    own = [lax.dynamic_slice_in_dim(b, c * r, r, 0) if own_half else b for r, b in zip(rows, blocks)]
    return [lax.dynamic_update_slice_in_dim(g_, o_[None], me, 0) for g_, o_ in zip(gathered, own)]


def _sibling_swap(name, arrays):
    n = len(arrays)
    halves = [a.shape[1] // 2 for a in arrays]

    def body(*refs):
        g_refs, theirs = refs[:n], refs[n:2 * n]
        send_sems, recv_sems = refs[2 * n:]
        x, y, c = lax.axis_index("x"), lax.axis_index("y"), lax.axis_index("c")
        remote = [pltpu.make_async_remote_copy(
            src_ref=g_refs[a].at[:, pl.ds((1 - c) * halves[a], halves[a]), :], dst_ref=theirs[a], send_sem=send_sems.at[a],
            recv_sem=recv_sems.at[a], device_id=(x, y, 1 - c), device_id_type=MESH) for a in range(n)]
        for cp in remote:
            cp.start()
        for cp in remote:
            cp.wait_recv()
        for cp in remote:
            cp.wait_send()

    return pl.pallas_call(
        body, name=name, in_specs=[ANY] * n, out_specs=[ANY] * n,
        out_shape=[jax.ShapeDtypeStruct((a.shape[0], h, a.shape[2]), a.dtype) for a, h in zip(arrays, halves)],
        scratch_shapes=[pltpu.SemaphoreType.DMA((n,)), pltpu.SemaphoreType.DMA((n,))],
    )(*arrays)


HBM = pl.BlockSpec(memory_space=pltpu.HBM)
SEM = pl.BlockSpec(memory_space=pltpu.SEMAPHORE)
DATAFLOW = pltpu.SideEffectType.DATAFLOW_SIDE_EFFECTING


def _chip_peers(x, y):
    return [(1 - x, y), (x, 1 - y), (1 - x, 1 - y)]


def _alltoall_start(name, arrays):
    n = len(arrays)
    lands = [lax.empty((3,) + a.shape[1:], a.dtype) for a in arrays]

    def body(*refs):
        p_refs, land_refs = refs[:n], refs[n:2 * n]
        send_sems, recv_sems = refs[2 * n], refs[2 * n + 1]
        token = refs[-1]
        x, y, c = lax.axis_index("x"), lax.axis_index("y"), lax.axis_index("c")
        for k, (px, py) in enumerate(_chip_peers(x, y)):
            for a in range(n):
                pltpu.make_async_remote_copy(
                    src_ref=p_refs[a].at[2 * px + py], dst_ref=land_refs[a].at[k], send_sem=send_sems.at[k * n + a],
                    recv_sem=recv_sems.at[k * n + a], device_id=(px, py, c), device_id_type=MESH).start()
        token[...] = jnp.zeros_like(token)

    thru = [pltpu.HBM(a.shape, a.dtype) for a in arrays] + [pltpu.HBM(l.shape, l.dtype) for l in lands]
    res = pl.pallas_call(
        body, name=name, in_specs=[HBM] * (2 * n),
        out_specs=(SEM, SEM, *[HBM] * (2 * n), pl.BlockSpec(memory_space=pltpu.VMEM)),
        out_shape=(pltpu.SemaphoreType.DMA((3 * n,)), pltpu.SemaphoreType.DMA((3 * n,)), *thru, jax.ShapeDtypeStruct((8, 128), F32)),
        input_output_aliases={i: 2 + i for i in range(2 * n)},
        compiler_params=pltpu.CompilerParams(has_side_effects=DATAFLOW),
    )(*[pltpu.with_memory_space_constraint(a, pltpu.HBM) for a in arrays],
      *[pltpu.with_memory_space_constraint(l, pltpu.HBM) for l in lands])
    return (res[0], res[1], list(res[2:2 + n]), list(res[2 + n:2 + 2 * n])), res[-1]


def _alltoall_wait(name, handle, after):
    send_sems, recv_sems, sources, lands = handle
    n = len(sources)

    def body(*refs):
        p_refs, land_refs = refs[:n], refs[n:2 * n]
        send_sems, recv_sems = refs[2 * n], refs[2 * n + 1]
        x, y, c = lax.axis_index("x"), lax.axis_index("y"), lax.axis_index("c")
        for k, (px, py) in enumerate(_chip_peers(x, y)):
            for a in range(n):
                cp = pltpu.make_async_remote_copy(
                    src_ref=p_refs[a].at[2 * px + py], dst_ref=land_refs[a].at[k], send_sem=send_sems.at[k * n + a],
                    recv_sem=recv_sems.at[k * n + a], device_id=(px, py, c), device_id_type=MESH)
                cp.wait_send()
                cp.wait_recv()

    res = pl.pallas_call(
        body, name=name, in_specs=[HBM] * (2 * n) + [SEM, SEM, ANY], out_specs=[HBM] * (2 * n),
        out_shape=[pltpu.HBM(a.shape, a.dtype) for a in sources + lands],
        input_output_aliases={i: i for i in range(2 * n)}, compiler_params=pltpu.CompilerParams(has_side_effects=DATAFLOW),
    )(*sources, *lands, send_sems, recv_sems, after)
    return list(res[n:])


def _sibling_exchange(name, halves):
    n = len(halves)

    def body(*refs):
        h_refs, out_refs = refs[:n], refs[n:2 * n]
        send_sems, recv_sems = refs[2 * n:]
        x, y, c = lax.axis_index("x"), lax.axis_index("y"), lax.axis_index("c")
        remote = [pltpu.make_async_remote_copy(
            src_ref=h_refs[a], dst_ref=out_refs[a], send_sem=send_sems.at[a], recv_sem=recv_sems.at[a],
            device_id=(x, y, 1 - c), device_id_type=MESH) for a in range(n)]
        for cp in remote:
            cp.start()
        for cp in remote:
            cp.wait_recv()
        for cp in remote:
            cp.wait_send()

    return pl.pallas_call(
        body, name=name, in_specs=[ANY] * n, out_specs=[ANY] * n,
        out_shape=[jax.ShapeDtypeStruct(h.shape, h.dtype) for h in halves],
        scratch_shapes=[pltpu.SemaphoreType.DMA((n,)), pltpu.SemaphoreType.DMA((n,))],
    )(*halves)


def _sum_arrays(name, terms, n_rows):
    c_ = terms[0][0].shape[1]

    def add_all(*vals):
        acc = vals[0]
        for v in vals[1:]:
            acc = acc + v
        return (acc,)

    return _rowwise(name, add_all, [(a, c_, 0, first) for a, first in terms], [], [(c_, F32)], [], n_rows, tile=128)[0]


def _adamw_fn(w, g, m, v):
    m = ADAM_B1 * m + (1.0 - ADAM_B1) * g
    v = ADAM_B2 * v + (1.0 - ADAM_B2) * jnp.square(g)
    m_hat = m / (1.0 - ADAM_B1 ** ADAM_STEP)
    v_hat = v / (1.0 - ADAM_B2 ** ADAM_STEP)
    delta = -ADAM_LR * (m_hat / (jnp.sqrt(v_hat) + ADAM_EPS) + ADAM_WD * w)
    return delta, m, v


def _adamw(name, w, g, m, v):
    r, c_ = w.shape
    return _rowwise(name, _adamw_fn, [(w, c_, 0), (g, c_, 0), (m, c_, 0), (v, c_, 0)], [], [(c_, F32)] * 3, [], r, tile=128)


def _ada_fwd(c_all, w_ada, b_ada):
    n = w_ada.shape[1]

    def body(c_ref, w_ref, b_ref, o_ref):
        o_ref[...] = jnp.dot(jax.nn.silu(c_ref[...]).astype(BF16), w_ref[...].astype(BF16),
                             preferred_element_type=F32) + b_ref[...]

    return pl.pallas_call(body, name="ada_fwd", out_shape=jax.ShapeDtypeStruct((N_DEV, n), F32),
                          compiler_params=pltpu.CompilerParams(vmem_limit_bytes=VMEM_LIMIT))(c_all, w_ada, b_ada)


def _ada_bwd(c_all, dmod_cols):
    d = c_all.shape[1]
    n = dmod_cols.shape[1]

    def body(c_ref, dm_ref, gw_ref):
        gw_ref[...] = lax.dot_general(jax.nn.silu(c_ref[...]).astype(BF16), dm_ref[...].astype(BF16),
                                      (((0,), (0,)), ((), ())), preferred_element_type=F32)

    return pl.pallas_call(body, name="ada_bwd", out_shape=jax.ShapeDtypeStruct((d, n), F32),
                          compiler_params=pltpu.CompilerParams(vmem_limit_bytes=VMEM_LIMIT))(c_all, dmod_cols)


def _cast_bf16(name, a):
    r, c_ = a.shape
    return _rowwise(name, lambda v: (v,), [(a, c_, 0)], [], [(c_, BF16)], [], r)[0]


def _pad_rows(vec, mult):
    n = vec.shape[0]
    return jnp.pad(vec, (0, (-n) % mult))


def kernel(x, c, positions, w_ada, b_ada, w_in, b_in, conv_dw_w, conv_dw_b, conv_ln_g, conv_ln_b, w_conv_out, ret_gn_g, ret_gn_b, w_ret_out, w_out, ln1_g, ln1_b, w_up, ffn_dw_w, ffn_dw_b, w_down, ln2_g, ln2_b, loss_target, m_w_ada, m_b_ada, m_w_in, m_b_in, m_conv_dw_w, m_conv_dw_b, m_conv_ln_g, m_conv_ln_b, m_w_conv_out, m_ret_gn_g, m_ret_gn_b, m_w_ret_out, m_w_out, m_ln1_g, m_ln1_b, m_w_up, m_ffn_dw_w, m_ffn_dw_b, m_w_down, m_ln2_g, m_ln2_b, v_w_ada, v_b_ada, v_w_in, v_b_in, v_conv_dw_w, v_conv_dw_b, v_conv_ln_g, v_conv_ln_b, v_w_conv_out, v_ret_gn_g, v_ret_gn_b, v_w_ret_out, v_w_out, v_ln1_g, v_ln1_b, v_w_up, v_ffn_dw_w, v_ffn_dw_b, v_w_down, v_ln2_g, v_ln2_b):
    given = dict(locals())
    n_rows = x.shape[1]
    d = D_MODEL
    my_c = lax.axis_index("c")
    chip = 2 * lax.axis_index("x") + lax.axis_index("y")
    dev = 2 * chip + my_c
    xr = x[0]
    target = loss_target[0]
    vw = N_HEADS * DV
    ffw = 2 * D_FF

    def flat_rows(arrays, mult=8):
        v = jnp.concatenate([a.reshape(-1) for a in arrays])
        return _pad_rows(v, mult * d).reshape(-1, d)

    def unflatten(flat2d, shapes):
        v, out, o = flat2d.reshape(-1), [], 0
        for shp in shapes:
            size = 1
            for e in shp:
                size *= e
            out.append(v[o:o + size].reshape(shp))
            o += size
        return out

    w_bf = [_cast_bf16("cast_" + n, given[n][0]) for n in BIG_WEIGHTS]
    w_all = dict(zip(BIG_WEIGHTS, _allgather8("gather_weights", w_bf, own_half=True)))
    wg_in = w_all["w_in"].reshape(N_CHIPS, d, w_in.shape[2])
    wg_up = w_all["w_up"].reshape(N_CHIPS, d, w_up.shape[2])
    wg_conv_out = w_all["w_conv_out"].reshape(1, d, d)
    wg_ret_out = w_all["w_ret_out"].reshape(1, vw, d)
    wg_out = w_all["w_out"].reshape(1, d, d)
    wg_down = w_all["w_down"].reshape(1, D_FF, d)

    kc, kf = conv_dw_w.shape[2], ffn_dw_w.shape[2]
    small_all = _allgather8("gather_small", [flat_rows([c, conv_dw_w, ffn_dw_w])])[0].reshape(N_DEV, -1)
    c_all = small_all[:, :d]
    per_chip = small_all[0::2]
    conv_w = per_chip[:, d:d + CONV_K * kc].reshape(N_CHIPS, CONV_K, kc).transpose(1, 0, 2).reshape(CONV_K, N_CHIPS * kc)
    o_f = d + CONV_K * kc
    ffn_w = per_chip[:, o_f:o_f + FFN_K * kf].reshape(N_CHIPS, FFN_K, kf).transpose(1, 0, 2).reshape(FFN_K, N_CHIPS * kf)

    n_ada = w_ada.shape[2]
    b_ada_cols = lax.dynamic_slice_in_dim(b_ada, chip * n_ada, n_ada, 1)
    mod_cols = _ada_fwd(c_all, w_ada[0], b_ada_cols)
    mod_all = _allgather8("gather_mod", [mod_cols])[0]
    mod = lax.dynamic_index_in_dim(mod_all[0::2], dev, 1, keepdims=False).reshape(1, N_CHIPS * n_ada)
    shift1, scale1, gate1, shift2, scale2, gate2 = [mod[:, k * d:(k + 1) * d] for k in range(6)]

    h1 = _rowwise("ln_mod1", lambda a, s, t: (_f_modulate(a, s, t),), [(xr, d, 0)], [(scale1, d, 0), (shift1, d, 0)],
                  [(d, BF16)], [], n_rows)[0]
    n_proj = N_CHIPS * w_in.shape[2]
    proj_qk = _mm_nn("mm_in_qk", h1, wg_in, b_in, F32, tile=(2048, 512), window=(0, 2 * d), rows_outer=True)
    proj_rest = _mm_nn("mm_in_rest", h1, wg_in, b_in, BF16, tile=(2048, 512), window=(2 * d, n_proj - 2 * d), rows_outer=True)
    a1 = _glu_conv_fwd(proj_rest, 4, 5, conv_w, conv_dw_b, n_rows)
    a2 = _rowwise("conv_norm", lambda a, g, b: (_f_conv_norm(a, g, b),), [(a1, d, 0)], [(conv_ln_g, d, 0), (conv_ln_b, d, 0)],
                  [(d, BF16)], [], n_rows)[0]
    y_a = _mm_nn("mm_conv_out", a2, wg_conv_out, None, F32)
    cos_t, sin_t = _rope_tables(positions[0])
    consts = _retention_consts()
    r, states = _retention_fwd(proj_qk, proj_rest, cos_t, sin_t, consts, n_rows)
    ret_rows = [(r, vw, 0), (proj_rest, vw, 1)]
    ret_vecs = [(ret_gn_g, vw, 0), (ret_gn_b, vw, 0)]
    r2 = _rowwise("ret_norm", _per_head(_f_group_norm_gate), ret_rows, ret_vecs, [(vw, BF16)], [], n_rows)[0]
    y_b = _mm_nn("mm_ret_out", r2, wg_ret_out, None, F32)
    merge_rows = [(proj_rest, d, 6), (proj_rest, d, 7), (y_a, d, 0), (y_b, d, 0)]
    m = _rowwise("merge", lambda *a: (_f_merge(*a),), merge_rows, [], [(d, BF16)], [], n_rows)[0]
    t = _mm_nn("mm_out", m, wg_out, None, F32)
    post1_vecs = [(gate1, d, 0), (ln1_g, d, 0), (ln1_b, d, 0), (scale2, d, 0), (shift2, d, 0)]
    x1, h2 = _rowwise("post1", _f_post1, [(xr, d, 0), (t, d, 0)], post1_vecs, [(d, F32), (d, BF16)], [], n_rows)
    pair_up = lambda c: (c % 2) * 2 + c // 2
    paired = lambda a: a.reshape(a.shape[0], N_CHIPS, kf)[:, jnp.array([0, 2, 1, 3])].reshape(a.shape[0], ffw)
    ffn_w_p, ffn_b_p = paired(ffn_w), paired(ffn_dw_b)
    u = _mm_nn("mm_up", h2, wg_up, None, BF16, gmap=pair_up)
    p = _ffn_fwd(u, ffn_w_p, ffn_b_p, n_rows)
    f = _mm_nn("mm_down", p, wg_down, None, F32)

    def loss_rows(x1v, fv, tv, g2v, lg, lb):
        loss, pull = jax.vjp(lambda a, b, c_, e, h: _f_loss(a, b, c_, e, h, tv), x1v, fv, g2v, lg, lb)
        return (*pull(jnp.ones((), F32)), jnp.full((1, 128), loss, F32))

    dx1_a, df, dgate2, dln2_g, dln2_b, loss_v = _rowwise(
        "loss", loss_rows, [(x1, d, 0), (f, d, 0), (target, d, 0)], [(gate2, d, 0), (ln2_g, d, 0), (ln2_b, d, 0)],
        [(d, F32), (d, BF16)], [d, d, d, 128], n_rows)
    loss = lax.psum(loss_v[0, 0], ("x", "y", "c"))

    dp = _mm_nt("mm_down_dx", df, wg_down, F32)
    gw_down = _mm_tn("mm_down_dw", p, df, 1)
    du, g_ffn_w, g_ffn_b = _ffn_bwd(u, dp, ffn_w_p, ffn_b_p, n_rows)
    g_ffn_w, g_ffn_b = paired(g_ffn_w), paired(g_ffn_b)
    dh2 = _mm_nt("mm_up_dx", du, wg_up, F32, gmap=pair_up)
    gw_up = _mm_tn("mm_up_dw", h2, du, N_CHIPS, gmap=pair_up)

    dx_a, dt, dgate1, dln1_g, dln1_b, dscale2, dshift2 = _rowwise(
        "post1_bwd", _vjp_rows(_f_post1, 2, 2), [(xr, d, 0), (t, d, 0), (dx1_a, d, 0), (dh2, d, 0)], post1_vecs,
        [(d, F32), (d, BF16)], [d] * 5, n_rows)
    dm = _mm_nt("mm_out_dx", dt, wg_out, F32)
    gw_out = _mm_tn("mm_out_dw", m, dt, 1)
    merge_vjp = _vjp_rows(_f_merge, 4, 1)

    def merge_bwd(*a):
        dga, dgb, dya, dyb = merge_vjp(*a)
        return jnp.concatenate([dga, dgb], axis=1), dya, dyb

    dproj, dya, dyb = _rowwise("merge_bwd", merge_bwd, merge_rows + [(dm, d, 0)], [],
                               [(2 * d, BF16, n_proj, 8192 // (2 * d)), (d, BF16), (d, BF16)], [], n_rows)

    da2 = _mm_nt("mm_conv_out_dx", dya, wg_conv_out, F32)
    gw_conv_out = _mm_tn("mm_conv_out_dw", a2, dya, 1)
    da1, dcl_g, dcl_b = _rowwise("conv_norm_bwd", _vjp_rows(_f_conv_norm, 1, 1), [(a1, d, 0), (da2, d, 0)],
                                 [(conv_ln_g, d, 0), (conv_ln_b, d, 0)], [(d, F32)], [d, d], n_rows)
    dproj, g_conv_w, g_conv_b = _glu_conv_bwd(proj_rest, 4, 5, da1, conv_w, dproj, 6144 // (2 * d), n_rows)

    dr2 = _mm_nt("mm_ret_out_dx", dyb, wg_ret_out, F32)
    gw_ret_out = _mm_tn("mm_ret_out_dw", r2, dyb, 1)

    def reduce_start(tag, names, fulls):
        shapes = [given[n].shape[1:] for n in names]
        fulls = [g_.reshape(N_CHIPS, r, c_) for g_, (r, c_) in zip(fulls, shapes)]
        theirs = _sibling_swap("grad_swap_" + tag, fulls)
        pair = []
        for n, (r, c_), g_, t_ in zip(names, shapes, fulls, theirs):
            mine = lax.dynamic_slice_in_dim(g_, my_c * (r // 2), r // 2, 1)
            pair.append(_sum_arrays("grad_pair_sum_" + n, [(mine.reshape(-1, c_), 0), (t_.reshape(-1, c_), 0)],
                                    N_CHIPS * (r // 2)).reshape(N_CHIPS, r // 2, c_))
        handle, token = _alltoall_start("grad_alltoall_start_" + tag, pair)
        own = [lax.dynamic_index_in_dim(p_, chip, 0, keepdims=False) for p_ in pair]
        return (names, shapes, handle, own), token[0:1, 0:1]

    def reduce_finish(tag, started, after):
        names, shapes, handle, own = started
        lands = _alltoall_wait("grad_alltoall_wait_" + tag, handle, after)
        return [_sum_arrays("grad_chip_sum_" + n, [(o_, 0)] + [(l_.reshape(-1, c_), k * (r // 2)) for k in range(3)], r // 2)
                for n, (r, c_), o_, l_ in zip(names, shapes, own, lands)]

    late = ("w_up", "w_down", "w_out", "w_conv_out", "w_ret_out")
    started_late, token_late = reduce_start("late", late, [gw_up, gw_down, gw_out, gw_conv_out, gw_ret_out])
    ret_vecs_bwd = [(ret_gn_g + token_late, vw, 0), (ret_gn_b, vw, 0)]
    dr, dproj, dgn_g, dgn_b = _rowwise(
        "ret_norm_bwd", _per_head(_vjp_rows(_f_group_norm_gate, 2, 1)), ret_rows + [(dr2, vw, 0)], ret_vecs_bwd,
        [(vw, F32), (vw, BF16, n_proj, 4096 // vw)], [vw, vw], n_rows, into=(dproj, 1))
    dproj = _retention_bwd(proj_qk, proj_rest, cos_t, sin_t, states, dr, consts, dproj, n_rows)

    gw_in, gb_in = _mm_tn("mm_in_dw", h1, dproj, N_CHIPS, colsum=True)
    started_in, token_in = reduce_start("in", ("w_in",), [gw_in])
    dh1 = _mm_nt("mm_in_dx", dproj, wg_in, F32)
    mod_bwd = _vjp_rows(_f_modulate, 1, 1)

    def mod1_bwd(xv, dhv, dxav, sv, tv):
        dx, ds, dsh = mod_bwd(xv, dhv, sv, tv)
        return dx + dxav, ds, dsh

    grad_x, dscale1, dshift1 = _rowwise("ln_mod1_bwd", mod1_bwd, [(xr, d, 0), (dh1, d, 0), (dx_a, d, 0)],
                                        [(scale1 + token_in, d, 0), (shift1, d, 0)], [(d, F32)], [d, d], n_rows)

    dmod = jnp.concatenate([dshift1, dscale1, dgate1, dshift2, dscale2, dgate2], axis=1)
    small_names = ["b_in", "conv_dw_w", "conv_dw_b", "conv_ln_g", "conv_ln_b", "ret_gn_g", "ret_gn_b", "ln1_g", "ln1_b",
                   "ffn_dw_w", "ffn_dw_b", "ln2_g", "ln2_b"]
    small_parts = [gb_in, g_conv_w, g_conv_b, dcl_g, dcl_b, dgn_g, dgn_b, dln1_g, dln1_b, g_ffn_w, g_ffn_b, dln2_g, dln2_b, dmod]
    small_shapes = [a.shape for a in small_parts]
    parts_all = _allgather8("gather_small_grads", [flat_rows(small_parts)])[0]
    part_rows = parts_all.shape[1]
    summed = _sum_arrays("sum_small_grads", [(parts_all.reshape(N_DEV * part_rows, d), k * part_rows) for k in range(N_DEV)],
                         part_rows)
    small_sum = unflatten(summed, small_shapes)
    grads = dict(zip(small_names, small_sum[:-1]))
    grads["b_ada"] = small_sum[-1]
    grads["conv_dw_w"] = lax.dynamic_slice_in_dim(grads["conv_dw_w"], chip * kc, kc, 1)
    grads["ffn_dw_w"] = lax.dynamic_slice_in_dim(grads["ffn_dw_w"], chip * kf, kf, 1)
    o_mod = sum(a.size for a in small_parts[:-1])
    dmod_all = parts_all.reshape(N_DEV, -1)[:, o_mod:o_mod + dmod.shape[1]]
    grads["w_ada"] = _ada_bwd(c_all, lax.dynamic_slice_in_dim(dmod_all, chip * n_ada, n_ada, 1))

    reduced = reduce_finish("late", started_late, grad_x) + reduce_finish("in", started_in, grad_x)
    big_order = late + ("w_in",)
    g_shards = []
    for n, mine_, theirs_ in zip(big_order, reduced, _sibling_exchange("grad_exchange", reduced)):
        both = lax.dynamic_update_slice_in_dim(jnp.stack([theirs_, theirs_]), mine_[None], my_c, 0)
        g_shards.append(both.reshape(given[n].shape[1:]))

    outs = {}
    for n, g_ in zip(big_order, g_shards):
        upd = _adamw("adamw_" + n, given[n][0], g_, given["m_" + n][0], given["v_" + n][0])
        for prefix, val in zip(("grad_", "delta_", "new_m_", "new_v_"), (g_, *upd)):
            outs[prefix + n] = val.reshape(given[n].shape)
    ada = _adamw("adamw_ada", w_ada[0], grads["w_ada"], m_w_ada[0], v_w_ada[0])
    for prefix, val in zip(("grad_", "delta_", "new_m_", "new_v_"), (grads["w_ada"], *ada)):
        outs[prefix + "w_ada"] = val.reshape(w_ada.shape)
    small_all_names = ["b_ada"] + small_names
    small_w_shapes = [given[n].shape for n in small_all_names]
    g_small = flat_rows([grads[n] for n in small_all_names])
    small_upd = _adamw("adamw_small", flat_rows([given[n] for n in small_all_names]), g_small,
                       flat_rows([given["m_" + n] for n in small_all_names]), flat_rows([given["v_" + n] for n in small_all_names]))
    for prefix, packed in zip(("grad_", "delta_", "new_m_", "new_v_"), (g_small, *small_upd)):
        for n, val in zip(small_all_names, unflatten(packed, small_w_shapes)):
            outs[prefix + n] = val

    weights = ["w_ada", "b_ada", "w_in", "b_in", "conv_dw_w", "conv_dw_b", "conv_ln_g", "conv_ln_b", "w_conv_out", "ret_gn_g",
               "ret_gn_b", "w_ret_out", "w_out", "ln1_g", "ln1_b", "w_up", "ffn_dw_w", "ffn_dw_b", "w_down", "ln2_g", "ln2_b"]
    result = [loss, grad_x.reshape(x.shape)]
    for prefix in ("grad_", "delta_", "new_m_", "new_v_"):
        result += [outs[prefix + n] for n in weights]
    return tuple(result)
```

```python
import jax
import jax.numpy as jnp
from jax import lax
from jax.experimental import pallas as pl
from jax.experimental.pallas import tpu as pltpu

F32 = jnp.float32
BF16 = jnp.bfloat16
MESH = pl.DeviceIdType.MESH

D_MODEL = 1024
N_HEADS = 8
DK = 128
DV = 256
CHUNK = 128
ROPE_BASE = 10000.0
D_FF = 2816
CONV_K = 31
FFN_K = 3
LN_EPS = 1e-5
ALPHA = (2.0 * 1) ** 0.25
ADAM_LR = 0.001
ADAM_B1 = 0.9
ADAM_B2 = 0.999
ADAM_EPS = 1e-08
ADAM_WD = 0.01
ADAM_STEP = 10

V7X_VMEM_BYTES = 64 * 1024 * 1024
VMEM_LIMIT = V7X_VMEM_BYTES - 8 * 1024 * 1024
ROW_TILE = 256
MM_TILE = 512
N_CHIPS = 4
N_DEV = 8

BIG_WEIGHTS = ("w_in", "w_up", "w_conv_out", "w_ret_out", "w_out", "w_down")


def _params(n_grid):
    return pltpu.CompilerParams(dimension_semantics=("arbitrary",) * n_grid, vmem_limit_bytes=VMEM_LIMIT)


def _rowwise(name, fn, rows, vecs, outs, reds, n_rows, tile=ROW_TILE, ncol=1, with_col=False, into=None):
    tile = _fit_tile(n_rows, tile)
    n_in = len(rows) + len(vecs)
    n_ref_in = n_in + (into is not None)
    n_out = len(outs)
    outs = [o if len(o) == 4 else (o[0], o[1], o[0] * ncol, 0) for o in outs]

    def col_map(off, row, first_row=0):
        def index(j, i):
            return (i + first_row // tile if row else 0, off(j) if callable(off) else off + j)
        return index

    def body(*refs):
        i = pl.program_id(1)
        vals = [r[...].astype(F32) for r in refs[:n_in]]
        res = fn(pl.program_id(0), *vals) if with_col else fn(*vals)
        for k in range(n_out):
            refs[n_ref_in + k][...] = res[k].astype(refs[n_ref_in + k].dtype)
        for k in range(len(reds)):
            o = refs[n_ref_in + n_out + k]

            @pl.when(i == 0)
            def _():
                o[...] = jnp.zeros_like(o)

            o[...] += res[n_out + k]

    in_specs = [pl.BlockSpec((tile, e[1]), col_map(e[2], True, e[3] if len(e) > 3 else 0)) for e in rows]
    in_specs += [pl.BlockSpec((1, w), col_map(off, False)) for _, w, off in vecs]
    args = [e[0] for e in rows] + [a for a, _, _ in vecs]
    aliases = {}
    if into is not None:
        in_specs.append(pl.BlockSpec(memory_space=pl.ANY))
        args.append(into[0])
        aliases = {n_in: into[1]}
    out_specs = [pl.BlockSpec((tile, w), col_map(off, True)) for w, _, _, off in outs]
    out_specs += [pl.BlockSpec((1, w), lambda j, i: (0, j)) for w in reds]
    out_shape = [jax.ShapeDtypeStruct((n_rows, total), dt) for _, dt, total, _ in outs]
    out_shape += [jax.ShapeDtypeStruct((1, w * ncol), F32) for w in reds]
    return pl.pallas_call(
        body, name=name, grid=(ncol, n_rows // tile), in_specs=in_specs, out_specs=out_specs, out_shape=out_shape,
        input_output_aliases=aliases, compiler_params=_params(2),
    )(*args)


def _vjp_rows(fn, n_row_in, n_ct):
    def bwd(*args):
        prim = [a.astype(F32) for a in args[:n_row_in] + args[n_row_in + n_ct:]]
        cts = tuple(a.astype(F32) for a in args[n_row_in:n_row_in + n_ct])
        _, pull = jax.vjp(fn, *prim)
        return pull(cts if n_ct > 1 else cts[0])

    return bwd


def _fit_tile(n, pref):
    if n <= pref:
        return n
    t = pref - pref % 16
    while n % t:
        t -= 16
    return t


def _col_tile(n):
    return n if n <= 1536 else n // 2


def _same_group(c):
    return c


def _mm_nn(name, a, w, bias, out_dtype, gmap=_same_group, tile=None, window=None, rows_outer=False):
    s, k = a.shape
    g, _, n = w.shape
    tm, tn = (min(MM_TILE, s), _col_tile(n)) if tile is None else (min(tile[0], s), tile[1])
    nt = n // tn
    first, n_cols = (0, g * n) if window is None else window
    t0 = first // tn

    def body(*refs):
        a_ref, w_ref = refs[0], refs[1]
        o_ref = refs[-1]
        acc = jnp.dot(a_ref[...].astype(BF16), w_ref[...], preferred_element_type=F32)
        if bias is not None:
            acc = acc + refs[2][...]
        o_ref[...] = acc.astype(o_ref.dtype)

    order = (lambda i, c: (c, i)) if rows_outer else (lambda c, i: (c, i))
    a_map = lambda *g_: (order(*g_)[1], 0)
    w_map = lambda *g_: (gmap((t0 + order(*g_)[0]) // nt), 0, (t0 + order(*g_)[0]) % nt)
    in_specs = [pl.BlockSpec((tm, k), a_map), pl.BlockSpec((None, k, tn), w_map)]
    args = [a, w]
    if bias is not None:
        in_specs.append(pl.BlockSpec((1, tn), lambda *g_: (0, t0 + order(*g_)[0])))
        args.append(bias)
    grid = (s // tm, n_cols // tn) if rows_outer else (n_cols // tn, s // tm)
    return pl.pallas_call(
        body, name=name, grid=grid, in_specs=in_specs, out_specs=pl.BlockSpec((tm, tn), lambda *g_: order(*g_)[::-1]),
        out_shape=jax.ShapeDtypeStruct((s, n_cols), out_dtype), compiler_params=_params(2),
    )(*args)


def _mm_nt(name, dy, w, out_dtype, gmap=_same_group):
    s = dy.shape[0]
    g, k, n = w.shape
    tm, tn = min(2 * MM_TILE if k <= 1024 else MM_TILE, s), n
    nt = 1
    steps = g

    def body(dy_ref, w_ref, o_ref, acc_ref):
        r = pl.program_id(1)
        part = lax.dot_general(dy_ref[...].astype(BF16), w_ref[...], (((1,), (1,)), ((), ())), preferred_element_type=F32)

        @pl.when(r == 0)
        def _():
            acc_ref[...] = part

        @pl.when(r > 0)
        def _():
            acc_ref[...] += part

        @pl.when(r == steps - 1)
        def _():
            o_ref[...] = acc_ref[...].astype(o_ref.dtype)

    return pl.pallas_call(
        body, name=name, grid=(s // tm, steps),
        in_specs=[pl.BlockSpec((tm, tn), lambda i, r: (i, r)),
                  pl.BlockSpec((None, k, tn), lambda i, r: (gmap(r // nt), 0, r % nt))],
        out_specs=pl.BlockSpec((tm, k), lambda i, r: (i, 0)), out_shape=jax.ShapeDtypeStruct((s, k), out_dtype),
        scratch_shapes=[pltpu.VMEM((tm, k), F32)], compiler_params=_params(2),
    )(dy, w)


def _mm_tn(name, a, dy, g, gmap=_same_group, colsum=False):
    s, k = a.shape
    n = dy.shape[1] // g
    ts = min(2048 if k <= 1024 else 1024, s)
    tn = n if k * n <= 1024 * 1536 else (n // 2 if (n // 2) % 128 == 0 else n)
    nt = n // tn

    def body(a_ref, dy_ref, o_ref, *sum_ref):
        t = pl.program_id(1)
        dyv = dy_ref[...].astype(BF16)
        part = lax.dot_general(a_ref[...].astype(BF16), dyv, (((0,), (0,)), ((), ())), preferred_element_type=F32)

        @pl.when(t == 0)
        def _():
            o_ref[...] = part

        @pl.when(t > 0)
        def _():
            o_ref[...] += part

        if colsum:
            col = jnp.sum(dyv.astype(F32), axis=0, keepdims=True)

            @pl.when(t == 0)
            def _():
                sum_ref[0][...] = col

            @pl.when(t > 0)
            def _():
                sum_ref[0][...] += col

    out_specs = [pl.BlockSpec((None, k, tn), lambda c, t: (gmap(c // nt), 0, c % nt))]
    out_shape = [jax.ShapeDtypeStruct((g, k, n), F32)]
    if colsum:
        out_specs.append(pl.BlockSpec((1, tn), lambda c, t: (0, c)))
        out_shape.append(jax.ShapeDtypeStruct((1, g * n), F32))
    res = pl.pallas_call(
        body, name=name, grid=(g * nt, s // ts),
        in_specs=[pl.BlockSpec((ts, k), lambda c, t: (t, 0)), pl.BlockSpec((ts, tn), lambda c, t: (t, c))],
        out_specs=out_specs, out_shape=out_shape, compiler_params=_params(2),
    )(a, dy)
    return res if colsum else res[0]


SUBLANES = 8


def _tap_sum(read, w_row, offsets, tile):
    acc = None
    for b in range(SUBLANES):
        group = [(k, o) for k, o in offsets if o % SUBLANES == b]
        if not group:
            continue
        rows = tile if b == 0 else tile + SUBLANES
        z = None
        for k, o in group:
            term = w_row(k) * read(o - b, rows)
            z = term if z is None else z + term
        part = z if b == 0 else z[b:b + tile]
        acc = part if acc is None else acc + part
    return acc


def _tap_grads(read, dy, offsets, tile):
    padded = jnp.concatenate([dy, jnp.zeros((SUBLANES, dy.shape[1]), dy.dtype)], axis=0)
    out = {}
    for b in range(SUBLANES):
        group = [(k, o) for k, o in offsets if o % SUBLANES == b]
        if not group:
            continue
        shifted = dy if b == 0 else pltpu.roll(padded, b, 0)
        rows = tile if b == 0 else tile + SUBLANES
        for k, o in group:
            out[k] = jnp.sum(shifted * read(o - b, rows), axis=0, keepdims=True)
    return out


CONV_HALO = 32
CONV_TILE = 256
CONV_LANES = 256


def _glu_conv_fwd(proj, col_v, col_g, w, b, n_rows):
    kw, n_ch = w.shape
    tile = min(CONV_TILE, n_rows)
    per = tile // CONV_HALO
    offsets = [(k, CONV_HALO - (kw - 1) + k) for k in range(kw)]

    def body(v_ref, g_ref, vh_ref, gh_ref, w_ref, b_ref, o_ref, buf):
        i = pl.program_id(0)
        prev = vh_ref[...].astype(F32) * jax.nn.sigmoid(gh_ref[...].astype(F32))
        buf[0:CONV_HALO, :] = jnp.where(i == 0, 0.0, prev)
        buf[CONV_HALO:CONV_HALO + tile, :] = v_ref[...].astype(F32) * jax.nn.sigmoid(g_ref[...].astype(F32))
        for c0 in range(0, n_ch, CONV_LANES):
            cols = slice(c0, c0 + CONV_LANES)
            o_ref[:, cols] = b_ref[:, cols] + _tap_sum(lambda s, n: buf[pl.ds(s, n), cols], lambda k: w_ref[k:k + 1, cols],
                                                      offsets, tile)

    main = lambda col: pl.BlockSpec((tile, n_ch), lambda i: (i, col))
    halo = lambda col: pl.BlockSpec((CONV_HALO, n_ch), lambda i: (jnp.maximum(i * per - 1, 0), col))
    return pl.pallas_call(
        body, name="conv_fwd", grid=(n_rows // tile,),
        in_specs=[main(col_v), main(col_g), halo(col_v), halo(col_g), pl.BlockSpec((kw, n_ch), lambda i: (0, 0)),
                  pl.BlockSpec((1, n_ch), lambda i: (0, 0))],
        out_specs=pl.BlockSpec((tile, n_ch), lambda i: (i, 0)), out_shape=jax.ShapeDtypeStruct((n_rows, n_ch), F32),
        scratch_shapes=[pltpu.VMEM((CONV_HALO + tile, n_ch), F32)], compiler_params=_params(1),
    )(proj, proj, proj, proj, w, b)


def _glu_conv_bwd(proj, col_v, col_g, dy, w, dproj, col_out, n_rows):
    kw, n_ch = w.shape
    tile = min(CONV_TILE, n_rows)
    per = tile // CONV_HALO
    n_tiles = n_rows // tile
    last_halo = n_rows // CONV_HALO - 1
    offsets = [(k, CONV_HALO - (kw - 1) + k) for k in range(kw)]
    back = [(k, kw - 1 - k) for k in range(kw)]

    def body(v_ref, g_ref, vh_ref, gh_ref, dy_ref, dyn_ref, w_ref, dp_any, dx_ref, dw_ref, db_ref, buf, dbuf):
        i = pl.program_id(0)
        cv, cg = v_ref[...].astype(F32), g_ref[...].astype(F32)
        sig = jax.nn.sigmoid(cg)
        prev = vh_ref[...].astype(F32) * jax.nn.sigmoid(gh_ref[...].astype(F32))
        buf[0:CONV_HALO, :] = jnp.where(i == 0, 0.0, prev)
        buf[CONV_HALO:CONV_HALO + tile, :] = cv * sig
        dbuf[0:tile, :] = dy_ref[...]
        dbuf[tile:tile + CONV_HALO, :] = jnp.where(i == n_tiles - 1, 0.0, dyn_ref[...])

        @pl.when(i == 0)
        def _():
            dw_ref[...] = jnp.zeros_like(dw_ref)
            db_ref[...] = jnp.zeros_like(db_ref)

        db_ref[...] += jnp.sum(dy_ref[...], axis=0, keepdims=True)
        for c0 in range(0, n_ch, CONV_LANES):
            cols = slice(c0, c0 + CONV_LANES)
            w_row = lambda k: w_ref[k:k + 1, cols]
            dx = _tap_sum(lambda s, n: dbuf[pl.ds(s, n), cols], w_row, back, tile)
            grads = _tap_grads(lambda s, n: buf[pl.ds(s, n), cols], dy_ref[:, cols], offsets, tile)
            for k in range(kw):
                dw_ref[k:k + 1, cols] += grads[k]
            sg = sig[:, cols]
            dx_ref[:, c0:c0 + CONV_LANES] = (dx * sg).astype(dx_ref.dtype)
            dx_ref[:, n_ch + c0:n_ch + c0 + CONV_LANES] = (dx * cv[:, cols] * sg * (1.0 - sg)).astype(dx_ref.dtype)

    main = lambda col: pl.BlockSpec((tile, n_ch), lambda i: (i, col))
    halo = lambda col: pl.BlockSpec((CONV_HALO, n_ch), lambda i: (jnp.maximum(i * per - 1, 0), col))
    return pl.pallas_call(
        body, name="conv_bwd", grid=(n_tiles,),
        in_specs=[main(col_v), main(col_g), halo(col_v), halo(col_g), main(0),
                  pl.BlockSpec((CONV_HALO, n_ch), lambda i: (jnp.minimum((i + 1) * per, last_halo), 0)),
                  pl.BlockSpec((kw, n_ch), lambda i: (0, 0)), pl.BlockSpec(memory_space=pl.ANY)],
        out_specs=[pl.BlockSpec((tile, 2 * n_ch), lambda i: (i, col_out)), pl.BlockSpec((kw, n_ch), lambda i: (0, 0)),
                   pl.BlockSpec((1, n_ch), lambda i: (0, 0))],
        out_shape=[jax.ShapeDtypeStruct(dproj.shape, dproj.dtype), jax.ShapeDtypeStruct((kw, n_ch), F32),
                   jax.ShapeDtypeStruct((1, n_ch), F32)],
        input_output_aliases={7: 0},
        scratch_shapes=[pltpu.VMEM((CONV_HALO + tile, n_ch), F32), pltpu.VMEM((tile + CONV_HALO, n_ch), F32)],
        compiler_params=_params(1),
    )(proj, proj, proj, proj, dy, dy, w, dproj)


FFN_HALO = 8


def _ffn_fwd(u, w, b, n_rows):
    kw = w.shape[0]
    half = u.shape[1] // 4
    tile = min(ROW_TILE, n_rows)
    per = tile // FFN_HALO
    offsets = [(k, FFN_HALO - (kw - 1) + k) for k in range(kw)]

    def body(u_ref, uh_ref, w_ref, b_ref, p_ref, f_ref, buf):
        i = pl.program_id(1)
        buf[0:FFN_HALO, :] = jnp.where(i == 0, 0.0, uh_ref[...].astype(F32))
        buf[FFN_HALO:FFN_HALO + tile, :] = u_ref[...].astype(F32)
        conv = []
        for c0 in (0, half):
            cols = slice(c0, c0 + half)
            conv.append(b_ref[:, cols] + _tap_sum(lambda s, n: buf[pl.ds(s, n), cols], lambda k: w_ref[k:k + 1, cols],
                                                  offsets, tile))
            f_ref[:, cols] = conv[-1].astype(f_ref.dtype)
        p_ref[...] = (conv[0] * jax.nn.silu(conv[1])).astype(p_ref.dtype)

    return pl.pallas_call(
        body, name="ffn_fwd", grid=(2, n_rows // tile),
        in_specs=[pl.BlockSpec((tile, 2 * half), lambda j, i: (i, j)),
                  pl.BlockSpec((FFN_HALO, 2 * half), lambda j, i: (jnp.maximum(i * per - 1, 0), j)),
                  pl.BlockSpec((kw, 2 * half), lambda j, i: (0, j)), pl.BlockSpec((1, 2 * half), lambda j, i: (0, j))],
        out_specs=[pl.BlockSpec((tile, half), lambda j, i: (i, j)), pl.BlockSpec((tile, 2 * half), lambda j, i: (i, j))],
        out_shape=[jax.ShapeDtypeStruct((n_rows, 2 * half), BF16), jax.ShapeDtypeStruct(u.shape, BF16)],
        scratch_shapes=[pltpu.VMEM((FFN_HALO + tile, 2 * half), F32)], compiler_params=_params(2),
    )(u, u, w, b)


def _ffn_bwd(u, filtered, dp, w, n_rows):
    kw = w.shape[0]
    half = u.shape[1] // 4
    tile = min(ROW_TILE, n_rows)
    per = tile // FFN_HALO
    n_tiles = n_rows // tile
    last_halo = n_rows // FFN_HALO - 1
    ext = tile + FFN_HALO
    offsets = [(k, FFN_HALO - (kw - 1) + k) for k in range(kw)]
    back = [(k, kw - 1 - k) for k in range(kw)]

    def body(u_ref, up_ref, f_ref, fn_ref, dp_ref, dpn_ref, w_ref, du_ref, dw_ref, db_ref, buf, dbuf):
        i = pl.program_id(1)
        buf[0:FFN_HALO, :] = jnp.where(i == 0, 0.0, up_ref[...].astype(F32))
        buf[FFN_HALO:FFN_HALO + tile, :] = u_ref[...].astype(F32)
        filt = jnp.concatenate([f_ref[...], fn_ref[...]], axis=0).astype(F32)
        val, gate = filt[:, 0:half], filt[:, half:2 * half]
        dpe = jnp.concatenate([dp_ref[...], jnp.where(i == n_tiles - 1, 0.0, dpn_ref[...])], axis=0)
        sig = jax.nn.sigmoid(gate)
        dbuf[:, 0:half] = dpe * gate * sig
        dbuf[:, half:2 * half] = dpe * val * sig * (1.0 + gate * (1.0 - sig))

        @pl.when(i == 0)
        def _():
            dw_ref[...] = jnp.zeros_like(dw_ref)
            db_ref[...] = jnp.zeros_like(db_ref)

        for c0 in (0, half):
            cols = slice(c0, c0 + half)
            du_ref[:, cols] = _tap_sum(lambda s, n: dbuf[pl.ds(s, n), cols], lambda k: w_ref[k:k + 1, cols], back,
                                       tile).astype(du_ref.dtype)
            d_main = dbuf[0:tile, cols]
            db_ref[:, cols] += jnp.sum(d_main, axis=0, keepdims=True)
            grads = _tap_grads(lambda s, n: buf[pl.ds(s, n), cols], d_main, offsets, tile)
            for k in range(kw):
                dw_ref[k:k + 1, cols] += grads[k]

    wide = 2 * half
    return pl.pallas_call(
        body, name="ffn_bwd", grid=(2, n_tiles),
        in_specs=[pl.BlockSpec((tile, wide), lambda j, i: (i, j)),
                  pl.BlockSpec((FFN_HALO, wide), lambda j, i: (jnp.maximum(i * per - 1, 0), j)),
                  pl.BlockSpec((tile, wide), lambda j, i: (i, j)),
                  pl.BlockSpec((FFN_HALO, wide), lambda j, i: (jnp.minimum((i + 1) * per, last_halo), j)),
                  pl.BlockSpec((tile, half), lambda j, i: (i, j)),
                  pl.BlockSpec((FFN_HALO, half), lambda j, i: (jnp.minimum((i + 1) * per, last_halo), j)),
                  pl.BlockSpec((kw, wide), lambda j, i: (0, j))],
        out_specs=[pl.BlockSpec((tile, wide), lambda j, i: (i, j)), pl.BlockSpec((kw, wide), lambda j, i: (0, j)),
                   pl.BlockSpec((1, wide), lambda j, i: (0, j))],
        out_shape=[jax.ShapeDtypeStruct(u.shape, BF16), jax.ShapeDtypeStruct((kw, u.shape[1]), F32),
                   jax.ShapeDtypeStruct((1, u.shape[1]), F32)],
        scratch_shapes=[pltpu.VMEM((FFN_HALO + tile, wide), F32), pltpu.VMEM((ext, wide), F32)], compiler_params=_params(2),
    )(u, u, filtered, filtered, dp, dp, w)


def _retention_consts():
    log_gamma = jnp.log(1.0 - 2.0 ** (-5.0 - jnp.arange(N_HEADS, dtype=F32)))
    idx = jnp.arange(CHUNK, dtype=F32)
    rel = idx[:, None] - idx[None, :]
    decay = jnp.where(rel[None] >= 0, jnp.exp(log_gamma[:, None, None] * jnp.maximum(rel, 0.0)[None]), 0.0)
    zeta = jnp.exp(log_gamma[:, None] * (CHUNK - 1.0 - idx)[None])
    xi = jnp.exp(log_gamma[:, None] * (idx + 1.0)[None])
    chunk_decay = jnp.exp(log_gamma * CHUNK)
    xi_b = jnp.broadcast_to(xi[:, :, None], (N_HEADS, CHUNK, DK))
    zeta_b = jnp.broadcast_to(zeta[:, :, None], (N_HEADS, CHUNK, DK))
    cd_b = jnp.broadcast_to(chunk_decay[:, None, None], (N_HEADS, 8, DV))
    return decay, xi_b, zeta_b, cd_b


def _rope_tables(positions):
    half = DK // 2
    inv_freq = ROPE_BASE ** (-jnp.arange(half, dtype=F32) / half)
    ang = positions.astype(F32)[:, None] * inv_freq
    cos, sin = jnp.cos(ang), jnp.sin(ang)
    return jnp.concatenate([cos, cos], axis=-1), jnp.concatenate([-sin, sin], axis=-1)


def _swap_halves(v):
    return pltpu.roll(v, DK // 2, 1)


def _dot(a, b):
    return jnp.dot(a, b, preferred_element_type=F32)


def _dot_nt(a, b):
    return lax.dot_general(a, b, (((1,), (1,)), ((), ())), preferred_element_type=F32)


def _dot_tn(a, b):
    return lax.dot_general(a, b, (((0,), (0,)), ((), ())), preferred_element_type=F32)


def _const_specs():
    return [pl.BlockSpec((N_HEADS, CHUNK, CHUNK), lambda n: (0, 0, 0)), pl.BlockSpec((N_HEADS, CHUNK, DK), lambda n: (0, 0, 0)),
            pl.BlockSpec((N_HEADS, CHUNK, DK), lambda n: (0, 0, 0)), pl.BlockSpec((N_HEADS, 8, DV), lambda n: (0, 0, 0))]


def _retention_fwd(proj_qk, proj_rest, cos_t, sin_t, consts, n_rows):
    n_chunks = n_rows // CHUNK
    scale = DK ** -0.5

    def body(q_ref, k_ref, v_ref, cf_ref, ss_ref, d_ref, xi_ref, zt_ref, cd_ref, r_ref, st_ref, state):
        @pl.when(pl.program_id(0) == 0)
        def _():
            state[...] = jnp.zeros_like(state)

        cf, ss = cf_ref[...], ss_ref[...]
        for h in range(N_HEADS):
            qh = q_ref[:, h * DK:(h + 1) * DK].astype(F32)
            kh = k_ref[:, h * DK:(h + 1) * DK].astype(F32)
            qh = qh * cf + _swap_halves(qh) * ss
            kh = (kh * cf + _swap_halves(kh) * ss) * scale
            vh = v_ref[:, h * DV:(h + 1) * DV].astype(BF16)
            st = state[h]
            st_ref[0, h] = st
            sd = _dot_nt(qh.astype(BF16), kh.astype(BF16)) * d_ref[h]
            inner = _dot(sd.astype(BF16), vh)
            cross = _dot((qh * xi_ref[h]).astype(BF16), st.astype(BF16))
            kv = _dot_tn((kh * zt_ref[h]).astype(BF16), vh)
            state[h] = st * cd_ref[h, 0:1, :] + kv
            r_ref[:, h * DV:(h + 1) * DV] = inner + cross

    qk = N_HEADS * DK
    vw = N_HEADS * DV
    return pl.pallas_call(
        body, name="retention_fwd", grid=(n_chunks,),
        in_specs=[pl.BlockSpec((CHUNK, qk), lambda n: (n, 0)), pl.BlockSpec((CHUNK, qk), lambda n: (n, 1)),
                  pl.BlockSpec((CHUNK, vw), lambda n: (n, 0)), pl.BlockSpec((CHUNK, DK), lambda n: (n, 0)),
                  pl.BlockSpec((CHUNK, DK), lambda n: (n, 0))] + _const_specs(),
        out_specs=[pl.BlockSpec((CHUNK, vw), lambda n: (n, 0)), pl.BlockSpec((1, N_HEADS, DK, DV), lambda n: (n, 0, 0, 0))],
        out_shape=[jax.ShapeDtypeStruct((n_rows, vw), F32), jax.ShapeDtypeStruct((n_chunks, N_HEADS, DK, DV), F32)],
        scratch_shapes=[pltpu.VMEM((N_HEADS, DK, DV), F32)], compiler_params=_params(1),
    )(proj_qk, proj_qk, proj_rest, cos_t, sin_t, *consts)


def _retention_bwd(proj_qk, proj_rest, cos_t, sin_t, states, dr, consts, dproj, n_rows):
    n_chunks = n_rows // CHUNK
    scale = DK ** -0.5
    qk = N_HEADS * DK
    vw = N_HEADS * DV

    def body(q_ref, k_ref, v_ref, cf_ref, ss_ref, st_ref, dr_ref, d_ref, xi_ref, zt_ref, cd_ref, dp_any, dqkv_ref, g_ref):
        dq_ref, dk_ref, dv_ref = dqkv_ref.at[:, 0:qk], dqkv_ref.at[:, qk:2 * qk], dqkv_ref.at[:, 2 * qk:2 * qk + vw]

        @pl.when(pl.program_id(0) == 0)
        def _():
            g_ref[...] = jnp.zeros_like(g_ref)

        cf, ss = cf_ref[...], ss_ref[...]
        for h in range(N_HEADS):
            qh = q_ref[:, h * DK:(h + 1) * DK].astype(F32)
            kh = k_ref[:, h * DK:(h + 1) * DK].astype(F32)
            qh = qh * cf + _swap_halves(qh) * ss
            kh = (kh * cf + _swap_halves(kh) * ss) * scale
            qb, kb = qh.astype(BF16), kh.astype(BF16)
            vh = v_ref[:, h * DV:(h + 1) * DV].astype(BF16)
            do = dr_ref[:, h * DV:(h + 1) * DV].astype(BF16)
            rb = st_ref[0, h].astype(BF16)
            g = g_ref[h]
            gb = g.astype(BF16)
            dec, xi, zt = d_ref[h], xi_ref[h], zt_ref[h]
            sd = (_dot_nt(qb, kb) * dec).astype(BF16)
            ds = (_dot_nt(do, vh) * dec).astype(BF16)
            dqh = _dot(ds, kb) + _dot_nt(do, rb) * xi
            dkh = (_dot_tn(ds, qb) + _dot_nt(vh, gb) * zt) * scale
            dvh = _dot_tn(sd, do) + _dot((kh * zt).astype(BF16), gb)
            g_ref[h] = g * cd_ref[h, 0:1, :] + _dot_tn((qh * xi).astype(BF16), do)
            dq_ref[:, h * DK:(h + 1) * DK] = (dqh * cf + _swap_halves(dqh * ss)).astype(dq_ref.dtype)
            dk_ref[:, h * DK:(h + 1) * DK] = (dkh * cf + _swap_halves(dkh * ss)).astype(dk_ref.dtype)
            dv_ref[:, h * DV:(h + 1) * DV] = dvh.astype(dv_ref.dtype)

    last = n_chunks - 1
    return pl.pallas_call(
        body, name="retention_bwd", grid=(n_chunks,),
        in_specs=[pl.BlockSpec((CHUNK, qk), lambda n: (last - n, 0)), pl.BlockSpec((CHUNK, qk), lambda n: (last - n, 1)),
                  pl.BlockSpec((CHUNK, vw), lambda n: (last - n, 0)), pl.BlockSpec((CHUNK, DK), lambda n: (last - n, 0)),
                  pl.BlockSpec((CHUNK, DK), lambda n: (last - n, 0)),
                  pl.BlockSpec((1, N_HEADS, DK, DV), lambda n: (last - n, 0, 0, 0)),
                  pl.BlockSpec((CHUNK, vw), lambda n: (last - n, 0))] + _const_specs() + [pl.BlockSpec(memory_space=pl.ANY)],
        out_specs=pl.BlockSpec((CHUNK, 2 * qk + vw), lambda n: (last - n, 0)),
        out_shape=jax.ShapeDtypeStruct(dproj.shape, dproj.dtype), input_output_aliases={11: 0},
        scratch_shapes=[pltpu.VMEM((N_HEADS, DK, DV), F32)], compiler_params=_params(1),
    )(proj_qk, proj_qk, proj_rest, cos_t, sin_t, states, dr, *consts, dproj)


def _ln(v):
    mu = jnp.mean(v, axis=-1, keepdims=True)
    var = jnp.mean(jnp.square(v - mu), axis=-1, keepdims=True)
    return (v - mu) * lax.rsqrt(var + LN_EPS)


def _f_modulate(x, scale, shift):
    return _ln(x) * (1.0 + scale) + shift


def _f_conv_norm(a1, g, b):
    return jax.nn.silu(_ln(a1) * g + b)


def _f_group_norm_gate(r, gate, g, b):
    return (_ln(r) * g + b) * jax.nn.silu(gate)


def _per_head(fn):
    def run(*arrays):
        parts = [fn(*[a[:, h * DV:(h + 1) * DV] for a in arrays]) for h in range(N_HEADS)]
        if not isinstance(parts[0], (tuple, list)):
            return (jnp.concatenate(parts, axis=1),)
        return tuple(jnp.concatenate([p[k] for p in parts], axis=1) for k in range(len(parts[0])))

    return run


def _f_merge(ga, gb, ya, yb):
    return jax.nn.sigmoid(ga) * ya + jax.nn.sigmoid(gb) * yb


def _f_post1(x, t, gate1, g1, b1, scale2, shift2):
    x1 = _ln(ALPHA * x + gate1 * t) * g1 + b1
    return x1, _ln(x1) * (1.0 + scale2) + shift2


def _f_loss(x1, f, gate2, g2, b2, target):
    y = _ln(ALPHA * x1 + gate2 * f) * g2 + b2
    return 0.5 * jnp.sum(jnp.mean(jnp.square(y - target), axis=-1))


ANY = pl.BlockSpec(memory_space=pl.ANY)


def _allgather8(name, blocks, own_half=False):
    n = len(blocks)
    rows = [b.shape[0] // 2 if own_half else b.shape[0] for b in blocks]

    def body(*refs):
        x_refs, out_refs = refs[:n], refs[n:2 * n]
        send_sems, recv_sems = refs[2 * n:]
        x, y, c = lax.axis_index("x"), lax.axis_index("y"), lax.axis_index("c")
        me, sibling = (x, y, c), (x, y, 1 - c)
        chips = [(1 - x, y), (x, 1 - y), (1 - x, 1 - y)]
        every = range(n)

        def src(a):
            return x_refs[a].at[pl.ds(c * rows[a], rows[a])] if own_half else x_refs[a]

        def slot(a, px, py, pc):
            return out_refs[a].at[4 * px + 2 * py + pc]

        def copy(k, a, block, to, from_input=False):
            return pltpu.make_async_remote_copy(
                src_ref=src(a) if from_input else slot(a, *block), dst_ref=slot(a, *block), send_sem=send_sems.at[k, a],
                recv_sem=recv_sems.at[k, a], device_id=to, device_id_type=MESH)

        first = [copy(0, a, me, sibling, True) for a in every]
        first += [copy(1 + j, a, me, (*chip, c), True) for j, chip in enumerate(chips) for a in every]
        for cp in first:
            cp.start()
        passed = [[copy(4 + j, a, (*chip, c), sibling) for a in every] for j, chip in enumerate(chips)]
        for j, chip in enumerate(chips):
            for a in every:
                copy(1 + j, a, (*chip, c), me).wait_recv()
            for cp in passed[j]:
                cp.start()
        for a in every:
            copy(0, a, sibling, me).wait_recv()
        for j, chip in enumerate(chips):
            for a in every:
                copy(4 + j, a, (*chip, 1 - c), me).wait_recv()
        for cp in first + [cp for group in passed for cp in group]:
            cp.wait_send()

    gathered = pl.pallas_call(
        body, name=name, in_specs=[ANY] * n, out_specs=[ANY] * n,
        out_shape=[jax.ShapeDtypeStruct((N_DEV, r, b.shape[1]), b.dtype) for r, b in zip(rows, blocks)],
        scratch_shapes=[pltpu.SemaphoreType.DMA((7, n)), pltpu.SemaphoreType.DMA((7, n))],
    )(*blocks)
    c = lax.axis_index("c")
    me = 4 * lax.axis_index("x") + 2 * lax.axis_index("y") + c
    own = [lax.dynamic_slice_in_dim(b, c * r, r, 0) if own_half else b for r, b in zip(rows, blocks)]
    return [lax.dynamic_update_slice_in_dim(g_, o_[None], me, 0) for g_, o_ in zip(gathered, own)]


def _sibling_swap(name, arrays):
    n = len(arrays)
    halves = [a.shape[1] // 2 for a in arrays]

    def body(*refs):
        g_refs, theirs = refs[:n], refs[n:2 * n]
        send_sems, recv_sems = refs[2 * n:]
        x, y, c = lax.axis_index("x"), lax.axis_index("y"), lax.axis_index("c")
        remote = [pltpu.make_async_remote_copy(
            src_ref=g_refs[a].at[:, pl.ds((1 - c) * halves[a], halves[a]), :], dst_ref=theirs[a], send_sem=send_sems.at[a],
            recv_sem=recv_sems.at[a], device_id=(x, y, 1 - c), device_id_type=MESH) for a in range(n)]
        for cp in remote:
            cp.start()
        for cp in remote:
            cp.wait_recv()
        for cp in remote:
            cp.wait_send()

    return pl.pallas_call(
        body, name=name, in_specs=[ANY] * n, out_specs=[ANY] * n,
        out_shape=[jax.ShapeDtypeStruct((a.shape[0], h, a.shape[2]), a.dtype) for a, h in zip(arrays, halves)],
        scratch_shapes=[pltpu.SemaphoreType.DMA((n,)), pltpu.SemaphoreType.DMA((n,))],
    )(*arrays)


HBM = pl.BlockSpec(memory_space=pltpu.HBM)
SEM = pl.BlockSpec(memory_space=pltpu.SEMAPHORE)
DATAFLOW = pltpu.SideEffectType.DATAFLOW_SIDE_EFFECTING


def _chip_peers(x, y):
    return [(1 - x, y), (x, 1 - y), (1 - x, 1 - y)]


def _other_devices(x, y, c):
    flip = lambda v, f: 1 - v if f else v
    return [(flip(x, k & 4), flip(y, k & 2), flip(c, k & 1)) for k in range(1, N_DEV)]


def _gather_start(name, arrays):
    n = len(arrays)
    rows = [a.shape[0] // 2 for a in arrays]
    zones = [lax.empty((N_DEV, r, a.shape[1]), a.dtype) for r, a in zip(rows, arrays)]

    def body(*refs):
        x_refs, zone_refs = refs[:n], refs[n:2 * n]
        send_sems, recv_sems = refs[2 * n], refs[2 * n + 1]
        token = refs[-1]
        x, y, c = lax.axis_index("x"), lax.axis_index("y"), lax.axis_index("c")
        for k, peer in enumerate(_other_devices(x, y, c)):
            for a in range(n):
                pltpu.make_async_remote_copy(
                    src_ref=x_refs[a].at[pl.ds(c * rows[a], rows[a])], dst_ref=zone_refs[a].at[4 * x + 2 * y + c],
                    send_sem=send_sems.at[k * n + a], recv_sem=recv_sems.at[k * n + a], device_id=peer,
                    device_id_type=MESH).start()
        token[...] = jnp.zeros_like(token)

    thru = [pltpu.HBM(a.shape, a.dtype) for a in arrays] + [pltpu.HBM(z.shape, z.dtype) for z in zones]
    n_sems = (N_DEV - 1) * n
    res = pl.pallas_call(
        body, name=name, in_specs=[HBM] * (2 * n),
        out_specs=(SEM, SEM, *[HBM] * (2 * n), pl.BlockSpec(memory_space=pltpu.VMEM)),
        out_shape=(pltpu.SemaphoreType.DMA((n_sems,)), pltpu.SemaphoreType.DMA((n_sems,)), *thru,
                   jax.ShapeDtypeStruct((8, 128), F32)),
        input_output_aliases={i: 2 + i for i in range(2 * n)},
        compiler_params=pltpu.CompilerParams(has_side_effects=DATAFLOW),
    )(*[pltpu.with_memory_space_constraint(a, pltpu.HBM) for a in arrays],
      *[pltpu.with_memory_space_constraint(z, pltpu.HBM) for z in zones])
    return (res[0], res[1], list(res[2:2 + n]), list(res[2 + n:2 + 2 * n])), res[-1]


def _gather_wait(name, handle, after):
    send_sems, recv_sems, sources, zones = handle
    n = len(sources)
    rows = [z.shape[1] for z in zones]

    def body(*refs):
        x_refs, zone_refs = refs[:n], refs[n:2 * n]
        send_sems, recv_sems = refs[2 * n], refs[2 * n + 1]
        x, y, c = lax.axis_index("x"), lax.axis_index("y"), lax.axis_index("c")
        for k, (px, py, pc) in enumerate(_other_devices(x, y, c)):
            for a in range(n):
                cp = pltpu.make_async_remote_copy(
                    src_ref=x_refs[a].at[pl.ds(c * rows[a], rows[a])], dst_ref=zone_refs[a].at[4 * px + 2 * py + pc],
                    send_sem=send_sems.at[k * n + a], recv_sem=recv_sems.at[k * n + a], device_id=(px, py, pc),
                    device_id_type=MESH)
                cp.wait_send()
                cp.wait_recv()

    res = pl.pallas_call(
        body, name=name, in_specs=[HBM] * (2 * n) + [SEM, SEM, ANY], out_specs=[HBM] * (2 * n),
        out_shape=[pltpu.HBM(a.shape, a.dtype) for a in sources + zones],
        input_output_aliases={i: i for i in range(2 * n)}, compiler_params=pltpu.CompilerParams(has_side_effects=DATAFLOW),
    )(*sources, *zones, send_sems, recv_sems, after)
    return list(res[n:])


def _alltoall_start(name, arrays):
    n = len(arrays)
    lands = [lax.empty((3,) + a.shape[1:], a.dtype) for a in arrays]

    def body(*refs):
        p_refs, land_refs = refs[:n], refs[n:2 * n]
        send_sems, recv_sems = refs[2 * n], refs[2 * n + 1]
        token = refs[-1]
        x, y, c = lax.axis_index("x"), lax.axis_index("y"), lax.axis_index("c")
        for k, (px, py) in enumerate(_chip_peers(x, y)):
            for a in range(n):
                pltpu.make_async_remote_copy(
                    src_ref=p_refs[a].at[2 * px + py], dst_ref=land_refs[a].at[k], send_sem=send_sems.at[k * n + a],
                    recv_sem=recv_sems.at[k * n + a], device_id=(px, py, c), device_id_type=MESH).start()
        token[...] = jnp.zeros_like(token)

    thru = [pltpu.HBM(a.shape, a.dtype) for a in arrays] + [pltpu.HBM(l.shape, l.dtype) for l in lands]
    res = pl.pallas_call(
        body, name=name, in_specs=[HBM] * (2 * n),
        out_specs=(SEM, SEM, *[HBM] * (2 * n), pl.BlockSpec(memory_space=pltpu.VMEM)),
        out_shape=(pltpu.SemaphoreType.DMA((3 * n,)), pltpu.SemaphoreType.DMA((3 * n,)), *thru, jax.ShapeDtypeStruct((8, 128), F32)),
        input_output_aliases={i: 2 + i for i in range(2 * n)},
        compiler_params=pltpu.CompilerParams(has_side_effects=DATAFLOW),
    )(*[pltpu.with_memory_space_constraint(a, pltpu.HBM) for a in arrays],
      *[pltpu.with_memory_space_constraint(l, pltpu.HBM) for l in lands])
    return (res[0], res[1], list(res[2:2 + n]), list(res[2 + n:2 + 2 * n])), res[-1]


def _alltoall_wait(name, handle, after):
    send_sems, recv_sems, sources, lands = handle
    n = len(sources)

    def body(*refs):
        p_refs, land_refs = refs[:n], refs[n:2 * n]
        send_sems, recv_sems = refs[2 * n], refs[2 * n + 1]
        x, y, c = lax.axis_index("x"), lax.axis_index("y"), lax.axis_index("c")
        for k, (px, py) in enumerate(_chip_peers(x, y)):
            for a in range(n):
                cp = pltpu.make_async_remote_copy(
                    src_ref=p_refs[a].at[2 * px + py], dst_ref=land_refs[a].at[k], send_sem=send_sems.at[k * n + a],
                    recv_sem=recv_sems.at[k * n + a], device_id=(px, py, c), device_id_type=MESH)
                cp.wait_send()
                cp.wait_recv()

    res = pl.pallas_call(
        body, name=name, in_specs=[HBM] * (2 * n) + [SEM, SEM, ANY], out_specs=[HBM] * (2 * n),
        out_shape=[pltpu.HBM(a.shape, a.dtype) for a in sources + lands],
        input_output_aliases={i: i for i in range(2 * n)}, compiler_params=pltpu.CompilerParams(has_side_effects=DATAFLOW),
    )(*sources, *lands, send_sems, recv_sems, after)
    return list(res[n:])


def _sibling_exchange(name, halves):
    n = len(halves)

    def body(*refs):
        h_refs, out_refs = refs[:n], refs[n:2 * n]
        send_sems, recv_sems = refs[2 * n:]
        x, y, c = lax.axis_index("x"), lax.axis_index("y"), lax.axis_index("c")
        remote = [pltpu.make_async_remote_copy(
            src_ref=h_refs[a], dst_ref=out_refs[a], send_sem=send_sems.at[a], recv_sem=recv_sems.at[a],
            device_id=(x, y, 1 - c), device_id_type=MESH) for a in range(n)]
        for cp in remote:
            cp.start()
        for cp in remote:
            cp.wait_recv()
        for cp in remote:
            cp.wait_send()

    return pl.pallas_call(
        body, name=name, in_specs=[ANY] * n, out_specs=[ANY] * n,
        out_shape=[jax.ShapeDtypeStruct(h.shape, h.dtype) for h in halves],
        scratch_shapes=[pltpu.SemaphoreType.DMA((n,)), pltpu.SemaphoreType.DMA((n,))],
    )(*halves)


def _sum_arrays(name, terms, n_rows):
    c_ = terms[0][0].shape[1]

    def add_all(*vals):
        acc = vals[0]
        for v in vals[1:]:
            acc = acc + v
        return (acc,)

    return _rowwise(name, add_all, [(a, c_, 0, first) for a, first in terms], [], [(c_, F32)], [], n_rows, tile=128)[0]


def _adamw_fn(w, g, m, v):
    m = ADAM_B1 * m + (1.0 - ADAM_B1) * g
    v = ADAM_B2 * v + (1.0 - ADAM_B2) * jnp.square(g)
    m_hat = m / (1.0 - ADAM_B1 ** ADAM_STEP)
    v_hat = v / (1.0 - ADAM_B2 ** ADAM_STEP)
    delta = -ADAM_LR * (m_hat / (jnp.sqrt(v_hat) + ADAM_EPS) + ADAM_WD * w)
    return delta, m, v


def _adamw(name, w, g, m, v):
    r, c_ = w.shape
    return _rowwise(name, _adamw_fn, [(w, c_, 0), (g, c_, 0), (m, c_, 0), (v, c_, 0)], [], [(c_, F32)] * 3, [], r, tile=128)


def _ada_fwd(c_all, w_ada, b_ada):
    n = w_ada.shape[1]

    def body(c_ref, w_ref, b_ref, o_ref):
        o_ref[...] = jnp.dot(jax.nn.silu(c_ref[...]).astype(BF16), w_ref[...].astype(BF16),
                             preferred_element_type=F32) + b_ref[...]

    return pl.pallas_call(body, name="ada_fwd", out_shape=jax.ShapeDtypeStruct((N_DEV, n), F32),
                          compiler_params=pltpu.CompilerParams(vmem_limit_bytes=VMEM_LIMIT))(c_all, w_ada, b_ada)


def _ada_bwd(c_all, dmod_cols):
    d = c_all.shape[1]
    n = dmod_cols.shape[1]

    def body(c_ref, dm_ref, gw_ref):
        gw_ref[...] = lax.dot_general(jax.nn.silu(c_ref[...]).astype(BF16), dm_ref[...].astype(BF16),
                                      (((0,), (0,)), ((), ())), preferred_element_type=F32)

    return pl.pallas_call(body, name="ada_bwd", out_shape=jax.ShapeDtypeStruct((d, n), F32),
                          compiler_params=pltpu.CompilerParams(vmem_limit_bytes=VMEM_LIMIT))(c_all, dmod_cols)


def _cast_bf16(name, a):
    r, c_ = a.shape
    return _rowwise(name, lambda v: (v,), [(a, c_, 0)], [], [(c_, BF16)], [], r)[0]


def _pad_rows(vec, mult):
    n = vec.shape[0]
    return jnp.pad(vec, (0, (-n) % mult))


def kernel(x, c, positions, w_ada, b_ada, w_in, b_in, conv_dw_w, conv_dw_b, conv_ln_g, conv_ln_b, w_conv_out, ret_gn_g, ret_gn_b, w_ret_out, w_out, ln1_g, ln1_b, w_up, ffn_dw_w, ffn_dw_b, w_down, ln2_g, ln2_b, loss_target, m_w_ada, m_b_ada, m_w_in, m_b_in, m_conv_dw_w, m_conv_dw_b, m_conv_ln_g, m_conv_ln_b, m_w_conv_out, m_ret_gn_g, m_ret_gn_b, m_w_ret_out, m_w_out, m_ln1_g, m_ln1_b, m_w_up, m_ffn_dw_w, m_ffn_dw_b, m_w_down, m_ln2_g, m_ln2_b, v_w_ada, v_b_ada, v_w_in, v_b_in, v_conv_dw_w, v_conv_dw_b, v_conv_ln_g, v_conv_ln_b, v_w_conv_out, v_ret_gn_g, v_ret_gn_b, v_w_ret_out, v_w_out, v_ln1_g, v_ln1_b, v_w_up, v_ffn_dw_w, v_ffn_dw_b, v_w_down, v_ln2_g, v_ln2_b):
    given = dict(locals())
    n_rows = x.shape[1]
    d = D_MODEL
    my_c = lax.axis_index("c")
    chip = 2 * lax.axis_index("x") + lax.axis_index("y")
    dev = 2 * chip + my_c
    xr = x[0]
    target = loss_target[0]
    vw = N_HEADS * DV
    ffw = 2 * D_FF

    def flat_rows(arrays, mult=8):
        v = jnp.concatenate([a.reshape(-1) for a in arrays])
        return _pad_rows(v, mult * d).reshape(-1, d)

    def unflatten(flat2d, shapes):
        v, out, o = flat2d.reshape(-1), [], 0
        for shp in shapes:
            size = 1
            for e in shp:
                size *= e
            out.append(v[o:o + size].reshape(shp))
            o += size
        return out

    w_bf = {n: _cast_bf16("cast_" + n, given[n][0]) for n in BIG_WEIGHTS}
    wg_in = _allgather8("gather_w_in", [w_bf["w_in"]], own_half=True)[0].reshape(N_CHIPS, d, w_in.shape[2])
    others = [n for n in BIG_WEIGHTS if n != "w_in"]
    zero = (lax.bitcast_convert_type(wg_in[0, 0:1, 0:1], jnp.uint16) * 0).astype(BF16)
    gather_handle, gather_token = _gather_start("gather_weights_start", [w_bf[n] + zero for n in others])

    kc, kf = conv_dw_w.shape[2], ffn_dw_w.shape[2]
    small_all = _allgather8("gather_small", [flat_rows([c, conv_dw_w, ffn_dw_w])])[0].reshape(N_DEV, -1)
    c_all = small_all[:, :d]
    per_chip = small_all[0::2]
    conv_w = per_chip[:, d:d + CONV_K * kc].reshape(N_CHIPS, CONV_K, kc).transpose(1, 0, 2).reshape(CONV_K, N_CHIPS * kc)
    o_f = d + CONV_K * kc
    ffn_w = per_chip[:, o_f:o_f + FFN_K * kf].reshape(N_CHIPS, FFN_K, kf).transpose(1, 0, 2).reshape(FFN_K, N_CHIPS * kf)

    n_ada = w_ada.shape[2]
    b_ada_cols = lax.dynamic_slice_in_dim(b_ada, chip * n_ada, n_ada, 1)
    mod_cols = _ada_fwd(c_all, w_ada[0], b_ada_cols)
    mod_all = _allgather8("gather_mod", [mod_cols])[0]
    mod = lax.dynamic_index_in_dim(mod_all[0::2], dev, 1, keepdims=False).reshape(1, N_CHIPS * n_ada)
    shift1, scale1, gate1, shift2, scale2, gate2 = [mod[:, k * d:(k + 1) * d] for k in range(6)]

    h1 = _rowwise("ln_mod1", lambda a, s, t: (_f_modulate(a, s, t),), [(xr, d, 0)],
                  [(scale1 + gather_token[0:1, 0:1], d, 0), (shift1, d, 0)], [(d, BF16)], [], n_rows)[0]
    n_proj = N_CHIPS * w_in.shape[2]
    proj_qk = _mm_nn("mm_in_qk", h1, wg_in, b_in, F32, tile=(2048, 512), window=(0, 2 * d), rows_outer=True)
    proj_rest = _mm_nn("mm_in_rest", h1, wg_in, b_in, BF16, tile=(2048, 512), window=(2 * d, n_proj - 2 * d), rows_outer=True)
    w_all = {}
    for n, zone in zip(others, _gather_wait("gather_weights_wait", gather_handle, proj_rest)):
        own = lax.dynamic_slice_in_dim(w_bf[n], my_c * zone.shape[1], zone.shape[1], 0)
        w_all[n] = lax.dynamic_update_slice_in_dim(zone, own[None], dev, 0)
    wg_up = w_all["w_up"].reshape(N_CHIPS, d, w_up.shape[2])
    wg_conv_out = w_all["w_conv_out"].reshape(1, d, d)
    wg_ret_out = w_all["w_ret_out"].reshape(1, vw, d)
    wg_out = w_all["w_out"].reshape(1, d, d)
    wg_down = w_all["w_down"].reshape(1, D_FF, d)
    a1 = _glu_conv_fwd(proj_rest, 4, 5, conv_w, conv_dw_b, n_rows)
    a2 = _rowwise("conv_norm", lambda a, g, b: (_f_conv_norm(a, g, b),), [(a1, d, 0)], [(conv_ln_g, d, 0), (conv_ln_b, d, 0)],
                  [(d, BF16)], [], n_rows)[0]
    y_a = _mm_nn("mm_conv_out", a2, wg_conv_out, None, F32)
    cos_t, sin_t = _rope_tables(positions[0])
    consts = _retention_consts()
    r, states = _retention_fwd(proj_qk, proj_rest, cos_t, sin_t, consts, n_rows)
    ret_rows = [(r, vw, 0), (proj_rest, vw, 1)]
    ret_vecs = [(ret_gn_g, vw, 0), (ret_gn_b, vw, 0)]
    r2 = _rowwise("ret_norm", _per_head(_f_group_norm_gate), ret_rows, ret_vecs, [(vw, BF16)], [], n_rows)[0]
    y_b = _mm_nn("mm_ret_out", r2, wg_ret_out, None, F32)
    merge_rows = [(proj_rest, d, 6), (proj_rest, d, 7), (y_a, d, 0), (y_b, d, 0)]
    m = _rowwise("merge", lambda *a: (_f_merge(*a),), merge_rows, [], [(d, BF16)], [], n_rows)[0]
    t = _mm_nn("mm_out", m, wg_out, None, F32)
    post1_vecs = [(gate1, d, 0), (ln1_g, d, 0), (ln1_b, d, 0), (scale2, d, 0), (shift2, d, 0)]
    x1, h2 = _rowwise("post1", _f_post1, [(xr, d, 0), (t, d, 0)], post1_vecs, [(d, F32), (d, BF16)], [], n_rows)
    pair_up = lambda c: (c % 2) * 2 + c // 2
    paired = lambda a: a.reshape(a.shape[0], N_CHIPS, kf)[:, jnp.array([0, 2, 1, 3])].reshape(a.shape[0], ffw)
    ffn_w_p, ffn_b_p = paired(ffn_w), paired(ffn_dw_b)
    u = _mm_nn("mm_up", h2, wg_up, None, BF16, gmap=pair_up)
    p, u_filtered = _ffn_fwd(u, ffn_w_p, ffn_b_p, n_rows)
    f = _mm_nn("mm_down", p, wg_down, None, F32)

    def loss_rows(x1v, fv, tv, g2v, lg, lb):
        loss, pull = jax.vjp(lambda a, b, c_, e, h: _f_loss(a, b, c_, e, h, tv), x1v, fv, g2v, lg, lb)
        return (*pull(jnp.ones((), F32)), jnp.full((1, 128), loss, F32))

    dx1_a, df, dgate2, dln2_g, dln2_b, loss_v = _rowwise(
        "loss", loss_rows, [(x1, d, 0), (f, d, 0), (target, d, 0)], [(gate2, d, 0), (ln2_g, d, 0), (ln2_b, d, 0)],
        [(d, F32), (d, BF16)], [d, d, d, 128], n_rows)
    loss = lax.psum(loss_v[0, 0], ("x", "y", "c"))

    dp = _mm_nt("mm_down_dx", df, wg_down, F32)
    gw_down = _mm_tn("mm_down_dw", p, df, 1)
    du, g_ffn_w, g_ffn_b = _ffn_bwd(u, u_filtered, dp, ffn_w_p, n_rows)
    g_ffn_w, g_ffn_b = paired(g_ffn_w), paired(g_ffn_b)
    dh2 = _mm_nt("mm_up_dx", du, wg_up, F32, gmap=pair_up)
    gw_up = _mm_tn("mm_up_dw", h2, du, N_CHIPS, gmap=pair_up)

    dx_a, dt, dgate1, dln1_g, dln1_b, dscale2, dshift2 = _rowwise(
        "post1_bwd", _vjp_rows(_f_post1, 2, 2), [(xr, d, 0), (t, d, 0), (dx1_a, d, 0), (dh2, d, 0)], post1_vecs,
        [(d, F32), (d, BF16)], [d] * 5, n_rows)
    dm = _mm_nt("mm_out_dx", dt, wg_out, F32)
    gw_out = _mm_tn("mm_out_dw", m, dt, 1)
    merge_vjp = _vjp_rows(_f_merge, 4, 1)

    def merge_bwd(*a):
        dga, dgb, dya, dyb = merge_vjp(*a)
        return jnp.concatenate([dga, dgb], axis=1), dya, dyb

    dproj, dya, dyb = _rowwise("merge_bwd", merge_bwd, merge_rows + [(dm, d, 0)], [],
                               [(2 * d, BF16, n_proj, 8192 // (2 * d)), (d, BF16), (d, BF16)], [], n_rows)

    da2 = _mm_nt("mm_conv_out_dx", dya, wg_conv_out, F32)
    gw_conv_out = _mm_tn("mm_conv_out_dw", a2, dya, 1)
    da1, dcl_g, dcl_b = _rowwise("conv_norm_bwd", _vjp_rows(_f_conv_norm, 1, 1), [(a1, d, 0), (da2, d, 0)],
                                 [(conv_ln_g, d, 0), (conv_ln_b, d, 0)], [(d, F32)], [d, d], n_rows)
    dproj, g_conv_w, g_conv_b = _glu_conv_bwd(proj_rest, 4, 5, da1, conv_w, dproj, 6144 // (2 * d), n_rows)

    dr2 = _mm_nt("mm_ret_out_dx", dyb, wg_ret_out, F32)
    gw_ret_out = _mm_tn("mm_ret_out_dw", r2, dyb, 1)

    def reduce_start(tag, names, fulls):
        shapes = [given[n].shape[1:] for n in names]
        fulls = [g_.reshape(N_CHIPS, r, c_) for g_, (r, c_) in zip(fulls, shapes)]
        theirs = _sibling_swap("grad_swap_" + tag, fulls)
        pair = []
        for n, (r, c_), g_, t_ in zip(names, shapes, fulls, theirs):
            mine = lax.dynamic_slice_in_dim(g_, my_c * (r // 2), r // 2, 1)
            pair.append(_sum_arrays("grad_pair_sum_" + n, [(mine.reshape(-1, c_), 0), (t_.reshape(-1, c_), 0)],
                                    N_CHIPS * (r // 2)).reshape(N_CHIPS, r // 2, c_))
        handle, token = _alltoall_start("grad_alltoall_start_" + tag, pair)
        own = [lax.dynamic_index_in_dim(p_, chip, 0, keepdims=False) for p_ in pair]
        return (names, shapes, handle, own), token[0:1, 0:1]

    def reduce_finish(tag, started, after):
        names, shapes, handle, own = started
        lands = _alltoall_wait("grad_alltoall_wait_" + tag, handle, after)
        return [_sum_arrays("grad_chip_sum_" + n, [(o_, 0)] + [(l_.reshape(-1, c_), k * (r // 2)) for k in range(3)], r // 2)
                for n, (r, c_), o_, l_ in zip(names, shapes, own, lands)]

    late = ("w_up", "w_down", "w_out", "w_conv_out", "w_ret_out")
    started_late, token_late = reduce_start("late", late, [gw_up, gw_down, gw_out, gw_conv_out, gw_ret_out])
    ret_vecs_bwd = [(ret_gn_g + token_late, vw, 0), (ret_gn_b, vw, 0)]
    dr, dproj, dgn_g, dgn_b = _rowwise(
        "ret_norm_bwd", _per_head(_vjp_rows(_f_group_norm_gate, 2, 1)), ret_rows + [(dr2, vw, 0)], ret_vecs_bwd,
        [(vw, F32), (vw, BF16, n_proj, 4096 // vw)], [vw, vw], n_rows, into=(dproj, 1))
    dproj = _retention_bwd(proj_qk, proj_rest, cos_t, sin_t, states, dr, consts, dproj, n_rows)

    gw_in, gb_in = _mm_tn("mm_in_dw", h1, dproj, N_CHIPS, colsum=True)
    started_in, token_in = reduce_start("in", ("w_in",), [gw_in])
    dh1 = _mm_nt("mm_in_dx", dproj, wg_in, F32)
    mod_bwd = _vjp_rows(_f_modulate, 1, 1)

    def mod1_bwd(xv, dhv, dxav, sv, tv):
        dx, ds, dsh = mod_bwd(xv, dhv, sv, tv)
        return dx + dxav, ds, dsh

    grad_x, dscale1, dshift1 = _rowwise("ln_mod1_bwd", mod1_bwd, [(xr, d, 0), (dh1, d, 0), (dx_a, d, 0)],
                                        [(scale1 + token_in, d, 0), (shift1, d, 0)], [(d, F32)], [d, d], n_rows)

    dmod = jnp.concatenate([dshift1, dscale1, dgate1, dshift2, dscale2, dgate2], axis=1)
    small_names = ["b_in", "conv_dw_w", "conv_dw_b", "conv_ln_g", "conv_ln_b", "ret_gn_g", "ret_gn_b", "ln1_g", "ln1_b",
                   "ffn_dw_w", "ffn_dw_b", "ln2_g", "ln2_b"]
    small_parts = [gb_in, g_conv_w, g_conv_b, dcl_g, dcl_b, dgn_g, dgn_b, dln1_g, dln1_b, g_ffn_w, g_ffn_b, dln2_g, dln2_b, dmod]
    small_shapes = [a.shape for a in small_parts]
    parts_all = _allgather8("gather_small_grads", [flat_rows(small_parts)])[0]
    part_rows = parts_all.shape[1]
    summed = _sum_arrays("sum_small_grads", [(parts_all.reshape(N_DEV * part_rows, d), k * part_rows) for k in range(N_DEV)],
                         part_rows)
    small_sum = unflatten(summed, small_shapes)
    grads = dict(zip(small_names, small_sum[:-1]))
    grads["b_ada"] = small_sum[-1]
    grads["conv_dw_w"] = lax.dynamic_slice_in_dim(grads["conv_dw_w"], chip * kc, kc, 1)
    grads["ffn_dw_w"] = lax.dynamic_slice_in_dim(grads["ffn_dw_w"], chip * kf, kf, 1)
    o_mod = sum(a.size for a in small_parts[:-1])
    dmod_all = parts_all.reshape(N_DEV, -1)[:, o_mod:o_mod + dmod.shape[1]]
    grads["w_ada"] = _ada_bwd(c_all, lax.dynamic_slice_in_dim(dmod_all, chip * n_ada, n_ada, 1))

    reduced = reduce_finish("late", started_late, grad_x) + reduce_finish("in", started_in, grad_x)
    big_order = late + ("w_in",)
    g_shards = []
    for n, mine_, theirs_ in zip(big_order, reduced, _sibling_exchange("grad_exchange", reduced)):
        both = lax.dynamic_update_slice_in_dim(jnp.stack([theirs_, theirs_]), mine_[None], my_c, 0)
        g_shards.append(both.reshape(given[n].shape[1:]))

    outs = {}
    for n, g_ in zip(big_order, g_shards):
        upd = _adamw("adamw_" + n, given[n][0], g_, given["m_" + n][0], given["v_" + n][0])
        for prefix, val in zip(("grad_", "delta_", "new_m_", "new_v_"), (g_, *upd)):
            outs[prefix + n] = val.reshape(given[n].shape)
    ada = _adamw("adamw_ada", w_ada[0], grads["w_ada"], m_w_ada[0], v_w_ada[0])
    for prefix, val in zip(("grad_", "delta_", "new_m_", "new_v_"), (grads["w_ada"], *ada)):
        outs[prefix + "w_ada"] = val.reshape(w_ada.shape)
    small_all_names = ["b_ada"] + small_names
    small_w_shapes = [given[n].shape for n in small_all_names]
    g_small = flat_rows([grads[n] for n in small_all_names])
    small_upd = _adamw("adamw_small", flat_rows([given[n] for n in small_all_names]), g_small,
                       flat_rows([given["m_" + n] for n in small_all_names]), flat_rows([given["v_" + n] for n in small_all_names]))
    for prefix, packed in zip(("grad_", "delta_", "new_m_", "new_v_"), (g_small, *small_upd)):
        for n, val in zip(small_all_names, unflatten(packed, small_w_shapes)):
            outs[prefix + n] = val

    weights = ["w_ada", "b_ada", "w_in", "b_in", "conv_dw_w", "conv_dw_b", "conv_ln_g", "conv_ln_b", "w_conv_out", "ret_gn_g",
               "ret_gn_b", "w_ret_out", "w_out", "ln1_g", "ln1_b", "w_up", "ffn_dw_w", "ffn_dw_b", "w_down", "ln2_g", "ln2_b"]
    result = [loss, grad_x.reshape(x.shape)]
    for prefix in ("grad_", "delta_", "new_m_", "new_v_"):
        result += [outs[prefix + n] for n in weights]
    return tuple(result)
```

```python
import jax
import jax.numpy as jnp
from jax import lax
from jax.experimental import pallas as pl
from jax.experimental.pallas import tpu as pltpu

F32 = jnp.float32
BF16 = jnp.bfloat16
MESH = pl.DeviceIdType.MESH

D_MODEL = 1024
N_HEADS = 8
DK = 128
DV = 256
CHUNK = 128
ROPE_BASE = 10000.0
D_FF = 2816
CONV_K = 31
FFN_K = 3
LN_EPS = 1e-5
ALPHA = (2.0 * 1) ** 0.25
ADAM_LR = 0.001
ADAM_B1 = 0.9
ADAM_B2 = 0.999
ADAM_EPS = 1e-08
ADAM_WD = 0.01
ADAM_STEP = 10

V7X_VMEM_BYTES = 64 * 1024 * 1024
VMEM_LIMIT = V7X_VMEM_BYTES - 8 * 1024 * 1024
ROW_TILE = 256
MM_TILE = 512
N_CHIPS = 4
N_DEV = 8

BIG_WEIGHTS = ("w_in", "w_up", "w_conv_out", "w_ret_out", "w_out", "w_down")


def _params(n_grid):
    return pltpu.CompilerParams(dimension_semantics=("arbitrary",) * n_grid, vmem_limit_bytes=VMEM_LIMIT)


def _rowwise(name, fn, rows, vecs, outs, reds, n_rows, tile=ROW_TILE, ncol=1, with_col=False, into=None):
    tile = _fit_tile(n_rows, tile)
    n_in = len(rows) + len(vecs)
    n_ref_in = n_in + (into is not None)
    n_out = len(outs)
    outs = [o if len(o) == 4 else (o[0], o[1], o[0] * ncol, 0) for o in outs]

    def col_map(off, row, first_row=0):
        def index(j, i):
            return (i + first_row // tile if row else 0, off(j) if callable(off) else off + j)
        return index

    def body(*refs):
        i = pl.program_id(1)
        vals = [r[...].astype(F32) for r in refs[:n_in]]
        res = fn(pl.program_id(0), *vals) if with_col else fn(*vals)
        for k in range(n_out):
            refs[n_ref_in + k][...] = res[k].astype(refs[n_ref_in + k].dtype)
        for k in range(len(reds)):
            o = refs[n_ref_in + n_out + k]

            @pl.when(i == 0)
            def _():
                o[...] = jnp.zeros_like(o)

            o[...] += res[n_out + k]

    in_specs = [pl.BlockSpec((tile, e[1]), col_map(e[2], True, e[3] if len(e) > 3 else 0)) for e in rows]
    in_specs += [pl.BlockSpec((1, w), col_map(off, False)) for _, w, off in vecs]
    args = [e[0] for e in rows] + [a for a, _, _ in vecs]
    aliases = {}
    if into is not None:
        in_specs.append(pl.BlockSpec(memory_space=pl.ANY))
        args.append(into[0])
        aliases = {n_in: into[1]}
    out_specs = [pl.BlockSpec((tile, w), col_map(off, True)) for w, _, _, off in outs]
    out_specs += [pl.BlockSpec((1, w), lambda j, i: (0, j)) for w in reds]
    out_shape = [jax.ShapeDtypeStruct((n_rows, total), dt) for _, dt, total, _ in outs]
    out_shape += [jax.ShapeDtypeStruct((1, w * ncol), F32) for w in reds]
    return pl.pallas_call(
        body, name=name, grid=(ncol, n_rows // tile), in_specs=in_specs, out_specs=out_specs, out_shape=out_shape,
        input_output_aliases=aliases, compiler_params=_params(2),
    )(*args)


def _vjp_rows(fn, n_row_in, n_ct):
    def bwd(*args):
        prim = [a.astype(F32) for a in args[:n_row_in] + args[n_row_in + n_ct:]]
        cts = tuple(a.astype(F32) for a in args[n_row_in:n_row_in + n_ct])
        _, pull = jax.vjp(fn, *prim)
        return pull(cts if n_ct > 1 else cts[0])

    return bwd


def _fit_tile(n, pref):
    if n <= pref:
        return n
    t = pref - pref % 16
    while n % t:
        t -= 16
    return t


def _col_tile(n):
    return n if n <= 1536 else n // 2


def _same_group(c):
    return c


def _mm_nn(name, a, w, bias, out_dtype, gmap=_same_group, tile=None, window=None, rows_outer=False):
    s, k = a.shape
    g, _, n = w.shape
    tm, tn = (min(MM_TILE, s), _col_tile(n)) if tile is None else (min(tile[0], s), tile[1])
    nt = n // tn
    first, n_cols = (0, g * n) if window is None else window
    t0 = first // tn

    def body(*refs):
        a_ref, w_ref = refs[0], refs[1]
        o_ref = refs[-1]
        acc = jnp.dot(a_ref[...].astype(BF16), w_ref[...], preferred_element_type=F32)
        if bias is not None:
            acc = acc + refs[2][...]
        o_ref[...] = acc.astype(o_ref.dtype)

    order = (lambda i, c: (c, i)) if rows_outer else (lambda c, i: (c, i))
    a_map = lambda *g_: (order(*g_)[1], 0)
    w_map = lambda *g_: (gmap((t0 + order(*g_)[0]) // nt), 0, (t0 + order(*g_)[0]) % nt)
    in_specs = [pl.BlockSpec((tm, k), a_map), pl.BlockSpec((None, k, tn), w_map)]
    args = [a, w]
    if bias is not None:
        in_specs.append(pl.BlockSpec((1, tn), lambda *g_: (0, t0 + order(*g_)[0])))
        args.append(bias)
    grid = (s // tm, n_cols // tn) if rows_outer else (n_cols // tn, s // tm)
    return pl.pallas_call(
        body, name=name, grid=grid, in_specs=in_specs, out_specs=pl.BlockSpec((tm, tn), lambda *g_: order(*g_)[::-1]),
        out_shape=jax.ShapeDtypeStruct((s, n_cols), out_dtype), compiler_params=_params(2),
    )(*args)


def _mm_nt(name, dy, w, out_dtype, gmap=_same_group):
    s = dy.shape[0]
    g, k, n = w.shape
    tm, tn = min(2 * MM_TILE if k <= 1024 else MM_TILE, s), n
    nt = 1
    steps = g

    def body(dy_ref, w_ref, o_ref, acc_ref):
        r = pl.program_id(1)
        part = lax.dot_general(dy_ref[...].astype(BF16), w_ref[...], (((1,), (1,)), ((), ())), preferred_element_type=F32)

        @pl.when(r == 0)
        def _():
            acc_ref[...] = part

        @pl.when(r > 0)
        def _():
            acc_ref[...] += part

        @pl.when(r == steps - 1)
        def _():
            o_ref[...] = acc_ref[...].astype(o_ref.dtype)

    return pl.pallas_call(
        body, name=name, grid=(s // tm, steps),
        in_specs=[pl.BlockSpec((tm, tn), lambda i, r: (i, r)),
                  pl.BlockSpec((None, k, tn), lambda i, r: (gmap(r // nt), 0, r % nt))],
        out_specs=pl.BlockSpec((tm, k), lambda i, r: (i, 0)), out_shape=jax.ShapeDtypeStruct((s, k), out_dtype),
        scratch_shapes=[pltpu.VMEM((tm, k), F32)], compiler_params=_params(2),
    )(dy, w)


def _mm_tn(name, a, dy, g, gmap=_same_group, colsum=False):
    s, k = a.shape
    n = dy.shape[1] // g
    ts = min(2048 if k <= 1024 else 1024, s)
    tn = n if k * n <= 1024 * 1536 else (n // 2 if (n // 2) % 128 == 0 else n)
    nt = n // tn

    def body(a_ref, dy_ref, o_ref, *sum_ref):
        t = pl.program_id(1)
        dyv = dy_ref[...].astype(BF16)
        part = lax.dot_general(a_ref[...].astype(BF16), dyv, (((0,), (0,)), ((), ())), preferred_element_type=F32)

        @pl.when(t == 0)
        def _():
            o_ref[...] = part

        @pl.when(t > 0)
        def _():
            o_ref[...] += part

        if colsum:
            col = jnp.sum(dyv.astype(F32), axis=0, keepdims=True)

            @pl.when(t == 0)
            def _():
                sum_ref[0][...] = col

            @pl.when(t > 0)
            def _():
                sum_ref[0][...] += col

    out_specs = [pl.BlockSpec((None, k, tn), lambda c, t: (gmap(c // nt), 0, c % nt))]
    out_shape = [jax.ShapeDtypeStruct((g, k, n), F32)]
    if colsum:
        out_specs.append(pl.BlockSpec((1, tn), lambda c, t: (0, c)))
        out_shape.append(jax.ShapeDtypeStruct((1, g * n), F32))
    res = pl.pallas_call(
        body, name=name, grid=(g * nt, s // ts),
        in_specs=[pl.BlockSpec((ts, k), lambda c, t: (t, 0)), pl.BlockSpec((ts, tn), lambda c, t: (t, c))],
        out_specs=out_specs, out_shape=out_shape, compiler_params=_params(2),
    )(a, dy)
    return res if colsum else res[0]


SUBLANES = 8


def _tap_sum(read, w_row, offsets, tile):
    acc = None
    for b in range(SUBLANES):
        group = [(k, o) for k, o in offsets if o % SUBLANES == b]
        if not group:
            continue
        rows = tile if b == 0 else tile + SUBLANES
        z = None
        for k, o in group:
            term = w_row(k) * read(o - b, rows)
            z = term if z is None else z + term
        part = z if b == 0 else z[b:b + tile]
        acc = part if acc is None else acc + part
    return acc


def _tap_grads(read, dy, offsets, tile):
    padded = jnp.concatenate([dy, jnp.zeros((SUBLANES, dy.shape[1]), dy.dtype)], axis=0)
    out = {}
    for b in range(SUBLANES):
        group = [(k, o) for k, o in offsets if o % SUBLANES == b]
        if not group:
            continue
        shifted = dy if b == 0 else pltpu.roll(padded, b, 0)
        rows = tile if b == 0 else tile + SUBLANES
        for k, o in group:
            out[k] = jnp.sum(shifted * read(o - b, rows), axis=0, keepdims=True)
    return out


CONV_HALO = 32
CONV_TILE = 256
CONV_LANES = 256


def _glu_conv_fwd(proj, col_v, col_g, w, b, n_rows):
    kw, n_ch = w.shape
    tile = min(CONV_TILE, n_rows)
    per = tile // CONV_HALO
    offsets = [(k, CONV_HALO - (kw - 1) + k) for k in range(kw)]

    def body(v_ref, g_ref, vh_ref, gh_ref, w_ref, b_ref, o_ref, buf):
        i = pl.program_id(0)
        prev = vh_ref[...].astype(F32) * jax.nn.sigmoid(gh_ref[...].astype(F32))
        buf[0:CONV_HALO, :] = jnp.where(i == 0, 0.0, prev)
        buf[CONV_HALO:CONV_HALO + tile, :] = v_ref[...].astype(F32) * jax.nn.sigmoid(g_ref[...].astype(F32))
        for c0 in range(0, n_ch, CONV_LANES):
            cols = slice(c0, c0 + CONV_LANES)
            o_ref[:, cols] = b_ref[:, cols] + _tap_sum(lambda s, n: buf[pl.ds(s, n), cols], lambda k: w_ref[k:k + 1, cols],
                                                      offsets, tile)

    main = lambda col: pl.BlockSpec((tile, n_ch), lambda i: (i, col))
    halo = lambda col: pl.BlockSpec((CONV_HALO, n_ch), lambda i: (jnp.maximum(i * per - 1, 0), col))
    return pl.pallas_call(
        body, name="conv_fwd", grid=(n_rows // tile,),
        in_specs=[main(col_v), main(col_g), halo(col_v), halo(col_g), pl.BlockSpec((kw, n_ch), lambda i: (0, 0)),
                  pl.BlockSpec((1, n_ch), lambda i: (0, 0))],
        out_specs=pl.BlockSpec((tile, n_ch), lambda i: (i, 0)), out_shape=jax.ShapeDtypeStruct((n_rows, n_ch), F32),
        scratch_shapes=[pltpu.VMEM((CONV_HALO + tile, n_ch), F32)], compiler_params=_params(1),
    )(proj, proj, proj, proj, w, b)


def _glu_conv_bwd(proj, col_v, col_g, dy, w, dproj, col_out, n_rows):
    kw, n_ch = w.shape
    tile = min(CONV_TILE, n_rows)
    per = tile // CONV_HALO
    n_tiles = n_rows // tile
    last_halo = n_rows // CONV_HALO - 1
    offsets = [(k, CONV_HALO - (kw - 1) + k) for k in range(kw)]
    back = [(k, kw - 1 - k) for k in range(kw)]

    def body(v_ref, g_ref, vh_ref, gh_ref, dy_ref, dyn_ref, w_ref, dp_any, dx_ref, dw_ref, db_ref, buf, dbuf):
        i = pl.program_id(0)
        cv, cg = v_ref[...].astype(F32), g_ref[...].astype(F32)
        sig = jax.nn.sigmoid(cg)
        prev = vh_ref[...].astype(F32) * jax.nn.sigmoid(gh_ref[...].astype(F32))
        buf[0:CONV_HALO, :] = jnp.where(i == 0, 0.0, prev)
        buf[CONV_HALO:CONV_HALO + tile, :] = cv * sig
        dbuf[0:tile, :] = dy_ref[...]
        dbuf[tile:tile + CONV_HALO, :] = jnp.where(i == n_tiles - 1, 0.0, dyn_ref[...])

        @pl.when(i == 0)
        def _():
            dw_ref[...] = jnp.zeros_like(dw_ref)
            db_ref[...] = jnp.zeros_like(db_ref)

        db_ref[...] += jnp.sum(dy_ref[...], axis=0, keepdims=True)
        for c0 in range(0, n_ch, CONV_LANES):
            cols = slice(c0, c0 + CONV_LANES)
            w_row = lambda k: w_ref[k:k + 1, cols]
            dx = _tap_sum(lambda s, n: dbuf[pl.ds(s, n), cols], w_row, back, tile)
            grads = _tap_grads(lambda s, n: buf[pl.ds(s, n), cols], dy_ref[:, cols], offsets, tile)
            for k in range(kw):
                dw_ref[k:k + 1, cols] += grads[k]
            sg = sig[:, cols]
            dx_ref[:, c0:c0 + CONV_LANES] = (dx * sg).astype(dx_ref.dtype)
            dx_ref[:, n_ch + c0:n_ch + c0 + CONV_LANES] = (dx * cv[:, cols] * sg * (1.0 - sg)).astype(dx_ref.dtype)

    main = lambda col: pl.BlockSpec((tile, n_ch), lambda i: (i, col))
    halo = lambda col: pl.BlockSpec((CONV_HALO, n_ch), lambda i: (jnp.maximum(i * per - 1, 0), col))
    return pl.pallas_call(
        body, name="conv_bwd", grid=(n_tiles,),
        in_specs=[main(col_v), main(col_g), halo(col_v), halo(col_g), main(0),
                  pl.BlockSpec((CONV_HALO, n_ch), lambda i: (jnp.minimum((i + 1) * per, last_halo), 0)),
                  pl.BlockSpec((kw, n_ch), lambda i: (0, 0)), pl.BlockSpec(memory_space=pl.ANY)],
        out_specs=[pl.BlockSpec((tile, 2 * n_ch), lambda i: (i, col_out)), pl.BlockSpec((kw, n_ch), lambda i: (0, 0)),
                   pl.BlockSpec((1, n_ch), lambda i: (0, 0))],
        out_shape=[jax.ShapeDtypeStruct(dproj.shape, dproj.dtype), jax.ShapeDtypeStruct((kw, n_ch), F32),
                   jax.ShapeDtypeStruct((1, n_ch), F32)],
        input_output_aliases={7: 0},
        scratch_shapes=[pltpu.VMEM((CONV_HALO + tile, n_ch), F32), pltpu.VMEM((tile + CONV_HALO, n_ch), F32)],
        compiler_params=_params(1),
    )(proj, proj, proj, proj, dy, dy, w, dproj)


FFN_HALO = 8


def _ffn_fwd(u, w, b, n_rows):
    kw = w.shape[0]
    half = u.shape[1] // 4
    tile = min(ROW_TILE, n_rows)
    per = tile // FFN_HALO
    offsets = [(k, FFN_HALO - (kw - 1) + k) for k in range(kw)]

    def body(u_ref, uh_ref, w_ref, b_ref, p_ref, f_ref, buf):
        i = pl.program_id(1)
        buf[0:FFN_HALO, :] = jnp.where(i == 0, 0.0, uh_ref[...].astype(F32))
        buf[FFN_HALO:FFN_HALO + tile, :] = u_ref[...].astype(F32)
        conv = []
        for c0 in (0, half):
            cols = slice(c0, c0 + half)
            conv.append(b_ref[:, cols] + _tap_sum(lambda s, n: buf[pl.ds(s, n), cols], lambda k: w_ref[k:k + 1, cols],
                                                  offsets, tile))
            f_ref[:, cols] = conv[-1].astype(f_ref.dtype)
        p_ref[...] = (conv[0] * jax.nn.silu(conv[1])).astype(p_ref.dtype)

    return pl.pallas_call(
        body, name="ffn_fwd", grid=(2, n_rows // tile),
        in_specs=[pl.BlockSpec((tile, 2 * half), lambda j, i: (i, j)),
                  pl.BlockSpec((FFN_HALO, 2 * half), lambda j, i: (jnp.maximum(i * per - 1, 0), j)),
                  pl.BlockSpec((kw, 2 * half), lambda j, i: (0, j)), pl.BlockSpec((1, 2 * half), lambda j, i: (0, j))],
        out_specs=[pl.BlockSpec((tile, half), lambda j, i: (i, j)), pl.BlockSpec((tile, 2 * half), lambda j, i: (i, j))],
        out_shape=[jax.ShapeDtypeStruct((n_rows, 2 * half), BF16), jax.ShapeDtypeStruct(u.shape, BF16)],
        scratch_shapes=[pltpu.VMEM((FFN_HALO + tile, 2 * half), F32)], compiler_params=_params(2),
    )(u, u, w, b)


def _ffn_bwd(u, filtered, dp, w, n_rows):
    kw = w.shape[0]
    half = u.shape[1] // 4
    tile = min(ROW_TILE, n_rows)
    per = tile // FFN_HALO
    n_tiles = n_rows // tile
    last_halo = n_rows // FFN_HALO - 1
    ext = tile + FFN_HALO
    offsets = [(k, FFN_HALO - (kw - 1) + k) for k in range(kw)]
    back = [(k, kw - 1 - k) for k in range(kw)]

    def body(u_ref, up_ref, f_ref, fn_ref, dp_ref, dpn_ref, w_ref, du_ref, dw_ref, db_ref, buf, dbuf):
        i = pl.program_id(1)
        buf[0:FFN_HALO, :] = jnp.where(i == 0, 0.0, up_ref[...].astype(F32))
        buf[FFN_HALO:FFN_HALO + tile, :] = u_ref[...].astype(F32)
        filt = jnp.concatenate([f_ref[...], fn_ref[...]], axis=0).astype(F32)
        val, gate = filt[:, 0:half], filt[:, half:2 * half]
        dpe = jnp.concatenate([dp_ref[...], jnp.where(i == n_tiles - 1, 0.0, dpn_ref[...])], axis=0)
        sig = jax.nn.sigmoid(gate)
        dbuf[:, 0:half] = dpe * gate * sig
        dbuf[:, half:2 * half] = dpe * val * sig * (1.0 + gate * (1.0 - sig))

        @pl.when(i == 0)
        def _():
            dw_ref[...] = jnp.zeros_like(dw_ref)
            db_ref[...] = jnp.zeros_like(db_ref)

        for c0 in (0, half):
            cols = slice(c0, c0 + half)
            du_ref[:, cols] = _tap_sum(lambda s, n: dbuf[pl.ds(s, n), cols], lambda k: w_ref[k:k + 1, cols], back,
                                       tile).astype(du_ref.dtype)
            d_main = dbuf[0:tile, cols]
            db_ref[:, cols] += jnp.sum(d_main, axis=0, keepdims=True)
            grads = _tap_grads(lambda s, n: buf[pl.ds(s, n), cols], d_main, offsets, tile)
            for k in range(kw):
                dw_ref[k:k + 1, cols] += grads[k]

    wide = 2 * half
    return pl.pallas_call(
        body, name="ffn_bwd", grid=(2, n_tiles),
        in_specs=[pl.BlockSpec((tile, wide), lambda j, i: (i, j)),
                  pl.BlockSpec((FFN_HALO, wide), lambda j, i: (jnp.maximum(i * per - 1, 0), j)),
                  pl.BlockSpec((tile, wide), lambda j, i: (i, j)),
                  pl.BlockSpec((FFN_HALO, wide), lambda j, i: (jnp.minimum((i + 1) * per, last_halo), j)),
                  pl.BlockSpec((tile, half), lambda j, i: (i, j)),
                  pl.BlockSpec((FFN_HALO, half), lambda j, i: (jnp.minimum((i + 1) * per, last_halo), j)),
                  pl.BlockSpec((kw, wide), lambda j, i: (0, j))],
        out_specs=[pl.BlockSpec((tile, wide), lambda j, i: (i, j)), pl.BlockSpec((kw, wide), lambda j, i: (0, j)),
                   pl.BlockSpec((1, wide), lambda j, i: (0, j))],
        out_shape=[jax.ShapeDtypeStruct(u.shape, BF16), jax.ShapeDtypeStruct((kw, u.shape[1]), F32),
                   jax.ShapeDtypeStruct((1, u.shape[1]), F32)],
        scratch_shapes=[pltpu.VMEM((FFN_HALO + tile, wide), F32), pltpu.VMEM((ext, wide), F32)], compiler_params=_params(2),
    )(u, u, filtered, filtered, dp, dp, w)


def _retention_consts():
    log_gamma = jnp.log(1.0 - 2.0 ** (-5.0 - jnp.arange(N_HEADS, dtype=F32)))
    idx = jnp.arange(CHUNK, dtype=F32)
    rel = idx[:, None] - idx[None, :]
    decay = jnp.where(rel[None] >= 0, jnp.exp(log_gamma[:, None, None] * jnp.maximum(rel, 0.0)[None]), 0.0)
    zeta = jnp.exp(log_gamma[:, None] * (CHUNK - 1.0 - idx)[None])
    xi = jnp.exp(log_gamma[:, None] * (idx + 1.0)[None])
    chunk_decay = jnp.exp(log_gamma * CHUNK)
    xi_b = jnp.broadcast_to(xi[:, :, None], (N_HEADS, CHUNK, DK))
    zeta_b = jnp.broadcast_to(zeta[:, :, None], (N_HEADS, CHUNK, DK))
    cd_b = jnp.broadcast_to(chunk_decay[:, None, None], (N_HEADS, 8, DV))
    return decay, xi_b, zeta_b, cd_b


def _rope_tables(positions):
    half = DK // 2
    inv_freq = ROPE_BASE ** (-jnp.arange(half, dtype=F32) / half)
    ang = positions.astype(F32)[:, None] * inv_freq
    cos, sin = jnp.cos(ang), jnp.sin(ang)
    return jnp.concatenate([cos, cos], axis=-1), jnp.concatenate([-sin, sin], axis=-1)


def _swap_halves(v):
    return pltpu.roll(v, DK // 2, 1)


def _dot(a, b):
    return jnp.dot(a, b, preferred_element_type=F32)


def _dot_nt(a, b):
    return lax.dot_general(a, b, (((1,), (1,)), ((), ())), preferred_element_type=F32)


def _dot_tn(a, b):
    return lax.dot_general(a, b, (((0,), (0,)), ((), ())), preferred_element_type=F32)


def _const_specs():
    return [pl.BlockSpec((N_HEADS, CHUNK, CHUNK), lambda n: (0, 0, 0)), pl.BlockSpec((N_HEADS, CHUNK, DK), lambda n: (0, 0, 0)),
            pl.BlockSpec((N_HEADS, CHUNK, DK), lambda n: (0, 0, 0)), pl.BlockSpec((N_HEADS, 8, DV), lambda n: (0, 0, 0))]


def _retention_fwd(proj_qk, proj_rest, cos_t, sin_t, consts, n_rows):
    n_chunks = n_rows // CHUNK
    scale = DK ** -0.5

    def body(q_ref, k_ref, v_ref, cf_ref, ss_ref, d_ref, xi_ref, zt_ref, cd_ref, r_ref, st_ref, state):
        @pl.when(pl.program_id(0) == 0)
        def _():
            state[...] = jnp.zeros_like(state)

        cf, ss = cf_ref[...], ss_ref[...]
        for h in range(N_HEADS):
            qh = q_ref[:, h * DK:(h + 1) * DK].astype(F32)
            kh = k_ref[:, h * DK:(h + 1) * DK].astype(F32)
            qh = qh * cf + _swap_halves(qh) * ss
            kh = (kh * cf + _swap_halves(kh) * ss) * scale
            vh = v_ref[:, h * DV:(h + 1) * DV].astype(BF16)
            st = state[h]
            st_ref[0, h] = st
            sd = _dot_nt(qh.astype(BF16), kh.astype(BF16)) * d_ref[h]
            inner = _dot(sd.astype(BF16), vh)
            cross = _dot((qh * xi_ref[h]).astype(BF16), st.astype(BF16))
            kv = _dot_tn((kh * zt_ref[h]).astype(BF16), vh)
            state[h] = st * cd_ref[h, 0:1, :] + kv
            r_ref[:, h * DV:(h + 1) * DV] = inner + cross

    qk = N_HEADS * DK
    vw = N_HEADS * DV
    return pl.pallas_call(
        body, name="retention_fwd", grid=(n_chunks,),
        in_specs=[pl.BlockSpec((CHUNK, qk), lambda n: (n, 0)), pl.BlockSpec((CHUNK, qk), lambda n: (n, 1)),
                  pl.BlockSpec((CHUNK, vw), lambda n: (n, 0)), pl.BlockSpec((CHUNK, DK), lambda n: (n, 0)),
                  pl.BlockSpec((CHUNK, DK), lambda n: (n, 0))] + _const_specs(),
        out_specs=[pl.BlockSpec((CHUNK, vw), lambda n: (n, 0)), pl.BlockSpec((1, N_HEADS, DK, DV), lambda n: (n, 0, 0, 0))],
        out_shape=[jax.ShapeDtypeStruct((n_rows, vw), F32), jax.ShapeDtypeStruct((n_chunks, N_HEADS, DK, DV), F32)],
        scratch_shapes=[pltpu.VMEM((N_HEADS, DK, DV), F32)], compiler_params=_params(1),
    )(proj_qk, proj_qk, proj_rest, cos_t, sin_t, *consts)


def _retention_bwd(proj_qk, proj_rest, cos_t, sin_t, states, dr, consts, dproj, n_rows):
    n_chunks = n_rows // CHUNK
    scale = DK ** -0.5
    qk = N_HEADS * DK
    vw = N_HEADS * DV

    def body(q_ref, k_ref, v_ref, cf_ref, ss_ref, st_ref, dr_ref, d_ref, xi_ref, zt_ref, cd_ref, dp_any, dqkv_ref, g_ref):
        dq_ref, dk_ref, dv_ref = dqkv_ref.at[:, 0:qk], dqkv_ref.at[:, qk:2 * qk], dqkv_ref.at[:, 2 * qk:2 * qk + vw]

        @pl.when(pl.program_id(0) == 0)
        def _():
            g_ref[...] = jnp.zeros_like(g_ref)

        cf, ss = cf_ref[...], ss_ref[...]
        for h in range(N_HEADS):
            qh = q_ref[:, h * DK:(h + 1) * DK].astype(F32)
            kh = k_ref[:, h * DK:(h + 1) * DK].astype(F32)
            qh = qh * cf + _swap_halves(qh) * ss
            kh = (kh * cf + _swap_halves(kh) * ss) * scale
            qb, kb = qh.astype(BF16), kh.astype(BF16)
            vh = v_ref[:, h * DV:(h + 1) * DV].astype(BF16)
            do = dr_ref[:, h * DV:(h + 1) * DV].astype(BF16)
            rb = st_ref[0, h].astype(BF16)
            g = g_ref[h]
            gb = g.astype(BF16)
            dec, xi, zt = d_ref[h], xi_ref[h], zt_ref[h]
            sd = (_dot_nt(qb, kb) * dec).astype(BF16)
            ds = (_dot_nt(do, vh) * dec).astype(BF16)
            dqh = _dot(ds, kb) + _dot_nt(do, rb) * xi
            dkh = (_dot_tn(ds, qb) + _dot_nt(vh, gb) * zt) * scale
            dvh = _dot_tn(sd, do) + _dot((kh * zt).astype(BF16), gb)
            g_ref[h] = g * cd_ref[h, 0:1, :] + _dot_tn((qh * xi).astype(BF16), do)
            dq_ref[:, h * DK:(h + 1) * DK] = (dqh * cf + _swap_halves(dqh * ss)).astype(dq_ref.dtype)
            dk_ref[:, h * DK:(h + 1) * DK] = (dkh * cf + _swap_halves(dkh * ss)).astype(dk_ref.dtype)
            dv_ref[:, h * DV:(h + 1) * DV] = dvh.astype(dv_ref.dtype)

    last = n_chunks - 1
    return pl.pallas_call(
        body, name="retention_bwd", grid=(n_chunks,),
        in_specs=[pl.BlockSpec((CHUNK, qk), lambda n: (last - n, 0)), pl.BlockSpec((CHUNK, qk), lambda n: (last - n, 1)),
                  pl.BlockSpec((CHUNK, vw), lambda n: (last - n, 0)), pl.BlockSpec((CHUNK, DK), lambda n: (last - n, 0)),
                  pl.BlockSpec((CHUNK, DK), lambda n: (last - n, 0)),
                  pl.BlockSpec((1, N_HEADS, DK, DV), lambda n: (last - n, 0, 0, 0)),
                  pl.BlockSpec((CHUNK, vw), lambda n: (last - n, 0))] + _const_specs() + [pl.BlockSpec(memory_space=pl.ANY)],
        out_specs=pl.BlockSpec((CHUNK, 2 * qk + vw), lambda n: (last - n, 0)),
        out_shape=jax.ShapeDtypeStruct(dproj.shape, dproj.dtype), input_output_aliases={11: 0},
        scratch_shapes=[pltpu.VMEM((N_HEADS, DK, DV), F32)], compiler_params=_params(1),
    )(proj_qk, proj_qk, proj_rest, cos_t, sin_t, states, dr, *consts, dproj)


def _ln(v):
    mu = jnp.mean(v, axis=-1, keepdims=True)
    var = jnp.mean(jnp.square(v - mu), axis=-1, keepdims=True)
    return (v - mu) * lax.rsqrt(var + LN_EPS)


def _f_modulate(x, scale, shift):
    return _ln(x) * (1.0 + scale) + shift


def _f_conv_norm(a1, g, b):
    return jax.nn.silu(_ln(a1) * g + b)


def _f_group_norm_gate(r, gate, g, b):
    return (_ln(r) * g + b) * jax.nn.silu(gate)


def _per_head(fn):
    def run(*arrays):
        parts = [fn(*[a[:, h * DV:(h + 1) * DV] for a in arrays]) for h in range(N_HEADS)]
        if not isinstance(parts[0], (tuple, list)):
            return (jnp.concatenate(parts, axis=1),)
        return tuple(jnp.concatenate([p[k] for p in parts], axis=1) for k in range(len(parts[0])))

    return run


def _f_merge(ga, gb, ya, yb):
    return jax.nn.sigmoid(ga) * ya + jax.nn.sigmoid(gb) * yb


def _f_post1(x, t, gate1, g1, b1, scale2, shift2):
    x1 = _ln(ALPHA * x + gate1 * t) * g1 + b1
    return x1, _ln(x1) * (1.0 + scale2) + shift2


def _f_loss(x1, f, gate2, g2, b2, target):
    y = _ln(ALPHA * x1 + gate2 * f) * g2 + b2
    return 0.5 * jnp.sum(jnp.mean(jnp.square(y - target), axis=-1))


ANY = pl.BlockSpec(memory_space=pl.ANY)


def _allgather8(name, blocks, own_half=False):
    n = len(blocks)
    rows = [b.shape[0] // 2 if own_half else b.shape[0] for b in blocks]

    def body(*refs):
        x_refs, out_refs = refs[:n], refs[n:2 * n]
        send_sems, recv_sems = refs[2 * n:]
        x, y, c = lax.axis_index("x"), lax.axis_index("y"), lax.axis_index("c")
        me, sibling = (x, y, c), (x, y, 1 - c)
        chips = [(1 - x, y), (x, 1 - y), (1 - x, 1 - y)]
        every = range(n)

        def src(a):
            return x_refs[a].at[pl.ds(c * rows[a], rows[a])] if own_half else x_refs[a]

        def slot(a, px, py, pc):
            return out_refs[a].at[4 * px + 2 * py + pc]

        def copy(k, a, block, to, from_input=False):
            return pltpu.make_async_remote_copy(
                src_ref=src(a) if from_input else slot(a, *block), dst_ref=slot(a, *block), send_sem=send_sems.at[k, a],
                recv_sem=recv_sems.at[k, a], device_id=to, device_id_type=MESH)

        first = [copy(0, a, me, sibling, True) for a in every]
        first += [copy(1 + j, a, me, (*chip, c), True) for j, chip in enumerate(chips) for a in every]
        for cp in first:
            cp.start()
        passed = [[copy(4 + j, a, (*chip, c), sibling) for a in every] for j, chip in enumerate(chips)]
        for j, chip in enumerate(chips):
            for a in every:
                copy(1 + j, a, (*chip, c), me).wait_recv()
            for cp in passed[j]:
                cp.start()
        for a in every:
            copy(0, a, sibling, me).wait_recv()
        for j, chip in enumerate(chips):
            for a in every:
                copy(4 + j, a, (*chip, 1 - c), me).wait_recv()
        for cp in first + [cp for group in passed for cp in group]:
            cp.wait_send()

    gathered = pl.pallas_call(
        body, name=name, in_specs=[ANY] * n, out_specs=[ANY] * n,
        out_shape=[jax.ShapeDtypeStruct((N_DEV, r, b.shape[1]), b.dtype) for r, b in zip(rows, blocks)],
        scratch_shapes=[pltpu.SemaphoreType.DMA((7, n)), pltpu.SemaphoreType.DMA((7, n))],
    )(*blocks)
    c = lax.axis_index("c")
    me = 4 * lax.axis_index("x") + 2 * lax.axis_index("y") + c
    own = [lax.dynamic_slice_in_dim(b, c * r, r, 0) if own_half else b for r, b in zip(rows, blocks)]
    return [lax.dynamic_update_slice_in_dim(g_, o_[None], me, 0) for g_, o_ in zip(gathered, own)]


def _sibling_swap(name, arrays):
    n = len(arrays)
    halves = [a.shape[1] // 2 for a in arrays]

    def body(*refs):
        g_refs, theirs = refs[:n], refs[n:2 * n]
        send_sems, recv_sems = refs[2 * n:]
        x, y, c = lax.axis_index("x"), lax.axis_index("y"), lax.axis_index("c")
        remote = [pltpu.make_async_remote_copy(
            src_ref=g_refs[a].at[:, pl.ds((1 - c) * halves[a], halves[a]), :], dst_ref=theirs[a], send_sem=send_sems.at[a],
            recv_sem=recv_sems.at[a], device_id=(x, y, 1 - c), device_id_type=MESH) for a in range(n)]
        for cp in remote:
            cp.start()
        for cp in remote:
            cp.wait_recv()
        for cp in remote:
            cp.wait_send()

    return pl.pallas_call(
        body, name=name, in_specs=[ANY] * n, out_specs=[ANY] * n,
        out_shape=[jax.ShapeDtypeStruct((a.shape[0], h, a.shape[2]), a.dtype) for a, h in zip(arrays, halves)],
        scratch_shapes=[pltpu.SemaphoreType.DMA((n,)), pltpu.SemaphoreType.DMA((n,))],
    )(*arrays)


HBM = pl.BlockSpec(memory_space=pltpu.HBM)
SEM = pl.BlockSpec(memory_space=pltpu.SEMAPHORE)
DATAFLOW = pltpu.SideEffectType.DATAFLOW_SIDE_EFFECTING


def _chip_peers(x, y):
    return [(1 - x, y), (x, 1 - y), (1 - x, 1 - y)]


def _other_devices(x, y, c):
    flip = lambda v, f: 1 - v if f else v
    return [(flip(x, k & 4), flip(y, k & 2), flip(c, k & 1)) for k in range(1, N_DEV)]


def _gather_start(name, arrays):
    n = len(arrays)
    rows = [a.shape[0] // 2 for a in arrays]
    zones = [lax.empty((N_DEV, r, a.shape[1]), a.dtype) for r, a in zip(rows, arrays)]

    def body(*refs):
        x_refs, zone_refs = refs[:n], refs[n:2 * n]
        send_sems, recv_sems = refs[2 * n], refs[2 * n + 1]
        token = refs[-1]
        x, y, c = lax.axis_index("x"), lax.axis_index("y"), lax.axis_index("c")
        for k, peer in enumerate(_other_devices(x, y, c)):
            for a in range(n):
                pltpu.make_async_remote_copy(
                    src_ref=x_refs[a].at[pl.ds(c * rows[a], rows[a])], dst_ref=zone_refs[a].at[4 * x + 2 * y + c],
                    send_sem=send_sems.at[k * n + a], recv_sem=recv_sems.at[k * n + a], device_id=peer,
                    device_id_type=MESH).start()
        token[...] = jnp.zeros_like(token)

    thru = [pltpu.HBM(a.shape, a.dtype) for a in arrays] + [pltpu.HBM(z.shape, z.dtype) for z in zones]
    n_sems = (N_DEV - 1) * n
    res = pl.pallas_call(
        body, name=name, in_specs=[HBM] * (2 * n),
        out_specs=(SEM, SEM, *[HBM] * (2 * n), pl.BlockSpec(memory_space=pltpu.VMEM)),
        out_shape=(pltpu.SemaphoreType.DMA((n_sems,)), pltpu.SemaphoreType.DMA((n_sems,)), *thru,
                   jax.ShapeDtypeStruct((8, 128), F32)),
        input_output_aliases={i: 2 + i for i in range(2 * n)},
        compiler_params=pltpu.CompilerParams(has_side_effects=DATAFLOW),
    )(*[pltpu.with_memory_space_constraint(a, pltpu.HBM) for a in arrays],
      *[pltpu.with_memory_space_constraint(z, pltpu.HBM) for z in zones])
    return (res[0], res[1], list(res[2:2 + n]), list(res[2 + n:2 + 2 * n])), res[-1]


def _gather_wait(name, handle, after):
    send_sems, recv_sems, sources, zones = handle
    n = len(sources)
    rows = [z.shape[1] for z in zones]

    def body(*refs):
        x_refs, zone_refs = refs[:n], refs[n:2 * n]
        send_sems, recv_sems = refs[2 * n], refs[2 * n + 1]
        x, y, c = lax.axis_index("x"), lax.axis_index("y"), lax.axis_index("c")
        for k, (px, py, pc) in enumerate(_other_devices(x, y, c)):
            for a in range(n):
                cp = pltpu.make_async_remote_copy(
                    src_ref=x_refs[a].at[pl.ds(c * rows[a], rows[a])], dst_ref=zone_refs[a].at[4 * px + 2 * py + pc],
                    send_sem=send_sems.at[k * n + a], recv_sem=recv_sems.at[k * n + a], device_id=(px, py, pc),
                    device_id_type=MESH)
                cp.wait_send()
                cp.wait_recv()

    res = pl.pallas_call(
        body, name=name, in_specs=[HBM] * (2 * n) + [SEM, SEM, ANY], out_specs=[HBM] * (2 * n),
        out_shape=[pltpu.HBM(a.shape, a.dtype) for a in sources + zones],
        input_output_aliases={i: i for i in range(2 * n)}, compiler_params=pltpu.CompilerParams(has_side_effects=DATAFLOW),
    )(*sources, *zones, send_sems, recv_sems, after)
    return list(res[n:])


def _alltoall_start(name, arrays):
    n = len(arrays)
    lands = [lax.empty((3,) + a.shape[1:], a.dtype) for a in arrays]

    def body(*refs):
        p_refs, land_refs = refs[:n], refs[n:2 * n]
        send_sems, recv_sems = refs[2 * n], refs[2 * n + 1]
        token = refs[-1]
        x, y, c = lax.axis_index("x"), lax.axis_index("y"), lax.axis_index("c")
        for k, (px, py) in enumerate(_chip_peers(x, y)):
            for a in range(n):
                pltpu.make_async_remote_copy(
                    src_ref=p_refs[a].at[2 * px + py], dst_ref=land_refs[a].at[k], send_sem=send_sems.at[k * n + a],
                    recv_sem=recv_sems.at[k * n + a], device_id=(px, py, c), device_id_type=MESH).start()
        token[...] = jnp.zeros_like(token)

    thru = [pltpu.HBM(a.shape, a.dtype) for a in arrays] + [pltpu.HBM(l.shape, l.dtype) for l in lands]
    res = pl.pallas_call(
        body, name=name, in_specs=[HBM] * (2 * n),
        out_specs=(SEM, SEM, *[HBM] * (2 * n), pl.BlockSpec(memory_space=pltpu.VMEM)),
        out_shape=(pltpu.SemaphoreType.DMA((3 * n,)), pltpu.SemaphoreType.DMA((3 * n,)), *thru, jax.ShapeDtypeStruct((8, 128), F32)),
        input_output_aliases={i: 2 + i for i in range(2 * n)},
        compiler_params=pltpu.CompilerParams(has_side_effects=DATAFLOW),
    )(*[pltpu.with_memory_space_constraint(a, pltpu.HBM) for a in arrays],
      *[pltpu.with_memory_space_constraint(l, pltpu.HBM) for l in lands])
    return (res[0], res[1], list(res[2:2 + n]), list(res[2 + n:2 + 2 * n])), res[-1]


def _alltoall_wait(name, handle, after):
    send_sems, recv_sems, sources, lands = handle
    n = len(sources)

    def body(*refs):
        p_refs, land_refs = refs[:n], refs[n:2 * n]
        send_sems, recv_sems = refs[2 * n], refs[2 * n + 1]
        x, y, c = lax.axis_index("x"), lax.axis_index("y"), lax.axis_index("c")
        for k, (px, py) in enumerate(_chip_peers(x, y)):
            for a in range(n):
                cp = pltpu.make_async_remote_copy(
                    src_ref=p_refs[a].at[2 * px + py], dst_ref=land_refs[a].at[k], send_sem=send_sems.at[k * n + a],
                    recv_sem=recv_sems.at[k * n + a], device_id=(px, py, c), device_id_type=MESH)
                cp.wait_send()
                cp.wait_recv()

    res = pl.pallas_call(
        body, name=name, in_specs=[HBM] * (2 * n) + [SEM, SEM, ANY], out_specs=[HBM] * (2 * n),
        out_shape=[pltpu.HBM(a.shape, a.dtype) for a in sources + lands],
        input_output_aliases={i: i for i in range(2 * n)}, compiler_params=pltpu.CompilerParams(has_side_effects=DATAFLOW),
    )(*sources, *lands, send_sems, recv_sems, after)
    return list(res[n:])


def _sibling_exchange(name, halves):
    n = len(halves)

    def body(*refs):
        h_refs, out_refs = refs[:n], refs[n:2 * n]
        send_sems, recv_sems = refs[2 * n:]
        x, y, c = lax.axis_index("x"), lax.axis_index("y"), lax.axis_index("c")
        remote = [pltpu.make_async_remote_copy(
            src_ref=h_refs[a], dst_ref=out_refs[a], send_sem=send_sems.at[a], recv_sem=recv_sems.at[a],
            device_id=(x, y, 1 - c), device_id_type=MESH) for a in range(n)]
        for cp in remote:
            cp.start()
        for cp in remote:
            cp.wait_recv()
        for cp in remote:
            cp.wait_send()

    return pl.pallas_call(
        body, name=name, in_specs=[ANY] * n, out_specs=[ANY] * n,
        out_shape=[jax.ShapeDtypeStruct(h.shape, h.dtype) for h in halves],
        scratch_shapes=[pltpu.SemaphoreType.DMA((n,)), pltpu.SemaphoreType.DMA((n,))],
    )(*halves)


def _sum_arrays(name, terms, n_rows):
    c_ = terms[0][0].shape[1]

    def add_all(*vals):
        acc = vals[0]
        for v in vals[1:]:
            acc = acc + v
        return (acc,)

    return _rowwise(name, add_all, [(a, c_, 0, first) for a, first in terms], [], [(c_, F32)], [], n_rows, tile=128)[0]


def _adamw_fn(w, g, m, v):
    m = ADAM_B1 * m + (1.0 - ADAM_B1) * g
    v = ADAM_B2 * v + (1.0 - ADAM_B2) * jnp.square(g)
    m_hat = m / (1.0 - ADAM_B1 ** ADAM_STEP)
    v_hat = v / (1.0 - ADAM_B2 ** ADAM_STEP)
    delta = -ADAM_LR * (m_hat / (jnp.sqrt(v_hat) + ADAM_EPS) + ADAM_WD * w)
    return delta, m, v


def _adamw(name, w, g, m, v):
    r, c_ = w.shape
    return _rowwise(name, _adamw_fn, [(w, c_, 0), (g, c_, 0), (m, c_, 0), (v, c_, 0)], [], [(c_, F32)] * 3, [], r, tile=128)


def _ada_fwd(c_all, w_ada, b_ada):
    n = w_ada.shape[1]

    def body(c_ref, w_ref, b_ref, o_ref):
        o_ref[...] = jnp.dot(jax.nn.silu(c_ref[...]).astype(BF16), w_ref[...].astype(BF16),
                             preferred_element_type=F32) + b_ref[...]

    return pl.pallas_call(body, name="ada_fwd", out_shape=jax.ShapeDtypeStruct((N_DEV, n), F32),
                          compiler_params=pltpu.CompilerParams(vmem_limit_bytes=VMEM_LIMIT))(c_all, w_ada, b_ada)


def _ada_bwd(c_all, dmod_cols):
    d = c_all.shape[1]
    n = dmod_cols.shape[1]

    def body(c_ref, dm_ref, gw_ref):
        gw_ref[...] = lax.dot_general(jax.nn.silu(c_ref[...]).astype(BF16), dm_ref[...].astype(BF16),
                                      (((0,), (0,)), ((), ())), preferred_element_type=F32)

    return pl.pallas_call(body, name="ada_bwd", out_shape=jax.ShapeDtypeStruct((d, n), F32),
                          compiler_params=pltpu.CompilerParams(vmem_limit_bytes=VMEM_LIMIT))(c_all, dmod_cols)


def _cast_bf16(name, a):
    r, c_ = a.shape
    return _rowwise(name, lambda v: (v,), [(a, c_, 0)], [], [(c_, BF16)], [], r)[0]


def _pad_rows(vec, mult):
    n = vec.shape[0]
    return jnp.pad(vec, (0, (-n) % mult))


def kernel(x, c, positions, w_ada, b_ada, w_in, b_in, conv_dw_w, conv_dw_b, conv_ln_g, conv_ln_b, w_conv_out, ret_gn_g, ret_gn_b, w_ret_out, w_out, ln1_g, ln1_b, w_up, ffn_dw_w, ffn_dw_b, w_down, ln2_g, ln2_b, loss_target, m_w_ada, m_b_ada, m_w_in, m_b_in, m_conv_dw_w, m_conv_dw_b, m_conv_ln_g, m_conv_ln_b, m_w_conv_out, m_ret_gn_g, m_ret_gn_b, m_w_ret_out, m_w_out, m_ln1_g, m_ln1_b, m_w_up, m_ffn_dw_w, m_ffn_dw_b, m_w_down, m_ln2_g, m_ln2_b, v_w_ada, v_b_ada, v_w_in, v_b_in, v_conv_dw_w, v_conv_dw_b, v_conv_ln_g, v_conv_ln_b, v_w_conv_out, v_ret_gn_g, v_ret_gn_b, v_w_ret_out, v_w_out, v_ln1_g, v_ln1_b, v_w_up, v_ffn_dw_w, v_ffn_dw_b, v_w_down, v_ln2_g, v_ln2_b):
    given = dict(locals())
    n_rows = x.shape[1]
    d = D_MODEL
    my_c = lax.axis_index("c")
    chip = 2 * lax.axis_index("x") + lax.axis_index("y")
    dev = 2 * chip + my_c
    xr = x[0]
    target = loss_target[0]
    vw = N_HEADS * DV
    ffw = 2 * D_FF

    def flat_rows(arrays, mult=8):
        v = jnp.concatenate([a.reshape(-1) for a in arrays])
        return _pad_rows(v, mult * d).reshape(-1, d)

    def unflatten(flat2d, shapes):
        v, out, o = flat2d.reshape(-1), [], 0
        for shp in shapes:
            size = 1
            for e in shp:
                size *= e
            out.append(v[o:o + size].reshape(shp))
            o += size
        return out

    w_bf = {n: _cast_bf16("cast_" + n, given[n][0]) for n in BIG_WEIGHTS}
    wg_in = _allgather8("gather_w_in", [w_bf["w_in"]], own_half=True)[0].reshape(N_CHIPS, d, w_in.shape[2])
    others = [n for n in BIG_WEIGHTS if n != "w_in"]
    probe = wg_in[0, 0:1, 0:1].astype(F32)
    zero = (jnp.where(jnp.isfinite(probe), probe, 0.0) * 0.0).astype(BF16)
    gather_handle, gather_token = _gather_start("gather_weights_start", [w_bf[n] + zero for n in others])

    kc, kf = conv_dw_w.shape[2], ffn_dw_w.shape[2]
    small_all = _allgather8("gather_small", [flat_rows([c, conv_dw_w, ffn_dw_w])])[0].reshape(N_DEV, -1)
    c_all = small_all[:, :d]
    per_chip = small_all[0::2]
    conv_w = per_chip[:, d:d + CONV_K * kc].reshape(N_CHIPS, CONV_K, kc).transpose(1, 0, 2).reshape(CONV_K, N_CHIPS * kc)
    o_f = d + CONV_K * kc
    ffn_w = per_chip[:, o_f:o_f + FFN_K * kf].reshape(N_CHIPS, FFN_K, kf).transpose(1, 0, 2).reshape(FFN_K, N_CHIPS * kf)

    n_ada = w_ada.shape[2]
    b_ada_cols = lax.dynamic_slice_in_dim(b_ada, chip * n_ada, n_ada, 1)
    mod_cols = _ada_fwd(c_all, w_ada[0], b_ada_cols)
    mod_all = _allgather8("gather_mod", [mod_cols])[0]
    mod = lax.dynamic_index_in_dim(mod_all[0::2], dev, 1, keepdims=False).reshape(1, N_CHIPS * n_ada)
    shift1, scale1, gate1, shift2, scale2, gate2 = [mod[:, k * d:(k + 1) * d] for k in range(6)]

    h1 = _rowwise("ln_mod1", lambda a, s, t: (_f_modulate(a, s, t),), [(xr, d, 0)],
                  [(scale1 + gather_token[0:1, 0:1], d, 0), (shift1, d, 0)], [(d, BF16)], [], n_rows)[0]
    n_proj = N_CHIPS * w_in.shape[2]
    proj_qk = _mm_nn("mm_in_qk", h1, wg_in, b_in, F32, tile=(2048, 512), window=(0, 2 * d), rows_outer=True)
    proj_rest = _mm_nn("mm_in_rest", h1, wg_in, b_in, BF16, tile=(2048, 512), window=(2 * d, n_proj - 2 * d), rows_outer=True)
    w_all = {}
    for n, zone in zip(others, _gather_wait("gather_weights_wait", gather_handle, proj_rest)):
        own = lax.dynamic_slice_in_dim(w_bf[n], my_c * zone.shape[1], zone.shape[1], 0)
        w_all[n] = lax.dynamic_update_slice_in_dim(zone, own[None], dev, 0)
    wg_up = w_all["w_up"].reshape(N_CHIPS, d, w_up.shape[2])
    wg_conv_out = w_all["w_conv_out"].reshape(1, d, d)
    wg_ret_out = w_all["w_ret_out"].reshape(1, vw, d)
    wg_out = w_all["w_out"].reshape(1, d, d)
    wg_down = w_all["w_down"].reshape(1, D_FF, d)
    a1 = _glu_conv_fwd(proj_rest, 4, 5, conv_w, conv_dw_b, n_rows)
    a2 = _rowwise("conv_norm", lambda a, g, b: (_f_conv_norm(a, g, b),), [(a1, d, 0)], [(conv_ln_g, d, 0), (conv_ln_b, d, 0)],
                  [(d, BF16)], [], n_rows)[0]
    y_a = _mm_nn("mm_conv_out", a2, wg_conv_out, None, F32)
    cos_t, sin_t = _rope_tables(positions[0])
    consts = _retention_consts()
    r, states = _retention_fwd(proj_qk, proj_rest, cos_t, sin_t, consts, n_rows)
    ret_rows = [(r, vw, 0), (proj_rest, vw, 1)]
    ret_vecs = [(ret_gn_g, vw, 0), (ret_gn_b, vw, 0)]
    r2 = _rowwise("ret_norm", _per_head(_f_group_norm_gate), ret_rows, ret_vecs, [(vw, BF16)], [], n_rows)[0]
    y_b = _mm_nn("mm_ret_out", r2, wg_ret_out, None, F32)
    merge_rows = [(proj_rest, d, 6), (proj_rest, d, 7), (y_a, d, 0), (y_b, d, 0)]
    m = _rowwise("merge", lambda *a: (_f_merge(*a),), merge_rows, [], [(d, BF16)], [], n_rows)[0]
    t = _mm_nn("mm_out", m, wg_out, None, F32)
    post1_vecs = [(gate1, d, 0), (ln1_g, d, 0), (ln1_b, d, 0), (scale2, d, 0), (shift2, d, 0)]
    x1, h2 = _rowwise("post1", _f_post1, [(xr, d, 0), (t, d, 0)], post1_vecs, [(d, F32), (d, BF16)], [], n_rows)
    pair_up = lambda c: (c % 2) * 2 + c // 2
    paired = lambda a: a.reshape(a.shape[0], N_CHIPS, kf)[:, jnp.array([0, 2, 1, 3])].reshape(a.shape[0], ffw)
    ffn_w_p, ffn_b_p = paired(ffn_w), paired(ffn_dw_b)
    u = _mm_nn("mm_up", h2, wg_up, None, BF16, gmap=pair_up)
    p, u_filtered = _ffn_fwd(u, ffn_w_p, ffn_b_p, n_rows)
    f = _mm_nn("mm_down", p, wg_down, None, F32)

    def loss_rows(x1v, fv, tv, g2v, lg, lb):
        loss, pull = jax.vjp(lambda a, b, c_, e, h: _f_loss(a, b, c_, e, h, tv), x1v, fv, g2v, lg, lb)
        return (*pull(jnp.ones((), F32)), jnp.full((1, 128), loss, F32))

    dx1_a, df, dgate2, dln2_g, dln2_b, loss_v = _rowwise(
        "loss", loss_rows, [(x1, d, 0), (f, d, 0), (target, d, 0)], [(gate2, d, 0), (ln2_g, d, 0), (ln2_b, d, 0)],
        [(d, F32), (d, BF16)], [d, d, d, 128], n_rows)
    loss = lax.psum(loss_v[0, 0], ("x", "y", "c"))

    dp = _mm_nt("mm_down_dx", df, wg_down, F32)
    gw_down = _mm_tn("mm_down_dw", p, df, 1)
    du, g_ffn_w, g_ffn_b = _ffn_bwd(u, u_filtered, dp, ffn_w_p, n_rows)
    g_ffn_w, g_ffn_b = paired(g_ffn_w), paired(g_ffn_b)
    dh2 = _mm_nt("mm_up_dx", du, wg_up, F32, gmap=pair_up)
    gw_up = _mm_tn("mm_up_dw", h2, du, N_CHIPS, gmap=pair_up)

    dx_a, dt, dgate1, dln1_g, dln1_b, dscale2, dshift2 = _rowwise(
        "post1_bwd", _vjp_rows(_f_post1, 2, 2), [(xr, d, 0), (t, d, 0), (dx1_a, d, 0), (dh2, d, 0)], post1_vecs,
        [(d, F32), (d, BF16)], [d] * 5, n_rows)
    dm = _mm_nt("mm_out_dx", dt, wg_out, F32)
    gw_out = _mm_tn("mm_out_dw", m, dt, 1)
    merge_vjp = _vjp_rows(_f_merge, 4, 1)

    def merge_bwd(*a):
        dga, dgb, dya, dyb = merge_vjp(*a)
        return jnp.concatenate([dga, dgb], axis=1), dya, dyb

    dproj, dya, dyb = _rowwise("merge_bwd", merge_bwd, merge_rows + [(dm, d, 0)], [],
                               [(2 * d, BF16, n_proj, 8192 // (2 * d)), (d, BF16), (d, BF16)], [], n_rows)

    da2 = _mm_nt("mm_conv_out_dx", dya, wg_conv_out, F32)
    gw_conv_out = _mm_tn("mm_conv_out_dw", a2, dya, 1)
    da1, dcl_g, dcl_b = _rowwise("conv_norm_bwd", _vjp_rows(_f_conv_norm, 1, 1), [(a1, d, 0), (da2, d, 0)],
                                 [(conv_ln_g, d, 0), (conv_ln_b, d, 0)], [(d, F32)], [d, d], n_rows)
    dproj, g_conv_w, g_conv_b = _glu_conv_bwd(proj_rest, 4, 5, da1, conv_w, dproj, 6144 // (2 * d), n_rows)

    dr2 = _mm_nt("mm_ret_out_dx", dyb, wg_ret_out, F32)
    gw_ret_out = _mm_tn("mm_ret_out_dw", r2, dyb, 1)

    def reduce_start(tag, names, fulls):
        shapes = [given[n].shape[1:] for n in names]
        fulls = [g_.reshape(N_CHIPS, r, c_) for g_, (r, c_) in zip(fulls, shapes)]
        theirs = _sibling_swap("grad_swap_" + tag, fulls)
        pair = []
        for n, (r, c_), g_, t_ in zip(names, shapes, fulls, theirs):
            mine = lax.dynamic_slice_in_dim(g_, my_c * (r // 2), r // 2, 1)
            pair.append(_sum_arrays("grad_pair_sum_" + n, [(mine.reshape(-1, c_), 0), (t_.reshape(-1, c_), 0)],
                                    N_CHIPS * (r // 2)).reshape(N_CHIPS, r // 2, c_))
        handle, token = _alltoall_start("grad_alltoall_start_" + tag, pair)
        own = [lax.dynamic_index_in_dim(p_, chip, 0, keepdims=False) for p_ in pair]
        return (names, shapes, handle, own), token[0:1, 0:1]

    def reduce_finish(tag, started, after):
        names, shapes, handle, own = started
        lands = _alltoall_wait("grad_alltoall_wait_" + tag, handle, after)
        return [_sum_arrays("grad_chip_sum_" + n, [(o_, 0)] + [(l_.reshape(-1, c_), k * (r // 2)) for k in range(3)], r // 2)
                for n, (r, c_), o_, l_ in zip(names, shapes, own, lands)]

    late = ("w_up", "w_down", "w_out", "w_conv_out", "w_ret_out")
    started_late, token_late = reduce_start("late", late, [gw_up, gw_down, gw_out, gw_conv_out, gw_ret_out])
    ret_vecs_bwd = [(ret_gn_g + token_late, vw, 0), (ret_gn_b, vw, 0)]
    dr, dproj, dgn_g, dgn_b = _rowwise(
        "ret_norm_bwd", _per_head(_vjp_rows(_f_group_norm_gate, 2, 1)), ret_rows + [(dr2, vw, 0)], ret_vecs_bwd,
        [(vw, F32), (vw, BF16, n_proj, 4096 // vw)], [vw, vw], n_rows, into=(dproj, 1))
    dproj = _retention_bwd(proj_qk, proj_rest, cos_t, sin_t, states, dr, consts, dproj, n_rows)

    gw_in, gb_in = _mm_tn("mm_in_dw", h1, dproj, N_CHIPS, colsum=True)
    started_in, token_in = reduce_start("in", ("w_in",), [gw_in])
    dh1 = _mm_nt("mm_in_dx", dproj, wg_in, F32)
    mod_bwd = _vjp_rows(_f_modulate, 1, 1)

    def mod1_bwd(xv, dhv, dxav, sv, tv):
        dx, ds, dsh = mod_bwd(xv, dhv, sv, tv)
        return dx + dxav, ds, dsh

    grad_x, dscale1, dshift1 = _rowwise("ln_mod1_bwd", mod1_bwd, [(xr, d, 0), (dh1, d, 0), (dx_a, d, 0)],
                                        [(scale1 + token_in, d, 0), (shift1, d, 0)], [(d, F32)], [d, d], n_rows)

    dmod = jnp.concatenate([dshift1, dscale1, dgate1, dshift2, dscale2, dgate2], axis=1)
    small_names = ["b_in", "conv_dw_w", "conv_dw_b", "conv_ln_g", "conv_ln_b", "ret_gn_g", "ret_gn_b", "ln1_g", "ln1_b",
                   "ffn_dw_w", "ffn_dw_b", "ln2_g", "ln2_b"]
    small_parts = [gb_in, g_conv_w, g_conv_b, dcl_g, dcl_b, dgn_g, dgn_b, dln1_g, dln1_b, g_ffn_w, g_ffn_b, dln2_g, dln2_b, dmod]
    small_shapes = [a.shape for a in small_parts]
    parts_all = _allgather8("gather_small_grads", [flat_rows(small_parts)])[0]
    part_rows = parts_all.shape[1]
    summed = _sum_arrays("sum_small_grads", [(parts_all.reshape(N_DEV * part_rows, d), k * part_rows) for k in range(N_DEV)],
                         part_rows)
    small_sum = unflatten(summed, small_shapes)
    grads = dict(zip(small_names, small_sum[:-1]))
    grads["b_ada"] = small_sum[-1]
    grads["conv_dw_w"] = lax.dynamic_slice_in_dim(grads["conv_dw_w"], chip * kc, kc, 1)
    grads["ffn_dw_w"] = lax.dynamic_slice_in_dim(grads["ffn_dw_w"], chip * kf, kf, 1)
    o_mod = sum(a.size for a in small_parts[:-1])
    dmod_all = parts_all.reshape(N_DEV, -1)[:, o_mod:o_mod + dmod.shape[1]]
    grads["w_ada"] = _ada_bwd(c_all, lax.dynamic_slice_in_dim(dmod_all, chip * n_ada, n_ada, 1))

    reduced = reduce_finish("late", started_late, grad_x) + reduce_finish("in", started_in, grad_x)
    big_order = late + ("w_in",)
    g_shards = []
    for n, mine_, theirs_ in zip(big_order, reduced, _sibling_exchange("grad_exchange", reduced)):
        both = lax.dynamic_update_slice_in_dim(jnp.stack([theirs_, theirs_]), mine_[None], my_c, 0)
        g_shards.append(both.reshape(given[n].shape[1:]))

    outs = {}
    for n, g_ in zip(big_order, g_shards):
        upd = _adamw("adamw_" + n, given[n][0], g_, given["m_" + n][0], given["v_" + n][0])
        for prefix, val in zip(("grad_", "delta_", "new_m_", "new_v_"), (g_, *upd)):
            outs[prefix + n] = val.reshape(given[n].shape)
    ada = _adamw("adamw_ada", w_ada[0], grads["w_ada"], m_w_ada[0], v_w_ada[0])
    for prefix, val in zip(("grad_", "delta_", "new_m_", "new_v_"), (grads["w_ada"], *ada)):
        outs[prefix + "w_ada"] = val.reshape(w_ada.shape)
    small_all_names = ["b_ada"] + small_names
    small_w_shapes = [given[n].shape for n in small_all_names]
    g_small = flat_rows([grads[n] for n in small_all_names])
    small_upd = _adamw("adamw_small", flat_rows([given[n] for n in small_all_names]), g_small,
                       flat_rows([given["m_" + n] for n in small_all_names]), flat_rows([given["v_" + n] for n in small_all_names]))
    for prefix, packed in zip(("grad_", "delta_", "new_m_", "new_v_"), (g_small, *small_upd)):
        for n, val in zip(small_all_names, unflatten(packed, small_w_shapes)):
            outs[prefix + n] = val

    weights = ["w_ada", "b_ada", "w_in", "b_in", "conv_dw_w", "conv_dw_b", "conv_ln_g", "conv_ln_b", "w_conv_out", "ret_gn_g",
               "ret_gn_b", "w_ret_out", "w_out", "ln1_g", "ln1_b", "w_up", "ffn_dw_w", "ffn_dw_b", "w_down", "ln2_g", "ln2_b"]
    result = [loss, grad_x.reshape(x.shape)]
    for prefix in ("grad_", "delta_", "new_m_", "new_v_"):
        result += [outs[prefix + n] for n in weights]
    return tuple(result)
```

```python
import jax
import jax.numpy as jnp
from jax import lax
from jax.experimental import pallas as pl
from jax.experimental.pallas import tpu as pltpu

F32 = jnp.float32
BF16 = jnp.bfloat16
MESH = pl.DeviceIdType.MESH

D_MODEL = 1024
N_HEADS = 8
DK = 128
DV = 256
CHUNK = 128
ROPE_BASE = 10000.0
D_FF = 2816
CONV_K = 31
FFN_K = 3
LN_EPS = 1e-5
ALPHA = (2.0 * 1) ** 0.25
ADAM_LR = 0.001
ADAM_B1 = 0.9
ADAM_B2 = 0.999
ADAM_EPS = 1e-08
ADAM_WD = 0.01
ADAM_STEP = 10

V7X_VMEM_BYTES = 64 * 1024 * 1024
VMEM_LIMIT = V7X_VMEM_BYTES - 8 * 1024 * 1024
ROW_TILE = 256
MM_TILE = 512
N_CHIPS = 4
N_DEV = 8

BIG_WEIGHTS = ("w_in", "w_up", "w_conv_out", "w_ret_out", "w_out", "w_down")


def _params(n_grid):
    return pltpu.CompilerParams(dimension_semantics=("arbitrary",) * n_grid, vmem_limit_bytes=VMEM_LIMIT)


def _rowwise(name, fn, rows, vecs, outs, reds, n_rows, tile=ROW_TILE, ncol=1, with_col=False, into=None):
    tile = _fit_tile(n_rows, tile)
    n_in = len(rows) + len(vecs)
    n_ref_in = n_in + (into is not None)
    n_out = len(outs)
    outs = [o if len(o) == 4 else (o[0], o[1], o[0] * ncol, 0) for o in outs]

    def col_map(off, row, first_row=0):
        def index(j, i):
            return (i + first_row // tile if row else 0, off(j) if callable(off) else off + j)
        return index

    def body(*refs):
        i = pl.program_id(1)
        vals = [r[...].astype(F32) for r in refs[:n_in]]
        res = fn(pl.program_id(0), *vals) if with_col else fn(*vals)
        for k in range(n_out):
            refs[n_ref_in + k][...] = res[k].astype(refs[n_ref_in + k].dtype)
        for k in range(len(reds)):
            o = refs[n_ref_in + n_out + k]

            @pl.when(i == 0)
            def _():
                o[...] = jnp.zeros_like(o)

            o[...] += res[n_out + k]

    in_specs = [pl.BlockSpec((tile, e[1]), col_map(e[2], True, e[3] if len(e) > 3 else 0)) for e in rows]
    in_specs += [pl.BlockSpec((1, w), col_map(off, False)) for _, w, off in vecs]
    args = [e[0] for e in rows] + [a for a, _, _ in vecs]
    aliases = {}
    if into is not None:
        in_specs.append(pl.BlockSpec(memory_space=pl.ANY))
        args.append(into[0])
        aliases = {n_in: into[1]}
    out_specs = [pl.BlockSpec((tile, w), col_map(off, True)) for w, _, _, off in outs]
    out_specs += [pl.BlockSpec((1, w), lambda j, i: (0, j)) for w in reds]
    out_shape = [jax.ShapeDtypeStruct((n_rows, total), dt) for _, dt, total, _ in outs]
    out_shape += [jax.ShapeDtypeStruct((1, w * ncol), F32) for w in reds]
    return pl.pallas_call(
        body, name=name, grid=(ncol, n_rows // tile), in_specs=in_specs, out_specs=out_specs, out_shape=out_shape,
        input_output_aliases=aliases, compiler_params=_params(2),
    )(*args)


def _vjp_rows(fn, n_row_in, n_ct):
    def bwd(*args):
        prim = [a.astype(F32) for a in args[:n_row_in] + args[n_row_in + n_ct:]]
        cts = tuple(a.astype(F32) for a in args[n_row_in:n_row_in + n_ct])
        _, pull = jax.vjp(fn, *prim)
        return pull(cts if n_ct > 1 else cts[0])

    return bwd


def _fit_tile(n, pref):
    if n <= pref:
        return n
    t = pref - pref % 16
    while n % t:
        t -= 16
    return t


def _col_tile(n):
    return n if n <= 1536 else n // 2


def _same_group(c):
    return c


def _mm_nn(name, a, w, bias, out_dtype, gmap=_same_group, tile=None, window=None, rows_outer=False):
    s, k = a.shape
    g, _, n = w.shape
    tm, tn = (min(MM_TILE, s), _col_tile(n)) if tile is None else (min(tile[0], s), tile[1])
    nt = n // tn
    first, n_cols = (0, g * n) if window is None else window
    t0 = first // tn

    def body(*refs):
        a_ref, w_ref = refs[0], refs[1]
        o_ref = refs[-1]
        acc = jnp.dot(a_ref[...].astype(BF16), w_ref[...], preferred_element_type=F32)
        if bias is not None:
            acc = acc + refs[2][...]
        o_ref[...] = acc.astype(o_ref.dtype)

    order = (lambda i, c: (c, i)) if rows_outer else (lambda c, i: (c, i))
    a_map = lambda *g_: (order(*g_)[1], 0)
    w_map = lambda *g_: (gmap((t0 + order(*g_)[0]) // nt), 0, (t0 + order(*g_)[0]) % nt)
    in_specs = [pl.BlockSpec((tm, k), a_map), pl.BlockSpec((None, k, tn), w_map)]
    args = [a, w]
    if bias is not None:
        in_specs.append(pl.BlockSpec((1, tn), lambda *g_: (0, t0 + order(*g_)[0])))
        args.append(bias)
    grid = (s // tm, n_cols // tn) if rows_outer else (n_cols // tn, s // tm)
    return pl.pallas_call(
        body, name=name, grid=grid, in_specs=in_specs, out_specs=pl.BlockSpec((tm, tn), lambda *g_: order(*g_)[::-1]),
        out_shape=jax.ShapeDtypeStruct((s, n_cols), out_dtype), compiler_params=_params(2),
    )(*args)


def _mm_nt(name, dy, w, out_dtype, gmap=_same_group):
    s = dy.shape[0]
    g, k, n = w.shape
    tm, tn = min(2 * MM_TILE if k <= 1024 else MM_TILE, s), n
    nt = 1
    steps = g

    def body(dy_ref, w_ref, o_ref, acc_ref):
        r = pl.program_id(1)
        part = lax.dot_general(dy_ref[...].astype(BF16), w_ref[...], (((1,), (1,)), ((), ())), preferred_element_type=F32)

        @pl.when(r == 0)
        def _():
            acc_ref[...] = part

        @pl.when(r > 0)
        def _():
            acc_ref[...] += part

        @pl.when(r == steps - 1)
        def _():
            o_ref[...] = acc_ref[...].astype(o_ref.dtype)

    return pl.pallas_call(
        body, name=name, grid=(s // tm, steps),
        in_specs=[pl.BlockSpec((tm, tn), lambda i, r: (i, r)),
                  pl.BlockSpec((None, k, tn), lambda i, r: (gmap(r // nt), 0, r % nt))],
        out_specs=pl.BlockSpec((tm, k), lambda i, r: (i, 0)), out_shape=jax.ShapeDtypeStruct((s, k), out_dtype),
        scratch_shapes=[pltpu.VMEM((tm, k), F32)], compiler_params=_params(2),
    )(dy, w)


def _mm_tn(name, a, dy, g, gmap=_same_group, colsum=False):
    s, k = a.shape
    n = dy.shape[1] // g
    ts = min(2048 if k <= 1024 else 1024, s)
    tn = n if k * n <= 1024 * 1536 else (n // 2 if (n // 2) % 128 == 0 else n)
    nt = n // tn

    def body(a_ref, dy_ref, o_ref, *sum_ref):
        t = pl.program_id(1)
        dyv = dy_ref[...].astype(BF16)
        part = lax.dot_general(a_ref[...].astype(BF16), dyv, (((0,), (0,)), ((), ())), preferred_element_type=F32)

        @pl.when(t == 0)
        def _():
            o_ref[...] = part

        @pl.when(t > 0)
        def _():
            o_ref[...] += part

        if colsum:
            col = jnp.sum(dyv.astype(F32), axis=0, keepdims=True)

            @pl.when(t == 0)
            def _():
                sum_ref[0][...] = col

            @pl.when(t > 0)
            def _():
                sum_ref[0][...] += col

    out_specs = [pl.BlockSpec((None, k, tn), lambda c, t: (gmap(c // nt), 0, c % nt))]
    out_shape = [jax.ShapeDtypeStruct((g, k, n), F32)]
    if colsum:
        out_specs.append(pl.BlockSpec((1, tn), lambda c, t: (0, c)))
        out_shape.append(jax.ShapeDtypeStruct((1, g * n), F32))
    res = pl.pallas_call(
        body, name=name, grid=(g * nt, s // ts),
        in_specs=[pl.BlockSpec((ts, k), lambda c, t: (t, 0)), pl.BlockSpec((ts, tn), lambda c, t: (t, c))],
        out_specs=out_specs, out_shape=out_shape, compiler_params=_params(2),
    )(a, dy)
    return res if colsum else res[0]


SUBLANES = 8


def _tap_sum(read, w_row, offsets, tile):
    acc = None
    for b in range(SUBLANES):
        group = [(k, o) for k, o in offsets if o % SUBLANES == b]
        if not group:
            continue
        rows = tile if b == 0 else tile + SUBLANES
        z = None
        for k, o in group:
            term = w_row(k) * read(o - b, rows)
            z = term if z is None else z + term
        part = z if b == 0 else z[b:b + tile]
        acc = part if acc is None else acc + part
    return acc


def _tap_grads(read, dy, offsets, tile):
    padded = jnp.concatenate([dy, jnp.zeros((SUBLANES, dy.shape[1]), dy.dtype)], axis=0)
    out = {}
    for b in range(SUBLANES):
        group = [(k, o) for k, o in offsets if o % SUBLANES == b]
        if not group:
            continue
        shifted = dy if b == 0 else pltpu.roll(padded, b, 0)
        rows = tile if b == 0 else tile + SUBLANES
        for k, o in group:
            out[k] = jnp.sum(shifted * read(o - b, rows), axis=0, keepdims=True)
    return out


CONV_HALO = 32
CONV_TILE = 256
CONV_LANES = 256


def _glu_conv_fwd(proj, col_v, col_g, w, b, n_rows):
    kw, n_ch = w.shape
    tile = min(CONV_TILE, n_rows)
    per = tile // CONV_HALO
    offsets = [(k, CONV_HALO - (kw - 1) + k) for k in range(kw)]

    def body(v_ref, g_ref, vh_ref, gh_ref, w_ref, b_ref, o_ref, buf):
        i = pl.program_id(0)
        prev = vh_ref[...].astype(F32) * jax.nn.sigmoid(gh_ref[...].astype(F32))
        buf[0:CONV_HALO, :] = jnp.where(i == 0, 0.0, prev)
        buf[CONV_HALO:CONV_HALO + tile, :] = v_ref[...].astype(F32) * jax.nn.sigmoid(g_ref[...].astype(F32))
        for c0 in range(0, n_ch, CONV_LANES):
            cols = slice(c0, c0 + CONV_LANES)
            o_ref[:, cols] = b_ref[:, cols] + _tap_sum(lambda s, n: buf[pl.ds(s, n), cols], lambda k: w_ref[k:k + 1, cols],
                                                      offsets, tile)

    main = lambda col: pl.BlockSpec((tile, n_ch), lambda i: (i, col))
    halo = lambda col: pl.BlockSpec((CONV_HALO, n_ch), lambda i: (jnp.maximum(i * per - 1, 0), col))
    return pl.pallas_call(
        body, name="conv_fwd", grid=(n_rows // tile,),
        in_specs=[main(col_v), main(col_g), halo(col_v), halo(col_g), pl.BlockSpec((kw, n_ch), lambda i: (0, 0)),
                  pl.BlockSpec((1, n_ch), lambda i: (0, 0))],
        out_specs=pl.BlockSpec((tile, n_ch), lambda i: (i, 0)), out_shape=jax.ShapeDtypeStruct((n_rows, n_ch), F32),
        scratch_shapes=[pltpu.VMEM((CONV_HALO + tile, n_ch), F32)], compiler_params=_params(1),
    )(proj, proj, proj, proj, w, b)


def _glu_conv_bwd(proj, col_v, col_g, dy, w, dproj, col_out, n_rows):
    kw, n_ch = w.shape
    tile = min(CONV_TILE, n_rows)
    per = tile // CONV_HALO
    n_tiles = n_rows // tile
    last_halo = n_rows // CONV_HALO - 1
    offsets = [(k, CONV_HALO - (kw - 1) + k) for k in range(kw)]
    back = [(k, kw - 1 - k) for k in range(kw)]

    def body(v_ref, g_ref, vh_ref, gh_ref, dy_ref, dyn_ref, w_ref, dp_any, dx_ref, dw_ref, db_ref, buf, dbuf):
        i = pl.program_id(0)
        cv, cg = v_ref[...].astype(F32), g_ref[...].astype(F32)
        sig = jax.nn.sigmoid(cg)
        prev = vh_ref[...].astype(F32) * jax.nn.sigmoid(gh_ref[...].astype(F32))
        buf[0:CONV_HALO, :] = jnp.where(i == 0, 0.0, prev)
        buf[CONV_HALO:CONV_HALO + tile, :] = cv * sig
        dbuf[0:tile, :] = dy_ref[...]
        dbuf[tile:tile + CONV_HALO, :] = jnp.where(i == n_tiles - 1, 0.0, dyn_ref[...])

        @pl.when(i == 0)
        def _():
            dw_ref[...] = jnp.zeros_like(dw_ref)
            db_ref[...] = jnp.zeros_like(db_ref)

        db_ref[...] += jnp.sum(dy_ref[...], axis=0, keepdims=True)
        for c0 in range(0, n_ch, CONV_LANES):
            cols = slice(c0, c0 + CONV_LANES)
            w_row = lambda k: w_ref[k:k + 1, cols]
            dx = _tap_sum(lambda s, n: dbuf[pl.ds(s, n), cols], w_row, back, tile)
            grads = _tap_grads(lambda s, n: buf[pl.ds(s, n), cols], dy_ref[:, cols], offsets, tile)
            for k in range(kw):
                dw_ref[k:k + 1, cols] += grads[k]
            sg = sig[:, cols]
            dx_ref[:, c0:c0 + CONV_LANES] = (dx * sg).astype(dx_ref.dtype)
            dx_ref[:, n_ch + c0:n_ch + c0 + CONV_LANES] = (dx * cv[:, cols] * sg * (1.0 - sg)).astype(dx_ref.dtype)

    main = lambda col: pl.BlockSpec((tile, n_ch), lambda i: (i, col))
    halo = lambda col: pl.BlockSpec((CONV_HALO, n_ch), lambda i: (jnp.maximum(i * per - 1, 0), col))
    return pl.pallas_call(
        body, name="conv_bwd", grid=(n_tiles,),
        in_specs=[main(col_v), main(col_g), halo(col_v), halo(col_g), main(0),
                  pl.BlockSpec((CONV_HALO, n_ch), lambda i: (jnp.minimum((i + 1) * per, last_halo), 0)),
                  pl.BlockSpec((kw, n_ch), lambda i: (0, 0)), pl.BlockSpec(memory_space=pl.ANY)],
        out_specs=[pl.BlockSpec((tile, 2 * n_ch), lambda i: (i, col_out)), pl.BlockSpec((kw, n_ch), lambda i: (0, 0)),
                   pl.BlockSpec((1, n_ch), lambda i: (0, 0))],
        out_shape=[jax.ShapeDtypeStruct(dproj.shape, dproj.dtype), jax.ShapeDtypeStruct((kw, n_ch), F32),
                   jax.ShapeDtypeStruct((1, n_ch), F32)],
        input_output_aliases={7: 0},
        scratch_shapes=[pltpu.VMEM((CONV_HALO + tile, n_ch), F32), pltpu.VMEM((tile + CONV_HALO, n_ch), F32)],
        compiler_params=_params(1),
    )(proj, proj, proj, proj, dy, dy, w, dproj)


FFN_HALO = 8


def _ffn_fwd(u, w, b, n_rows):
    kw = w.shape[0]
    half = u.shape[1] // 4
    tile = min(ROW_TILE, n_rows)
    per = tile // FFN_HALO
    offsets = [(k, FFN_HALO - (kw - 1) + k) for k in range(kw)]

    def body(u_ref, uh_ref, w_ref, b_ref, p_ref, f_ref, buf):
        i = pl.program_id(1)
        buf[0:FFN_HALO, :] = jnp.where(i == 0, 0.0, uh_ref[...].astype(F32))
        buf[FFN_HALO:FFN_HALO + tile, :] = u_ref[...].astype(F32)
        conv = []
        for c0 in (0, half):
            cols = slice(c0, c0 + half)
            conv.append(b_ref[:, cols] + _tap_sum(lambda s, n: buf[pl.ds(s, n), cols], lambda k: w_ref[k:k + 1, cols],
                                                  offsets, tile))
            f_ref[:, cols] = conv[-1].astype(f_ref.dtype)
        p_ref[...] = (conv[0] * jax.nn.silu(conv[1])).astype(p_ref.dtype)

    return pl.pallas_call(
        body, name="ffn_fwd", grid=(2, n_rows // tile),
        in_specs=[pl.BlockSpec((tile, 2 * half), lambda j, i: (i, j)),
                  pl.BlockSpec((FFN_HALO, 2 * half), lambda j, i: (jnp.maximum(i * per - 1, 0), j)),
                  pl.BlockSpec((kw, 2 * half), lambda j, i: (0, j)), pl.BlockSpec((1, 2 * half), lambda j, i: (0, j))],
        out_specs=[pl.BlockSpec((tile, half), lambda j, i: (i, j)), pl.BlockSpec((tile, 2 * half), lambda j, i: (i, j))],
        out_shape=[jax.ShapeDtypeStruct((n_rows, 2 * half), BF16), jax.ShapeDtypeStruct(u.shape, BF16)],
        scratch_shapes=[pltpu.VMEM((FFN_HALO + tile, 2 * half), F32)], compiler_params=_params(2),
    )(u, u, w, b)


def _ffn_bwd(u, filtered, dp, w, n_rows):
    kw = w.shape[0]
    half = u.shape[1] // 4
    tile = min(ROW_TILE, n_rows)
    per = tile // FFN_HALO
    n_tiles = n_rows // tile
    last_halo = n_rows // FFN_HALO - 1
    ext = tile + FFN_HALO
    offsets = [(k, FFN_HALO - (kw - 1) + k) for k in range(kw)]
    back = [(k, kw - 1 - k) for k in range(kw)]

    def body(u_ref, up_ref, f_ref, fn_ref, dp_ref, dpn_ref, w_ref, du_ref, dw_ref, db_ref, buf, dbuf):
        i = pl.program_id(1)
        buf[0:FFN_HALO, :] = jnp.where(i == 0, 0.0, up_ref[...].astype(F32))
        buf[FFN_HALO:FFN_HALO + tile, :] = u_ref[...].astype(F32)
        filt = jnp.concatenate([f_ref[...], fn_ref[...]], axis=0).astype(F32)
        val, gate = filt[:, 0:half], filt[:, half:2 * half]
        dpe = jnp.concatenate([dp_ref[...].astype(F32), jnp.where(i == n_tiles - 1, 0.0, dpn_ref[...].astype(F32))], axis=0)
        sig = jax.nn.sigmoid(gate)
        dbuf[:, 0:half] = dpe * gate * sig
        dbuf[:, half:2 * half] = dpe * val * sig * (1.0 + gate * (1.0 - sig))

        @pl.when(i == 0)
        def _():
            dw_ref[...] = jnp.zeros_like(dw_ref)
            db_ref[...] = jnp.zeros_like(db_ref)

        for c0 in (0, half):
            cols = slice(c0, c0 + half)
            du_ref[:, cols] = _tap_sum(lambda s, n: dbuf[pl.ds(s, n), cols], lambda k: w_ref[k:k + 1, cols], back,
                                       tile).astype(du_ref.dtype)
            d_main = dbuf[0:tile, cols]
            db_ref[:, cols] += jnp.sum(d_main, axis=0, keepdims=True)
            grads = _tap_grads(lambda s, n: buf[pl.ds(s, n), cols], d_main, offsets, tile)
            for k in range(kw):
                dw_ref[k:k + 1, cols] += grads[k]

    wide = 2 * half
    return pl.pallas_call(
        body, name="ffn_bwd", grid=(2, n_tiles),
        in_specs=[pl.BlockSpec((tile, wide), lambda j, i: (i, j)),
                  pl.BlockSpec((FFN_HALO, wide), lambda j, i: (jnp.maximum(i * per - 1, 0), j)),
                  pl.BlockSpec((tile, wide), lambda j, i: (i, j)),
                  pl.BlockSpec((FFN_HALO, wide), lambda j, i: (jnp.minimum((i + 1) * per, last_halo), j)),
                  pl.BlockSpec((tile, half), lambda j, i: (i, j)),
                  pl.BlockSpec((FFN_HALO, half), lambda j, i: (jnp.minimum((i + 1) * per, last_halo), j)),
                  pl.BlockSpec((kw, wide), lambda j, i: (0, j))],
        out_specs=[pl.BlockSpec((tile, wide), lambda j, i: (i, j)), pl.BlockSpec((kw, wide), lambda j, i: (0, j)),
                   pl.BlockSpec((1, wide), lambda j, i: (0, j))],
        out_shape=[jax.ShapeDtypeStruct(u.shape, BF16), jax.ShapeDtypeStruct((kw, u.shape[1]), F32),
                   jax.ShapeDtypeStruct((1, u.shape[1]), F32)],
        scratch_shapes=[pltpu.VMEM((FFN_HALO + tile, wide), F32), pltpu.VMEM((ext, wide), F32)], compiler_params=_params(2),
    )(u, u, filtered, filtered, dp, dp, w)


def _retention_consts():
    log_gamma = jnp.log(1.0 - 2.0 ** (-5.0 - jnp.arange(N_HEADS, dtype=F32)))
    idx = jnp.arange(CHUNK, dtype=F32)
    rel = idx[:, None] - idx[None, :]
    decay = jnp.where(rel[None] >= 0, jnp.exp(log_gamma[:, None, None] * jnp.maximum(rel, 0.0)[None]), 0.0)
    zeta = jnp.exp(log_gamma[:, None] * (CHUNK - 1.0 - idx)[None])
    xi = jnp.exp(log_gamma[:, None] * (idx + 1.0)[None])
    chunk_decay = jnp.exp(log_gamma * CHUNK)
    xi_b = jnp.broadcast_to(xi[:, :, None], (N_HEADS, CHUNK, DK))
    zeta_b = jnp.broadcast_to(zeta[:, :, None], (N_HEADS, CHUNK, DK))
    cd_b = jnp.broadcast_to(chunk_decay[:, None, None], (N_HEADS, 8, DV))
    return decay, xi_b, zeta_b, cd_b


def _rope_tables(positions):
    half = DK // 2
    inv_freq = ROPE_BASE ** (-jnp.arange(half, dtype=F32) / half)
    ang = positions.astype(F32)[:, None] * inv_freq
    cos, sin = jnp.cos(ang), jnp.sin(ang)
    return jnp.concatenate([cos, cos], axis=-1), jnp.concatenate([-sin, sin], axis=-1)


def _swap_halves(v):
    return pltpu.roll(v, DK // 2, 1)


def _dot(a, b):
    return jnp.dot(a, b, preferred_element_type=F32)


def _dot_nt(a, b):
    return lax.dot_general(a, b, (((1,), (1,)), ((), ())), preferred_element_type=F32)


def _dot_tn(a, b):
    return lax.dot_general(a, b, (((0,), (0,)), ((), ())), preferred_element_type=F32)


def _const_specs():
    return [pl.BlockSpec((N_HEADS, CHUNK, CHUNK), lambda n: (0, 0, 0)), pl.BlockSpec((N_HEADS, CHUNK, DK), lambda n: (0, 0, 0)),
            pl.BlockSpec((N_HEADS, CHUNK, DK), lambda n: (0, 0, 0)), pl.BlockSpec((N_HEADS, 8, DV), lambda n: (0, 0, 0))]


def _retention_fwd(proj_qk, proj_rest, cos_t, sin_t, consts, n_rows):
    n_chunks = n_rows // CHUNK
    scale = DK ** -0.5

    def body(q_ref, k_ref, v_ref, cf_ref, ss_ref, d_ref, xi_ref, zt_ref, cd_ref, r_ref, st_ref, state):
        @pl.when(pl.program_id(0) == 0)
        def _():
            state[...] = jnp.zeros_like(state)

        cf, ss = cf_ref[...], ss_ref[...]
        for h in range(N_HEADS):
            qh = q_ref[:, h * DK:(h + 1) * DK].astype(F32)
            kh = k_ref[:, h * DK:(h + 1) * DK].astype(F32)
            qh = qh * cf + _swap_halves(qh) * ss
            kh = (kh * cf + _swap_halves(kh) * ss) * scale
            vh = v_ref[:, h * DV:(h + 1) * DV].astype(BF16)
            st = state[h]
            st_ref[0, h] = st
            sd = _dot_nt(qh.astype(BF16), kh.astype(BF16)) * d_ref[h]
            inner = _dot(sd.astype(BF16), vh)
            cross = _dot((qh * xi_ref[h]).astype(BF16), st.astype(BF16))
            kv = _dot_tn((kh * zt_ref[h]).astype(BF16), vh)
            state[h] = st * cd_ref[h, 0:1, :] + kv
            r_ref[:, h * DV:(h + 1) * DV] = inner + cross

    qk = N_HEADS * DK
    vw = N_HEADS * DV
    return pl.pallas_call(
        body, name="retention_fwd", grid=(n_chunks,),
        in_specs=[pl.BlockSpec((CHUNK, qk), lambda n: (n, 0)), pl.BlockSpec((CHUNK, qk), lambda n: (n, 1)),
                  pl.BlockSpec((CHUNK, vw), lambda n: (n, 0)), pl.BlockSpec((CHUNK, DK), lambda n: (n, 0)),
                  pl.BlockSpec((CHUNK, DK), lambda n: (n, 0))] + _const_specs(),
        out_specs=[pl.BlockSpec((CHUNK, vw), lambda n: (n, 0)), pl.BlockSpec((1, N_HEADS, DK, DV), lambda n: (n, 0, 0, 0))],
        out_shape=[jax.ShapeDtypeStruct((n_rows, vw), F32), jax.ShapeDtypeStruct((n_chunks, N_HEADS, DK, DV), F32)],
        scratch_shapes=[pltpu.VMEM((N_HEADS, DK, DV), F32)], compiler_params=_params(1),
    )(proj_qk, proj_qk, proj_rest, cos_t, sin_t, *consts)


def _retention_bwd(proj_qk, proj_rest, cos_t, sin_t, states, dr, consts, dproj, n_rows):
    n_chunks = n_rows // CHUNK
    scale = DK ** -0.5
    qk = N_HEADS * DK
    vw = N_HEADS * DV

    def body(q_ref, k_ref, v_ref, cf_ref, ss_ref, st_ref, dr_ref, d_ref, xi_ref, zt_ref, cd_ref, dp_any, dqkv_ref, g_ref):
        dq_ref, dk_ref, dv_ref = dqkv_ref.at[:, 0:qk], dqkv_ref.at[:, qk:2 * qk], dqkv_ref.at[:, 2 * qk:2 * qk + vw]

        @pl.when(pl.program_id(0) == 0)
        def _():
            g_ref[...] = jnp.zeros_like(g_ref)

        cf, ss = cf_ref[...], ss_ref[...]
        for h in range(N_HEADS):
            qh = q_ref[:, h * DK:(h + 1) * DK].astype(F32)
            kh = k_ref[:, h * DK:(h + 1) * DK].astype(F32)
            qh = qh * cf + _swap_halves(qh) * ss
            kh = (kh * cf + _swap_halves(kh) * ss) * scale
            qb, kb = qh.astype(BF16), kh.astype(BF16)
            vh = v_ref[:, h * DV:(h + 1) * DV].astype(BF16)
            do = dr_ref[:, h * DV:(h + 1) * DV].astype(BF16)
            rb = st_ref[0, h].astype(BF16)
            g = g_ref[h]
            gb = g.astype(BF16)
            dec, xi, zt = d_ref[h], xi_ref[h], zt_ref[h]
            sd = (_dot_nt(qb, kb) * dec).astype(BF16)
            ds = (_dot_nt(do, vh) * dec).astype(BF16)
            dqh = _dot(ds, kb) + _dot_nt(do, rb) * xi
            dkh = (_dot_tn(ds, qb) + _dot_nt(vh, gb) * zt) * scale
            dvh = _dot_tn(sd, do) + _dot((kh * zt).astype(BF16), gb)
            g_ref[h] = g * cd_ref[h, 0:1, :] + _dot_tn((qh * xi).astype(BF16), do)
            dq_ref[:, h * DK:(h + 1) * DK] = (dqh * cf + _swap_halves(dqh * ss)).astype(dq_ref.dtype)
            dk_ref[:, h * DK:(h + 1) * DK] = (dkh * cf + _swap_halves(dkh * ss)).astype(dk_ref.dtype)
            dv_ref[:, h * DV:(h + 1) * DV] = dvh.astype(dv_ref.dtype)

    last = n_chunks - 1
    return pl.pallas_call(
        body, name="retention_bwd", grid=(n_chunks,),
        in_specs=[pl.BlockSpec((CHUNK, qk), lambda n: (last - n, 0)), pl.BlockSpec((CHUNK, qk), lambda n: (last - n, 1)),
                  pl.BlockSpec((CHUNK, vw), lambda n: (last - n, 0)), pl.BlockSpec((CHUNK, DK), lambda n: (last - n, 0)),
                  pl.BlockSpec((CHUNK, DK), lambda n: (last - n, 0)),
                  pl.BlockSpec((1, N_HEADS, DK, DV), lambda n: (last - n, 0, 0, 0)),
                  pl.BlockSpec((CHUNK, vw), lambda n: (last - n, 0))] + _const_specs() + [pl.BlockSpec(memory_space=pl.ANY)],
        out_specs=pl.BlockSpec((CHUNK, 2 * qk + vw), lambda n: (last - n, 0)),
        out_shape=jax.ShapeDtypeStruct(dproj.shape, dproj.dtype), input_output_aliases={11: 0},
        scratch_shapes=[pltpu.VMEM((N_HEADS, DK, DV), F32)], compiler_params=_params(1),
    )(proj_qk, proj_qk, proj_rest, cos_t, sin_t, states, dr, *consts, dproj)


def _ln(v):
    mu = jnp.mean(v, axis=-1, keepdims=True)
    var = jnp.mean(jnp.square(v - mu), axis=-1, keepdims=True)
    return (v - mu) * lax.rsqrt(var + LN_EPS)


def _f_modulate(x, scale, shift):
    return _ln(x) * (1.0 + scale) + shift


def _f_conv_norm(a1, g, b):
    return jax.nn.silu(_ln(a1) * g + b)


def _f_group_norm_gate(r, gate, g, b):
    return (_ln(r) * g + b) * jax.nn.silu(gate)


def _per_head(fn):
    def run(*arrays):
        parts = [fn(*[a[:, h * DV:(h + 1) * DV] for a in arrays]) for h in range(N_HEADS)]
        if not isinstance(parts[0], (tuple, list)):
            return (jnp.concatenate(parts, axis=1),)
        return tuple(jnp.concatenate([p[k] for p in parts], axis=1) for k in range(len(parts[0])))

    return run


def _f_merge(ga, gb, ya, yb):
    return jax.nn.sigmoid(ga) * ya + jax.nn.sigmoid(gb) * yb


def _f_post1(x, t, gate1, g1, b1, scale2, shift2):
    x1 = _ln(ALPHA * x + gate1 * t) * g1 + b1
    return x1, _ln(x1) * (1.0 + scale2) + shift2


def _f_loss(x1, f, gate2, g2, b2, target):
    y = _ln(ALPHA * x1 + gate2 * f) * g2 + b2
    return 0.5 * jnp.sum(jnp.mean(jnp.square(y - target), axis=-1))


ANY = pl.BlockSpec(memory_space=pl.ANY)


def _allgather8(name, blocks, own_half=False):
    n = len(blocks)
    rows = [b.shape[0] // 2 if own_half else b.shape[0] for b in blocks]

    def body(*refs):
        x_refs, out_refs = refs[:n], refs[n:2 * n]
        send_sems, recv_sems = refs[2 * n:]
        x, y, c = lax.axis_index("x"), lax.axis_index("y"), lax.axis_index("c")
        me, sibling = (x, y, c), (x, y, 1 - c)
        chips = [(1 - x, y), (x, 1 - y), (1 - x, 1 - y)]
        every = range(n)

        def src(a):
            return x_refs[a].at[pl.ds(c * rows[a], rows[a])] if own_half else x_refs[a]

        def slot(a, px, py, pc):
            return out_refs[a].at[4 * px + 2 * py + pc]

        def copy(k, a, block, to, from_input=False):
            return pltpu.make_async_remote_copy(
                src_ref=src(a) if from_input else slot(a, *block), dst_ref=slot(a, *block), send_sem=send_sems.at[k, a],
                recv_sem=recv_sems.at[k, a], device_id=to, device_id_type=MESH)

        first = [copy(0, a, me, sibling, True) for a in every]
        first += [copy(1 + j, a, me, (*chip, c), True) for j, chip in enumerate(chips) for a in every]
        for cp in first:
            cp.start()
        passed = [[copy(4 + j, a, (*chip, c), sibling) for a in every] for j, chip in enumerate(chips)]
        for j, chip in enumerate(chips):
            for a in every:
                copy(1 + j, a, (*chip, c), me).wait_recv()
            for cp in passed[j]:
                cp.start()
        for a in every:
            copy(0, a, sibling, me).wait_recv()
        for j, chip in enumerate(chips):
            for a in every:
                copy(4 + j, a, (*chip, 1 - c), me).wait_recv()
        for cp in first + [cp for group in passed for cp in group]:
            cp.wait_send()

    gathered = pl.pallas_call(
        body, name=name, in_specs=[ANY] * n, out_specs=[ANY] * n,
        out_shape=[jax.ShapeDtypeStruct((N_DEV, r, b.shape[1]), b.dtype) for r, b in zip(rows, blocks)],
        scratch_shapes=[pltpu.SemaphoreType.DMA((7, n)), pltpu.SemaphoreType.DMA((7, n))],
    )(*blocks)
    c = lax.axis_index("c")
    me = 4 * lax.axis_index("x") + 2 * lax.axis_index("y") + c
    own = [lax.dynamic_slice_in_dim(b, c * r, r, 0) if own_half else b for r, b in zip(rows, blocks)]
    return [lax.dynamic_update_slice_in_dim(g_, o_[None], me, 0) for g_, o_ in zip(gathered, own)]


def _sibling_swap(name, arrays):
    n = len(arrays)
    halves = [a.shape[1] // 2 for a in arrays]

    def body(*refs):
        g_refs, theirs = refs[:n], refs[n:2 * n]
        send_sems, recv_sems = refs[2 * n:]
        x, y, c = lax.axis_index("x"), lax.axis_index("y"), lax.axis_index("c")
        remote = [pltpu.make_async_remote_copy(
            src_ref=g_refs[a].at[:, pl.ds((1 - c) * halves[a], halves[a]), :], dst_ref=theirs[a], send_sem=send_sems.at[a],
            recv_sem=recv_sems.at[a], device_id=(x, y, 1 - c), device_id_type=MESH) for a in range(n)]
        for cp in remote:
            cp.start()
        for cp in remote:
            cp.wait_recv()
        for cp in remote:
            cp.wait_send()

    return pl.pallas_call(
        body, name=name, in_specs=[ANY] * n, out_specs=[ANY] * n,
        out_shape=[jax.ShapeDtypeStruct((a.shape[0], h, a.shape[2]), a.dtype) for a, h in zip(arrays, halves)],
        scratch_shapes=[pltpu.SemaphoreType.DMA((n,)), pltpu.SemaphoreType.DMA((n,))],
    )(*arrays)


HBM = pl.BlockSpec(memory_space=pltpu.HBM)
SEM = pl.BlockSpec(memory_space=pltpu.SEMAPHORE)
DATAFLOW = pltpu.SideEffectType.DATAFLOW_SIDE_EFFECTING


def _chip_peers(x, y):
    return [(1 - x, y), (x, 1 - y), (1 - x, 1 - y)]


def _other_devices(x, y, c):
    flip = lambda v, f: 1 - v if f else v
    return [(flip(x, k & 4), flip(y, k & 2), flip(c, k & 1)) for k in range(1, N_DEV)]


def _gather_start(name, arrays):
    n = len(arrays)
    rows = [a.shape[0] // 2 for a in arrays]
    zones = [lax.empty((N_DEV, r, a.shape[1]), a.dtype) for r, a in zip(rows, arrays)]

    def body(*refs):
        x_refs, zone_refs = refs[:n], refs[n:2 * n]
        send_sems, recv_sems = refs[2 * n], refs[2 * n + 1]
        token = refs[-1]
        x, y, c = lax.axis_index("x"), lax.axis_index("y"), lax.axis_index("c")
        for k, peer in enumerate(_other_devices(x, y, c)):
            for a in range(n):
                pltpu.make_async_remote_copy(
                    src_ref=x_refs[a].at[pl.ds(c * rows[a], rows[a])], dst_ref=zone_refs[a].at[4 * x + 2 * y + c],
                    send_sem=send_sems.at[k * n + a], recv_sem=recv_sems.at[k * n + a], device_id=peer,
                    device_id_type=MESH).start()
        token[...] = jnp.zeros_like(token)

    thru = [pltpu.HBM(a.shape, a.dtype) for a in arrays] + [pltpu.HBM(z.shape, z.dtype) for z in zones]
    n_sems = (N_DEV - 1) * n
    res = pl.pallas_call(
        body, name=name, in_specs=[HBM] * (2 * n),
        out_specs=(SEM, SEM, *[HBM] * (2 * n), pl.BlockSpec(memory_space=pltpu.VMEM)),
        out_shape=(pltpu.SemaphoreType.DMA((n_sems,)), pltpu.SemaphoreType.DMA((n_sems,)), *thru,
                   jax.ShapeDtypeStruct((8, 128), F32)),
        input_output_aliases={i: 2 + i for i in range(2 * n)},
        compiler_params=pltpu.CompilerParams(has_side_effects=DATAFLOW),
    )(*[pltpu.with_memory_space_constraint(a, pltpu.HBM) for a in arrays],
      *[pltpu.with_memory_space_constraint(z, pltpu.HBM) for z in zones])
    return (res[0], res[1], list(res[2:2 + n]), list(res[2 + n:2 + 2 * n])), res[-1]


def _gather_wait(name, handle, after):
    send_sems, recv_sems, sources, zones = handle
    n = len(sources)
    rows = [z.shape[1] for z in zones]

    def body(*refs):
        x_refs, zone_refs = refs[:n], refs[n:2 * n]
        send_sems, recv_sems = refs[2 * n], refs[2 * n + 1]
        x, y, c = lax.axis_index("x"), lax.axis_index("y"), lax.axis_index("c")
        for k, (px, py, pc) in enumerate(_other_devices(x, y, c)):
            for a in range(n):
                cp = pltpu.make_async_remote_copy(
                    src_ref=x_refs[a].at[pl.ds(c * rows[a], rows[a])], dst_ref=zone_refs[a].at[4 * px + 2 * py + pc],
                    send_sem=send_sems.at[k * n + a], recv_sem=recv_sems.at[k * n + a], device_id=(px, py, pc),
                    device_id_type=MESH)
                cp.wait_send()
                cp.wait_recv()

    res = pl.pallas_call(
        body, name=name, in_specs=[HBM] * (2 * n) + [SEM, SEM, ANY], out_specs=[HBM] * (2 * n),
        out_shape=[pltpu.HBM(a.shape, a.dtype) for a in sources + zones],
        input_output_aliases={i: i for i in range(2 * n)}, compiler_params=pltpu.CompilerParams(has_side_effects=DATAFLOW),
    )(*sources, *zones, send_sems, recv_sems, after)
    return list(res[n:])


def _alltoall_start(name, arrays):
    n = len(arrays)
    lands = [lax.empty((3,) + a.shape[1:], a.dtype) for a in arrays]

    def body(*refs):
        p_refs, land_refs = refs[:n], refs[n:2 * n]
        send_sems, recv_sems = refs[2 * n], refs[2 * n + 1]
        token = refs[-1]
        x, y, c = lax.axis_index("x"), lax.axis_index("y"), lax.axis_index("c")
        for k, (px, py) in enumerate(_chip_peers(x, y)):
            for a in range(n):
                pltpu.make_async_remote_copy(
                    src_ref=p_refs[a].at[2 * px + py], dst_ref=land_refs[a].at[k], send_sem=send_sems.at[k * n + a],
                    recv_sem=recv_sems.at[k * n + a], device_id=(px, py, c), device_id_type=MESH).start()
        token[...] = jnp.zeros_like(token)

    thru = [pltpu.HBM(a.shape, a.dtype) for a in arrays] + [pltpu.HBM(l.shape, l.dtype) for l in lands]
    res = pl.pallas_call(
        body, name=name, in_specs=[HBM] * (2 * n),
        out_specs=(SEM, SEM, *[HBM] * (2 * n), pl.BlockSpec(memory_space=pltpu.VMEM)),
        out_shape=(pltpu.SemaphoreType.DMA((3 * n,)), pltpu.SemaphoreType.DMA((3 * n,)), *thru, jax.ShapeDtypeStruct((8, 128), F32)),
        input_output_aliases={i: 2 + i for i in range(2 * n)},
        compiler_params=pltpu.CompilerParams(has_side_effects=DATAFLOW),
    )(*[pltpu.with_memory_space_constraint(a, pltpu.HBM) for a in arrays],
      *[pltpu.with_memory_space_constraint(l, pltpu.HBM) for l in lands])
    return (res[0], res[1], list(res[2:2 + n]), list(res[2 + n:2 + 2 * n])), res[-1]


def _alltoall_wait(name, handle, after):
    send_sems, recv_sems, sources, lands = handle
    n = len(sources)

    def body(*refs):
        p_refs, land_refs = refs[:n], refs[n:2 * n]
        send_sems, recv_sems = refs[2 * n], refs[2 * n + 1]
        x, y, c = lax.axis_index("x"), lax.axis_index("y"), lax.axis_index("c")
        for k, (px, py) in enumerate(_chip_peers(x, y)):
            for a in range(n):
                cp = pltpu.make_async_remote_copy(
                    src_ref=p_refs[a].at[2 * px + py], dst_ref=land_refs[a].at[k], send_sem=send_sems.at[k * n + a],
                    recv_sem=recv_sems.at[k * n + a], device_id=(px, py, c), device_id_type=MESH)
                cp.wait_send()
                cp.wait_recv()

    res = pl.pallas_call(
        body, name=name, in_specs=[HBM] * (2 * n) + [SEM, SEM, ANY], out_specs=[HBM] * (2 * n),
        out_shape=[pltpu.HBM(a.shape, a.dtype) for a in sources + lands],
        input_output_aliases={i: i for i in range(2 * n)}, compiler_params=pltpu.CompilerParams(has_side_effects=DATAFLOW),
    )(*sources, *lands, send_sems, recv_sems, after)
    return list(res[n:])


def _sibling_exchange(name, halves):
    n = len(halves)

    def body(*refs):
        h_refs, out_refs = refs[:n], refs[n:2 * n]
        send_sems, recv_sems = refs[2 * n:]
        x, y, c = lax.axis_index("x"), lax.axis_index("y"), lax.axis_index("c")
        remote = [pltpu.make_async_remote_copy(
            src_ref=h_refs[a], dst_ref=out_refs[a], send_sem=send_sems.at[a], recv_sem=recv_sems.at[a],
            device_id=(x, y, 1 - c), device_id_type=MESH) for a in range(n)]
        for cp in remote:
            cp.start()
        for cp in remote:
            cp.wait_recv()
        for cp in remote:
            cp.wait_send()

    return pl.pallas_call(
        body, name=name, in_specs=[ANY] * n, out_specs=[ANY] * n,
        out_shape=[jax.ShapeDtypeStruct(h.shape, h.dtype) for h in halves],
        scratch_shapes=[pltpu.SemaphoreType.DMA((n,)), pltpu.SemaphoreType.DMA((n,))],
    )(*halves)


def _sum_arrays(name, terms, n_rows):
    c_ = terms[0][0].shape[1]

    def add_all(*vals):
        acc = vals[0]
        for v in vals[1:]:
            acc = acc + v
        return (acc,)

    return _rowwise(name, add_all, [(a, c_, 0, first) for a, first in terms], [], [(c_, F32)], [], n_rows, tile=128)[0]


def _adamw_fn(w, g, m, v):
    m = ADAM_B1 * m + (1.0 - ADAM_B1) * g
    v = ADAM_B2 * v + (1.0 - ADAM_B2) * jnp.square(g)
    m_hat = m / (1.0 - ADAM_B1 ** ADAM_STEP)
    v_hat = v / (1.0 - ADAM_B2 ** ADAM_STEP)
    delta = -ADAM_LR * (m_hat / (jnp.sqrt(v_hat) + ADAM_EPS) + ADAM_WD * w)
    return delta, m, v


def _adamw(name, w, g, m, v):
    r, c_ = w.shape
    return _rowwise(name, _adamw_fn, [(w, c_, 0), (g, c_, 0), (m, c_, 0), (v, c_, 0)], [], [(c_, F32)] * 3, [], r, tile=128)


def _ada_fwd(c_all, w_ada, b_ada):
    n = w_ada.shape[1]

    def body(c_ref, w_ref, b_ref, o_ref):
        o_ref[...] = jnp.dot(jax.nn.silu(c_ref[...]).astype(BF16), w_ref[...].astype(BF16),
                             preferred_element_type=F32) + b_ref[...]

    return pl.pallas_call(body, name="ada_fwd", out_shape=jax.ShapeDtypeStruct((N_DEV, n), F32),
                          compiler_params=pltpu.CompilerParams(vmem_limit_bytes=VMEM_LIMIT))(c_all, w_ada, b_ada)


def _ada_bwd(c_all, dmod_cols):
    d = c_all.shape[1]
    n = dmod_cols.shape[1]

    def body(c_ref, dm_ref, gw_ref):
        gw_ref[...] = lax.dot_general(jax.nn.silu(c_ref[...]).astype(BF16), dm_ref[...].astype(BF16),
                                      (((0,), (0,)), ((), ())), preferred_element_type=F32)

    return pl.pallas_call(body, name="ada_bwd", out_shape=jax.ShapeDtypeStruct((d, n), F32),
                          compiler_params=pltpu.CompilerParams(vmem_limit_bytes=VMEM_LIMIT))(c_all, dmod_cols)


def _cast_bf16(name, a):
    r, c_ = a.shape
    return _rowwise(name, lambda v: (v,), [(a, c_, 0)], [], [(c_, BF16)], [], r)[0]


def _pad_rows(vec, mult):
    n = vec.shape[0]
    return jnp.pad(vec, (0, (-n) % mult))


def kernel(x, c, positions, w_ada, b_ada, w_in, b_in, conv_dw_w, conv_dw_b, conv_ln_g, conv_ln_b, w_conv_out, ret_gn_g, ret_gn_b, w_ret_out, w_out, ln1_g, ln1_b, w_up, ffn_dw_w, ffn_dw_b, w_down, ln2_g, ln2_b, loss_target, m_w_ada, m_b_ada, m_w_in, m_b_in, m_conv_dw_w, m_conv_dw_b, m_conv_ln_g, m_conv_ln_b, m_w_conv_out, m_ret_gn_g, m_ret_gn_b, m_w_ret_out, m_w_out, m_ln1_g, m_ln1_b, m_w_up, m_ffn_dw_w, m_ffn_dw_b, m_w_down, m_ln2_g, m_ln2_b, v_w_ada, v_b_ada, v_w_in, v_b_in, v_conv_dw_w, v_conv_dw_b, v_conv_ln_g, v_conv_ln_b, v_w_conv_out, v_ret_gn_g, v_ret_gn_b, v_w_ret_out, v_w_out, v_ln1_g, v_ln1_b, v_w_up, v_ffn_dw_w, v_ffn_dw_b, v_w_down, v_ln2_g, v_ln2_b):
    given = dict(locals())
    n_rows = x.shape[1]
    d = D_MODEL
    my_c = lax.axis_index("c")
    chip = 2 * lax.axis_index("x") + lax.axis_index("y")
    dev = 2 * chip + my_c
    xr = x[0]
    target = loss_target[0]
    vw = N_HEADS * DV
    ffw = 2 * D_FF

    def flat_rows(arrays, mult=8):
        v = jnp.concatenate([a.reshape(-1) for a in arrays])
        return _pad_rows(v, mult * d).reshape(-1, d)

    def unflatten(flat2d, shapes):
        v, out, o = flat2d.reshape(-1), [], 0
        for shp in shapes:
            size = 1
            for e in shp:
                size *= e
            out.append(v[o:o + size].reshape(shp))
            o += size
        return out

    w_bf = {n: _cast_bf16("cast_" + n, given[n][0]) for n in BIG_WEIGHTS}
    wg_in = _allgather8("gather_w_in", [w_bf["w_in"]], own_half=True)[0].reshape(N_CHIPS, d, w_in.shape[2])
    others = [n for n in BIG_WEIGHTS if n != "w_in"]
    probe = wg_in[0, 0:1, 0:1].astype(F32)
    zero = (jnp.where(jnp.isfinite(probe), probe, 0.0) * 0.0).astype(BF16)
    gather_handle, gather_token = _gather_start("gather_weights_start", [w_bf[n] + zero for n in others])

    kc, kf = conv_dw_w.shape[2], ffn_dw_w.shape[2]
    small_all = _allgather8("gather_small", [flat_rows([c, conv_dw_w, ffn_dw_w])])[0].reshape(N_DEV, -1)
    c_all = small_all[:, :d]
    per_chip = small_all[0::2]
    conv_w = per_chip[:, d:d + CONV_K * kc].reshape(N_CHIPS, CONV_K, kc).transpose(1, 0, 2).reshape(CONV_K, N_CHIPS * kc)
    o_f = d + CONV_K * kc
    ffn_w = per_chip[:, o_f:o_f + FFN_K * kf].reshape(N_CHIPS, FFN_K, kf).transpose(1, 0, 2).reshape(FFN_K, N_CHIPS * kf)

    n_ada = w_ada.shape[2]
    b_ada_cols = lax.dynamic_slice_in_dim(b_ada, chip * n_ada, n_ada, 1)
    mod_cols = _ada_fwd(c_all, w_ada[0], b_ada_cols)
    mod_all = _allgather8("gather_mod", [mod_cols])[0]
    mod = lax.dynamic_index_in_dim(mod_all[0::2], dev, 1, keepdims=False).reshape(1, N_CHIPS * n_ada)
    shift1, scale1, gate1, shift2, scale2, gate2 = [mod[:, k * d:(k + 1) * d] for k in range(6)]

    h1 = _rowwise("ln_mod1", lambda a, s, t: (_f_modulate(a, s, t),), [(xr, d, 0)],
                  [(scale1 + gather_token[0:1, 0:1], d, 0), (shift1, d, 0)], [(d, BF16)], [], n_rows)[0]
    n_proj = N_CHIPS * w_in.shape[2]
    proj_qk = _mm_nn("mm_in_qk", h1, wg_in, b_in, F32, tile=(2048, 512), window=(0, 2 * d), rows_outer=True)
    proj_rest = _mm_nn("mm_in_rest", h1, wg_in, b_in, BF16, tile=(2048, 512), window=(2 * d, n_proj - 2 * d), rows_outer=True)
    w_all = {}
    for n, zone in zip(others, _gather_wait("gather_weights_wait", gather_handle, proj_rest)):
        own = lax.dynamic_slice_in_dim(w_bf[n], my_c * zone.shape[1], zone.shape[1], 0)
        w_all[n] = lax.dynamic_update_slice_in_dim(zone, own[None], dev, 0)
    wg_up = w_all["w_up"].reshape(N_CHIPS, d, w_up.shape[2])
    wg_conv_out = w_all["w_conv_out"].reshape(1, d, d)
    wg_ret_out = w_all["w_ret_out"].reshape(1, vw, d)
    wg_out = w_all["w_out"].reshape(1, d, d)
    wg_down = w_all["w_down"].reshape(1, D_FF, d)
    a1 = _glu_conv_fwd(proj_rest, 4, 5, conv_w, conv_dw_b, n_rows)
    a2 = _rowwise("conv_norm", lambda a, g, b: (_f_conv_norm(a, g, b),), [(a1, d, 0)], [(conv_ln_g, d, 0), (conv_ln_b, d, 0)],
                  [(d, BF16)], [], n_rows)[0]
    y_a = _mm_nn("mm_conv_out", a2, wg_conv_out, None, F32)
    cos_t, sin_t = _rope_tables(positions[0])
    consts = _retention_consts()
    r, states = _retention_fwd(proj_qk, proj_rest, cos_t, sin_t, consts, n_rows)
    ret_rows = [(r, vw, 0), (proj_rest, vw, 1)]
    ret_vecs = [(ret_gn_g, vw, 0), (ret_gn_b, vw, 0)]
    r2 = _rowwise("ret_norm", _per_head(_f_group_norm_gate), ret_rows, ret_vecs, [(vw, BF16)], [], n_rows)[0]
    y_b = _mm_nn("mm_ret_out", r2, wg_ret_out, None, F32)
    merge_rows = [(proj_rest, d, 6), (proj_rest, d, 7), (y_a, d, 0), (y_b, d, 0)]
    m = _rowwise("merge", lambda *a: (_f_merge(*a),), merge_rows, [], [(d, BF16)], [], n_rows)[0]
    t = _mm_nn("mm_out", m, wg_out, None, F32)
    post1_vecs = [(gate1, d, 0), (ln1_g, d, 0), (ln1_b, d, 0), (scale2, d, 0), (shift2, d, 0)]
    x1, h2 = _rowwise("post1", _f_post1, [(xr, d, 0), (t, d, 0)], post1_vecs, [(d, F32), (d, BF16)], [], n_rows)
    pair_up = lambda c: (c % 2) * 2 + c // 2
    paired = lambda a: a.reshape(a.shape[0], N_CHIPS, kf)[:, jnp.array([0, 2, 1, 3])].reshape(a.shape[0], ffw)
    ffn_w_p, ffn_b_p = paired(ffn_w), paired(ffn_dw_b)
    u = _mm_nn("mm_up", h2, wg_up, None, BF16, gmap=pair_up)
    p, u_filtered = _ffn_fwd(u, ffn_w_p, ffn_b_p, n_rows)
    f = _mm_nn("mm_down", p, wg_down, None, F32)

    def loss_rows(x1v, fv, tv, g2v, lg, lb):
        loss, pull = jax.vjp(lambda a, b, c_, e, h: _f_loss(a, b, c_, e, h, tv), x1v, fv, g2v, lg, lb)
        return (*pull(jnp.ones((), F32)), jnp.full((1, 128), loss, F32))

    dx1_a, df, dgate2, dln2_g, dln2_b, loss_v = _rowwise(
        "loss", loss_rows, [(x1, d, 0), (f, d, 0), (target, d, 0)], [(gate2, d, 0), (ln2_g, d, 0), (ln2_b, d, 0)],
        [(d, F32), (d, BF16)], [d, d, d, 128], n_rows)
    loss = lax.psum(loss_v[0, 0], ("x", "y", "c"))

    dp = _mm_nt("mm_down_dx", df, wg_down, BF16)
    gw_down = _mm_tn("mm_down_dw", p, df, 1)
    du, g_ffn_w, g_ffn_b = _ffn_bwd(u, u_filtered, dp, ffn_w_p, n_rows)
    g_ffn_w, g_ffn_b = paired(g_ffn_w), paired(g_ffn_b)
    dh2 = _mm_nt("mm_up_dx", du, wg_up, BF16, gmap=pair_up)
    gw_up = _mm_tn("mm_up_dw", h2, du, N_CHIPS, gmap=pair_up)

    dx_a, dt, dgate1, dln1_g, dln1_b, dscale2, dshift2 = _rowwise(
        "post1_bwd", _vjp_rows(_f_post1, 2, 2), [(xr, d, 0), (t, d, 0), (dx1_a, d, 0), (dh2, d, 0)], post1_vecs,
        [(d, F32), (d, BF16)], [d] * 5, n_rows)
    dm = _mm_nt("mm_out_dx", dt, wg_out, BF16)
    gw_out = _mm_tn("mm_out_dw", m, dt, 1)
    merge_vjp = _vjp_rows(_f_merge, 4, 1)

    def merge_bwd(*a):
        dga, dgb, dya, dyb = merge_vjp(*a)
        return jnp.concatenate([dga, dgb], axis=1), dya, dyb

    dproj, dya, dyb = _rowwise("merge_bwd", merge_bwd, merge_rows + [(dm, d, 0)], [],
                               [(2 * d, BF16, n_proj, 8192 // (2 * d)), (d, BF16), (d, BF16)], [], n_rows)

    da2 = _mm_nt("mm_conv_out_dx", dya, wg_conv_out, BF16)
    gw_conv_out = _mm_tn("mm_conv_out_dw", a2, dya, 1)
    da1, dcl_g, dcl_b = _rowwise("conv_norm_bwd", _vjp_rows(_f_conv_norm, 1, 1), [(a1, d, 0), (da2, d, 0)],
                                 [(conv_ln_g, d, 0), (conv_ln_b, d, 0)], [(d, F32)], [d, d], n_rows)
    dproj, g_conv_w, g_conv_b = _glu_conv_bwd(proj_rest, 4, 5, da1, conv_w, dproj, 6144 // (2 * d), n_rows)

    dr2 = _mm_nt("mm_ret_out_dx", dyb, wg_ret_out, BF16)
    gw_ret_out = _mm_tn("mm_ret_out_dw", r2, dyb, 1)

    def reduce_start(tag, names, fulls):
        shapes = [given[n].shape[1:] for n in names]
        fulls = [g_.reshape(N_CHIPS, r, c_) for g_, (r, c_) in zip(fulls, shapes)]
        theirs = _sibling_swap("grad_swap_" + tag, fulls)
        pair = []
        for n, (r, c_), g_, t_ in zip(names, shapes, fulls, theirs):
            mine = lax.dynamic_slice_in_dim(g_, my_c * (r // 2), r // 2, 1)
            pair.append(_sum_arrays("grad_pair_sum_" + n, [(mine.reshape(-1, c_), 0), (t_.reshape(-1, c_), 0)],
                                    N_CHIPS * (r // 2)).reshape(N_CHIPS, r // 2, c_))
        handle, token = _alltoall_start("grad_alltoall_start_" + tag, pair)
        own = [lax.dynamic_index_in_dim(p_, chip, 0, keepdims=False) for p_ in pair]
        return (names, shapes, handle, own), token[0:1, 0:1]

    def reduce_finish(tag, started, after):
        names, shapes, handle, own = started
        lands = _alltoall_wait("grad_alltoall_wait_" + tag, handle, after)
        return [_sum_arrays("grad_chip_sum_" + n, [(o_, 0)] + [(l_.reshape(-1, c_), k * (r // 2)) for k in range(3)], r // 2)
                for n, (r, c_), o_, l_ in zip(names, shapes, own, lands)]

    late = ("w_up", "w_down", "w_out", "w_conv_out", "w_ret_out")
    started_late, token_late = reduce_start("late", late, [gw_up, gw_down, gw_out, gw_conv_out, gw_ret_out])
    ret_vecs_bwd = [(ret_gn_g + token_late, vw, 0), (ret_gn_b, vw, 0)]
    dr, dproj, dgn_g, dgn_b = _rowwise(
        "ret_norm_bwd", _per_head(_vjp_rows(_f_group_norm_gate, 2, 1)), ret_rows + [(dr2, vw, 0)], ret_vecs_bwd,
        [(vw, BF16), (vw, BF16, n_proj, 4096 // vw)], [vw, vw], n_rows, into=(dproj, 1))
    dproj = _retention_bwd(proj_qk, proj_rest, cos_t, sin_t, states, dr, consts, dproj, n_rows)

    gw_in, gb_in = _mm_tn("mm_in_dw", h1, dproj, N_CHIPS, colsum=True)
    started_in, token_in = reduce_start("in", ("w_in",), [gw_in])
    dh1 = _mm_nt("mm_in_dx", dproj, wg_in, BF16)
    mod_bwd = _vjp_rows(_f_modulate, 1, 1)

    def mod1_bwd(xv, dhv, dxav, sv, tv):
        dx, ds, dsh = mod_bwd(xv, dhv, sv, tv)
        return dx + dxav, ds, dsh

    grad_x, dscale1, dshift1 = _rowwise("ln_mod1_bwd", mod1_bwd, [(xr, d, 0), (dh1, d, 0), (dx_a, d, 0)],
                                        [(scale1 + token_in, d, 0), (shift1, d, 0)], [(d, F32)], [d, d], n_rows)

    dmod = jnp.concatenate([dshift1, dscale1, dgate1, dshift2, dscale2, dgate2], axis=1)
    small_names = ["b_in", "conv_dw_w", "conv_dw_b", "conv_ln_g", "conv_ln_b", "ret_gn_g", "ret_gn_b", "ln1_g", "ln1_b",
                   "ffn_dw_w", "ffn_dw_b", "ln2_g", "ln2_b"]
    small_parts = [gb_in, g_conv_w, g_conv_b, dcl_g, dcl_b, dgn_g, dgn_b, dln1_g, dln1_b, g_ffn_w, g_ffn_b, dln2_g, dln2_b, dmod]
    small_shapes = [a.shape for a in small_parts]
    parts_all = _allgather8("gather_small_grads", [flat_rows(small_parts)])[0]
    part_rows = parts_all.shape[1]
    summed = _sum_arrays("sum_small_grads", [(parts_all.reshape(N_DEV * part_rows, d), k * part_rows) for k in range(N_DEV)],
                         part_rows)
    small_sum = unflatten(summed, small_shapes)
    grads = dict(zip(small_names, small_sum[:-1]))
    grads["b_ada"] = small_sum[-1]
    grads["conv_dw_w"] = lax.dynamic_slice_in_dim(grads["conv_dw_w"], chip * kc, kc, 1)
    grads["ffn_dw_w"] = lax.dynamic_slice_in_dim(grads["ffn_dw_w"], chip * kf, kf, 1)
    o_mod = sum(a.size for a in small_parts[:-1])
    dmod_all = parts_all.reshape(N_DEV, -1)[:, o_mod:o_mod + dmod.shape[1]]
    grads["w_ada"] = _ada_bwd(c_all, lax.dynamic_slice_in_dim(dmod_all, chip * n_ada, n_ada, 1))

    reduced = reduce_finish("late", started_late, grad_x) + reduce_finish("in", started_in, grad_x)
    big_order = late + ("w_in",)
    g_shards = []
    for n, mine_, theirs_ in zip(big_order, reduced, _sibling_exchange("grad_exchange", reduced)):
        both = lax.dynamic_update_slice_in_dim(jnp.stack([theirs_, theirs_]), mine_[None], my_c, 0)
        g_shards.append(both.reshape(given[n].shape[1:]))

    outs = {}
    for n, g_ in zip(big_order, g_shards):
        upd = _adamw("adamw_" + n, given[n][0], g_, given["m_" + n][0], given["v_" + n][0])
        for prefix, val in zip(("grad_", "delta_", "new_m_", "new_v_"), (g_, *upd)):
            outs[prefix + n] = val.reshape(given[n].shape)
    ada = _adamw("adamw_ada", w_ada[0], grads["w_ada"], m_w_ada[0], v_w_ada[0])
    for prefix, val in zip(("grad_", "delta_", "new_m_", "new_v_"), (grads["w_ada"], *ada)):
        outs[prefix + "w_ada"] = val.reshape(w_ada.shape)
    small_all_names = ["b_ada"] + small_names
    small_w_shapes = [given[n].shape for n in small_all_names]
    g_small = flat_rows([grads[n] for n in small_all_names])
    small_upd = _adamw("adamw_small", flat_rows([given[n] for n in small_all_names]), g_small,
                       flat_rows([given["m_" + n] for n in small_all_names]), flat_rows([given["v_" + n] for n in small_all_names]))
    for prefix, packed in zip(("grad_", "delta_", "new_m_", "new_v_"), (g_small, *small_upd)):
        for n, val in zip(small_all_names, unflatten(packed, small_w_shapes)):
            outs[prefix + n] = val

    weights = ["w_ada", "b_ada", "w_in", "b_in", "conv_dw_w", "conv_dw_b", "conv_ln_g", "conv_ln_b", "w_conv_out", "ret_gn_g",
               "ret_gn_b", "w_ret_out", "w_out", "ln1_g", "ln1_b", "w_up", "ffn_dw_w", "ffn_dw_b", "w_down", "ln2_g", "ln2_b"]
    result = [loss, grad_x.reshape(x.shape)]
    for prefix in ("grad_", "delta_", "new_m_", "new_v_"):
        result += [outs[prefix + n] for n in weights]
    return tuple(result)
```

```python
import jax
import jax.numpy as jnp
from jax import lax
from jax.experimental import pallas as pl
from jax.experimental.pallas import tpu as pltpu

F32 = jnp.float32
BF16 = jnp.bfloat16
MESH = pl.DeviceIdType.MESH

D_MODEL = 1024
N_HEADS = 8
DK = 128
DV = 256
CHUNK = 128
ROPE_BASE = 10000.0
D_FF = 2816
CONV_K = 31
FFN_K = 3
LN_EPS = 1e-5
ALPHA = (2.0 * 1) ** 0.25
ADAM_LR = 0.001
ADAM_B1 = 0.9
ADAM_B2 = 0.999
ADAM_EPS = 1e-08
ADAM_WD = 0.01
ADAM_STEP = 10

V7X_VMEM_BYTES = 64 * 1024 * 1024
VMEM_LIMIT = V7X_VMEM_BYTES - 8 * 1024 * 1024
ROW_TILE = 256
MM_TILE = 512
N_CHIPS = 4
N_DEV = 8

BIG_WEIGHTS = ("w_in", "w_up", "w_conv_out", "w_ret_out", "w_out", "w_down")


def _params(n_grid):
    return pltpu.CompilerParams(dimension_semantics=("arbitrary",) * n_grid, vmem_limit_bytes=VMEM_LIMIT)


def _rowwise(name, fn, rows, vecs, outs, reds, n_rows, tile=ROW_TILE, ncol=1, with_col=False, into=None):
    tile = _fit_tile(n_rows, tile)
    n_in = len(rows) + len(vecs)
    n_ref_in = n_in + (into is not None)
    n_out = len(outs)
    outs = [o if len(o) == 4 else (o[0], o[1], o[0] * ncol, 0) for o in outs]

    def col_map(off, row, first_row=0):
        def index(j, i):
            return (i + first_row // tile if row else 0, off(j) if callable(off) else off + j)
        return index

    def body(*refs):
        i = pl.program_id(1)
        vals = [r[...].astype(F32) for r in refs[:n_in]]
        res = fn(pl.program_id(0), *vals) if with_col else fn(*vals)
        for k in range(n_out):
            refs[n_ref_in + k][...] = res[k].astype(refs[n_ref_in + k].dtype)
        for k in range(len(reds)):
            o = refs[n_ref_in + n_out + k]

            @pl.when(i == 0)
            def _():
                o[...] = jnp.zeros_like(o)

            o[...] += res[n_out + k]

    in_specs = [pl.BlockSpec((tile, e[1]), col_map(e[2], True, e[3] if len(e) > 3 else 0)) for e in rows]
    in_specs += [pl.BlockSpec((1, w), col_map(off, False)) for _, w, off in vecs]
    args = [e[0] for e in rows] + [a for a, _, _ in vecs]
    aliases = {}
    if into is not None:
        in_specs.append(pl.BlockSpec(memory_space=pl.ANY))
        args.append(into[0])
        aliases = {n_in: into[1]}
    out_specs = [pl.BlockSpec((tile, w), col_map(off, True)) for w, _, _, off in outs]
    out_specs += [pl.BlockSpec((1, w), lambda j, i: (0, j)) for w in reds]
    out_shape = [jax.ShapeDtypeStruct((n_rows, total), dt) for _, dt, total, _ in outs]
    out_shape += [jax.ShapeDtypeStruct((1, w * ncol), F32) for w in reds]
    return pl.pallas_call(
        body, name=name, grid=(ncol, n_rows // tile), in_specs=in_specs, out_specs=out_specs, out_shape=out_shape,
        input_output_aliases=aliases, compiler_params=_params(2),
    )(*args)


def _vjp_rows(fn, n_row_in, n_ct):
    def bwd(*args):
        prim = [a.astype(F32) for a in args[:n_row_in] + args[n_row_in + n_ct:]]
        cts = tuple(a.astype(F32) for a in args[n_row_in:n_row_in + n_ct])
        _, pull = jax.vjp(fn, *prim)
        return pull(cts if n_ct > 1 else cts[0])

    return bwd


def _fit_tile(n, pref):
    if n <= pref:
        return n
    t = pref - pref % 16
    while n % t:
        t -= 16
    return t


def _col_tile(n):
    return n if n <= 1536 else n // 2


def _same_group(c):
    return c


def _mm_nn(name, a, w, bias, out_dtype, gmap=_same_group, tile=None, window=None, rows_outer=False):
    s, k = a.shape
    g, _, n = w.shape
    tm, tn = (min(MM_TILE, s), _col_tile(n)) if tile is None else (min(tile[0], s), tile[1])
    nt = n // tn
    first, n_cols = (0, g * n) if window is None else window
    t0 = first // tn

    def body(*refs):
        a_ref, w_ref = refs[0], refs[1]
        o_ref = refs[-1]
        acc = jnp.dot(a_ref[...].astype(BF16), w_ref[...], preferred_element_type=F32)
        if bias is not None:
            acc = acc + refs[2][...]
        o_ref[...] = acc.astype(o_ref.dtype)

    order = (lambda i, c: (c, i)) if rows_outer else (lambda c, i: (c, i))
    a_map = lambda *g_: (order(*g_)[1], 0)
    w_map = lambda *g_: (gmap((t0 + order(*g_)[0]) // nt), 0, (t0 + order(*g_)[0]) % nt)
    in_specs = [pl.BlockSpec((tm, k), a_map), pl.BlockSpec((None, k, tn), w_map)]
    args = [a, w]
    if bias is not None:
        in_specs.append(pl.BlockSpec((1, tn), lambda *g_: (0, t0 + order(*g_)[0])))
        args.append(bias)
    grid = (s // tm, n_cols // tn) if rows_outer else (n_cols // tn, s // tm)
    return pl.pallas_call(
        body, name=name, grid=grid, in_specs=in_specs, out_specs=pl.BlockSpec((tm, tn), lambda *g_: order(*g_)[::-1]),
        out_shape=jax.ShapeDtypeStruct((s, n_cols), out_dtype), compiler_params=_params(2),
    )(*args)


def _mm_nt(name, dy, w, out_dtype, gmap=_same_group):
    s = dy.shape[0]
    g, k, n = w.shape
    tm, tn = min(2 * MM_TILE if k <= 1024 else MM_TILE, s), n
    nt = 1
    steps = g

    def body(dy_ref, w_ref, o_ref, acc_ref):
        r = pl.program_id(1)
        part = lax.dot_general(dy_ref[...].astype(BF16), w_ref[...], (((1,), (1,)), ((), ())), preferred_element_type=F32)

        @pl.when(r == 0)
        def _():
            acc_ref[...] = part

        @pl.when(r > 0)
        def _():
            acc_ref[...] += part

        @pl.when(r == steps - 1)
        def _():
            o_ref[...] = acc_ref[...].astype(o_ref.dtype)

    return pl.pallas_call(
        body, name=name, grid=(s // tm, steps),
        in_specs=[pl.BlockSpec((tm, tn), lambda i, r: (i, r)),
                  pl.BlockSpec((None, k, tn), lambda i, r: (gmap(r // nt), 0, r % nt))],
        out_specs=pl.BlockSpec((tm, k), lambda i, r: (i, 0)), out_shape=jax.ShapeDtypeStruct((s, k), out_dtype),
        scratch_shapes=[pltpu.VMEM((tm, k), F32)], compiler_params=_params(2),
    )(dy, w)


def _mm_tn(name, a, dy, g, gmap=_same_group, colsum=False):
    s, k = a.shape
    n = dy.shape[1] // g
    ts = min(2048 if k <= 1024 else 1024, s)
    tn = n if k * n <= 1024 * 1536 else (n // 2 if (n // 2) % 128 == 0 else n)
    nt = n // tn

    def body(a_ref, dy_ref, o_ref, *sum_ref):
        t = pl.program_id(1)
        dyv = dy_ref[...].astype(BF16)
        part = lax.dot_general(a_ref[...].astype(BF16), dyv, (((0,), (0,)), ((), ())), preferred_element_type=F32)

        @pl.when(t == 0)
        def _():
            o_ref[...] = part

        @pl.when(t > 0)
        def _():
            o_ref[...] += part

        if colsum:
            col = jnp.sum(dyv.astype(F32), axis=0, keepdims=True)

            @pl.when(t == 0)
            def _():
                sum_ref[0][...] = col

            @pl.when(t > 0)
            def _():
                sum_ref[0][...] += col

    out_specs = [pl.BlockSpec((None, k, tn), lambda c, t: (gmap(c // nt), 0, c % nt))]
    out_shape = [jax.ShapeDtypeStruct((g, k, n), F32)]
    if colsum:
        out_specs.append(pl.BlockSpec((1, tn), lambda c, t: (0, c)))
        out_shape.append(jax.ShapeDtypeStruct((1, g * n), F32))
    res = pl.pallas_call(
        body, name=name, grid=(g * nt, s // ts),
        in_specs=[pl.BlockSpec((ts, k), lambda c, t: (t, 0)), pl.BlockSpec((ts, tn), lambda c, t: (t, c))],
        out_specs=out_specs, out_shape=out_shape, compiler_params=_params(2),
    )(a, dy)
    return res if colsum else res[0]


SUBLANES = 8


def _tap_sum(read, w_row, offsets, tile):
    acc = None
    for b in range(SUBLANES):
        group = [(k, o) for k, o in offsets if o % SUBLANES == b]
        if not group:
            continue
        rows = tile if b == 0 else tile + SUBLANES
        z = None
        for k, o in group:
            term = w_row(k) * read(o - b, rows)
            z = term if z is None else z + term
        part = z if b == 0 else z[b:b + tile]
        acc = part if acc is None else acc + part
    return acc


def _tap_grads(read, dy, offsets, tile):
    padded = jnp.concatenate([dy, jnp.zeros((SUBLANES, dy.shape[1]), dy.dtype)], axis=0)
    out = {}
    for b in range(SUBLANES):
        group = [(k, o) for k, o in offsets if o % SUBLANES == b]
        if not group:
            continue
        shifted = dy if b == 0 else pltpu.roll(padded, b, 0)
        rows = tile if b == 0 else tile + SUBLANES
        for k, o in group:
            out[k] = jnp.sum(shifted * read(o - b, rows), axis=0, keepdims=True)
    return out


CONV_HALO = 32
CONV_TILE = 256
CONV_LANES = 256


def _glu_conv_fwd(proj, col_v, col_g, w, b, n_rows):
    kw, n_ch = w.shape
    tile = min(CONV_TILE, n_rows)
    per = tile // CONV_HALO
    offsets = [(k, CONV_HALO - (kw - 1) + k) for k in range(kw)]

    def body(v_ref, g_ref, vh_ref, gh_ref, w_ref, b_ref, o_ref, buf):
        i = pl.program_id(0)
        prev = vh_ref[...].astype(F32) * jax.nn.sigmoid(gh_ref[...].astype(F32))
        buf[0:CONV_HALO, :] = jnp.where(i == 0, 0.0, prev)
        buf[CONV_HALO:CONV_HALO + tile, :] = v_ref[...].astype(F32) * jax.nn.sigmoid(g_ref[...].astype(F32))
        for c0 in range(0, n_ch, CONV_LANES):
            cols = slice(c0, c0 + CONV_LANES)
            o_ref[:, cols] = b_ref[:, cols] + _tap_sum(lambda s, n: buf[pl.ds(s, n), cols], lambda k: w_ref[k:k + 1, cols],
                                                      offsets, tile)

    main = lambda col: pl.BlockSpec((tile, n_ch), lambda i: (i, col))
    halo = lambda col: pl.BlockSpec((CONV_HALO, n_ch), lambda i: (jnp.maximum(i * per - 1, 0), col))
    return pl.pallas_call(
        body, name="conv_fwd", grid=(n_rows // tile,),
        in_specs=[main(col_v), main(col_g), halo(col_v), halo(col_g), pl.BlockSpec((kw, n_ch), lambda i: (0, 0)),
                  pl.BlockSpec((1, n_ch), lambda i: (0, 0))],
        out_specs=pl.BlockSpec((tile, n_ch), lambda i: (i, 0)), out_shape=jax.ShapeDtypeStruct((n_rows, n_ch), F32),
        scratch_shapes=[pltpu.VMEM((CONV_HALO + tile, n_ch), F32)], compiler_params=_params(1),
    )(proj, proj, proj, proj, w, b)


def _glu_conv_bwd(proj, col_v, col_g, dy, w, dproj, col_out, n_rows):
    kw, n_ch = w.shape
    tile = min(CONV_TILE, n_rows)
    per = tile // CONV_HALO
    n_tiles = n_rows // tile
    last_halo = n_rows // CONV_HALO - 1
    offsets = [(k, CONV_HALO - (kw - 1) + k) for k in range(kw)]
    back = [(k, kw - 1 - k) for k in range(kw)]

    def body(v_ref, g_ref, vh_ref, gh_ref, dy_ref, dyn_ref, w_ref, dp_any, dx_ref, dw_ref, db_ref, buf, dbuf):
        i = pl.program_id(0)
        cv, cg = v_ref[...].astype(F32), g_ref[...].astype(F32)
        sig = jax.nn.sigmoid(cg)
        prev = vh_ref[...].astype(F32) * jax.nn.sigmoid(gh_ref[...].astype(F32))
        buf[0:CONV_HALO, :] = jnp.where(i == 0, 0.0, prev)
        buf[CONV_HALO:CONV_HALO + tile, :] = cv * sig
        dbuf[0:tile, :] = dy_ref[...]
        dbuf[tile:tile + CONV_HALO, :] = jnp.where(i == n_tiles - 1, 0.0, dyn_ref[...])

        @pl.when(i == 0)
        def _():
            dw_ref[...] = jnp.zeros_like(dw_ref)
            db_ref[...] = jnp.zeros_like(db_ref)

        db_ref[...] += jnp.sum(dy_ref[...], axis=0, keepdims=True)
        for c0 in range(0, n_ch, CONV_LANES):
            cols = slice(c0, c0 + CONV_LANES)
            w_row = lambda k: w_ref[k:k + 1, cols]
            dx = _tap_sum(lambda s, n: dbuf[pl.ds(s, n), cols], w_row, back, tile)
            grads = _tap_grads(lambda s, n: buf[pl.ds(s, n), cols], dy_ref[:, cols], offsets, tile)
            for k in range(kw):
                dw_ref[k:k + 1, cols] += grads[k]
            sg = sig[:, cols]
            dx_ref[:, c0:c0 + CONV_LANES] = (dx * sg).astype(dx_ref.dtype)
            dx_ref[:, n_ch + c0:n_ch + c0 + CONV_LANES] = (dx * cv[:, cols] * sg * (1.0 - sg)).astype(dx_ref.dtype)

    main = lambda col: pl.BlockSpec((tile, n_ch), lambda i: (i, col))
    halo = lambda col: pl.BlockSpec((CONV_HALO, n_ch), lambda i: (jnp.maximum(i * per - 1, 0), col))
    return pl.pallas_call(
        body, name="conv_bwd", grid=(n_tiles,),
        in_specs=[main(col_v), main(col_g), halo(col_v), halo(col_g), main(0),
                  pl.BlockSpec((CONV_HALO, n_ch), lambda i: (jnp.minimum((i + 1) * per, last_halo), 0)),
                  pl.BlockSpec((kw, n_ch), lambda i: (0, 0)), pl.BlockSpec(memory_space=pl.ANY)],
        out_specs=[pl.BlockSpec((tile, 2 * n_ch), lambda i: (i, col_out)), pl.BlockSpec((kw, n_ch), lambda i: (0, 0)),
                   pl.BlockSpec((1, n_ch), lambda i: (0, 0))],
        out_shape=[jax.ShapeDtypeStruct(dproj.shape, dproj.dtype), jax.ShapeDtypeStruct((kw, n_ch), F32),
                   jax.ShapeDtypeStruct((1, n_ch), F32)],
        input_output_aliases={7: 0},
        scratch_shapes=[pltpu.VMEM((CONV_HALO + tile, n_ch), F32), pltpu.VMEM((tile + CONV_HALO, n_ch), F32)],
        compiler_params=_params(1),
    )(proj, proj, proj, proj, dy, dy, w, dproj)


FFN_HALO = 8


def _ffn_fwd(u, w, b, n_rows):
    kw = w.shape[0]
    half = u.shape[1] // 4
    tile = min(ROW_TILE, n_rows)
    per = tile // FFN_HALO
    offsets = [(k, FFN_HALO - (kw - 1) + k) for k in range(kw)]

    def body(u_ref, uh_ref, w_ref, b_ref, p_ref, f_ref, buf):
        i = pl.program_id(1)
        buf[0:FFN_HALO, :] = jnp.where(i == 0, 0.0, uh_ref[...].astype(F32))
        buf[FFN_HALO:FFN_HALO + tile, :] = u_ref[...].astype(F32)
        conv = []
        for c0 in (0, half):
            cols = slice(c0, c0 + half)
            conv.append(b_ref[:, cols] + _tap_sum(lambda s, n: buf[pl.ds(s, n), cols], lambda k: w_ref[k:k + 1, cols],
                                                  offsets, tile))
            f_ref[:, cols] = conv[-1].astype(f_ref.dtype)
        p_ref[...] = (conv[0] * jax.nn.silu(conv[1])).astype(p_ref.dtype)

    return pl.pallas_call(
        body, name="ffn_fwd", grid=(2, n_rows // tile),
        in_specs=[pl.BlockSpec((tile, 2 * half), lambda j, i: (i, j)),
                  pl.BlockSpec((FFN_HALO, 2 * half), lambda j, i: (jnp.maximum(i * per - 1, 0), j)),
                  pl.BlockSpec((kw, 2 * half), lambda j, i: (0, j)), pl.BlockSpec((1, 2 * half), lambda j, i: (0, j))],
        out_specs=[pl.BlockSpec((tile, half), lambda j, i: (i, j)), pl.BlockSpec((tile, 2 * half), lambda j, i: (i, j))],
        out_shape=[jax.ShapeDtypeStruct((n_rows, 2 * half), BF16), jax.ShapeDtypeStruct(u.shape, BF16)],
        scratch_shapes=[pltpu.VMEM((FFN_HALO + tile, 2 * half), F32)], compiler_params=_params(2),
    )(u, u, w, b)


def _ffn_bwd(u, filtered, dp, w, n_rows):
    kw = w.shape[0]
    half = u.shape[1] // 4
    tile = min(ROW_TILE, n_rows)
    per = tile // FFN_HALO
    n_tiles = n_rows // tile
    last_halo = n_rows // FFN_HALO - 1
    ext = tile + FFN_HALO
    offsets = [(k, FFN_HALO - (kw - 1) + k) for k in range(kw)]
    back = [(k, kw - 1 - k) for k in range(kw)]

    def body(u_ref, up_ref, f_ref, fn_ref, dp_ref, dpn_ref, w_ref, du_ref, dw_ref, db_ref, buf, dbuf):
        i = pl.program_id(1)
        buf[0:FFN_HALO, :] = jnp.where(i == 0, 0.0, up_ref[...].astype(F32))
        buf[FFN_HALO:FFN_HALO + tile, :] = u_ref[...].astype(F32)
        filt = jnp.concatenate([f_ref[...], fn_ref[...]], axis=0).astype(F32)
        val, gate = filt[:, 0:half], filt[:, half:2 * half]
        dpe = jnp.concatenate([dp_ref[...].astype(F32), jnp.where(i == n_tiles - 1, 0.0, dpn_ref[...].astype(F32))], axis=0)
        sig = jax.nn.sigmoid(gate)
        dbuf[:, 0:half] = dpe * gate * sig
        dbuf[:, half:2 * half] = dpe * val * sig * (1.0 + gate * (1.0 - sig))

        @pl.when(i == 0)
        def _():
            dw_ref[...] = jnp.zeros_like(dw_ref)
            db_ref[...] = jnp.zeros_like(db_ref)

        for c0 in (0, half):
            cols = slice(c0, c0 + half)
            du_ref[:, cols] = _tap_sum(lambda s, n: dbuf[pl.ds(s, n), cols], lambda k: w_ref[k:k + 1, cols], back,
                                       tile).astype(du_ref.dtype)
            d_main = dbuf[0:tile, cols]
            db_ref[:, cols] += jnp.sum(d_main, axis=0, keepdims=True)
            grads = _tap_grads(lambda s, n: buf[pl.ds(s, n), cols], d_main, offsets, tile)
            for k in range(kw):
                dw_ref[k:k + 1, cols] += grads[k]

    wide = 2 * half
    return pl.pallas_call(
        body, name="ffn_bwd", grid=(2, n_tiles),
        in_specs=[pl.BlockSpec((tile, wide), lambda j, i: (i, j)),
                  pl.BlockSpec((FFN_HALO, wide), lambda j, i: (jnp.maximum(i * per - 1, 0), j)),
                  pl.BlockSpec((tile, wide), lambda j, i: (i, j)),
                  pl.BlockSpec((FFN_HALO, wide), lambda j, i: (jnp.minimum((i + 1) * per, last_halo), j)),
                  pl.BlockSpec((tile, half), lambda j, i: (i, j)),
                  pl.BlockSpec((FFN_HALO, half), lambda j, i: (jnp.minimum((i + 1) * per, last_halo), j)),
                  pl.BlockSpec((kw, wide), lambda j, i: (0, j))],
        out_specs=[pl.BlockSpec((tile, wide), lambda j, i: (i, j)), pl.BlockSpec((kw, wide), lambda j, i: (0, j)),
                   pl.BlockSpec((1, wide), lambda j, i: (0, j))],
        out_shape=[jax.ShapeDtypeStruct(u.shape, BF16), jax.ShapeDtypeStruct((kw, u.shape[1]), F32),
                   jax.ShapeDtypeStruct((1, u.shape[1]), F32)],
        scratch_shapes=[pltpu.VMEM((FFN_HALO + tile, wide), F32), pltpu.VMEM((ext, wide), F32)], compiler_params=_params(2),
    )(u, u, filtered, filtered, dp, dp, w)


def _retention_consts():
    log_gamma = jnp.log(1.0 - 2.0 ** (-5.0 - jnp.arange(N_HEADS, dtype=F32)))
    idx = jnp.arange(CHUNK, dtype=F32)
    rel = idx[:, None] - idx[None, :]
    decay = jnp.where(rel[None] >= 0, jnp.exp(log_gamma[:, None, None] * jnp.maximum(rel, 0.0)[None]), 0.0)
    zeta = jnp.exp(log_gamma[:, None] * (CHUNK - 1.0 - idx)[None])
    xi = jnp.exp(log_gamma[:, None] * (idx + 1.0)[None])
    chunk_decay = jnp.exp(log_gamma * CHUNK)
    xi_b = jnp.broadcast_to(xi[:, :, None], (N_HEADS, CHUNK, DK))
    zeta_b = jnp.broadcast_to(zeta[:, :, None], (N_HEADS, CHUNK, DK))
    cd_b = jnp.broadcast_to(chunk_decay[:, None, None], (N_HEADS, 8, DV))
    return decay, xi_b, zeta_b, cd_b


def _rope_tables(positions):
    half = DK // 2
    inv_freq = ROPE_BASE ** (-jnp.arange(half, dtype=F32) / half)
    ang = positions.astype(F32)[:, None] * inv_freq
    cos, sin = jnp.cos(ang), jnp.sin(ang)
    return jnp.concatenate([cos, cos], axis=-1), jnp.concatenate([-sin, sin], axis=-1)


def _swap_halves(v):
    return pltpu.roll(v, DK // 2, 1)


def _dot(a, b):
    return jnp.dot(a, b, preferred_element_type=F32)


def _dot_nt(a, b):
    return lax.dot_general(a, b, (((1,), (1,)), ((), ())), preferred_element_type=F32)


def _dot_tn(a, b):
    return lax.dot_general(a, b, (((0,), (0,)), ((), ())), preferred_element_type=F32)


def _const_specs():
    return [pl.BlockSpec((N_HEADS, CHUNK, CHUNK), lambda n: (0, 0, 0)), pl.BlockSpec((N_HEADS, CHUNK, DK), lambda n: (0, 0, 0)),
            pl.BlockSpec((N_HEADS, CHUNK, DK), lambda n: (0, 0, 0)), pl.BlockSpec((N_HEADS, 8, DV), lambda n: (0, 0, 0))]


def _retention_fwd(proj_qk, proj_rest, cos_t, sin_t, consts, n_rows):
    n_chunks = n_rows // CHUNK
    scale = DK ** -0.5

    def body(q_ref, k_ref, v_ref, cf_ref, ss_ref, d_ref, xi_ref, zt_ref, cd_ref, r_ref, st_ref, state):
        @pl.when(pl.program_id(0) == 0)
        def _():
            state[...] = jnp.zeros_like(state)

        cf, ss = cf_ref[...], ss_ref[...]
        for h in range(N_HEADS):
            qh = q_ref[:, h * DK:(h + 1) * DK].astype(F32)
            kh = k_ref[:, h * DK:(h + 1) * DK].astype(F32)
            qh = qh * cf + _swap_halves(qh) * ss
            kh = (kh * cf + _swap_halves(kh) * ss) * scale
            vh = v_ref[:, h * DV:(h + 1) * DV].astype(BF16)
            st = state[h]
            st_ref[0, h] = st
            sd = _dot_nt(qh.astype(BF16), kh.astype(BF16)) * d_ref[h]
            inner = _dot(sd.astype(BF16), vh)
            cross = _dot((qh * xi_ref[h]).astype(BF16), st.astype(BF16))
            kv = _dot_tn((kh * zt_ref[h]).astype(BF16), vh)
            state[h] = st * cd_ref[h, 0:1, :] + kv
            r_ref[:, h * DV:(h + 1) * DV] = inner + cross

    qk = N_HEADS * DK
    vw = N_HEADS * DV
    return pl.pallas_call(
        body, name="retention_fwd", grid=(n_chunks,),
        in_specs=[pl.BlockSpec((CHUNK, qk), lambda n: (n, 0)), pl.BlockSpec((CHUNK, qk), lambda n: (n, 1)),
                  pl.BlockSpec((CHUNK, vw), lambda n: (n, 0)), pl.BlockSpec((CHUNK, DK), lambda n: (n, 0)),
                  pl.BlockSpec((CHUNK, DK), lambda n: (n, 0))] + _const_specs(),
        out_specs=[pl.BlockSpec((CHUNK, vw), lambda n: (n, 0)), pl.BlockSpec((1, N_HEADS, DK, DV), lambda n: (n, 0, 0, 0))],
        out_shape=[jax.ShapeDtypeStruct((n_rows, vw), F32), jax.ShapeDtypeStruct((n_chunks, N_HEADS, DK, DV), F32)],
        scratch_shapes=[pltpu.VMEM((N_HEADS, DK, DV), F32)], compiler_params=_params(1),
    )(proj_qk, proj_qk, proj_rest, cos_t, sin_t, *consts)


def _retention_bwd(proj_qk, proj_rest, cos_t, sin_t, states, dr, consts, dproj, n_rows):
    n_chunks = n_rows // CHUNK
    scale = DK ** -0.5
    qk = N_HEADS * DK
    vw = N_HEADS * DV

    def body(q_ref, k_ref, v_ref, cf_ref, ss_ref, st_ref, dr_ref, d_ref, xi_ref, zt_ref, cd_ref, dp_any, dqkv_ref, g_ref):
        dq_ref, dk_ref, dv_ref = dqkv_ref.at[:, 0:qk], dqkv_ref.at[:, qk:2 * qk], dqkv_ref.at[:, 2 * qk:2 * qk + vw]

        @pl.when(pl.program_id(0) == 0)
        def _():
            g_ref[...] = jnp.zeros_like(g_ref)

        cf, ss = cf_ref[...], ss_ref[...]
        for h in range(N_HEADS):
            qh = q_ref[:, h * DK:(h + 1) * DK].astype(F32)
            kh = k_ref[:, h * DK:(h + 1) * DK].astype(F32)
            qh = qh * cf + _swap_halves(qh) * ss
            kh = (kh * cf + _swap_halves(kh) * ss) * scale
            qb, kb = qh.astype(BF16), kh.astype(BF16)
            vh = v_ref[:, h * DV:(h + 1) * DV].astype(BF16)
            do = dr_ref[:, h * DV:(h + 1) * DV].astype(BF16)
            rb = st_ref[0, h].astype(BF16)
            g = g_ref[h]
            gb = g.astype(BF16)
            dec, xi, zt = d_ref[h], xi_ref[h], zt_ref[h]
            sd = (_dot_nt(qb, kb) * dec).astype(BF16)
            ds = (_dot_nt(do, vh) * dec).astype(BF16)
            dqh = _dot(ds, kb) + _dot_nt(do, rb) * xi
            dkh = (_dot_tn(ds, qb) + _dot_nt(vh, gb) * zt) * scale
            dvh = _dot_tn(sd, do) + _dot((kh * zt).astype(BF16), gb)
            g_ref[h] = g * cd_ref[h, 0:1, :] + _dot_tn((qh * xi).astype(BF16), do)
            dq_ref[:, h * DK:(h + 1) * DK] = (dqh * cf + _swap_halves(dqh * ss)).astype(dq_ref.dtype)
            dk_ref[:, h * DK:(h + 1) * DK] = (dkh * cf + _swap_halves(dkh * ss)).astype(dk_ref.dtype)
            dv_ref[:, h * DV:(h + 1) * DV] = dvh.astype(dv_ref.dtype)

    last = n_chunks - 1
    return pl.pallas_call(
        body, name="retention_bwd", grid=(n_chunks,),
        in_specs=[pl.BlockSpec((CHUNK, qk), lambda n: (last - n, 0)), pl.BlockSpec((CHUNK, qk), lambda n: (last - n, 1)),
                  pl.BlockSpec((CHUNK, vw), lambda n: (last - n, 0)), pl.BlockSpec((CHUNK, DK), lambda n: (last - n, 0)),
                  pl.BlockSpec((CHUNK, DK), lambda n: (last - n, 0)),
                  pl.BlockSpec((1, N_HEADS, DK, DV), lambda n: (last - n, 0, 0, 0)),
                  pl.BlockSpec((CHUNK, vw), lambda n: (last - n, 0))] + _const_specs() + [pl.BlockSpec(memory_space=pl.ANY)],
        out_specs=pl.BlockSpec((CHUNK, 2 * qk + vw), lambda n: (last - n, 0)),
        out_shape=jax.ShapeDtypeStruct(dproj.shape, dproj.dtype), input_output_aliases={11: 0},
        scratch_shapes=[pltpu.VMEM((N_HEADS, DK, DV), F32)], compiler_params=_params(1),
    )(proj_qk, proj_qk, proj_rest, cos_t, sin_t, states, dr, *consts, dproj)


def _ln(v):
    mu = jnp.mean(v, axis=-1, keepdims=True)
    var = jnp.mean(jnp.square(v - mu), axis=-1, keepdims=True)
    return (v - mu) * lax.rsqrt(var + LN_EPS)


def _f_modulate(x, scale, shift):
    return _ln(x) * (1.0 + scale) + shift


def _f_conv_norm(a1, g, b):
    return jax.nn.silu(_ln(a1) * g + b)


def _f_group_norm_gate(r, gate, g, b):
    return (_ln(r) * g + b) * jax.nn.silu(gate)


def _per_head(fn):
    def run(*arrays):
        parts = [fn(*[a[:, h * DV:(h + 1) * DV] for a in arrays]) for h in range(N_HEADS)]
        if not isinstance(parts[0], (tuple, list)):
            return (jnp.concatenate(parts, axis=1),)
        return tuple(jnp.concatenate([p[k] for p in parts], axis=1) for k in range(len(parts[0])))

    return run


def _f_merge(ga, gb, ya, yb):
    return jax.nn.sigmoid(ga) * ya + jax.nn.sigmoid(gb) * yb


def _f_post1(x, t, gate1, g1, b1, scale2, shift2):
    x1 = _ln(ALPHA * x + gate1 * t) * g1 + b1
    return x1, _ln(x1) * (1.0 + scale2) + shift2


def _f_loss(x1, f, gate2, g2, b2, target):
    y = _ln(ALPHA * x1 + gate2 * f) * g2 + b2
    return 0.5 * jnp.sum(jnp.mean(jnp.square(y - target), axis=-1))


ANY = pl.BlockSpec(memory_space=pl.ANY)


def _allgather8(name, blocks, own_half=False):
    n = len(blocks)
    rows = [b.shape[0] // 2 if own_half else b.shape[0] for b in blocks]

    def body(*refs):
        x_refs, out_refs = refs[:n], refs[n:2 * n]
        send_sems, recv_sems = refs[2 * n:]
        x, y, c = lax.axis_index("x"), lax.axis_index("y"), lax.axis_index("c")
        me, sibling = (x, y, c), (x, y, 1 - c)
        chips = [(1 - x, y), (x, 1 - y), (1 - x, 1 - y)]
        every = range(n)

        def src(a):
            return x_refs[a].at[pl.ds(c * rows[a], rows[a])] if own_half else x_refs[a]

        def slot(a, px, py, pc):
            return out_refs[a].at[4 * px + 2 * py + pc]

        def copy(k, a, block, to, from_input=False):
            return pltpu.make_async_remote_copy(
                src_ref=src(a) if from_input else slot(a, *block), dst_ref=slot(a, *block), send_sem=send_sems.at[k, a],
                recv_sem=recv_sems.at[k, a], device_id=to, device_id_type=MESH)

        first = [copy(0, a, me, sibling, True) for a in every]
        first += [copy(1 + j, a, me, (*chip, c), True) for j, chip in enumerate(chips) for a in every]
        for cp in first:
            cp.start()
        passed = [[copy(4 + j, a, (*chip, c), sibling) for a in every] for j, chip in enumerate(chips)]
        for j, chip in enumerate(chips):
            for a in every:
                copy(1 + j, a, (*chip, c), me).wait_recv()
            for cp in passed[j]:
                cp.start()
        for a in every:
            copy(0, a, sibling, me).wait_recv()
        for j, chip in enumerate(chips):
            for a in every:
                copy(4 + j, a, (*chip, 1 - c), me).wait_recv()
        for cp in first + [cp for group in passed for cp in group]:
            cp.wait_send()

    gathered = pl.pallas_call(
        body, name=name, in_specs=[ANY] * n, out_specs=[ANY] * n,
        out_shape=[jax.ShapeDtypeStruct((N_DEV, r, b.shape[1]), b.dtype) for r, b in zip(rows, blocks)],
        scratch_shapes=[pltpu.SemaphoreType.DMA((7, n)), pltpu.SemaphoreType.DMA((7, n))],
    )(*blocks)
    c = lax.axis_index("c")
    me = 4 * lax.axis_index("x") + 2 * lax.axis_index("y") + c
    own = [lax.dynamic_slice_in_dim(b, c * r, r, 0) if own_half else b for r, b in zip(rows, blocks)]
    return [lax.dynamic_update_slice_in_dim(g_, o_[None], me, 0) for g_, o_ in zip(gathered, own)]


def _sibling_swap(name, arrays):
    n = len(arrays)
    halves = [a.shape[1] // 2 for a in arrays]

    def body(*refs):
        g_refs, theirs = refs[:n], refs[n:2 * n]
        send_sems, recv_sems = refs[2 * n:]
        x, y, c = lax.axis_index("x"), lax.axis_index("y"), lax.axis_index("c")
        remote = [pltpu.make_async_remote_copy(
            src_ref=g_refs[a].at[:, pl.ds((1 - c) * halves[a], halves[a]), :], dst_ref=theirs[a], send_sem=send_sems.at[a],
            recv_sem=recv_sems.at[a], device_id=(x, y, 1 - c), device_id_type=MESH) for a in range(n)]
        for cp in remote:
            cp.start()
        for cp in remote:
            cp.wait_recv()
        for cp in remote:
            cp.wait_send()

    return pl.pallas_call(
        body, name=name, in_specs=[ANY] * n, out_specs=[ANY] * n,
        out_shape=[jax.ShapeDtypeStruct((a.shape[0], h, a.shape[2]), a.dtype) for a, h in zip(arrays, halves)],
        scratch_shapes=[pltpu.SemaphoreType.DMA((n,)), pltpu.SemaphoreType.DMA((n,))],
    )(*arrays)


HBM = pl.BlockSpec(memory_space=pltpu.HBM)
SEM = pl.BlockSpec(memory_space=pltpu.SEMAPHORE)
DATAFLOW = pltpu.SideEffectType.DATAFLOW_SIDE_EFFECTING


def _chip_peers(x, y):
    return [(1 - x, y), (x, 1 - y), (1 - x, 1 - y)]


def _other_devices(x, y, c):
    flip = lambda v, f: 1 - v if f else v
    return [(flip(x, k & 4), flip(y, k & 2), flip(c, k & 1)) for k in range(1, N_DEV)]


def _gather_start(name, arrays):
    n = len(arrays)
    rows = [a.shape[0] // 2 for a in arrays]
    zones = [lax.empty((N_DEV, r, a.shape[1]), a.dtype) for r, a in zip(rows, arrays)]

    def body(*refs):
        x_refs, zone_refs = refs[:n], refs[n:2 * n]
        send_sems, recv_sems = refs[2 * n], refs[2 * n + 1]
        token = refs[-1]
        x, y, c = lax.axis_index("x"), lax.axis_index("y"), lax.axis_index("c")
        for k, peer in enumerate(_other_devices(x, y, c)):
            for a in range(n):
                pltpu.make_async_remote_copy(
                    src_ref=x_refs[a].at[pl.ds(c * rows[a], rows[a])], dst_ref=zone_refs[a].at[4 * x + 2 * y + c],
                    send_sem=send_sems.at[k * n + a], recv_sem=recv_sems.at[k * n + a], device_id=peer,
                    device_id_type=MESH).start()
        token[...] = jnp.zeros_like(token)

    thru = [pltpu.HBM(a.shape, a.dtype) for a in arrays] + [pltpu.HBM(z.shape, z.dtype) for z in zones]
    n_sems = (N_DEV - 1) * n
    res = pl.pallas_call(
        body, name=name, in_specs=[HBM] * (2 * n),
        out_specs=(SEM, SEM, *[HBM] * (2 * n), pl.BlockSpec(memory_space=pltpu.VMEM)),
        out_shape=(pltpu.SemaphoreType.DMA((n_sems,)), pltpu.SemaphoreType.DMA((n_sems,)), *thru,
                   jax.ShapeDtypeStruct((8, 128), F32)),
        input_output_aliases={i: 2 + i for i in range(2 * n)},
        compiler_params=pltpu.CompilerParams(has_side_effects=DATAFLOW),
    )(*[pltpu.with_memory_space_constraint(a, pltpu.HBM) for a in arrays],
      *[pltpu.with_memory_space_constraint(z, pltpu.HBM) for z in zones])
    return (res[0], res[1], list(res[2:2 + n]), list(res[2 + n:2 + 2 * n])), res[-1]


def _gather_wait(name, handle, after):
    send_sems, recv_sems, sources, zones = handle
    n = len(sources)
    rows = [z.shape[1] for z in zones]

    def body(*refs):
        x_refs, zone_refs = refs[:n], refs[n:2 * n]
        send_sems, recv_sems = refs[2 * n], refs[2 * n + 1]
        x, y, c = lax.axis_index("x"), lax.axis_index("y"), lax.axis_index("c")
        for k, (px, py, pc) in enumerate(_other_devices(x, y, c)):
            for a in range(n):
                cp = pltpu.make_async_remote_copy(
                    src_ref=x_refs[a].at[pl.ds(c * rows[a], rows[a])], dst_ref=zone_refs[a].at[4 * px + 2 * py + pc],
                    send_sem=send_sems.at[k * n + a], recv_sem=recv_sems.at[k * n + a], device_id=(px, py, pc),
                    device_id_type=MESH)
                cp.wait_send()
                cp.wait_recv()

    res = pl.pallas_call(
        body, name=name, in_specs=[HBM] * (2 * n) + [SEM, SEM, ANY], out_specs=[HBM] * (2 * n),
        out_shape=[pltpu.HBM(a.shape, a.dtype) for a in sources + zones],
        input_output_aliases={i: i for i in range(2 * n)}, compiler_params=pltpu.CompilerParams(has_side_effects=DATAFLOW),
    )(*sources, *zones, send_sems, recv_sems, after)
    return list(res[n:])


def _alltoall_start(name, arrays):
    n = len(arrays)
    lands = [lax.empty((3,) + a.shape[1:], a.dtype) for a in arrays]

    def body(*refs):
        p_refs, land_refs = refs[:n], refs[n:2 * n]
        send_sems, recv_sems = refs[2 * n], refs[2 * n + 1]
        token = refs[-1]
        x, y, c = lax.axis_index("x"), lax.axis_index("y"), lax.axis_index("c")
        for k, (px, py) in enumerate(_chip_peers(x, y)):
            for a in range(n):
                pltpu.make_async_remote_copy(
                    src_ref=p_refs[a].at[2 * px + py], dst_ref=land_refs[a].at[k], send_sem=send_sems.at[k * n + a],
                    recv_sem=recv_sems.at[k * n + a], device_id=(px, py, c), device_id_type=MESH).start()
        token[...] = jnp.zeros_like(token)

    thru = [pltpu.HBM(a.shape, a.dtype) for a in arrays] + [pltpu.HBM(l.shape, l.dtype) for l in lands]
    res = pl.pallas_call(
        body, name=name, in_specs=[HBM] * (2 * n),
        out_specs=(SEM, SEM, *[HBM] * (2 * n), pl.BlockSpec(memory_space=pltpu.VMEM)),
        out_shape=(pltpu.SemaphoreType.DMA((3 * n,)), pltpu.SemaphoreType.DMA((3 * n,)), *thru, jax.ShapeDtypeStruct((8, 128), F32)),
        input_output_aliases={i: 2 + i for i in range(2 * n)},
        compiler_params=pltpu.CompilerParams(has_side_effects=DATAFLOW),
    )(*[pltpu.with_memory_space_constraint(a, pltpu.HBM) for a in arrays],
      *[pltpu.with_memory_space_constraint(l, pltpu.HBM) for l in lands])
    return (res[0], res[1], list(res[2:2 + n]), list(res[2 + n:2 + 2 * n])), res[-1]


def _alltoall_wait(name, handle, after):
    send_sems, recv_sems, sources, lands = handle
    n = len(sources)

    def body(*refs):
        p_refs, land_refs = refs[:n], refs[n:2 * n]
        send_sems, recv_sems = refs[2 * n], refs[2 * n + 1]
        x, y, c = lax.axis_index("x"), lax.axis_index("y"), lax.axis_index("c")
        for k, (px, py) in enumerate(_chip_peers(x, y)):
            for a in range(n):
                cp = pltpu.make_async_remote_copy(
                    src_ref=p_refs[a].at[2 * px + py], dst_ref=land_refs[a].at[k], send_sem=send_sems.at[k * n + a],
                    recv_sem=recv_sems.at[k * n + a], device_id=(px, py, c), device_id_type=MESH)
                cp.wait_send()
                cp.wait_recv()

    res = pl.pallas_call(
        body, name=name, in_specs=[HBM] * (2 * n) + [SEM, SEM, ANY], out_specs=[HBM] * (2 * n),
        out_shape=[pltpu.HBM(a.shape, a.dtype) for a in sources + lands],
        input_output_aliases={i: i for i in range(2 * n)}, compiler_params=pltpu.CompilerParams(has_side_effects=DATAFLOW),
    )(*sources, *lands, send_sems, recv_sems, after)
    return list(res[n:])


def _sibling_exchange(name, halves):
    n = len(halves)

    def body(*refs):
        h_refs, out_refs = refs[:n], refs[n:2 * n]
        send_sems, recv_sems = refs[2 * n:]
        x, y, c = lax.axis_index("x"), lax.axis_index("y"), lax.axis_index("c")
        remote = [pltpu.make_async_remote_copy(
            src_ref=h_refs[a], dst_ref=out_refs[a], send_sem=send_sems.at[a], recv_sem=recv_sems.at[a],
            device_id=(x, y, 1 - c), device_id_type=MESH) for a in range(n)]
        for cp in remote:
            cp.start()
        for cp in remote:
            cp.wait_recv()
        for cp in remote:
            cp.wait_send()

    return pl.pallas_call(
        body, name=name, in_specs=[ANY] * n, out_specs=[ANY] * n,
        out_shape=[jax.ShapeDtypeStruct(h.shape, h.dtype) for h in halves],
        scratch_shapes=[pltpu.SemaphoreType.DMA((n,)), pltpu.SemaphoreType.DMA((n,))],
    )(*halves)


def _sum_arrays(name, terms, n_rows):
    c_ = terms[0][0].shape[1]

    def add_all(*vals):
        acc = vals[0]
        for v in vals[1:]:
            acc = acc + v
        return (acc,)

    return _rowwise(name, add_all, [(a, c_, 0, first) for a, first in terms], [], [(c_, F32)], [], n_rows, tile=128)[0]


def _adamw_fn(w, g, m, v):
    m = ADAM_B1 * m + (1.0 - ADAM_B1) * g
    v = ADAM_B2 * v + (1.0 - ADAM_B2) * jnp.square(g)
    m_hat = m / (1.0 - ADAM_B1 ** ADAM_STEP)
    v_hat = v / (1.0 - ADAM_B2 ** ADAM_STEP)
    delta = -ADAM_LR * (m_hat / (jnp.sqrt(v_hat) + ADAM_EPS) + ADAM_WD * w)
    return delta, m, v


def _adamw(name, w, g, m, v):
    r, c_ = w.shape
    return _rowwise(name, _adamw_fn, [(w, c_, 0), (g, c_, 0), (m, c_, 0), (v, c_, 0)], [], [(c_, F32)] * 3, [], r, tile=128)


def _ada_fwd(c_all, w_ada, b_ada):
    n = w_ada.shape[1]

    def body(c_ref, w_ref, b_ref, o_ref):
        o_ref[...] = jnp.dot(jax.nn.silu(c_ref[...]).astype(BF16), w_ref[...].astype(BF16),
                             preferred_element_type=F32) + b_ref[...]

    return pl.pallas_call(body, name="ada_fwd", out_shape=jax.ShapeDtypeStruct((N_DEV, n), F32),
                          compiler_params=pltpu.CompilerParams(vmem_limit_bytes=VMEM_LIMIT))(c_all, w_ada, b_ada)


def _ada_bwd(c_all, dmod_cols):
    d = c_all.shape[1]
    n = dmod_cols.shape[1]

    def body(c_ref, dm_ref, gw_ref):
        gw_ref[...] = lax.dot_general(jax.nn.silu(c_ref[...]).astype(BF16), dm_ref[...].astype(BF16),
                                      (((0,), (0,)), ((), ())), preferred_element_type=F32)

    return pl.pallas_call(body, name="ada_bwd", out_shape=jax.ShapeDtypeStruct((d, n), F32),
                          compiler_params=pltpu.CompilerParams(vmem_limit_bytes=VMEM_LIMIT))(c_all, dmod_cols)


def _cast_bf16(name, a):
    r, c_ = a.shape
    return _rowwise(name, lambda v: (v,), [(a, c_, 0)], [], [(c_, BF16)], [], r)[0]


def _pad_rows(vec, mult):
    n = vec.shape[0]
    return jnp.pad(vec, (0, (-n) % mult))


def kernel(x, c, positions, w_ada, b_ada, w_in, b_in, conv_dw_w, conv_dw_b, conv_ln_g, conv_ln_b, w_conv_out, ret_gn_g, ret_gn_b, w_ret_out, w_out, ln1_g, ln1_b, w_up, ffn_dw_w, ffn_dw_b, w_down, ln2_g, ln2_b, loss_target, m_w_ada, m_b_ada, m_w_in, m_b_in, m_conv_dw_w, m_conv_dw_b, m_conv_ln_g, m_conv_ln_b, m_w_conv_out, m_ret_gn_g, m_ret_gn_b, m_w_ret_out, m_w_out, m_ln1_g, m_ln1_b, m_w_up, m_ffn_dw_w, m_ffn_dw_b, m_w_down, m_ln2_g, m_ln2_b, v_w_ada, v_b_ada, v_w_in, v_b_in, v_conv_dw_w, v_conv_dw_b, v_conv_ln_g, v_conv_ln_b, v_w_conv_out, v_ret_gn_g, v_ret_gn_b, v_w_ret_out, v_w_out, v_ln1_g, v_ln1_b, v_w_up, v_ffn_dw_w, v_ffn_dw_b, v_w_down, v_ln2_g, v_ln2_b):
    given = dict(locals())
    n_rows = x.shape[1]
    d = D_MODEL
    my_c = lax.axis_index("c")
    chip = 2 * lax.axis_index("x") + lax.axis_index("y")
    dev = 2 * chip + my_c
    xr = x[0]
    target = loss_target[0]
    vw = N_HEADS * DV
    ffw = 2 * D_FF

    def flat_rows(arrays, mult=8):
        v = jnp.concatenate([a.reshape(-1) for a in arrays])
        return _pad_rows(v, mult * d).reshape(-1, d)

    def unflatten(flat2d, shapes):
        v, out, o = flat2d.reshape(-1), [], 0
        for shp in shapes:
            size = 1
            for e in shp:
                size *= e
            out.append(v[o:o + size].reshape(shp))
            o += size
        return out

    w_bf = {n: _cast_bf16("cast_" + n, given[n][0]) for n in BIG_WEIGHTS}
    wg_in = _allgather8("gather_w_in", [w_bf["w_in"]], own_half=True)[0].reshape(N_CHIPS, d, w_in.shape[2])
    others = [n for n in BIG_WEIGHTS if n != "w_in"]
    probe = wg_in[0, 0:1, 0:1].astype(F32)
    zero = (jnp.where(jnp.isfinite(probe), probe, 0.0) * 0.0).astype(BF16)
    gather_handle, gather_token = _gather_start("gather_weights_start", [w_bf[n] + zero for n in others])

    kc, kf = conv_dw_w.shape[2], ffn_dw_w.shape[2]
    small_all = _allgather8("gather_small", [flat_rows([c, conv_dw_w, ffn_dw_w])])[0].reshape(N_DEV, -1)
    c_all = small_all[:, :d]
    per_chip = small_all[0::2]
    conv_w = per_chip[:, d:d + CONV_K * kc].reshape(N_CHIPS, CONV_K, kc).transpose(1, 0, 2).reshape(CONV_K, N_CHIPS * kc)
    o_f = d + CONV_K * kc
    ffn_w = per_chip[:, o_f:o_f + FFN_K * kf].reshape(N_CHIPS, FFN_K, kf).transpose(1, 0, 2).reshape(FFN_K, N_CHIPS * kf)

    n_ada = w_ada.shape[2]
    b_ada_cols = lax.dynamic_slice_in_dim(b_ada, chip * n_ada, n_ada, 1)
    mod_cols = _ada_fwd(c_all, w_ada[0], b_ada_cols)
    mod_all = _allgather8("gather_mod", [mod_cols])[0]
    mod = lax.dynamic_index_in_dim(mod_all[0::2], dev, 1, keepdims=False).reshape(1, N_CHIPS * n_ada)
    shift1, scale1, gate1, shift2, scale2, gate2 = [mod[:, k * d:(k + 1) * d] for k in range(6)]

    h1 = _rowwise("ln_mod1", lambda a, s, t: (_f_modulate(a, s, t),), [(xr, d, 0)],
                  [(scale1 + gather_token[0:1, 0:1], d, 0), (shift1, d, 0)], [(d, BF16)], [], n_rows)[0]
    n_proj = N_CHIPS * w_in.shape[2]
    proj_qk = _mm_nn("mm_in_qk", h1, wg_in, b_in, F32, tile=(2048, 512), window=(0, 2 * d), rows_outer=True)
    proj_rest = _mm_nn("mm_in_rest", h1, wg_in, b_in, BF16, tile=(2048, 512), window=(2 * d, n_proj - 2 * d), rows_outer=True)
    w_all = {}
    for n, zone in zip(others, _gather_wait("gather_weights_wait", gather_handle, proj_rest)):
        own = lax.dynamic_slice_in_dim(w_bf[n], my_c * zone.shape[1], zone.shape[1], 0)
        w_all[n] = lax.dynamic_update_slice_in_dim(zone, own[None], dev, 0)
    wg_up = w_all["w_up"].reshape(N_CHIPS, d, w_up.shape[2])
    wg_conv_out = w_all["w_conv_out"].reshape(1, d, d)
    wg_ret_out = w_all["w_ret_out"].reshape(1, vw, d)
    wg_out = w_all["w_out"].reshape(1, d, d)
    wg_down = w_all["w_down"].reshape(1, D_FF, d)
    a1 = _glu_conv_fwd(proj_rest, 4, 5, conv_w, conv_dw_b, n_rows)
    a2 = _rowwise("conv_norm", lambda a, g, b: (_f_conv_norm(a, g, b),), [(a1, d, 0)], [(conv_ln_g, d, 0), (conv_ln_b, d, 0)],
                  [(d, BF16)], [], n_rows)[0]
    y_a = _mm_nn("mm_conv_out", a2, wg_conv_out, None, F32)
    cos_t, sin_t = _rope_tables(positions[0])
    consts = _retention_consts()
    r, states = _retention_fwd(proj_qk, proj_rest, cos_t, sin_t, consts, n_rows)
    ret_rows = [(r, vw, 0), (proj_rest, vw, 1)]
    ret_vecs = [(ret_gn_g, vw, 0), (ret_gn_b, vw, 0)]
    r2 = _rowwise("ret_norm", _per_head(_f_group_norm_gate), ret_rows, ret_vecs, [(vw, BF16)], [], n_rows)[0]
    y_b = _mm_nn("mm_ret_out", r2, wg_ret_out, None, F32)
    merge_rows = [(proj_rest, d, 6), (proj_rest, d, 7), (y_a, d, 0), (y_b, d, 0)]
    m = _rowwise("merge", lambda *a: (_f_merge(*a),), merge_rows, [], [(d, BF16)], [], n_rows)[0]
    t = _mm_nn("mm_out", m, wg_out, None, F32)
    post1_vecs = [(gate1, d, 0), (ln1_g, d, 0), (ln1_b, d, 0), (scale2, d, 0), (shift2, d, 0)]
    x1, h2 = _rowwise("post1", _f_post1, [(xr, d, 0), (t, d, 0)], post1_vecs, [(d, F32), (d, BF16)], [], n_rows)
    pair_up = lambda c: (c % 2) * 2 + c // 2
    paired = lambda a: a.reshape(a.shape[0], N_CHIPS, kf)[:, jnp.array([0, 2, 1, 3])].reshape(a.shape[0], ffw)
    ffn_w_p, ffn_b_p = paired(ffn_w), paired(ffn_dw_b)
    u = _mm_nn("mm_up", h2, wg_up, None, BF16, gmap=pair_up)
    p, u_filtered = _ffn_fwd(u, ffn_w_p, ffn_b_p, n_rows)
    f = _mm_nn("mm_down", p, wg_down, None, F32)

    def loss_rows(x1v, fv, tv, g2v, lg, lb):
        loss, pull = jax.vjp(lambda a, b, c_, e, h: _f_loss(a, b, c_, e, h, tv), x1v, fv, g2v, lg, lb)
        return (*pull(jnp.ones((), F32)), jnp.full((1, 128), loss, F32))

    dx1_a, df, dgate2, dln2_g, dln2_b, loss_v = _rowwise(
        "loss", loss_rows, [(x1, d, 0), (f, d, 0), (target, d, 0)], [(gate2, d, 0), (ln2_g, d, 0), (ln2_b, d, 0)],
        [(d, F32), (d, BF16)], [d, d, d, 128], n_rows)
    loss = lax.psum(loss_v[0, 0], ("x", "y", "c"))

    dp = _mm_nt("mm_down_dx", df, wg_down, BF16)
    gw_down = _mm_tn("mm_down_dw", p, df, 1)
    du, g_ffn_w, g_ffn_b = _ffn_bwd(u, u_filtered, dp, ffn_w_p, n_rows)
    g_ffn_w, g_ffn_b = paired(g_ffn_w), paired(g_ffn_b)
    dh2 = _mm_nt("mm_up_dx", du, wg_up, BF16, gmap=pair_up)
    gw_up = _mm_tn("mm_up_dw", h2, du, N_CHIPS, gmap=pair_up)

    dx_a, dt, dgate1, dln1_g, dln1_b, dscale2, dshift2 = _rowwise(
        "post1_bwd", _vjp_rows(_f_post1, 2, 2), [(xr, d, 0), (t, d, 0), (dx1_a, d, 0), (dh2, d, 0)], post1_vecs,
        [(d, F32), (d, BF16)], [d] * 5, n_rows)
    dm = _mm_nt("mm_out_dx", dt, wg_out, BF16)
    gw_out = _mm_tn("mm_out_dw", m, dt, 1)
    merge_vjp = _vjp_rows(_f_merge, 4, 1)

    def merge_bwd(*a):
        dga, dgb, dya, dyb = merge_vjp(*a)
        return jnp.concatenate([dga, dgb], axis=1), dya, dyb

    dproj, dya, dyb = _rowwise("merge_bwd", merge_bwd, merge_rows + [(dm, d, 0)], [],
                               [(2 * d, BF16, n_proj, 8192 // (2 * d)), (d, BF16), (d, BF16)], [], n_rows)

    da2 = _mm_nt("mm_conv_out_dx", dya, wg_conv_out, BF16)
    gw_conv_out = _mm_tn("mm_conv_out_dw", a2, dya, 1)
    da1, dcl_g, dcl_b = _rowwise("conv_norm_bwd", _vjp_rows(_f_conv_norm, 1, 1), [(a1, d, 0), (da2, d, 0)],
                                 [(conv_ln_g, d, 0), (conv_ln_b, d, 0)], [(d, F32)], [d, d], n_rows)
    dproj, g_conv_w, g_conv_b = _glu_conv_bwd(proj_rest, 4, 5, da1, conv_w, dproj, 6144 // (2 * d), n_rows)

    dr2 = _mm_nt("mm_ret_out_dx", dyb, wg_ret_out, BF16)
    gw_ret_out = _mm_tn("mm_ret_out_dw", r2, dyb, 1)

    def reduce_start(tag, names, fulls):
        shapes = [given[n].shape[1:] for n in names]
        fulls = [g_.reshape(N_CHIPS, r, c_) for g_, (r, c_) in zip(fulls, shapes)]
        theirs = _sibling_swap("grad_swap_" + tag, fulls)
        pair = []
        for n, (r, c_), g_, t_ in zip(names, shapes, fulls, theirs):
            mine = lax.dynamic_slice_in_dim(g_, my_c * (r // 2), r // 2, 1)
            pair.append(_sum_arrays("grad_pair_sum_" + n, [(mine.reshape(-1, c_), 0), (t_.reshape(-1, c_), 0)],
                                    N_CHIPS * (r // 2)).reshape(N_CHIPS, r // 2, c_))
        handle, token = _alltoall_start("grad_alltoall_start_" + tag, pair)
        own = [lax.dynamic_index_in_dim(p_, chip, 0, keepdims=False) for p_ in pair]
        return (names, shapes, handle, own), token[0:1, 0:1]

    def reduce_finish(tag, started, after):
        names, shapes, handle, own = started
        lands = _alltoall_wait("grad_alltoall_wait_" + tag, handle, after)
        return [_sum_arrays("grad_chip_sum_" + n, [(o_, 0)] + [(l_.reshape(-1, c_), k * (r // 2)) for k in range(3)], r // 2)
                for n, (r, c_), o_, l_ in zip(names, shapes, own, lands)]

    late = ("w_up", "w_down", "w_out", "w_conv_out", "w_ret_out")
    started_late, token_late = reduce_start("late", late, [gw_up, gw_down, gw_out, gw_conv_out, gw_ret_out])
    ret_vecs_bwd = [(ret_gn_g + token_late, vw, 0), (ret_gn_b, vw, 0)]
    dr, dproj, dgn_g, dgn_b = _rowwise(
        "ret_norm_bwd", _per_head(_vjp_rows(_f_group_norm_gate, 2, 1)), ret_rows + [(dr2, vw, 0)], ret_vecs_bwd,
        [(vw, BF16), (vw, BF16, n_proj, 4096 // vw)], [vw, vw], n_rows, into=(dproj, 1))
    dproj = _retention_bwd(proj_qk, proj_rest, cos_t, sin_t, states, dr, consts, dproj, n_rows)

    gw_in, gb_in = _mm_tn("mm_in_dw", h1, dproj, N_CHIPS, colsum=True)
    started_in, token_in = reduce_start("in", ("w_in",), [gw_in])
    dh1 = _mm_nt("mm_in_dx", dproj, wg_in + token_in.astype(BF16), BF16)
    mod_bwd = _vjp_rows(_f_modulate, 1, 1)

    def mod1_bwd(xv, dhv, dxav, sv, tv):
        dx, ds, dsh = mod_bwd(xv, dhv, sv, tv)
        return dx + dxav, ds, dsh

    grad_x, dscale1, dshift1 = _rowwise("ln_mod1_bwd", mod1_bwd, [(xr, d, 0), (dh1, d, 0), (dx_a, d, 0)],
                                        [(scale1 + token_in, d, 0), (shift1, d, 0)], [(d, F32)], [d, d], n_rows)

    dmod = jnp.concatenate([dshift1, dscale1, dgate1, dshift2, dscale2, dgate2], axis=1)
    small_names = ["b_in", "conv_dw_w", "conv_dw_b", "conv_ln_g", "conv_ln_b", "ret_gn_g", "ret_gn_b", "ln1_g", "ln1_b",
                   "ffn_dw_w", "ffn_dw_b", "ln2_g", "ln2_b"]
    small_parts = [gb_in, g_conv_w, g_conv_b, dcl_g, dcl_b, dgn_g, dgn_b, dln1_g, dln1_b, g_ffn_w, g_ffn_b, dln2_g, dln2_b, dmod]
    small_shapes = [a.shape for a in small_parts]
    parts_all = _allgather8("gather_small_grads", [flat_rows(small_parts)])[0]
    part_rows = parts_all.shape[1]
    summed = _sum_arrays("sum_small_grads", [(parts_all.reshape(N_DEV * part_rows, d), k * part_rows) for k in range(N_DEV)],
                         part_rows)
    small_sum = unflatten(summed, small_shapes)
    grads = dict(zip(small_names, small_sum[:-1]))
    grads["b_ada"] = small_sum[-1]
    grads["conv_dw_w"] = lax.dynamic_slice_in_dim(grads["conv_dw_w"], chip * kc, kc, 1)
    grads["ffn_dw_w"] = lax.dynamic_slice_in_dim(grads["ffn_dw_w"], chip * kf, kf, 1)
    o_mod = sum(a.size for a in small_parts[:-1])
    dmod_all = parts_all.reshape(N_DEV, -1)[:, o_mod:o_mod + dmod.shape[1]]
    grads["w_ada"] = _ada_bwd(c_all, lax.dynamic_slice_in_dim(dmod_all, chip * n_ada, n_ada, 1))

    reduced = reduce_finish("late", started_late, grad_x) + reduce_finish("in", started_in, grad_x)
    big_order = late + ("w_in",)
    g_shards = []
    for n, mine_, theirs_ in zip(big_order, reduced, _sibling_exchange("grad_exchange", reduced)):
        both = lax.dynamic_update_slice_in_dim(jnp.stack([theirs_, theirs_]), mine_[None], my_c, 0)
        g_shards.append(both.reshape(given[n].shape[1:]))

    outs = {}
    for n, g_ in zip(big_order, g_shards):
        upd = _adamw("adamw_" + n, given[n][0], g_, given["m_" + n][0], given["v_" + n][0])
        for prefix, val in zip(("grad_", "delta_", "new_m_", "new_v_"), (g_, *upd)):
            outs[prefix + n] = val.reshape(given[n].shape)
    ada = _adamw("adamw_ada", w_ada[0], grads["w_ada"], m_w_ada[0], v_w_ada[0])
    for prefix, val in zip(("grad_", "delta_", "new_m_", "new_v_"), (grads["w_ada"], *ada)):
        outs[prefix + "w_ada"] = val.reshape(w_ada.shape)
    small_all_names = ["b_ada"] + small_names
    small_w_shapes = [given[n].shape for n in small_all_names]
    g_small = flat_rows([grads[n] for n in small_all_names])
    small_upd = _adamw("adamw_small", flat_rows([given[n] for n in small_all_names]), g_small,
                       flat_rows([given["m_" + n] for n in small_all_names]), flat_rows([given["v_" + n] for n in small_all_names]))
    for prefix, packed in zip(("grad_", "delta_", "new_m_", "new_v_"), (g_small, *small_upd)):
        for n, val in zip(small_all_names, unflatten(packed, small_w_shapes)):
            outs[prefix + n] = val

    weights = ["w_ada", "b_ada", "w_in", "b_in", "conv_dw_w", "conv_dw_b", "conv_ln_g", "conv_ln_b", "w_conv_out", "ret_gn_g",
               "ret_gn_b", "w_ret_out", "w_out", "ln1_g", "ln1_b", "w_up", "ffn_dw_w", "ffn_dw_b", "w_down", "ln2_g", "ln2_b"]
    result = [loss, grad_x.reshape(x.shape)]
    for prefix in ("grad_", "delta_", "new_m_", "new_v_"):
        result += [outs[prefix + n] for n in weights]
    return tuple(result)
```

```python
import jax
import jax.numpy as jnp
from jax import lax
from jax.experimental import pallas as pl
from jax.experimental.pallas import tpu as pltpu

F32 = jnp.float32
BF16 = jnp.bfloat16
MESH = pl.DeviceIdType.MESH

D_MODEL = 1024
N_HEADS = 8
DK = 128
DV = 256
CHUNK = 128
ROPE_BASE = 10000.0
D_FF = 2816
CONV_K = 31
FFN_K = 3
LN_EPS = 1e-5
ALPHA = (2.0 * 1) ** 0.25
ADAM_LR = 0.001
ADAM_B1 = 0.9
ADAM_B2 = 0.999
ADAM_EPS = 1e-08
ADAM_WD = 0.01
ADAM_STEP = 10

V7X_VMEM_BYTES = 64 * 1024 * 1024
VMEM_LIMIT = V7X_VMEM_BYTES - 8 * 1024 * 1024
ROW_TILE = 256
FWD_ROW_TILE = 512
MM_TILE = 512
N_CHIPS = 4
N_DEV = 8

BIG_WEIGHTS = ("w_in", "w_up", "w_conv_out", "w_ret_out", "w_out", "w_down")


def _params(n_grid):
    return pltpu.CompilerParams(dimension_semantics=("arbitrary",) * n_grid, vmem_limit_bytes=VMEM_LIMIT)


def _rowwise(name, fn, rows, vecs, outs, reds, n_rows, tile=ROW_TILE, ncol=1, with_col=False, into=None):
    tile = _fit_tile(n_rows, tile)
    n_in = len(rows) + len(vecs)
    n_ref_in = n_in + (into is not None)
    n_out = len(outs)
    outs = [o if len(o) == 4 else (o[0], o[1], o[0] * ncol, 0) for o in outs]

    def col_map(off, row, first_row=0):
        def index(j, i):
            return (i + first_row // tile if row else 0, off(j) if callable(off) else off + j)
        return index

    def body(*refs):
        i = pl.program_id(1)
        vals = [r[...].astype(F32) for r in refs[:n_in]]
        res = fn(pl.program_id(0), *vals) if with_col else fn(*vals)
        for k in range(n_out):
            refs[n_ref_in + k][...] = res[k].astype(refs[n_ref_in + k].dtype)
        for k in range(len(reds)):
            o = refs[n_ref_in + n_out + k]

            @pl.when(i == 0)
            def _():
                o[...] = jnp.zeros_like(o)

            o[...] += res[n_out + k]

    in_specs = [pl.BlockSpec((tile, e[1]), col_map(e[2], True, e[3] if len(e) > 3 else 0)) for e in rows]
    in_specs += [pl.BlockSpec((1, w), col_map(off, False)) for _, w, off in vecs]
    args = [e[0] for e in rows] + [a for a, _, _ in vecs]
    aliases = {}
    if into is not None:
        in_specs.append(pl.BlockSpec(memory_space=pl.ANY))
        args.append(into[0])
        aliases = {n_in: into[1]}
    out_specs = [pl.BlockSpec((tile, w), col_map(off, True)) for w, _, _, off in outs]
    out_specs += [pl.BlockSpec((1, w), lambda j, i: (0, j)) for w in reds]
    out_shape = [jax.ShapeDtypeStruct((n_rows, total), dt) for _, dt, total, _ in outs]
    out_shape += [jax.ShapeDtypeStruct((1, w * ncol), F32) for w in reds]
    return pl.pallas_call(
        body, name=name, grid=(ncol, n_rows // tile), in_specs=in_specs, out_specs=out_specs, out_shape=out_shape,
        input_output_aliases=aliases, compiler_params=_params(2),
    )(*args)


def _vjp_rows(fn, n_row_in, n_ct):
    def bwd(*args):
        prim = [a.astype(F32) for a in args[:n_row_in] + args[n_row_in + n_ct:]]
        cts = tuple(a.astype(F32) for a in args[n_row_in:n_row_in + n_ct])
        _, pull = jax.vjp(fn, *prim)
        return pull(cts if n_ct > 1 else cts[0])

    return bwd


def _fit_tile(n, pref):
    if n <= pref:
        return n
    t = pref - pref % 16
    while n % t:
        t -= 16
    return t


def _col_tile(n):
    return n if n <= 1536 else n // 2


def _same_group(c):
    return c


def _mm_nn(name, a, w, bias, out_dtype, gmap=_same_group, tile=None, window=None, rows_outer=False):
    s, k = a.shape
    g, _, n = w.shape
    tm, tn = (min(MM_TILE, s), _col_tile(n)) if tile is None else (min(tile[0], s), tile[1])
    nt = n // tn
    first, n_cols = (0, g * n) if window is None else window
    t0 = first // tn

    def body(*refs):
        a_ref, w_ref = refs[0], refs[1]
        o_ref = refs[-1]
        acc = jnp.dot(a_ref[...].astype(BF16), w_ref[...], preferred_element_type=F32)
        if bias is not None:
            acc = acc + refs[2][...]
        o_ref[...] = acc.astype(o_ref.dtype)

    order = (lambda i, c: (c, i)) if rows_outer else (lambda c, i: (c, i))
    a_map = lambda *g_: (order(*g_)[1], 0)
    w_map = lambda *g_: (gmap((t0 + order(*g_)[0]) // nt), 0, (t0 + order(*g_)[0]) % nt)
    in_specs = [pl.BlockSpec((tm, k), a_map), pl.BlockSpec((None, k, tn), w_map)]
    args = [a, w]
    if bias is not None:
        in_specs.append(pl.BlockSpec((1, tn), lambda *g_: (0, t0 + order(*g_)[0])))
        args.append(bias)
    grid = (s // tm, n_cols // tn) if rows_outer else (n_cols // tn, s // tm)
    return pl.pallas_call(
        body, name=name, grid=grid, in_specs=in_specs, out_specs=pl.BlockSpec((tm, tn), lambda *g_: order(*g_)[::-1]),
        out_shape=jax.ShapeDtypeStruct((s, n_cols), out_dtype), compiler_params=_params(2),
    )(*args)


def _mm_nt(name, dy, w, out_dtype, gmap=_same_group):
    s = dy.shape[0]
    g, k, n = w.shape
    tm, tn = min(2 * MM_TILE if k <= 1024 else MM_TILE, s), n
    nt = 1
    steps = g

    def body(dy_ref, w_ref, o_ref, acc_ref):
        r = pl.program_id(1)
        part = lax.dot_general(dy_ref[...].astype(BF16), w_ref[...], (((1,), (1,)), ((), ())), preferred_element_type=F32)

        @pl.when(r == 0)
        def _():
            acc_ref[...] = part

        @pl.when(r > 0)
        def _():
            acc_ref[...] += part

        @pl.when(r == steps - 1)
        def _():
            o_ref[...] = acc_ref[...].astype(o_ref.dtype)

    return pl.pallas_call(
        body, name=name, grid=(s // tm, steps),
        in_specs=[pl.BlockSpec((tm, tn), lambda i, r: (i, r)),
                  pl.BlockSpec((None, k, tn), lambda i, r: (gmap(r // nt), 0, r % nt))],
        out_specs=pl.BlockSpec((tm, k), lambda i, r: (i, 0)), out_shape=jax.ShapeDtypeStruct((s, k), out_dtype),
        scratch_shapes=[pltpu.VMEM((tm, k), F32)], compiler_params=_params(2),
    )(dy, w)


def _mm_tn(name, a, dy, g, gmap=_same_group, colsum=False):
    s, k = a.shape
    n = dy.shape[1] // g
    ts = min(2048 if k <= 1024 else 1024, s)
    tn = n if k * n <= 1024 * 1536 else (n // 2 if (n // 2) % 128 == 0 else n)
    nt = n // tn

    def body(a_ref, dy_ref, o_ref, *sum_ref):
        t = pl.program_id(1)
        dyv = dy_ref[...].astype(BF16)
        part = lax.dot_general(a_ref[...].astype(BF16), dyv, (((0,), (0,)), ((), ())), preferred_element_type=F32)

        @pl.when(t == 0)
        def _():
            o_ref[...] = part

        @pl.when(t > 0)
        def _():
            o_ref[...] += part

        if colsum:
            col = jnp.sum(dyv.astype(F32), axis=0, keepdims=True)

            @pl.when(t == 0)
            def _():
                sum_ref[0][...] = col

            @pl.when(t > 0)
            def _():
                sum_ref[0][...] += col

    out_specs = [pl.BlockSpec((None, k, tn), lambda c, t: (gmap(c // nt), 0, c % nt))]
    out_shape = [jax.ShapeDtypeStruct((g, k, n), F32)]
    if colsum:
        out_specs.append(pl.BlockSpec((1, tn), lambda c, t: (0, c)))
        out_shape.append(jax.ShapeDtypeStruct((1, g * n), F32))
    res = pl.pallas_call(
        body, name=name, grid=(g * nt, s // ts),
        in_specs=[pl.BlockSpec((ts, k), lambda c, t: (t, 0)), pl.BlockSpec((ts, tn), lambda c, t: (t, c))],
        out_specs=out_specs, out_shape=out_shape, compiler_params=_params(2),
    )(a, dy)
    return res if colsum else res[0]


SUBLANES = 8


def _tap_sum(read, w_row, offsets, tile):
    acc = None
    for b in range(SUBLANES):
        group = [(k, o) for k, o in offsets if o % SUBLANES == b]
        if not group:
            continue
        rows = tile if b == 0 else tile + SUBLANES
        z = None
        for k, o in group:
            term = w_row(k) * read(o - b, rows)
            z = term if z is None else z + term
        part = z if b == 0 else z[b:b + tile]
        acc = part if acc is None else acc + part
    return acc


def _tap_grads(read, dy, offsets, tile):
    padded = jnp.concatenate([dy, jnp.zeros((SUBLANES, dy.shape[1]), dy.dtype)], axis=0)
    out = {}
    for b in range(SUBLANES):
        group = [(k, o) for k, o in offsets if o % SUBLANES == b]
        if not group:
            continue
        shifted = dy if b == 0 else pltpu.roll(padded, b, 0)
        rows = tile if b == 0 else tile + SUBLANES
        for k, o in group:
            out[k] = jnp.sum(shifted * read(o - b, rows), axis=0, keepdims=True)
    return out


CONV_HALO = 32
CONV_TILE = 256
CONV_LANES = 256


def _glu_conv_fwd(proj, col_v, col_g, w, b, n_rows):
    kw, n_ch = w.shape
    tile = min(CONV_TILE, n_rows)
    per = tile // CONV_HALO
    offsets = [(k, CONV_HALO - (kw - 1) + k) for k in range(kw)]

    def body(v_ref, g_ref, vh_ref, gh_ref, w_ref, b_ref, o_ref, buf):
        i = pl.program_id(0)
        prev = vh_ref[...].astype(F32) * jax.nn.sigmoid(gh_ref[...].astype(F32))
        buf[0:CONV_HALO, :] = jnp.where(i == 0, 0.0, prev)
        buf[CONV_HALO:CONV_HALO + tile, :] = v_ref[...].astype(F32) * jax.nn.sigmoid(g_ref[...].astype(F32))
        for c0 in range(0, n_ch, CONV_LANES):
            cols = slice(c0, c0 + CONV_LANES)
            o_ref[:, cols] = b_ref[:, cols] + _tap_sum(lambda s, n: buf[pl.ds(s, n), cols], lambda k: w_ref[k:k + 1, cols],
                                                      offsets, tile)

    main = lambda col: pl.BlockSpec((tile, n_ch), lambda i: (i, col))
    halo = lambda col: pl.BlockSpec((CONV_HALO, n_ch), lambda i: (jnp.maximum(i * per - 1, 0), col))
    return pl.pallas_call(
        body, name="conv_fwd", grid=(n_rows // tile,),
        in_specs=[main(col_v), main(col_g), halo(col_v), halo(col_g), pl.BlockSpec((kw, n_ch), lambda i: (0, 0)),
                  pl.BlockSpec((1, n_ch), lambda i: (0, 0))],
        out_specs=pl.BlockSpec((tile, n_ch), lambda i: (i, 0)), out_shape=jax.ShapeDtypeStruct((n_rows, n_ch), F32),
        scratch_shapes=[pltpu.VMEM((CONV_HALO + tile, n_ch), F32)], compiler_params=_params(1),
    )(proj, proj, proj, proj, w, b)


def _glu_conv_bwd(proj, col_v, col_g, dy, w, dproj, col_out, n_rows):
    kw, n_ch = w.shape
    tile = min(CONV_TILE, n_rows)
    per = tile // CONV_HALO
    n_tiles = n_rows // tile
    last_halo = n_rows // CONV_HALO - 1
    offsets = [(k, CONV_HALO - (kw - 1) + k) for k in range(kw)]
    back = [(k, kw - 1 - k) for k in range(kw)]

    def body(v_ref, g_ref, vh_ref, gh_ref, dy_ref, dyn_ref, w_ref, dp_any, dx_ref, dw_ref, db_ref, buf, dbuf):
        i = pl.program_id(0)
        cv, cg = v_ref[...].astype(F32), g_ref[...].astype(F32)
        sig = jax.nn.sigmoid(cg)
        prev = vh_ref[...].astype(F32) * jax.nn.sigmoid(gh_ref[...].astype(F32))
        buf[0:CONV_HALO, :] = jnp.where(i == 0, 0.0, prev)
        buf[CONV_HALO:CONV_HALO + tile, :] = cv * sig
        dbuf[0:tile, :] = dy_ref[...]
        dbuf[tile:tile + CONV_HALO, :] = jnp.where(i == n_tiles - 1, 0.0, dyn_ref[...])

        @pl.when(i == 0)
        def _():
            dw_ref[...] = jnp.zeros_like(dw_ref)
            db_ref[...] = jnp.zeros_like(db_ref)

        db_ref[...] += jnp.sum(dy_ref[...], axis=0, keepdims=True)
        for c0 in range(0, n_ch, CONV_LANES):
            cols = slice(c0, c0 + CONV_LANES)
            w_row = lambda k: w_ref[k:k + 1, cols]
            dx = _tap_sum(lambda s, n: dbuf[pl.ds(s, n), cols], w_row, back, tile)
            grads = _tap_grads(lambda s, n: buf[pl.ds(s, n), cols], dy_ref[:, cols], offsets, tile)
            for k in range(kw):
                dw_ref[k:k + 1, cols] += grads[k]
            sg = sig[:, cols]
            dx_ref[:, c0:c0 + CONV_LANES] = (dx * sg).astype(dx_ref.dtype)
            dx_ref[:, n_ch + c0:n_ch + c0 + CONV_LANES] = (dx * cv[:, cols] * sg * (1.0 - sg)).astype(dx_ref.dtype)

    main = lambda col: pl.BlockSpec((tile, n_ch), lambda i: (i, col))
    halo = lambda col: pl.BlockSpec((CONV_HALO, n_ch), lambda i: (jnp.maximum(i * per - 1, 0), col))
    return pl.pallas_call(
        body, name="conv_bwd", grid=(n_tiles,),
        in_specs=[main(col_v), main(col_g), halo(col_v), halo(col_g), main(0),
                  pl.BlockSpec((CONV_HALO, n_ch), lambda i: (jnp.minimum((i + 1) * per, last_halo), 0)),
                  pl.BlockSpec((kw, n_ch), lambda i: (0, 0)), pl.BlockSpec(memory_space=pl.ANY)],
        out_specs=[pl.BlockSpec((tile, 2 * n_ch), lambda i: (i, col_out)), pl.BlockSpec((kw, n_ch), lambda i: (0, 0)),
                   pl.BlockSpec((1, n_ch), lambda i: (0, 0))],
        out_shape=[jax.ShapeDtypeStruct(dproj.shape, dproj.dtype), jax.ShapeDtypeStruct((kw, n_ch), F32),
                   jax.ShapeDtypeStruct((1, n_ch), F32)],
        input_output_aliases={7: 0},
        scratch_shapes=[pltpu.VMEM((CONV_HALO + tile, n_ch), F32), pltpu.VMEM((tile + CONV_HALO, n_ch), F32)],
        compiler_params=_params(1),
    )(proj, proj, proj, proj, dy, dy, w, dproj)


FFN_HALO = 8


def _ffn_fwd(u, w, b, n_rows):
    kw = w.shape[0]
    half = u.shape[1] // 4
    tile = min(ROW_TILE, n_rows)
    per = tile // FFN_HALO
    offsets = [(k, FFN_HALO - (kw - 1) + k) for k in range(kw)]

    def body(u_ref, uh_ref, w_ref, b_ref, p_ref, f_ref, buf):
        i = pl.program_id(1)
        buf[0:FFN_HALO, :] = jnp.where(i == 0, 0.0, uh_ref[...].astype(F32))
        buf[FFN_HALO:FFN_HALO + tile, :] = u_ref[...].astype(F32)
        conv = []
        for c0 in (0, half):
            cols = slice(c0, c0 + half)
            conv.append(b_ref[:, cols] + _tap_sum(lambda s, n: buf[pl.ds(s, n), cols], lambda k: w_ref[k:k + 1, cols],
                                                  offsets, tile))
            f_ref[:, cols] = conv[-1].astype(f_ref.dtype)
        p_ref[...] = (conv[0] * jax.nn.silu(conv[1])).astype(p_ref.dtype)

    return pl.pallas_call(
        body, name="ffn_fwd", grid=(2, n_rows // tile),
        in_specs=[pl.BlockSpec((tile, 2 * half), lambda j, i: (i, j)),
                  pl.BlockSpec((FFN_HALO, 2 * half), lambda j, i: (jnp.maximum(i * per - 1, 0), j)),
                  pl.BlockSpec((kw, 2 * half), lambda j, i: (0, j)), pl.BlockSpec((1, 2 * half), lambda j, i: (0, j))],
        out_specs=[pl.BlockSpec((tile, half), lambda j, i: (i, j)), pl.BlockSpec((tile, 2 * half), lambda j, i: (i, j))],
        out_shape=[jax.ShapeDtypeStruct((n_rows, 2 * half), BF16), jax.ShapeDtypeStruct(u.shape, BF16)],
        scratch_shapes=[pltpu.VMEM((FFN_HALO + tile, 2 * half), F32)], compiler_params=_params(2),
    )(u, u, w, b)


def _ffn_bwd(u, filtered, dp, w, n_rows):
    kw = w.shape[0]
    half = u.shape[1] // 4
    tile = min(ROW_TILE, n_rows)
    per = tile // FFN_HALO
    n_tiles = n_rows // tile
    last_halo = n_rows // FFN_HALO - 1
    ext = tile + FFN_HALO
    offsets = [(k, FFN_HALO - (kw - 1) + k) for k in range(kw)]
    back = [(k, kw - 1 - k) for k in range(kw)]

    def body(u_ref, up_ref, f_ref, fn_ref, dp_ref, dpn_ref, w_ref, du_ref, dw_ref, db_ref, buf, dbuf):
        i = pl.program_id(1)
        buf[0:FFN_HALO, :] = jnp.where(i == 0, 0.0, up_ref[...].astype(F32))
        buf[FFN_HALO:FFN_HALO + tile, :] = u_ref[...].astype(F32)
        filt = jnp.concatenate([f_ref[...], fn_ref[...]], axis=0).astype(F32)
        val, gate = filt[:, 0:half], filt[:, half:2 * half]
        dpe = jnp.concatenate([dp_ref[...].astype(F32), jnp.where(i == n_tiles - 1, 0.0, dpn_ref[...].astype(F32))], axis=0)
        sig = jax.nn.sigmoid(gate)
        dbuf[:, 0:half] = dpe * gate * sig
        dbuf[:, half:2 * half] = dpe * val * sig * (1.0 + gate * (1.0 - sig))

        @pl.when(i == 0)
        def _():
            dw_ref[...] = jnp.zeros_like(dw_ref)
            db_ref[...] = jnp.zeros_like(db_ref)

        for c0 in (0, half):
            cols = slice(c0, c0 + half)
            du_ref[:, cols] = _tap_sum(lambda s, n: dbuf[pl.ds(s, n), cols], lambda k: w_ref[k:k + 1, cols], back,
                                       tile).astype(du_ref.dtype)
            d_main = dbuf[0:tile, cols]
            db_ref[:, cols] += jnp.sum(d_main, axis=0, keepdims=True)
            grads = _tap_grads(lambda s, n: buf[pl.ds(s, n), cols], d_main, offsets, tile)
            for k in range(kw):
                dw_ref[k:k + 1, cols] += grads[k]

    wide = 2 * half
    return pl.pallas_call(
        body, name="ffn_bwd", grid=(2, n_tiles),
        in_specs=[pl.BlockSpec((tile, wide), lambda j, i: (i, j)),
                  pl.BlockSpec((FFN_HALO, wide), lambda j, i: (jnp.maximum(i * per - 1, 0), j)),
                  pl.BlockSpec((tile, wide), lambda j, i: (i, j)),
                  pl.BlockSpec((FFN_HALO, wide), lambda j, i: (jnp.minimum((i + 1) * per, last_halo), j)),
                  pl.BlockSpec((tile, half), lambda j, i: (i, j)),
                  pl.BlockSpec((FFN_HALO, half), lambda j, i: (jnp.minimum((i + 1) * per, last_halo), j)),
                  pl.BlockSpec((kw, wide), lambda j, i: (0, j))],
        out_specs=[pl.BlockSpec((tile, wide), lambda j, i: (i, j)), pl.BlockSpec((kw, wide), lambda j, i: (0, j)),
                   pl.BlockSpec((1, wide), lambda j, i: (0, j))],
        out_shape=[jax.ShapeDtypeStruct(u.shape, BF16), jax.ShapeDtypeStruct((kw, u.shape[1]), F32),
                   jax.ShapeDtypeStruct((1, u.shape[1]), F32)],
        scratch_shapes=[pltpu.VMEM((FFN_HALO + tile, wide), F32), pltpu.VMEM((ext, wide), F32)], compiler_params=_params(2),
    )(u, u, filtered, filtered, dp, dp, w)


def _retention_consts():
    log_gamma = jnp.log(1.0 - 2.0 ** (-5.0 - jnp.arange(N_HEADS, dtype=F32)))
    idx = jnp.arange(CHUNK, dtype=F32)
    rel = idx[:, None] - idx[None, :]
    decay = jnp.where(rel[None] >= 0, jnp.exp(log_gamma[:, None, None] * jnp.maximum(rel, 0.0)[None]), 0.0)
    zeta = jnp.exp(log_gamma[:, None] * (CHUNK - 1.0 - idx)[None])
    xi = jnp.exp(log_gamma[:, None] * (idx + 1.0)[None])
    chunk_decay = jnp.exp(log_gamma * CHUNK)
    xi_b = jnp.broadcast_to(xi[:, :, None], (N_HEADS, CHUNK, DK))
    zeta_b = jnp.broadcast_to(zeta[:, :, None], (N_HEADS, CHUNK, DK))
    cd_b = jnp.broadcast_to(chunk_decay[:, None, None], (N_HEADS, 8, DV))
    return decay, xi_b, zeta_b, cd_b


def _rope_tables(positions):
    half = DK // 2
    inv_freq = ROPE_BASE ** (-jnp.arange(half, dtype=F32) / half)
    ang = positions.astype(F32)[:, None] * inv_freq
    cos, sin = jnp.cos(ang), jnp.sin(ang)
    return jnp.concatenate([cos, cos], axis=-1), jnp.concatenate([-sin, sin], axis=-1)


def _swap_halves(v):
    return pltpu.roll(v, DK // 2, 1)


def _dot(a, b):
    return jnp.dot(a, b, preferred_element_type=F32)


def _dot_nt(a, b):
    return lax.dot_general(a, b, (((1,), (1,)), ((), ())), preferred_element_type=F32)


def _dot_tn(a, b):
    return lax.dot_general(a, b, (((0,), (0,)), ((), ())), preferred_element_type=F32)


def _const_specs():
    return [pl.BlockSpec((N_HEADS, CHUNK, CHUNK), lambda n: (0, 0, 0)), pl.BlockSpec((N_HEADS, CHUNK, DK), lambda n: (0, 0, 0)),
            pl.BlockSpec((N_HEADS, CHUNK, DK), lambda n: (0, 0, 0)), pl.BlockSpec((N_HEADS, 8, DV), lambda n: (0, 0, 0))]


def _retention_fwd(proj_qk, proj_rest, cos_t, sin_t, consts, n_rows):
    n_chunks = n_rows // CHUNK
    scale = DK ** -0.5

    def body(q_ref, k_ref, v_ref, cf_ref, ss_ref, d_ref, xi_ref, zt_ref, cd_ref, r_ref, st_ref, state):
        @pl.when(pl.program_id(0) == 0)
        def _():
            state[...] = jnp.zeros_like(state)

        cf, ss = cf_ref[...], ss_ref[...]
        for h in range(N_HEADS):
            qh = q_ref[:, h * DK:(h + 1) * DK].astype(F32)
            kh = k_ref[:, h * DK:(h + 1) * DK].astype(F32)
            qh = qh * cf + _swap_halves(qh) * ss
            kh = (kh * cf + _swap_halves(kh) * ss) * scale
            vh = v_ref[:, h * DV:(h + 1) * DV].astype(BF16)
            st = state[h]
            st_ref[0, h] = st
            sd = _dot_nt(qh.astype(BF16), kh.astype(BF16)) * d_ref[h]
            inner = _dot(sd.astype(BF16), vh)
            cross = _dot((qh * xi_ref[h]).astype(BF16), st.astype(BF16))
            kv = _dot_tn((kh * zt_ref[h]).astype(BF16), vh)
            state[h] = st * cd_ref[h, 0:1, :] + kv
            r_ref[:, h * DV:(h + 1) * DV] = inner + cross

    qk = N_HEADS * DK
    vw = N_HEADS * DV
    return pl.pallas_call(
        body, name="retention_fwd", grid=(n_chunks,),
        in_specs=[pl.BlockSpec((CHUNK, qk), lambda n: (n, 0)), pl.BlockSpec((CHUNK, qk), lambda n: (n, 1)),
                  pl.BlockSpec((CHUNK, vw), lambda n: (n, 0)), pl.BlockSpec((CHUNK, DK), lambda n: (n, 0)),
                  pl.BlockSpec((CHUNK, DK), lambda n: (n, 0))] + _const_specs(),
        out_specs=[pl.BlockSpec((CHUNK, vw), lambda n: (n, 0)), pl.BlockSpec((1, N_HEADS, DK, DV), lambda n: (n, 0, 0, 0))],
        out_shape=[jax.ShapeDtypeStruct((n_rows, vw), F32), jax.ShapeDtypeStruct((n_chunks, N_HEADS, DK, DV), F32)],
        scratch_shapes=[pltpu.VMEM((N_HEADS, DK, DV), F32)], compiler_params=_params(1),
    )(proj_qk, proj_qk, proj_rest, cos_t, sin_t, *consts)


def _retention_bwd(proj_qk, proj_rest, cos_t, sin_t, states, dr, consts, dproj, n_rows):
    n_chunks = n_rows // CHUNK
    scale = DK ** -0.5
    qk = N_HEADS * DK
    vw = N_HEADS * DV

    def body(q_ref, k_ref, v_ref, cf_ref, ss_ref, st_ref, dr_ref, d_ref, xi_ref, zt_ref, cd_ref, dp_any, dqkv_ref, g_ref):
        dq_ref, dk_ref, dv_ref = dqkv_ref.at[:, 0:qk], dqkv_ref.at[:, qk:2 * qk], dqkv_ref.at[:, 2 * qk:2 * qk + vw]

        @pl.when(pl.program_id(0) == 0)
        def _():
            g_ref[...] = jnp.zeros_like(g_ref)

        cf, ss = cf_ref[...], ss_ref[...]
        for h in range(N_HEADS):
            qh = q_ref[:, h * DK:(h + 1) * DK].astype(F32)
            kh = k_ref[:, h * DK:(h + 1) * DK].astype(F32)
            qh = qh * cf + _swap_halves(qh) * ss
            kh = (kh * cf + _swap_halves(kh) * ss) * scale
            qb, kb = qh.astype(BF16), kh.astype(BF16)
            vh = v_ref[:, h * DV:(h + 1) * DV].astype(BF16)
            do = dr_ref[:, h * DV:(h + 1) * DV].astype(BF16)
            rb = st_ref[0, h].astype(BF16)
            g = g_ref[h]
            gb = g.astype(BF16)
            dec, xi, zt = d_ref[h], xi_ref[h], zt_ref[h]
            sd = (_dot_nt(qb, kb) * dec).astype(BF16)
            ds = (_dot_nt(do, vh) * dec).astype(BF16)
            dqh = _dot(ds, kb) + _dot_nt(do, rb) * xi
            dkh = (_dot_tn(ds, qb) + _dot_nt(vh, gb) * zt) * scale
            dvh = _dot_tn(sd, do) + _dot((kh * zt).astype(BF16), gb)
            g_ref[h] = g * cd_ref[h, 0:1, :] + _dot_tn((qh * xi).astype(BF16), do)
            dq_ref[:, h * DK:(h + 1) * DK] = (dqh * cf + _swap_halves(dqh * ss)).astype(dq_ref.dtype)
            dk_ref[:, h * DK:(h + 1) * DK] = (dkh * cf + _swap_halves(dkh * ss)).astype(dk_ref.dtype)
            dv_ref[:, h * DV:(h + 1) * DV] = dvh.astype(dv_ref.dtype)

    last = n_chunks - 1
    return pl.pallas_call(
        body, name="retention_bwd", grid=(n_chunks,),
        in_specs=[pl.BlockSpec((CHUNK, qk), lambda n: (last - n, 0)), pl.BlockSpec((CHUNK, qk), lambda n: (last - n, 1)),
                  pl.BlockSpec((CHUNK, vw), lambda n: (last - n, 0)), pl.BlockSpec((CHUNK, DK), lambda n: (last - n, 0)),
                  pl.BlockSpec((CHUNK, DK), lambda n: (last - n, 0)),
                  pl.BlockSpec((1, N_HEADS, DK, DV), lambda n: (last - n, 0, 0, 0)),
                  pl.BlockSpec((CHUNK, vw), lambda n: (last - n, 0))] + _const_specs() + [pl.BlockSpec(memory_space=pl.ANY)],
        out_specs=pl.BlockSpec((CHUNK, 2 * qk + vw), lambda n: (last - n, 0)),
        out_shape=jax.ShapeDtypeStruct(dproj.shape, dproj.dtype), input_output_aliases={11: 0},
        scratch_shapes=[pltpu.VMEM((N_HEADS, DK, DV), F32)], compiler_params=_params(1),
    )(proj_qk, proj_qk, proj_rest, cos_t, sin_t, states, dr, *consts, dproj)


def _ln(v):
    mu = jnp.mean(v, axis=-1, keepdims=True)
    var = jnp.mean(jnp.square(v - mu), axis=-1, keepdims=True)
    return (v - mu) * lax.rsqrt(var + LN_EPS)


def _f_modulate(x, scale, shift):
    return _ln(x) * (1.0 + scale) + shift


def _f_conv_norm(a1, g, b):
    return jax.nn.silu(_ln(a1) * g + b)


def _f_group_norm_gate(r, gate, g, b):
    return (_ln(r) * g + b) * jax.nn.silu(gate)


def _per_head(fn):
    def run(*arrays):
        parts = [fn(*[a[:, h * DV:(h + 1) * DV] for a in arrays]) for h in range(N_HEADS)]
        if not isinstance(parts[0], (tuple, list)):
            return (jnp.concatenate(parts, axis=1),)
        return tuple(jnp.concatenate([p[k] for p in parts], axis=1) for k in range(len(parts[0])))

    return run


def _f_merge(ga, gb, ya, yb):
    return jax.nn.sigmoid(ga) * ya + jax.nn.sigmoid(gb) * yb


def _f_post1(x, t, gate1, g1, b1, scale2, shift2):
    x1 = _ln(ALPHA * x + gate1 * t) * g1 + b1
    return x1, _ln(x1) * (1.0 + scale2) + shift2


def _f_loss(x1, f, gate2, g2, b2, target):
    y = _ln(ALPHA * x1 + gate2 * f) * g2 + b2
    return 0.5 * jnp.sum(jnp.mean(jnp.square(y - target), axis=-1))


ANY = pl.BlockSpec(memory_space=pl.ANY)


def _allgather8(name, blocks, own_half=False):
    n = len(blocks)
    rows = [b.shape[0] // 2 if own_half else b.shape[0] for b in blocks]

    def body(*refs):
        x_refs, out_refs = refs[:n], refs[n:2 * n]
        send_sems, recv_sems = refs[2 * n:]
        x, y, c = lax.axis_index("x"), lax.axis_index("y"), lax.axis_index("c")
        me, sibling = (x, y, c), (x, y, 1 - c)
        chips = [(1 - x, y), (x, 1 - y), (1 - x, 1 - y)]
        every = range(n)

        def src(a):
            return x_refs[a].at[pl.ds(c * rows[a], rows[a])] if own_half else x_refs[a]

        def slot(a, px, py, pc):
            return out_refs[a].at[4 * px + 2 * py + pc]

        def copy(k, a, block, to, from_input=False):
            return pltpu.make_async_remote_copy(
                src_ref=src(a) if from_input else slot(a, *block), dst_ref=slot(a, *block), send_sem=send_sems.at[k, a],
                recv_sem=recv_sems.at[k, a], device_id=to, device_id_type=MESH)

        first = [copy(0, a, me, sibling, True) for a in every]
        first += [copy(1 + j, a, me, (*chip, c), True) for j, chip in enumerate(chips) for a in every]
        for cp in first:
            cp.start()
        passed = [[copy(4 + j, a, (*chip, c), sibling) for a in every] for j, chip in enumerate(chips)]
        for j, chip in enumerate(chips):
            for a in every:
                copy(1 + j, a, (*chip, c), me).wait_recv()
            for cp in passed[j]:
                cp.start()
        for a in every:
            copy(0, a, sibling, me).wait_recv()
        for j, chip in enumerate(chips):
            for a in every:
                copy(4 + j, a, (*chip, 1 - c), me).wait_recv()
        for cp in first + [cp for group in passed for cp in group]:
            cp.wait_send()

    gathered = pl.pallas_call(
        body, name=name, in_specs=[ANY] * n, out_specs=[ANY] * n,
        out_shape=[jax.ShapeDtypeStruct((N_DEV, r, b.shape[1]), b.dtype) for r, b in zip(rows, blocks)],
        scratch_shapes=[pltpu.SemaphoreType.DMA((7, n)), pltpu.SemaphoreType.DMA((7, n))],
    )(*blocks)
    c = lax.axis_index("c")
    me = 4 * lax.axis_index("x") + 2 * lax.axis_index("y") + c
    own = [lax.dynamic_slice_in_dim(b, c * r, r, 0) if own_half else b for r, b in zip(rows, blocks)]
    return [lax.dynamic_update_slice_in_dim(g_, o_[None], me, 0) for g_, o_ in zip(gathered, own)]


def _sibling_swap(name, arrays):
    n = len(arrays)
    halves = [a.shape[1] // 2 for a in arrays]

    def body(*refs):
        g_refs, theirs = refs[:n], refs[n:2 * n]
        send_sems, recv_sems = refs[2 * n:]
        x, y, c = lax.axis_index("x"), lax.axis_index("y"), lax.axis_index("c")
        remote = [pltpu.make_async_remote_copy(
            src_ref=g_refs[a].at[:, pl.ds((1 - c) * halves[a], halves[a]), :], dst_ref=theirs[a], send_sem=send_sems.at[a],
            recv_sem=recv_sems.at[a], device_id=(x, y, 1 - c), device_id_type=MESH) for a in range(n)]
        for cp in remote:
            cp.start()
        for cp in remote:
            cp.wait_recv()
        for cp in remote:
            cp.wait_send()

    return pl.pallas_call(
        body, name=name, in_specs=[ANY] * n, out_specs=[ANY] * n,
        out_shape=[jax.ShapeDtypeStruct((a.shape[0], h, a.shape[2]), a.dtype) for a, h in zip(arrays, halves)],
        scratch_shapes=[pltpu.SemaphoreType.DMA((n,)), pltpu.SemaphoreType.DMA((n,))],
    )(*arrays)


HBM = pl.BlockSpec(memory_space=pltpu.HBM)
SEM = pl.BlockSpec(memory_space=pltpu.SEMAPHORE)
DATAFLOW = pltpu.SideEffectType.DATAFLOW_SIDE_EFFECTING


def _chip_peers(x, y):
    return [(1 - x, y), (x, 1 - y), (1 - x, 1 - y)]


def _other_devices(x, y, c):
    flip = lambda v, f: 1 - v if f else v
    return [(flip(x, k & 4), flip(y, k & 2), flip(c, k & 1)) for k in range(1, N_DEV)]


def _gather_start(name, arrays):
    n = len(arrays)
    rows = [a.shape[0] // 2 for a in arrays]
    zones = [lax.empty((N_DEV, r, a.shape[1]), a.dtype) for r, a in zip(rows, arrays)]

    def body(*refs):
        x_refs, zone_refs = refs[:n], refs[n:2 * n]
        send_sems, recv_sems = refs[2 * n], refs[2 * n + 1]
        token = refs[-1]
        x, y, c = lax.axis_index("x"), lax.axis_index("y"), lax.axis_index("c")
        for k, peer in enumerate(_other_devices(x, y, c)):
            for a in range(n):
                pltpu.make_async_remote_copy(
                    src_ref=x_refs[a].at[pl.ds(c * rows[a], rows[a])], dst_ref=zone_refs[a].at[4 * x + 2 * y + c],
                    send_sem=send_sems.at[k * n + a], recv_sem=recv_sems.at[k * n + a], device_id=peer,
                    device_id_type=MESH).start()
        token[...] = jnp.zeros_like(token)

    thru = [pltpu.HBM(a.shape, a.dtype) for a in arrays] + [pltpu.HBM(z.shape, z.dtype) for z in zones]
    n_sems = (N_DEV - 1) * n
    res = pl.pallas_call(
        body, name=name, in_specs=[HBM] * (2 * n),
        out_specs=(SEM, SEM, *[HBM] * (2 * n), pl.BlockSpec(memory_space=pltpu.VMEM)),
        out_shape=(pltpu.SemaphoreType.DMA((n_sems,)), pltpu.SemaphoreType.DMA((n_sems,)), *thru,
                   jax.ShapeDtypeStruct((8, 128), F32)),
        input_output_aliases={i: 2 + i for i in range(2 * n)},
        compiler_params=pltpu.CompilerParams(has_side_effects=DATAFLOW),
    )(*[pltpu.with_memory_space_constraint(a, pltpu.HBM) for a in arrays],
      *[pltpu.with_memory_space_constraint(z, pltpu.HBM) for z in zones])
    return (res[0], res[1], list(res[2:2 + n]), list(res[2 + n:2 + 2 * n])), res[-1]


def _gather_wait(name, handle, after):
    send_sems, recv_sems, sources, zones = handle
    n = len(sources)
    rows = [z.shape[1] for z in zones]

    def body(*refs):
        x_refs, zone_refs = refs[:n], refs[n:2 * n]
        send_sems, recv_sems = refs[2 * n], refs[2 * n + 1]
        x, y, c = lax.axis_index("x"), lax.axis_index("y"), lax.axis_index("c")
        for k, (px, py, pc) in enumerate(_other_devices(x, y, c)):
            for a in range(n):
                cp = pltpu.make_async_remote_copy(
                    src_ref=x_refs[a].at[pl.ds(c * rows[a], rows[a])], dst_ref=zone_refs[a].at[4 * px + 2 * py + pc],
                    send_sem=send_sems.at[k * n + a], recv_sem=recv_sems.at[k * n + a], device_id=(px, py, pc),
                    device_id_type=MESH)
                cp.wait_send()
                cp.wait_recv()

    res = pl.pallas_call(
        body, name=name, in_specs=[HBM] * (2 * n) + [SEM, SEM, ANY], out_specs=[HBM] * (2 * n),
        out_shape=[pltpu.HBM(a.shape, a.dtype) for a in sources + zones],
        input_output_aliases={i: i for i in range(2 * n)}, compiler_params=pltpu.CompilerParams(has_side_effects=DATAFLOW),
    )(*sources, *zones, send_sems, recv_sems, after)
    return list(res[n:])


def _alltoall_start(name, arrays):
    n = len(arrays)
    lands = [lax.empty((3,) + a.shape[1:], a.dtype) for a in arrays]

    def body(*refs):
        p_refs, land_refs = refs[:n], refs[n:2 * n]
        send_sems, recv_sems = refs[2 * n], refs[2 * n + 1]
        token = refs[-1]
        x, y, c = lax.axis_index("x"), lax.axis_index("y"), lax.axis_index("c")
        for k, (px, py) in enumerate(_chip_peers(x, y)):
            for a in range(n):
                pltpu.make_async_remote_copy(
                    src_ref=p_refs[a].at[2 * px + py], dst_ref=land_refs[a].at[k], send_sem=send_sems.at[k * n + a],
                    recv_sem=recv_sems.at[k * n + a], device_id=(px, py, c), device_id_type=MESH).start()
        token[...] = jnp.zeros_like(token)

    thru = [pltpu.HBM(a.shape, a.dtype) for a in arrays] + [pltpu.HBM(l.shape, l.dtype) for l in lands]
    res = pl.pallas_call(
        body, name=name, in_specs=[HBM] * (2 * n),
        out_specs=(SEM, SEM, *[HBM] * (2 * n), pl.BlockSpec(memory_space=pltpu.VMEM)),
        out_shape=(pltpu.SemaphoreType.DMA((3 * n,)), pltpu.SemaphoreType.DMA((3 * n,)), *thru, jax.ShapeDtypeStruct((8, 128), F32)),
        input_output_aliases={i: 2 + i for i in range(2 * n)},
        compiler_params=pltpu.CompilerParams(has_side_effects=DATAFLOW),
    )(*[pltpu.with_memory_space_constraint(a, pltpu.HBM) for a in arrays],
      *[pltpu.with_memory_space_constraint(l, pltpu.HBM) for l in lands])
    return (res[0], res[1], list(res[2:2 + n]), list(res[2 + n:2 + 2 * n])), res[-1]


def _alltoall_wait(name, handle, after):
    send_sems, recv_sems, sources, lands = handle
    n = len(sources)

    def body(*refs):
        p_refs, land_refs = refs[:n], refs[n:2 * n]
        send_sems, recv_sems = refs[2 * n], refs[2 * n + 1]
        x, y, c = lax.axis_index("x"), lax.axis_index("y"), lax.axis_index("c")
        for k, (px, py) in enumerate(_chip_peers(x, y)):
            for a in range(n):
                cp = pltpu.make_async_remote_copy(
                    src_ref=p_refs[a].at[2 * px + py], dst_ref=land_refs[a].at[k], send_sem=send_sems.at[k * n + a],
                    recv_sem=recv_sems.at[k * n + a], device_id=(px, py, c), device_id_type=MESH)
                cp.wait_send()
                cp.wait_recv()

    res = pl.pallas_call(
        body, name=name, in_specs=[HBM] * (2 * n) + [SEM, SEM, ANY], out_specs=[HBM] * (2 * n),
        out_shape=[pltpu.HBM(a.shape, a.dtype) for a in sources + lands],
        input_output_aliases={i: i for i in range(2 * n)}, compiler_params=pltpu.CompilerParams(has_side_effects=DATAFLOW),
    )(*sources, *lands, send_sems, recv_sems, after)
    return list(res[n:])


def _sibling_exchange(name, halves):
    n = len(halves)

    def body(*refs):
        h_refs, out_refs = refs[:n], refs[n:2 * n]
        send_sems, recv_sems = refs[2 * n:]
        x, y, c = lax.axis_index("x"), lax.axis_index("y"), lax.axis_index("c")
        remote = [pltpu.make_async_remote_copy(
            src_ref=h_refs[a], dst_ref=out_refs[a], send_sem=send_sems.at[a], recv_sem=recv_sems.at[a],
            device_id=(x, y, 1 - c), device_id_type=MESH) for a in range(n)]
        for cp in remote:
            cp.start()
        for cp in remote:
            cp.wait_recv()
        for cp in remote:
            cp.wait_send()

    return pl.pallas_call(
        body, name=name, in_specs=[ANY] * n, out_specs=[ANY] * n,
        out_shape=[jax.ShapeDtypeStruct(h.shape, h.dtype) for h in halves],
        scratch_shapes=[pltpu.SemaphoreType.DMA((n,)), pltpu.SemaphoreType.DMA((n,))],
    )(*halves)


def _sum_arrays(name, terms, n_rows):
    c_ = terms[0][0].shape[1]

    def add_all(*vals):
        acc = vals[0]
        for v in vals[1:]:
            acc = acc + v
        return (acc,)

    return _rowwise(name, add_all, [(a, c_, 0, first) for a, first in terms], [], [(c_, F32)], [], n_rows, tile=128)[0]


def _adamw_fn(w, g, m, v):
    m = ADAM_B1 * m + (1.0 - ADAM_B1) * g
    v = ADAM_B2 * v + (1.0 - ADAM_B2) * jnp.square(g)
    m_hat = m / (1.0 - ADAM_B1 ** ADAM_STEP)
    v_hat = v / (1.0 - ADAM_B2 ** ADAM_STEP)
    delta = -ADAM_LR * (m_hat / (jnp.sqrt(v_hat) + ADAM_EPS) + ADAM_WD * w)
    return delta, m, v


def _adamw(name, w, g, m, v):
    r, c_ = w.shape
    return _rowwise(name, _adamw_fn, [(w, c_, 0), (g, c_, 0), (m, c_, 0), (v, c_, 0)], [], [(c_, F32)] * 3, [], r, tile=128)


def _ada_fwd(c_all, w_ada, b_ada):
    n = w_ada.shape[1]

    def body(c_ref, w_ref, b_ref, o_ref):
        o_ref[...] = jnp.dot(jax.nn.silu(c_ref[...]).astype(BF16), w_ref[...].astype(BF16),
                             preferred_element_type=F32) + b_ref[...]

    return pl.pallas_call(body, name="ada_fwd", out_shape=jax.ShapeDtypeStruct((N_DEV, n), F32),
                          compiler_params=pltpu.CompilerParams(vmem_limit_bytes=VMEM_LIMIT))(c_all, w_ada, b_ada)


def _ada_bwd(c_all, dmod_cols):
    d = c_all.shape[1]
    n = dmod_cols.shape[1]

    def body(c_ref, dm_ref, gw_ref):
        gw_ref[...] = lax.dot_general(jax.nn.silu(c_ref[...]).astype(BF16), dm_ref[...].astype(BF16),
                                      (((0,), (0,)), ((), ())), preferred_element_type=F32)

    return pl.pallas_call(body, name="ada_bwd", out_shape=jax.ShapeDtypeStruct((d, n), F32),
                          compiler_params=pltpu.CompilerParams(vmem_limit_bytes=VMEM_LIMIT))(c_all, dmod_cols)


def _cast_bf16(name, a):
    r, c_ = a.shape
    return _rowwise(name, lambda v: (v,), [(a, c_, 0)], [], [(c_, BF16)], [], r)[0]


def _pad_rows(vec, mult):
    n = vec.shape[0]
    return jnp.pad(vec, (0, (-n) % mult))


def kernel(x, c, positions, w_ada, b_ada, w_in, b_in, conv_dw_w, conv_dw_b, conv_ln_g, conv_ln_b, w_conv_out, ret_gn_g, ret_gn_b, w_ret_out, w_out, ln1_g, ln1_b, w_up, ffn_dw_w, ffn_dw_b, w_down, ln2_g, ln2_b, loss_target, m_w_ada, m_b_ada, m_w_in, m_b_in, m_conv_dw_w, m_conv_dw_b, m_conv_ln_g, m_conv_ln_b, m_w_conv_out, m_ret_gn_g, m_ret_gn_b, m_w_ret_out, m_w_out, m_ln1_g, m_ln1_b, m_w_up, m_ffn_dw_w, m_ffn_dw_b, m_w_down, m_ln2_g, m_ln2_b, v_w_ada, v_b_ada, v_w_in, v_b_in, v_conv_dw_w, v_conv_dw_b, v_conv_ln_g, v_conv_ln_b, v_w_conv_out, v_ret_gn_g, v_ret_gn_b, v_w_ret_out, v_w_out, v_ln1_g, v_ln1_b, v_w_up, v_ffn_dw_w, v_ffn_dw_b, v_w_down, v_ln2_g, v_ln2_b):
    given = dict(locals())
    n_rows = x.shape[1]
    d = D_MODEL
    my_c = lax.axis_index("c")
    chip = 2 * lax.axis_index("x") + lax.axis_index("y")
    dev = 2 * chip + my_c
    xr = x[0]
    target = loss_target[0]
    vw = N_HEADS * DV
    ffw = 2 * D_FF

    def flat_rows(arrays, mult=8):
        v = jnp.concatenate([a.reshape(-1) for a in arrays])
        return _pad_rows(v, mult * d).reshape(-1, d)

    def unflatten(flat2d, shapes):
        v, out, o = flat2d.reshape(-1), [], 0
        for shp in shapes:
            size = 1
            for e in shp:
                size *= e
            out.append(v[o:o + size].reshape(shp))
            o += size
        return out

    w_bf = {n: _cast_bf16("cast_" + n, given[n][0]) for n in BIG_WEIGHTS}
    wg_in = _allgather8("gather_w_in", [w_bf["w_in"]], own_half=True)[0].reshape(N_CHIPS, d, w_in.shape[2])
    others = [n for n in BIG_WEIGHTS if n != "w_in"]
    probe = wg_in[0, 0:1, 0:1].astype(F32)
    zero = (jnp.where(jnp.isfinite(probe), probe, 0.0) * 0.0).astype(BF16)
    gather_handle, gather_token = _gather_start("gather_weights_start", [w_bf[n] + zero for n in others])

    kc, kf = conv_dw_w.shape[2], ffn_dw_w.shape[2]
    small_all = _allgather8("gather_small", [flat_rows([c, conv_dw_w, ffn_dw_w])])[0].reshape(N_DEV, -1)
    c_all = small_all[:, :d]
    per_chip = small_all[0::2]
    conv_w = per_chip[:, d:d + CONV_K * kc].reshape(N_CHIPS, CONV_K, kc).transpose(1, 0, 2).reshape(CONV_K, N_CHIPS * kc)
    o_f = d + CONV_K * kc
    ffn_w = per_chip[:, o_f:o_f + FFN_K * kf].reshape(N_CHIPS, FFN_K, kf).transpose(1, 0, 2).reshape(FFN_K, N_CHIPS * kf)

    n_ada = w_ada.shape[2]
    b_ada_cols = lax.dynamic_slice_in_dim(b_ada, chip * n_ada, n_ada, 1)
    mod_cols = _ada_fwd(c_all, w_ada[0], b_ada_cols)
    mod_all = _allgather8("gather_mod", [mod_cols])[0]
    mod = lax.dynamic_index_in_dim(mod_all[0::2], dev, 1, keepdims=False).reshape(1, N_CHIPS * n_ada)
    shift1, scale1, gate1, shift2, scale2, gate2 = [mod[:, k * d:(k + 1) * d] for k in range(6)]

    h1 = _rowwise("ln_mod1", lambda a, s, t: (_f_modulate(a, s, t),), [(xr, d, 0)],
                  [(scale1 + gather_token[0:1, 0:1], d, 0), (shift1, d, 0)], [(d, BF16)], [], n_rows, tile=FWD_ROW_TILE)[0]
    n_proj = N_CHIPS * w_in.shape[2]
    proj_qk = _mm_nn("mm_in_qk", h1, wg_in, b_in, F32, tile=(2048, 512), window=(0, 2 * d), rows_outer=True)
    proj_rest = _mm_nn("mm_in_rest", h1, wg_in, b_in, BF16, tile=(2048, 512), window=(2 * d, n_proj - 2 * d), rows_outer=True)
    w_all = {}
    for n, zone in zip(others, _gather_wait("gather_weights_wait", gather_handle, proj_rest)):
        own = lax.dynamic_slice_in_dim(w_bf[n], my_c * zone.shape[1], zone.shape[1], 0)
        w_all[n] = lax.dynamic_update_slice_in_dim(zone, own[None], dev, 0)
    wg_up = w_all["w_up"].reshape(N_CHIPS, d, w_up.shape[2])
    wg_conv_out = w_all["w_conv_out"].reshape(1, d, d)
    wg_ret_out = w_all["w_ret_out"].reshape(1, vw, d)
    wg_out = w_all["w_out"].reshape(1, d, d)
    wg_down = w_all["w_down"].reshape(1, D_FF, d)
    a1 = _glu_conv_fwd(proj_rest, 4, 5, conv_w, conv_dw_b, n_rows)
    a2 = _rowwise("conv_norm", lambda a, g, b: (_f_conv_norm(a, g, b),), [(a1, d, 0)], [(conv_ln_g, d, 0), (conv_ln_b, d, 0)],
                  [(d, BF16)], [], n_rows, tile=FWD_ROW_TILE)[0]
    y_a = _mm_nn("mm_conv_out", a2, wg_conv_out, None, F32)
    cos_t, sin_t = _rope_tables(positions[0])
    consts = _retention_consts()
    r, states = _retention_fwd(proj_qk, proj_rest, cos_t, sin_t, consts, n_rows)
    ret_rows = [(r, vw, 0), (proj_rest, vw, 1)]
    ret_vecs = [(ret_gn_g, vw, 0), (ret_gn_b, vw, 0)]
    r2 = _rowwise("ret_norm", _per_head(_f_group_norm_gate), ret_rows, ret_vecs, [(vw, BF16)], [], n_rows)[0]
    y_b = _mm_nn("mm_ret_out", r2, wg_ret_out, None, F32)
    merge_rows = [(proj_rest, d, 6), (proj_rest, d, 7), (y_a, d, 0), (y_b, d, 0)]
    m = _rowwise("merge", lambda *a: (_f_merge(*a),), merge_rows, [], [(d, BF16)], [], n_rows, tile=FWD_ROW_TILE)[0]
    t = _mm_nn("mm_out", m, wg_out, None, F32)
    post1_vecs = [(gate1, d, 0), (ln1_g, d, 0), (ln1_b, d, 0), (scale2, d, 0), (shift2, d, 0)]
    x1, h2 = _rowwise("post1", _f_post1, [(xr, d, 0), (t, d, 0)], post1_vecs, [(d, F32), (d, BF16)], [], n_rows,
                      tile=FWD_ROW_TILE)
    pair_up = lambda c: (c % 2) * 2 + c // 2
    paired = lambda a: a.reshape(a.shape[0], N_CHIPS, kf)[:, jnp.array([0, 2, 1, 3])].reshape(a.shape[0], ffw)
    ffn_w_p, ffn_b_p = paired(ffn_w), paired(ffn_dw_b)
    u = _mm_nn("mm_up", h2, wg_up, None, BF16, gmap=pair_up)
    p, u_filtered = _ffn_fwd(u, ffn_w_p, ffn_b_p, n_rows)
    f = _mm_nn("mm_down", p, wg_down, None, F32)

    def loss_rows(x1v, fv, tv, g2v, lg, lb):
        loss, pull = jax.vjp(lambda a, b, c_, e, h: _f_loss(a, b, c_, e, h, tv), x1v, fv, g2v, lg, lb)
        return (*pull(jnp.ones((), F32)), jnp.full((1, 128), loss, F32))

    dx1_a, df, dgate2, dln2_g, dln2_b, loss_v = _rowwise(
        "loss", loss_rows, [(x1, d, 0), (f, d, 0), (target, d, 0)], [(gate2, d, 0), (ln2_g, d, 0), (ln2_b, d, 0)],
        [(d, F32), (d, BF16)], [d, d, d, 128], n_rows)
    loss = lax.psum(loss_v[0, 0], ("x", "y", "c"))

    dp = _mm_nt("mm_down_dx", df, wg_down, BF16)
    gw_down = _mm_tn("mm_down_dw", p, df, 1)
    du, g_ffn_w, g_ffn_b = _ffn_bwd(u, u_filtered, dp, ffn_w_p, n_rows)
    g_ffn_w, g_ffn_b = paired(g_ffn_w), paired(g_ffn_b)
    dh2 = _mm_nt("mm_up_dx", du, wg_up, BF16, gmap=pair_up)
    gw_up = _mm_tn("mm_up_dw", h2, du, N_CHIPS, gmap=pair_up)

    dx_a, dt, dgate1, dln1_g, dln1_b, dscale2, dshift2 = _rowwise(
        "post1_bwd", _vjp_rows(_f_post1, 2, 2), [(xr, d, 0), (t, d, 0), (dx1_a, d, 0), (dh2, d, 0)], post1_vecs,
        [(d, F32), (d, BF16)], [d] * 5, n_rows)
    dm = _mm_nt("mm_out_dx", dt, wg_out, BF16)
    gw_out = _mm_tn("mm_out_dw", m, dt, 1)
    merge_vjp = _vjp_rows(_f_merge, 4, 1)

    def merge_bwd(*a):
        dga, dgb, dya, dyb = merge_vjp(*a)
        return jnp.concatenate([dga, dgb], axis=1), dya, dyb

    dproj, dya, dyb = _rowwise("merge_bwd", merge_bwd, merge_rows + [(dm, d, 0)], [],
                               [(2 * d, BF16, n_proj, 8192 // (2 * d)), (d, BF16), (d, BF16)], [], n_rows)

    da2 = _mm_nt("mm_conv_out_dx", dya, wg_conv_out, BF16)
    gw_conv_out = _mm_tn("mm_conv_out_dw", a2, dya, 1)
    da1, dcl_g, dcl_b = _rowwise("conv_norm_bwd", _vjp_rows(_f_conv_norm, 1, 1), [(a1, d, 0), (da2, d, 0)],
                                 [(conv_ln_g, d, 0), (conv_ln_b, d, 0)], [(d, F32)], [d, d], n_rows)
    dproj, g_conv_w, g_conv_b = _glu_conv_bwd(proj_rest, 4, 5, da1, conv_w, dproj, 6144 // (2 * d), n_rows)

    dr2 = _mm_nt("mm_ret_out_dx", dyb, wg_ret_out, BF16)
    gw_ret_out = _mm_tn("mm_ret_out_dw", r2, dyb, 1)

    def reduce_start(tag, names, fulls):
        shapes = [given[n].shape[1:] for n in names]
        fulls = [g_.reshape(N_CHIPS, r, c_) for g_, (r, c_) in zip(fulls, shapes)]
        theirs = _sibling_swap("grad_swap_" + tag, fulls)
        pair = []
        for n, (r, c_), g_, t_ in zip(names, shapes, fulls, theirs):
            mine = lax.dynamic_slice_in_dim(g_, my_c * (r // 2), r // 2, 1)
            pair.append(_sum_arrays("grad_pair_sum_" + n, [(mine.reshape(-1, c_), 0), (t_.reshape(-1, c_), 0)],
                                    N_CHIPS * (r // 2)).reshape(N_CHIPS, r // 2, c_))
        handle, token = _alltoall_start("grad_alltoall_start_" + tag, pair)
        own = [lax.dynamic_index_in_dim(p_, chip, 0, keepdims=False) for p_ in pair]
        return (names, shapes, handle, own), token[0:1, 0:1]

    def reduce_finish(tag, started, after):
        names, shapes, handle, own = started
        lands = _alltoall_wait("grad_alltoall_wait_" + tag, handle, after)
        return [_sum_arrays("grad_chip_sum_" + n, [(o_, 0)] + [(l_.reshape(-1, c_), k * (r // 2)) for k in range(3)], r // 2)
                for n, (r, c_), o_, l_ in zip(names, shapes, own, lands)]

    late = ("w_up", "w_down", "w_out", "w_conv_out", "w_ret_out")
    started_late, token_late = reduce_start("late", late, [gw_up, gw_down, gw_out, gw_conv_out, gw_ret_out])
    ret_vecs_bwd = [(ret_gn_g + token_late, vw, 0), (ret_gn_b, vw, 0)]
    dr, dproj, dgn_g, dgn_b = _rowwise(
        "ret_norm_bwd", _per_head(_vjp_rows(_f_group_norm_gate, 2, 1)), ret_rows + [(dr2, vw, 0)], ret_vecs_bwd,
        [(vw, BF16), (vw, BF16, n_proj, 4096 // vw)], [vw, vw], n_rows, into=(dproj, 1))
    dproj = _retention_bwd(proj_qk, proj_rest, cos_t, sin_t, states, dr, consts, dproj, n_rows)

    gw_in, gb_in = _mm_tn("mm_in_dw", h1, dproj, N_CHIPS, colsum=True)
    started_in, token_in = reduce_start("in", ("w_in",), [gw_in])
    dh1 = _mm_nt("mm_in_dx", dproj, wg_in + token_in.astype(BF16), BF16)
    mod_bwd = _vjp_rows(_f_modulate, 1, 1)

    def mod1_bwd(xv, dhv, dxav, sv, tv):
        dx, ds, dsh = mod_bwd(xv, dhv, sv, tv)
        return dx + dxav, ds, dsh

    grad_x, dscale1, dshift1 = _rowwise("ln_mod1_bwd", mod1_bwd, [(xr, d, 0), (dh1, d, 0), (dx_a, d, 0)],
                                        [(scale1 + token_in, d, 0), (shift1, d, 0)], [(d, F32)], [d, d], n_rows)

    dmod = jnp.concatenate([dshift1, dscale1, dgate1, dshift2, dscale2, dgate2], axis=1)
    small_names = ["b_in", "conv_dw_w", "conv_dw_b", "conv_ln_g", "conv_ln_b", "ret_gn_g", "ret_gn_b", "ln1_g", "ln1_b",
                   "ffn_dw_w", "ffn_dw_b", "ln2_g", "ln2_b"]
    small_parts = [gb_in, g_conv_w, g_conv_b, dcl_g, dcl_b, dgn_g, dgn_b, dln1_g, dln1_b, g_ffn_w, g_ffn_b, dln2_g, dln2_b, dmod]
    small_shapes = [a.shape for a in small_parts]
    parts_all = _allgather8("gather_small_grads", [flat_rows(small_parts)])[0]
    part_rows = parts_all.shape[1]
    summed = _sum_arrays("sum_small_grads", [(parts_all.reshape(N_DEV * part_rows, d), k * part_rows) for k in range(N_DEV)],
                         part_rows)
    small_sum = unflatten(summed, small_shapes)
    grads = dict(zip(small_names, small_sum[:-1]))
    grads["b_ada"] = small_sum[-1]
    grads["conv_dw_w"] = lax.dynamic_slice_in_dim(grads["conv_dw_w"], chip * kc, kc, 1)
    grads["ffn_dw_w"] = lax.dynamic_slice_in_dim(grads["ffn_dw_w"], chip * kf, kf, 1)
    o_mod = sum(a.size for a in small_parts[:-1])
    dmod_all = parts_all.reshape(N_DEV, -1)[:, o_mod:o_mod + dmod.shape[1]]
    grads["w_ada"] = _ada_bwd(c_all, lax.dynamic_slice_in_dim(dmod_all, chip * n_ada, n_ada, 1))

    reduced = reduce_finish("late", started_late, grad_x) + reduce_finish("in", started_in, grad_x)
    big_order = late + ("w_in",)
    g_shards = []
    for n, mine_, theirs_ in zip(big_order, reduced, _sibling_exchange("grad_exchange", reduced)):
        both = lax.dynamic_update_slice_in_dim(jnp.stack([theirs_, theirs_]), mine_[None], my_c, 0)
        g_shards.append(both.reshape(given[n].shape[1:]))

    outs = {}
    for n, g_ in zip(big_order, g_shards):
        upd = _adamw("adamw_" + n, given[n][0], g_, given["m_" + n][0], given["v_" + n][0])
        for prefix, val in zip(("grad_", "delta_", "new_m_", "new_v_"), (g_, *upd)):
            outs[prefix + n] = val.reshape(given[n].shape)
    ada = _adamw("adamw_ada", w_ada[0], grads["w_ada"], m_w_ada[0], v_w_ada[0])
    for prefix, val in zip(("grad_", "delta_", "new_m_", "new_v_"), (grads["w_ada"], *ada)):
        outs[prefix + "w_ada"] = val.reshape(w_ada.shape)
    small_all_names = ["b_ada"] + small_names
    small_w_shapes = [given[n].shape for n in small_all_names]
    g_small = flat_rows([grads[n] for n in small_all_names])
    small_upd = _adamw("adamw_small", flat_rows([given[n] for n in small_all_names]), g_small,
                       flat_rows([given["m_" + n] for n in small_all_names]), flat_rows([given["v_" + n] for n in small_all_names]))
    for prefix, packed in zip(("grad_", "delta_", "new_m_", "new_v_"), (g_small, *small_upd)):
        for n, val in zip(small_all_names, unflatten(packed, small_w_shapes)):
            outs[prefix + n] = val

    weights = ["w_ada", "b_ada", "w_in", "b_in", "conv_dw_w", "conv_dw_b", "conv_ln_g", "conv_ln_b", "w_conv_out", "ret_gn_g",
               "ret_gn_b", "w_ret_out", "w_out", "ln1_g", "ln1_b", "w_up", "ffn_dw_w", "ffn_dw_b", "w_down", "ln2_g", "ln2_b"]
    result = [loss, grad_x.reshape(x.shape)]
    for prefix in ("grad_", "delta_", "new_m_", "new_v_"):
        result += [outs[prefix + n] for n in weights]
    return tuple(result)
```

```python
import jax
import jax.numpy as jnp
from jax import lax
from jax.experimental import pallas as pl
from jax.experimental.pallas import tpu as pltpu

F32 = jnp.float32
BF16 = jnp.bfloat16
MESH = pl.DeviceIdType.MESH

D_MODEL = 1024
N_HEADS = 8
DK = 128
DV = 256
CHUNK = 128
ROPE_BASE = 10000.0
D_FF = 2816
CONV_K = 31
FFN_K = 3
LN_EPS = 1e-5
ALPHA = (2.0 * 1) ** 0.25
ADAM_LR = 0.001
ADAM_B1 = 0.9
ADAM_B2 = 0.999
ADAM_EPS = 1e-08
ADAM_WD = 0.01
ADAM_STEP = 10

V7X_VMEM_BYTES = 64 * 1024 * 1024
VMEM_LIMIT = V7X_VMEM_BYTES - 8 * 1024 * 1024
ROW_TILE = 256
FWD_ROW_TILE = 512
MM_TILE = 512
N_CHIPS = 4
N_DEV = 8

BIG_WEIGHTS = ("w_in", "w_up", "w_conv_out", "w_ret_out", "w_out", "w_down")


def _params(n_grid):
    return pltpu.CompilerParams(dimension_semantics=("arbitrary",) * n_grid, vmem_limit_bytes=VMEM_LIMIT)


def _rowwise(name, fn, rows, vecs, outs, reds, n_rows, tile=ROW_TILE, ncol=1, with_col=False, into=None):
    tile = _fit_tile(n_rows, tile)
    n_in = len(rows) + len(vecs)
    n_ref_in = n_in + (into is not None)
    n_out = len(outs)
    outs = [o if len(o) == 4 else (o[0], o[1], o[0] * ncol, 0) for o in outs]

    def col_map(off, row, first_row=0):
        def index(j, i):
            return (i + first_row // tile if row else 0, off(j) if callable(off) else off + j)
        return index

    def body(*refs):
        i = pl.program_id(1)
        vals = [r[...].astype(F32) for r in refs[:n_in]]
        res = fn(pl.program_id(0), *vals) if with_col else fn(*vals)
        for k in range(n_out):
            refs[n_ref_in + k][...] = res[k].astype(refs[n_ref_in + k].dtype)
        for k in range(len(reds)):
            o = refs[n_ref_in + n_out + k]

            @pl.when(i == 0)
            def _():
                o[...] = jnp.zeros_like(o)

            o[...] += res[n_out + k]

    in_specs = [pl.BlockSpec((tile, e[1]), col_map(e[2], True, e[3] if len(e) > 3 else 0)) for e in rows]
    in_specs += [pl.BlockSpec((1, w), col_map(off, False)) for _, w, off in vecs]
    args = [e[0] for e in rows] + [a for a, _, _ in vecs]
    aliases = {}
    if into is not None:
        in_specs.append(pl.BlockSpec(memory_space=pl.ANY))
        args.append(into[0])
        aliases = {n_in: into[1]}
    out_specs = [pl.BlockSpec((tile, w), col_map(off, True)) for w, _, _, off in outs]
    out_specs += [pl.BlockSpec((1, w), lambda j, i: (0, j)) for w in reds]
    out_shape = [jax.ShapeDtypeStruct((n_rows, total), dt) for _, dt, total, _ in outs]
    out_shape += [jax.ShapeDtypeStruct((1, w * ncol), F32) for w in reds]
    return pl.pallas_call(
        body, name=name, grid=(ncol, n_rows // tile), in_specs=in_specs, out_specs=out_specs, out_shape=out_shape,
        input_output_aliases=aliases, compiler_params=_params(2),
    )(*args)


def _vjp_rows(fn, n_row_in, n_ct):
    def bwd(*args):
        prim = [a.astype(F32) for a in args[:n_row_in] + args[n_row_in + n_ct:]]
        cts = tuple(a.astype(F32) for a in args[n_row_in:n_row_in + n_ct])
        _, pull = jax.vjp(fn, *prim)
        return pull(cts if n_ct > 1 else cts[0])

    return bwd


def _fit_tile(n, pref):
    if n <= pref:
        return n
    t = pref - pref % 16
    while n % t:
        t -= 16
    return t


def _col_tile(n):
    return n if n <= 1536 else n // 2


def _same_group(c):
    return c


def _mm_nn(name, a, w, bias, out_dtype, gmap=_same_group, tile=None, window=None, rows_outer=False):
    s, k = a.shape
    g, _, n = w.shape
    tm, tn = (min(MM_TILE, s), _col_tile(n)) if tile is None else (min(tile[0], s), tile[1])
    nt = n // tn
    first, n_cols = (0, g * n) if window is None else window
    t0 = first // tn

    def body(*refs):
        a_ref, w_ref = refs[0], refs[1]
        o_ref = refs[-1]
        acc = jnp.dot(a_ref[...].astype(BF16), w_ref[...], preferred_element_type=F32)
        if bias is not None:
            acc = acc + refs[2][...]
        o_ref[...] = acc.astype(o_ref.dtype)

    order = (lambda i, c: (c, i)) if rows_outer else (lambda c, i: (c, i))
    a_map = lambda *g_: (order(*g_)[1], 0)
    w_map = lambda *g_: (gmap((t0 + order(*g_)[0]) // nt), 0, (t0 + order(*g_)[0]) % nt)
    in_specs = [pl.BlockSpec((tm, k), a_map), pl.BlockSpec((None, k, tn), w_map)]
    args = [a, w]
    if bias is not None:
        in_specs.append(pl.BlockSpec((1, tn), lambda *g_: (0, t0 + order(*g_)[0])))
        args.append(bias)
    grid = (s // tm, n_cols // tn) if rows_outer else (n_cols // tn, s // tm)
    return pl.pallas_call(
        body, name=name, grid=grid, in_specs=in_specs, out_specs=pl.BlockSpec((tm, tn), lambda *g_: order(*g_)[::-1]),
        out_shape=jax.ShapeDtypeStruct((s, n_cols), out_dtype), compiler_params=_params(2),
    )(*args)


def _mm_nt(name, dy, w, out_dtype, gmap=_same_group):
    s = dy.shape[0]
    g, k, n = w.shape
    tm, tn = min(2 * MM_TILE if k <= 1024 else MM_TILE, s), n
    nt = 1
    steps = g

    def body(dy_ref, w_ref, o_ref, acc_ref):
        r = pl.program_id(1)
        part = lax.dot_general(dy_ref[...].astype(BF16), w_ref[...], (((1,), (1,)), ((), ())), preferred_element_type=F32)

        @pl.when(r == 0)
        def _():
            acc_ref[...] = part

        @pl.when(r > 0)
        def _():
            acc_ref[...] += part

        @pl.when(r == steps - 1)
        def _():
            o_ref[...] = acc_ref[...].astype(o_ref.dtype)

    return pl.pallas_call(
        body, name=name, grid=(s // tm, steps),
        in_specs=[pl.BlockSpec((tm, tn), lambda i, r: (i, r)),
                  pl.BlockSpec((None, k, tn), lambda i, r: (gmap(r // nt), 0, r % nt))],
        out_specs=pl.BlockSpec((tm, k), lambda i, r: (i, 0)), out_shape=jax.ShapeDtypeStruct((s, k), out_dtype),
        scratch_shapes=[pltpu.VMEM((tm, k), F32)], compiler_params=_params(2),
    )(dy, w)


def _mm_tn(name, a, dy, g, gmap=_same_group, colsum=False):
    s, k = a.shape
    n = dy.shape[1] // g
    ts = min(2048 if k <= 1024 else 1024, s)
    tn = n if k * n <= 1024 * 1536 else (n // 2 if (n // 2) % 128 == 0 else n)
    nt = n // tn

    def body(a_ref, dy_ref, o_ref, *sum_ref):
        t = pl.program_id(1)
        dyv = dy_ref[...].astype(BF16)
        part = lax.dot_general(a_ref[...].astype(BF16), dyv, (((0,), (0,)), ((), ())), preferred_element_type=F32)

        @pl.when(t == 0)
        def _():
            o_ref[...] = part

        @pl.when(t > 0)
        def _():
            o_ref[...] += part

        if colsum:
            col = jnp.sum(dyv.astype(F32), axis=0, keepdims=True)

            @pl.when(t == 0)
            def _():
                sum_ref[0][...] = col

            @pl.when(t > 0)
            def _():
                sum_ref[0][...] += col

    out_specs = [pl.BlockSpec((None, k, tn), lambda c, t: (gmap(c // nt), 0, c % nt))]
    out_shape = [jax.ShapeDtypeStruct((g, k, n), F32)]
    if colsum:
        out_specs.append(pl.BlockSpec((1, tn), lambda c, t: (0, c)))
        out_shape.append(jax.ShapeDtypeStruct((1, g * n), F32))
    res = pl.pallas_call(
        body, name=name, grid=(g * nt, s // ts),
        in_specs=[pl.BlockSpec((ts, k), lambda c, t: (t, 0)), pl.BlockSpec((ts, tn), lambda c, t: (t, c))],
        out_specs=out_specs, out_shape=out_shape, compiler_params=_params(2),
    )(a, dy)
    return res if colsum else res[0]


SUBLANES = 8


def _tap_sum(read, w_row, offsets, tile):
    acc = None
    for b in range(SUBLANES):
        group = [(k, o) for k, o in offsets if o % SUBLANES == b]
        if not group:
            continue
        rows = tile if b == 0 else tile + SUBLANES
        z = None
        for k, o in group:
            term = w_row(k) * read(o - b, rows)
            z = term if z is None else z + term
        part = z if b == 0 else z[b:b + tile]
        acc = part if acc is None else acc + part
    return acc


def _tap_grads(read, dy, offsets, tile):
    padded = jnp.concatenate([dy, jnp.zeros((SUBLANES, dy.shape[1]), dy.dtype)], axis=0)
    out = {}
    for b in range(SUBLANES):
        group = [(k, o) for k, o in offsets if o % SUBLANES == b]
        if not group:
            continue
        shifted = dy if b == 0 else pltpu.roll(padded, b, 0)
        rows = tile if b == 0 else tile + SUBLANES
        for k, o in group:
            out[k] = jnp.sum(shifted * read(o - b, rows), axis=0, keepdims=True)
    return out


CONV_HALO = 32
CONV_TILE = 256
CONV_LANES = 256


def _glu_conv_fwd(proj, col_v, col_g, w, b, n_rows):
    kw, n_ch = w.shape
    tile = min(CONV_TILE, n_rows)
    per = tile // CONV_HALO
    offsets = [(k, CONV_HALO - (kw - 1) + k) for k in range(kw)]

    def body(v_ref, g_ref, vh_ref, gh_ref, w_ref, b_ref, o_ref, buf):
        i = pl.program_id(0)
        prev = vh_ref[...].astype(F32) * jax.nn.sigmoid(gh_ref[...].astype(F32))
        buf[0:CONV_HALO, :] = jnp.where(i == 0, 0.0, prev)
        buf[CONV_HALO:CONV_HALO + tile, :] = v_ref[...].astype(F32) * jax.nn.sigmoid(g_ref[...].astype(F32))
        for c0 in range(0, n_ch, CONV_LANES):
            cols = slice(c0, c0 + CONV_LANES)
            o_ref[:, cols] = b_ref[:, cols] + _tap_sum(lambda s, n: buf[pl.ds(s, n), cols], lambda k: w_ref[k:k + 1, cols],
                                                      offsets, tile)

    main = lambda col: pl.BlockSpec((tile, n_ch), lambda i: (i, col))
    halo = lambda col: pl.BlockSpec((CONV_HALO, n_ch), lambda i: (jnp.maximum(i * per - 1, 0), col))
    return pl.pallas_call(
        body, name="conv_fwd", grid=(n_rows // tile,),
        in_specs=[main(col_v), main(col_g), halo(col_v), halo(col_g), pl.BlockSpec((kw, n_ch), lambda i: (0, 0)),
                  pl.BlockSpec((1, n_ch), lambda i: (0, 0))],
        out_specs=pl.BlockSpec((tile, n_ch), lambda i: (i, 0)), out_shape=jax.ShapeDtypeStruct((n_rows, n_ch), F32),
        scratch_shapes=[pltpu.VMEM((CONV_HALO + tile, n_ch), F32)], compiler_params=_params(1),
    )(proj, proj, proj, proj, w, b)


def _glu_conv_bwd(proj, col_v, col_g, dy, w, dproj, col_out, n_rows):
    kw, n_ch = w.shape
    tile = min(CONV_TILE, n_rows)
    per = tile // CONV_HALO
    n_tiles = n_rows // tile
    last_halo = n_rows // CONV_HALO - 1
    offsets = [(k, CONV_HALO - (kw - 1) + k) for k in range(kw)]
    back = [(k, kw - 1 - k) for k in range(kw)]

    def body(v_ref, g_ref, vh_ref, gh_ref, dy_ref, dyn_ref, w_ref, dp_any, dx_ref, dw_ref, db_ref, buf, dbuf):
        i = pl.program_id(0)
        cv, cg = v_ref[...].astype(F32), g_ref[...].astype(F32)
        sig = jax.nn.sigmoid(cg)
        prev = vh_ref[...].astype(F32) * jax.nn.sigmoid(gh_ref[...].astype(F32))
        buf[0:CONV_HALO, :] = jnp.where(i == 0, 0.0, prev)
        buf[CONV_HALO:CONV_HALO + tile, :] = cv * sig
        dbuf[0:tile, :] = dy_ref[...]
        dbuf[tile:tile + CONV_HALO, :] = jnp.where(i == n_tiles - 1, 0.0, dyn_ref[...])

        @pl.when(i == 0)
        def _():
            dw_ref[...] = jnp.zeros_like(dw_ref)
            db_ref[...] = jnp.zeros_like(db_ref)

        db_ref[...] += jnp.sum(dy_ref[...], axis=0, keepdims=True)
        for c0 in range(0, n_ch, CONV_LANES):
            cols = slice(c0, c0 + CONV_LANES)
            w_row = lambda k: w_ref[k:k + 1, cols]
            dx = _tap_sum(lambda s, n: dbuf[pl.ds(s, n), cols], w_row, back, tile)
            grads = _tap_grads(lambda s, n: buf[pl.ds(s, n), cols], dy_ref[:, cols], offsets, tile)
            for k in range(kw):
                dw_ref[k:k + 1, cols] += grads[k]
            sg = sig[:, cols]
            dx_ref[:, c0:c0 + CONV_LANES] = (dx * sg).astype(dx_ref.dtype)
            dx_ref[:, n_ch + c0:n_ch + c0 + CONV_LANES] = (dx * cv[:, cols] * sg * (1.0 - sg)).astype(dx_ref.dtype)

    main = lambda col: pl.BlockSpec((tile, n_ch), lambda i: (i, col))
    halo = lambda col: pl.BlockSpec((CONV_HALO, n_ch), lambda i: (jnp.maximum(i * per - 1, 0), col))
    return pl.pallas_call(
        body, name="conv_bwd", grid=(n_tiles,),
        in_specs=[main(col_v), main(col_g), halo(col_v), halo(col_g), main(0),
                  pl.BlockSpec((CONV_HALO, n_ch), lambda i: (jnp.minimum((i + 1) * per, last_halo), 0)),
                  pl.BlockSpec((kw, n_ch), lambda i: (0, 0)), pl.BlockSpec(memory_space=pl.ANY)],
        out_specs=[pl.BlockSpec((tile, 2 * n_ch), lambda i: (i, col_out)), pl.BlockSpec((kw, n_ch), lambda i: (0, 0)),
                   pl.BlockSpec((1, n_ch), lambda i: (0, 0))],
        out_shape=[jax.ShapeDtypeStruct(dproj.shape, dproj.dtype), jax.ShapeDtypeStruct((kw, n_ch), F32),
                   jax.ShapeDtypeStruct((1, n_ch), F32)],
        input_output_aliases={7: 0},
        scratch_shapes=[pltpu.VMEM((CONV_HALO + tile, n_ch), F32), pltpu.VMEM((tile + CONV_HALO, n_ch), F32)],
        compiler_params=_params(1),
    )(proj, proj, proj, proj, dy, dy, w, dproj)


FFN_HALO = 8


def _ffn_fwd(u, w, b, n_rows):
    kw = w.shape[0]
    half = u.shape[1] // 4
    tile = min(ROW_TILE, n_rows)
    per = tile // FFN_HALO
    offsets = [(k, FFN_HALO - (kw - 1) + k) for k in range(kw)]

    def body(u_ref, uh_ref, w_ref, b_ref, p_ref, f_ref, buf):
        i = pl.program_id(1)
        buf[0:FFN_HALO, :] = jnp.where(i == 0, 0.0, uh_ref[...].astype(F32))
        buf[FFN_HALO:FFN_HALO + tile, :] = u_ref[...].astype(F32)
        conv = []
        for c0 in (0, half):
            cols = slice(c0, c0 + half)
            conv.append(b_ref[:, cols] + _tap_sum(lambda s, n: buf[pl.ds(s, n), cols], lambda k: w_ref[k:k + 1, cols],
                                                  offsets, tile))
            f_ref[:, cols] = conv[-1].astype(f_ref.dtype)
        p_ref[...] = (conv[0] * jax.nn.silu(conv[1])).astype(p_ref.dtype)

    return pl.pallas_call(
        body, name="ffn_fwd", grid=(2, n_rows // tile),
        in_specs=[pl.BlockSpec((tile, 2 * half), lambda j, i: (i, j)),
                  pl.BlockSpec((FFN_HALO, 2 * half), lambda j, i: (jnp.maximum(i * per - 1, 0), j)),
                  pl.BlockSpec((kw, 2 * half), lambda j, i: (0, j)), pl.BlockSpec((1, 2 * half), lambda j, i: (0, j))],
        out_specs=[pl.BlockSpec((tile, half), lambda j, i: (i, j)), pl.BlockSpec((tile, 2 * half), lambda j, i: (i, j))],
        out_shape=[jax.ShapeDtypeStruct((n_rows, 2 * half), BF16), jax.ShapeDtypeStruct(u.shape, BF16)],
        scratch_shapes=[pltpu.VMEM((FFN_HALO + tile, 2 * half), F32)], compiler_params=_params(2),
    )(u, u, w, b)


def _ffn_bwd(u, filtered, dp, w, n_rows):
    kw = w.shape[0]
    half = u.shape[1] // 4
    tile = min(ROW_TILE, n_rows)
    per = tile // FFN_HALO
    n_tiles = n_rows // tile
    last_halo = n_rows // FFN_HALO - 1
    ext = tile + FFN_HALO
    offsets = [(k, FFN_HALO - (kw - 1) + k) for k in range(kw)]
    back = [(k, kw - 1 - k) for k in range(kw)]

    def body(u_ref, up_ref, f_ref, fn_ref, dp_ref, dpn_ref, w_ref, du_ref, dw_ref, db_ref, buf, dbuf):
        i = pl.program_id(1)
        buf[0:FFN_HALO, :] = jnp.where(i == 0, 0.0, up_ref[...].astype(F32))
        buf[FFN_HALO:FFN_HALO + tile, :] = u_ref[...].astype(F32)
        filt = jnp.concatenate([f_ref[...], fn_ref[...]], axis=0).astype(F32)
        val, gate = filt[:, 0:half], filt[:, half:2 * half]
        dpe = jnp.concatenate([dp_ref[...].astype(F32), jnp.where(i == n_tiles - 1, 0.0, dpn_ref[...].astype(F32))], axis=0)
        sig = jax.nn.sigmoid(gate)
        dbuf[:, 0:half] = dpe * gate * sig
        dbuf[:, half:2 * half] = dpe * val * sig * (1.0 + gate * (1.0 - sig))

        @pl.when(i == 0)
        def _():
            dw_ref[...] = jnp.zeros_like(dw_ref)
            db_ref[...] = jnp.zeros_like(db_ref)

        for c0 in (0, half):
            cols = slice(c0, c0 + half)
            du_ref[:, cols] = _tap_sum(lambda s, n: dbuf[pl.ds(s, n), cols], lambda k: w_ref[k:k + 1, cols], back,
                                       tile).astype(du_ref.dtype)
            d_main = dbuf[0:tile, cols]
            db_ref[:, cols] += jnp.sum(d_main, axis=0, keepdims=True)
            grads = _tap_grads(lambda s, n: buf[pl.ds(s, n), cols], d_main, offsets, tile)
            for k in range(kw):
                dw_ref[k:k + 1, cols] += grads[k]

    wide = 2 * half
    return pl.pallas_call(
        body, name="ffn_bwd", grid=(2, n_tiles),
        in_specs=[pl.BlockSpec((tile, wide), lambda j, i: (i, j)),
                  pl.BlockSpec((FFN_HALO, wide), lambda j, i: (jnp.maximum(i * per - 1, 0), j)),
                  pl.BlockSpec((tile, wide), lambda j, i: (i, j)),
                  pl.BlockSpec((FFN_HALO, wide), lambda j, i: (jnp.minimum((i + 1) * per, last_halo), j)),
                  pl.BlockSpec((tile, half), lambda j, i: (i, j)),
                  pl.BlockSpec((FFN_HALO, half), lambda j, i: (jnp.minimum((i + 1) * per, last_halo), j)),
                  pl.BlockSpec((kw, wide), lambda j, i: (0, j))],
        out_specs=[pl.BlockSpec((tile, wide), lambda j, i: (i, j)), pl.BlockSpec((kw, wide), lambda j, i: (0, j)),
                   pl.BlockSpec((1, wide), lambda j, i: (0, j))],
        out_shape=[jax.ShapeDtypeStruct(u.shape, BF16), jax.ShapeDtypeStruct((kw, u.shape[1]), F32),
                   jax.ShapeDtypeStruct((1, u.shape[1]), F32)],
        scratch_shapes=[pltpu.VMEM((FFN_HALO + tile, wide), F32), pltpu.VMEM((ext, wide), F32)], compiler_params=_params(2),
    )(u, u, filtered, filtered, dp, dp, w)


def _retention_consts():
    log_gamma = jnp.log(1.0 - 2.0 ** (-5.0 - jnp.arange(N_HEADS, dtype=F32)))
    idx = jnp.arange(CHUNK, dtype=F32)
    rel = idx[:, None] - idx[None, :]
    decay = jnp.where(rel[None] >= 0, jnp.exp(log_gamma[:, None, None] * jnp.maximum(rel, 0.0)[None]), 0.0)
    zeta = jnp.exp(log_gamma[:, None] * (CHUNK - 1.0 - idx)[None])
    xi = jnp.exp(log_gamma[:, None] * (idx + 1.0)[None])
    chunk_decay = jnp.exp(log_gamma * CHUNK)
    xi_b = jnp.broadcast_to(xi[:, :, None], (N_HEADS, CHUNK, DK))
    zeta_b = jnp.broadcast_to(zeta[:, :, None], (N_HEADS, CHUNK, DK))
    cd_b = jnp.broadcast_to(chunk_decay[:, None, None], (N_HEADS, 8, DV))
    return decay, xi_b, zeta_b, cd_b


def _rope_tables(positions):
    half = DK // 2
    inv_freq = ROPE_BASE ** (-jnp.arange(half, dtype=F32) / half)
    ang = positions.astype(F32)[:, None] * inv_freq
    cos, sin = jnp.cos(ang), jnp.sin(ang)
    return jnp.concatenate([cos, cos], axis=-1), jnp.concatenate([-sin, sin], axis=-1)


def _swap_halves(v):
    return pltpu.roll(v, DK // 2, 1)


def _dot(a, b):
    return jnp.dot(a, b, preferred_element_type=F32)


def _dot_nt(a, b):
    return lax.dot_general(a, b, (((1,), (1,)), ((), ())), preferred_element_type=F32)


def _dot_tn(a, b):
    return lax.dot_general(a, b, (((0,), (0,)), ((), ())), preferred_element_type=F32)


def _const_specs():
    return [pl.BlockSpec((N_HEADS, CHUNK, CHUNK), lambda n: (0, 0, 0)), pl.BlockSpec((N_HEADS, CHUNK, DK), lambda n: (0, 0, 0)),
            pl.BlockSpec((N_HEADS, CHUNK, DK), lambda n: (0, 0, 0)), pl.BlockSpec((N_HEADS, 8, DV), lambda n: (0, 0, 0))]


def _retention_fwd(proj_qk, proj_rest, cos_t, sin_t, consts, n_rows):
    n_chunks = n_rows // CHUNK
    scale = DK ** -0.5

    def body(q_ref, k_ref, v_ref, cf_ref, ss_ref, d_ref, xi_ref, zt_ref, cd_ref, r_ref, st_ref, state):
        @pl.when(pl.program_id(0) == 0)
        def _():
            state[...] = jnp.zeros_like(state)

        cf, ss = cf_ref[...], ss_ref[...]
        for h in range(N_HEADS):
            qh = q_ref[:, h * DK:(h + 1) * DK].astype(F32)
            kh = k_ref[:, h * DK:(h + 1) * DK].astype(F32)
            qh = qh * cf + _swap_halves(qh) * ss
            kh = (kh * cf + _swap_halves(kh) * ss) * scale
            vh = v_ref[:, h * DV:(h + 1) * DV].astype(BF16)
            st = state[h]
            st_ref[0, h] = st
            sd = _dot_nt(qh.astype(BF16), kh.astype(BF16)) * d_ref[h]
            inner = _dot(sd.astype(BF16), vh)
            cross = _dot((qh * xi_ref[h]).astype(BF16), st.astype(BF16))
            kv = _dot_tn((kh * zt_ref[h]).astype(BF16), vh)
            state[h] = st * cd_ref[h, 0:1, :] + kv
            r_ref[:, h * DV:(h + 1) * DV] = inner + cross

    qk = N_HEADS * DK
    vw = N_HEADS * DV
    return pl.pallas_call(
        body, name="retention_fwd", grid=(n_chunks,),
        in_specs=[pl.BlockSpec((CHUNK, qk), lambda n: (n, 0)), pl.BlockSpec((CHUNK, qk), lambda n: (n, 1)),
                  pl.BlockSpec((CHUNK, vw), lambda n: (n, 0)), pl.BlockSpec((CHUNK, DK), lambda n: (n, 0)),
                  pl.BlockSpec((CHUNK, DK), lambda n: (n, 0))] + _const_specs(),
        out_specs=[pl.BlockSpec((CHUNK, vw), lambda n: (n, 0)), pl.BlockSpec((1, N_HEADS, DK, DV), lambda n: (n, 0, 0, 0))],
        out_shape=[jax.ShapeDtypeStruct((n_rows, vw), F32), jax.ShapeDtypeStruct((n_chunks, N_HEADS, DK, DV), F32)],
        scratch_shapes=[pltpu.VMEM((N_HEADS, DK, DV), F32)], compiler_params=_params(1),
    )(proj_qk, proj_qk, proj_rest, cos_t, sin_t, *consts)


def _retention_bwd(proj_qk, proj_rest, cos_t, sin_t, states, dr, consts, dproj, n_rows):
    n_chunks = n_rows // CHUNK
    scale = DK ** -0.5
    qk = N_HEADS * DK
    vw = N_HEADS * DV

    def body(q_ref, k_ref, v_ref, cf_ref, ss_ref, st_ref, dr_ref, d_ref, xi_ref, zt_ref, cd_ref, dp_any, dqkv_ref, g_ref):
        dq_ref, dk_ref, dv_ref = dqkv_ref.at[:, 0:qk], dqkv_ref.at[:, qk:2 * qk], dqkv_ref.at[:, 2 * qk:2 * qk + vw]

        @pl.when(pl.program_id(0) == 0)
        def _():
            g_ref[...] = jnp.zeros_like(g_ref)

        cf, ss = cf_ref[...], ss_ref[...]
        for h in range(N_HEADS):
            qh = q_ref[:, h * DK:(h + 1) * DK].astype(F32)
            kh = k_ref[:, h * DK:(h + 1) * DK].astype(F32)
            qh = qh * cf + _swap_halves(qh) * ss
            kh = (kh * cf + _swap_halves(kh) * ss) * scale
            qb, kb = qh.astype(BF16), kh.astype(BF16)
            vh = v_ref[:, h * DV:(h + 1) * DV].astype(BF16)
            do = dr_ref[:, h * DV:(h + 1) * DV].astype(BF16)
            rb = st_ref[0, h].astype(BF16)
            g = g_ref[h]
            gb = g.astype(BF16)
            dec, xi, zt = d_ref[h], xi_ref[h], zt_ref[h]
            sd = (_dot_nt(qb, kb) * dec).astype(BF16)
            ds = (_dot_nt(do, vh) * dec).astype(BF16)
            dqh = _dot(ds, kb) + _dot_nt(do, rb) * xi
            dkh = (_dot_tn(ds, qb) + _dot_nt(vh, gb) * zt) * scale
            dvh = _dot_tn(sd, do) + _dot((kh * zt).astype(BF16), gb)
            g_ref[h] = g * cd_ref[h, 0:1, :] + _dot_tn((qh * xi).astype(BF16), do)
            dq_ref[:, h * DK:(h + 1) * DK] = (dqh * cf + _swap_halves(dqh * ss)).astype(dq_ref.dtype)
            dk_ref[:, h * DK:(h + 1) * DK] = (dkh * cf + _swap_halves(dkh * ss)).astype(dk_ref.dtype)
            dv_ref[:, h * DV:(h + 1) * DV] = dvh.astype(dv_ref.dtype)

    last = n_chunks - 1
    return pl.pallas_call(
        body, name="retention_bwd", grid=(n_chunks,),
        in_specs=[pl.BlockSpec((CHUNK, qk), lambda n: (last - n, 0)), pl.BlockSpec((CHUNK, qk), lambda n: (last - n, 1)),
                  pl.BlockSpec((CHUNK, vw), lambda n: (last - n, 0)), pl.BlockSpec((CHUNK, DK), lambda n: (last - n, 0)),
                  pl.BlockSpec((CHUNK, DK), lambda n: (last - n, 0)),
                  pl.BlockSpec((1, N_HEADS, DK, DV), lambda n: (last - n, 0, 0, 0)),
                  pl.BlockSpec((CHUNK, vw), lambda n: (last - n, 0))] + _const_specs() + [pl.BlockSpec(memory_space=pl.ANY)],
        out_specs=pl.BlockSpec((CHUNK, 2 * qk + vw), lambda n: (last - n, 0)),
        out_shape=jax.ShapeDtypeStruct(dproj.shape, dproj.dtype), input_output_aliases={11: 0},
        scratch_shapes=[pltpu.VMEM((N_HEADS, DK, DV), F32)], compiler_params=_params(1),
    )(proj_qk, proj_qk, proj_rest, cos_t, sin_t, states, dr, *consts, dproj)


def _ln(v):
    mu = jnp.mean(v, axis=-1, keepdims=True)
    var = jnp.mean(jnp.square(v - mu), axis=-1, keepdims=True)
    return (v - mu) * lax.rsqrt(var + LN_EPS)


def _f_modulate(x, scale, shift):
    return _ln(x) * (1.0 + scale) + shift


def _f_conv_norm(a1, g, b):
    return jax.nn.silu(_ln(a1) * g + b)


def _f_group_norm_gate(r, gate, g, b):
    return (_ln(r) * g + b) * jax.nn.silu(gate)


def _per_head(fn):
    def run(*arrays):
        parts = [fn(*[a[:, h * DV:(h + 1) * DV] for a in arrays]) for h in range(N_HEADS)]
        if not isinstance(parts[0], (tuple, list)):
            return (jnp.concatenate(parts, axis=1),)
        return tuple(jnp.concatenate([p[k] for p in parts], axis=1) for k in range(len(parts[0])))

    return run


def _f_merge(ga, gb, ya, yb):
    return jax.nn.sigmoid(ga) * ya + jax.nn.sigmoid(gb) * yb


def _f_post1(x, t, gate1, g1, b1, scale2, shift2):
    x1 = _ln(ALPHA * x + gate1 * t) * g1 + b1
    return x1, _ln(x1) * (1.0 + scale2) + shift2


def _f_loss(x1, f, gate2, g2, b2, target):
    y = _ln(ALPHA * x1 + gate2 * f) * g2 + b2
    return 0.5 * jnp.sum(jnp.mean(jnp.square(y - target), axis=-1))


ANY = pl.BlockSpec(memory_space=pl.ANY)


def _allgather8(name, blocks, own_half=False):
    n = len(blocks)
    rows = [b.shape[0] // 2 if own_half else b.shape[0] for b in blocks]

    def body(*refs):
        x_refs, out_refs = refs[:n], refs[n:2 * n]
        send_sems, recv_sems = refs[2 * n:]
        x, y, c = lax.axis_index("x"), lax.axis_index("y"), lax.axis_index("c")
        me, sibling = (x, y, c), (x, y, 1 - c)
        chips = [(1 - x, y), (x, 1 - y), (1 - x, 1 - y)]
        every = range(n)

        def src(a):
            return x_refs[a].at[pl.ds(c * rows[a], rows[a])] if own_half else x_refs[a]

        def slot(a, px, py, pc):
            return out_refs[a].at[4 * px + 2 * py + pc]

        def copy(k, a, block, to, from_input=False):
            return pltpu.make_async_remote_copy(
                src_ref=src(a) if from_input else slot(a, *block), dst_ref=slot(a, *block), send_sem=send_sems.at[k, a],
                recv_sem=recv_sems.at[k, a], device_id=to, device_id_type=MESH)

        first = [copy(0, a, me, sibling, True) for a in every]
        first += [copy(1 + j, a, me, (*chip, c), True) for j, chip in enumerate(chips) for a in every]
        for cp in first:
            cp.start()
        passed = [[copy(4 + j, a, (*chip, c), sibling) for a in every] for j, chip in enumerate(chips)]
        for j, chip in enumerate(chips):
            for a in every:
                copy(1 + j, a, (*chip, c), me).wait_recv()
            for cp in passed[j]:
                cp.start()
        for a in every:
            copy(0, a, sibling, me).wait_recv()
        for j, chip in enumerate(chips):
            for a in every:
                copy(4 + j, a, (*chip, 1 - c), me).wait_recv()
        for cp in first + [cp for group in passed for cp in group]:
            cp.wait_send()

    gathered = pl.pallas_call(
        body, name=name, in_specs=[ANY] * n, out_specs=[ANY] * n,
        out_shape=[jax.ShapeDtypeStruct((N_DEV, r, b.shape[1]), b.dtype) for r, b in zip(rows, blocks)],
        scratch_shapes=[pltpu.SemaphoreType.DMA((7, n)), pltpu.SemaphoreType.DMA((7, n))],
    )(*blocks)
    c = lax.axis_index("c")
    me = 4 * lax.axis_index("x") + 2 * lax.axis_index("y") + c
    own = [lax.dynamic_slice_in_dim(b, c * r, r, 0) if own_half else b for r, b in zip(rows, blocks)]
    return [lax.dynamic_update_slice_in_dim(g_, o_[None], me, 0) for g_, o_ in zip(gathered, own)]


def _sibling_swap(name, arrays):
    n = len(arrays)
    halves = [a.shape[1] // 2 for a in arrays]

    def body(*refs):
        g_refs, theirs = refs[:n], refs[n:2 * n]
        send_sems, recv_sems = refs[2 * n:]
        x, y, c = lax.axis_index("x"), lax.axis_index("y"), lax.axis_index("c")
        remote = [pltpu.make_async_remote_copy(
            src_ref=g_refs[a].at[:, pl.ds((1 - c) * halves[a], halves[a]), :], dst_ref=theirs[a], send_sem=send_sems.at[a],
            recv_sem=recv_sems.at[a], device_id=(x, y, 1 - c), device_id_type=MESH) for a in range(n)]
        for cp in remote:
            cp.start()
        for cp in remote:
            cp.wait_recv()
        for cp in remote:
            cp.wait_send()

    return pl.pallas_call(
        body, name=name, in_specs=[ANY] * n, out_specs=[ANY] * n,
        out_shape=[jax.ShapeDtypeStruct((a.shape[0], h, a.shape[2]), a.dtype) for a, h in zip(arrays, halves)],
        scratch_shapes=[pltpu.SemaphoreType.DMA((n,)), pltpu.SemaphoreType.DMA((n,))],
    )(*arrays)


HBM = pl.BlockSpec(memory_space=pltpu.HBM)
SEM = pl.BlockSpec(memory_space=pltpu.SEMAPHORE)
DATAFLOW = pltpu.SideEffectType.DATAFLOW_SIDE_EFFECTING


def _chip_peers(x, y):
    return [(1 - x, y), (x, 1 - y), (1 - x, 1 - y)]


def _other_devices(x, y, c):
    flip = lambda v, f: 1 - v if f else v
    return [(flip(x, k & 4), flip(y, k & 2), flip(c, k & 1)) for k in range(1, N_DEV)]


def _gather_start(name, arrays):
    n = len(arrays)
    rows = [a.shape[0] // 2 for a in arrays]
    zones = [lax.empty((N_DEV, r, a.shape[1]), a.dtype) for r, a in zip(rows, arrays)]

    def body(*refs):
        x_refs, zone_refs = refs[:n], refs[n:2 * n]
        send_sems, recv_sems = refs[2 * n], refs[2 * n + 1]
        token = refs[-1]
        x, y, c = lax.axis_index("x"), lax.axis_index("y"), lax.axis_index("c")
        for k, peer in enumerate(_other_devices(x, y, c)):
            for a in range(n):
                pltpu.make_async_remote_copy(
                    src_ref=x_refs[a].at[pl.ds(c * rows[a], rows[a])], dst_ref=zone_refs[a].at[4 * x + 2 * y + c],
                    send_sem=send_sems.at[k * n + a], recv_sem=recv_sems.at[k * n + a], device_id=peer,
                    device_id_type=MESH).start()
        token[...] = jnp.zeros_like(token)

    thru = [pltpu.HBM(a.shape, a.dtype) for a in arrays] + [pltpu.HBM(z.shape, z.dtype) for z in zones]
    n_sems = (N_DEV - 1) * n
    res = pl.pallas_call(
        body, name=name, in_specs=[HBM] * (2 * n),
        out_specs=(SEM, SEM, *[HBM] * (2 * n), pl.BlockSpec(memory_space=pltpu.VMEM)),
        out_shape=(pltpu.SemaphoreType.DMA((n_sems,)), pltpu.SemaphoreType.DMA((n_sems,)), *thru,
                   jax.ShapeDtypeStruct((8, 128), F32)),
        input_output_aliases={i: 2 + i for i in range(2 * n)},
        compiler_params=pltpu.CompilerParams(has_side_effects=DATAFLOW),
    )(*[pltpu.with_memory_space_constraint(a, pltpu.HBM) for a in arrays],
      *[pltpu.with_memory_space_constraint(z, pltpu.HBM) for z in zones])
    return (res[0], res[1], list(res[2:2 + n]), list(res[2 + n:2 + 2 * n])), res[-1]


def _gather_wait(name, handle, after):
    send_sems, recv_sems, sources, zones = handle
    n = len(sources)
    rows = [z.shape[1] for z in zones]

    def body(*refs):
        x_refs, zone_refs = refs[:n], refs[n:2 * n]
        send_sems, recv_sems = refs[2 * n], refs[2 * n + 1]
        x, y, c = lax.axis_index("x"), lax.axis_index("y"), lax.axis_index("c")
        for k, (px, py, pc) in enumerate(_other_devices(x, y, c)):
            for a in range(n):
                cp = pltpu.make_async_remote_copy(
                    src_ref=x_refs[a].at[pl.ds(c * rows[a], rows[a])], dst_ref=zone_refs[a].at[4 * px + 2 * py + pc],
                    send_sem=send_sems.at[k * n + a], recv_sem=recv_sems.at[k * n + a], device_id=(px, py, pc),
                    device_id_type=MESH)
                cp.wait_send()
                cp.wait_recv()

    res = pl.pallas_call(
        body, name=name, in_specs=[HBM] * (2 * n) + [SEM, SEM, ANY], out_specs=[HBM] * (2 * n),
        out_shape=[pltpu.HBM(a.shape, a.dtype) for a in sources + zones],
        input_output_aliases={i: i for i in range(2 * n)}, compiler_params=pltpu.CompilerParams(has_side_effects=DATAFLOW),
    )(*sources, *zones, send_sems, recv_sems, after)
    return list(res[n:])


def _alltoall_start(name, arrays):
    n = len(arrays)
    lands = [lax.empty((3,) + a.shape[1:], a.dtype) for a in arrays]

    def body(*refs):
        p_refs, land_refs = refs[:n], refs[n:2 * n]
        send_sems, recv_sems = refs[2 * n], refs[2 * n + 1]
        token = refs[-1]
        x, y, c = lax.axis_index("x"), lax.axis_index("y"), lax.axis_index("c")
        for k, (px, py) in enumerate(_chip_peers(x, y)):
            for a in range(n):
                pltpu.make_async_remote_copy(
                    src_ref=p_refs[a].at[2 * px + py], dst_ref=land_refs[a].at[k], send_sem=send_sems.at[k * n + a],
                    recv_sem=recv_sems.at[k * n + a], device_id=(px, py, c), device_id_type=MESH).start()
        token[...] = jnp.zeros_like(token)

    thru = [pltpu.HBM(a.shape, a.dtype) for a in arrays] + [pltpu.HBM(l.shape, l.dtype) for l in lands]
    res = pl.pallas_call(
        body, name=name, in_specs=[HBM] * (2 * n),
        out_specs=(SEM, SEM, *[HBM] * (2 * n), pl.BlockSpec(memory_space=pltpu.VMEM)),
        out_shape=(pltpu.SemaphoreType.DMA((3 * n,)), pltpu.SemaphoreType.DMA((3 * n,)), *thru, jax.ShapeDtypeStruct((8, 128), F32)),
        input_output_aliases={i: 2 + i for i in range(2 * n)},
        compiler_params=pltpu.CompilerParams(has_side_effects=DATAFLOW),
    )(*[pltpu.with_memory_space_constraint(a, pltpu.HBM) for a in arrays],
      *[pltpu.with_memory_space_constraint(l, pltpu.HBM) for l in lands])
    return (res[0], res[1], list(res[2:2 + n]), list(res[2 + n:2 + 2 * n])), res[-1]


def _alltoall_wait(name, handle, after):
    send_sems, recv_sems, sources, lands = handle
    n = len(sources)

    def body(*refs):
        p_refs, land_refs = refs[:n], refs[n:2 * n]
        send_sems, recv_sems = refs[2 * n], refs[2 * n + 1]
        x, y, c = lax.axis_index("x"), lax.axis_index("y"), lax.axis_index("c")
        for k, (px, py) in enumerate(_chip_peers(x, y)):
            for a in range(n):
                cp = pltpu.make_async_remote_copy(
                    src_ref=p_refs[a].at[2 * px + py], dst_ref=land_refs[a].at[k], send_sem=send_sems.at[k * n + a],
                    recv_sem=recv_sems.at[k * n + a], device_id=(px, py, c), device_id_type=MESH)
                cp.wait_send()
                cp.wait_recv()

    res = pl.pallas_call(
        body, name=name, in_specs=[HBM] * (2 * n) + [SEM, SEM, ANY], out_specs=[HBM] * (2 * n),
        out_shape=[pltpu.HBM(a.shape, a.dtype) for a in sources + lands],
        input_output_aliases={i: i for i in range(2 * n)}, compiler_params=pltpu.CompilerParams(has_side_effects=DATAFLOW),
    )(*sources, *lands, send_sems, recv_sems, after)
    return list(res[n:])


def _sibling_exchange(name, halves):
    n = len(halves)

    def body(*refs):
        h_refs, out_refs = refs[:n], refs[n:2 * n]
        send_sems, recv_sems = refs[2 * n:]
        x, y, c = lax.axis_index("x"), lax.axis_index("y"), lax.axis_index("c")
        remote = [pltpu.make_async_remote_copy(
            src_ref=h_refs[a], dst_ref=out_refs[a], send_sem=send_sems.at[a], recv_sem=recv_sems.at[a],
            device_id=(x, y, 1 - c), device_id_type=MESH) for a in range(n)]
        for cp in remote:
            cp.start()
        for cp in remote:
            cp.wait_recv()
        for cp in remote:
            cp.wait_send()

    return pl.pallas_call(
        body, name=name, in_specs=[ANY] * n, out_specs=[ANY] * n,
        out_shape=[jax.ShapeDtypeStruct(h.shape, h.dtype) for h in halves],
        scratch_shapes=[pltpu.SemaphoreType.DMA((n,)), pltpu.SemaphoreType.DMA((n,))],
    )(*halves)


def _sum_arrays(name, terms, n_rows):
    c_ = terms[0][0].shape[1]

    def add_all(*vals):
        acc = vals[0]
        for v in vals[1:]:
            acc = acc + v
        return (acc,)

    return _rowwise(name, add_all, [(a, c_, 0, first) for a, first in terms], [], [(c_, F32)], [], n_rows, tile=128)[0]


def _adamw_fn(w, g, m, v):
    m = ADAM_B1 * m + (1.0 - ADAM_B1) * g
    v = ADAM_B2 * v + (1.0 - ADAM_B2) * jnp.square(g)
    m_hat = m / (1.0 - ADAM_B1 ** ADAM_STEP)
    v_hat = v / (1.0 - ADAM_B2 ** ADAM_STEP)
    delta = -ADAM_LR * (m_hat / (jnp.sqrt(v_hat) + ADAM_EPS) + ADAM_WD * w)
    return delta, m, v


def _adamw(name, w, g, m, v):
    r, c_ = w.shape
    return _rowwise(name, _adamw_fn, [(w, c_, 0), (g, c_, 0), (m, c_, 0), (v, c_, 0)], [], [(c_, F32)] * 3, [], r, tile=128)


def _ada_fwd(c_all, w_ada, b_ada):
    n = w_ada.shape[1]

    def body(c_ref, w_ref, b_ref, o_ref):
        o_ref[...] = jnp.dot(jax.nn.silu(c_ref[...]).astype(BF16), w_ref[...].astype(BF16),
                             preferred_element_type=F32) + b_ref[...]

    return pl.pallas_call(body, name="ada_fwd", out_shape=jax.ShapeDtypeStruct((N_DEV, n), F32),
                          compiler_params=pltpu.CompilerParams(vmem_limit_bytes=VMEM_LIMIT))(c_all, w_ada, b_ada)


def _ada_bwd(c_all, dmod_cols):
    d = c_all.shape[1]
    n = dmod_cols.shape[1]

    def body(c_ref, dm_ref, gw_ref):
        gw_ref[...] = lax.dot_general(jax.nn.silu(c_ref[...]).astype(BF16), dm_ref[...].astype(BF16),
                                      (((0,), (0,)), ((), ())), preferred_element_type=F32)

    return pl.pallas_call(body, name="ada_bwd", out_shape=jax.ShapeDtypeStruct((d, n), F32),
                          compiler_params=pltpu.CompilerParams(vmem_limit_bytes=VMEM_LIMIT))(c_all, dmod_cols)


def _cast_bf16(name, a):
    r, c_ = a.shape
    return _rowwise(name, lambda v: (v,), [(a, c_, 0)], [], [(c_, BF16)], [], r)[0]


def _pad_rows(vec, mult):
    n = vec.shape[0]
    return jnp.pad(vec, (0, (-n) % mult))


def kernel(x, c, positions, w_ada, b_ada, w_in, b_in, conv_dw_w, conv_dw_b, conv_ln_g, conv_ln_b, w_conv_out, ret_gn_g, ret_gn_b, w_ret_out, w_out, ln1_g, ln1_b, w_up, ffn_dw_w, ffn_dw_b, w_down, ln2_g, ln2_b, loss_target, m_w_ada, m_b_ada, m_w_in, m_b_in, m_conv_dw_w, m_conv_dw_b, m_conv_ln_g, m_conv_ln_b, m_w_conv_out, m_ret_gn_g, m_ret_gn_b, m_w_ret_out, m_w_out, m_ln1_g, m_ln1_b, m_w_up, m_ffn_dw_w, m_ffn_dw_b, m_w_down, m_ln2_g, m_ln2_b, v_w_ada, v_b_ada, v_w_in, v_b_in, v_conv_dw_w, v_conv_dw_b, v_conv_ln_g, v_conv_ln_b, v_w_conv_out, v_ret_gn_g, v_ret_gn_b, v_w_ret_out, v_w_out, v_ln1_g, v_ln1_b, v_w_up, v_ffn_dw_w, v_ffn_dw_b, v_w_down, v_ln2_g, v_ln2_b):
    given = dict(locals())
    n_rows = x.shape[1]
    d = D_MODEL
    my_c = lax.axis_index("c")
    chip = 2 * lax.axis_index("x") + lax.axis_index("y")
    dev = 2 * chip + my_c
    xr = x[0]
    target = loss_target[0]
    vw = N_HEADS * DV
    ffw = 2 * D_FF

    def flat_rows(arrays, mult=8):
        v = jnp.concatenate([a.reshape(-1) for a in arrays])
        return _pad_rows(v, mult * d).reshape(-1, d)

    def unflatten(flat2d, shapes):
        v, out, o = flat2d.reshape(-1), [], 0
        for shp in shapes:
            size = 1
            for e in shp:
                size *= e
            out.append(v[o:o + size].reshape(shp))
            o += size
        return out

    w_bf = {n: _cast_bf16("cast_" + n, given[n][0]) for n in BIG_WEIGHTS}
    wg_in = _allgather8("gather_w_in", [w_bf["w_in"]], own_half=True)[0].reshape(N_CHIPS, d, w_in.shape[2])
    others = [n for n in BIG_WEIGHTS if n != "w_in"]
    probe = wg_in[0, 0:1, 0:1].astype(F32)
    zero = (jnp.where(jnp.isfinite(probe), probe, 0.0) * 0.0).astype(BF16)
    gather_handle, gather_token = _gather_start("gather_weights_start", [w_bf[n] + zero for n in others])

    kc, kf = conv_dw_w.shape[2], ffn_dw_w.shape[2]
    small_all = _allgather8("gather_small", [flat_rows([c, conv_dw_w, ffn_dw_w])])[0].reshape(N_DEV, -1)
    c_all = small_all[:, :d]
    per_chip = small_all[0::2]
    conv_w = per_chip[:, d:d + CONV_K * kc].reshape(N_CHIPS, CONV_K, kc).transpose(1, 0, 2).reshape(CONV_K, N_CHIPS * kc)
    o_f = d + CONV_K * kc
    ffn_w = per_chip[:, o_f:o_f + FFN_K * kf].reshape(N_CHIPS, FFN_K, kf).transpose(1, 0, 2).reshape(FFN_K, N_CHIPS * kf)

    n_ada = w_ada.shape[2]
    b_ada_cols = lax.dynamic_slice_in_dim(b_ada, chip * n_ada, n_ada, 1)
    mod_cols = _ada_fwd(c_all, w_ada[0], b_ada_cols)
    mod_all = _allgather8("gather_mod", [mod_cols])[0]
    mod = lax.dynamic_index_in_dim(mod_all[0::2], dev, 1, keepdims=False).reshape(1, N_CHIPS * n_ada)
    shift1, scale1, gate1, shift2, scale2, gate2 = [mod[:, k * d:(k + 1) * d] for k in range(6)]

    h1 = _rowwise("ln_mod1", lambda a, s, t: (_f_modulate(a, s, t),), [(xr, d, 0)],
                  [(scale1 + gather_token[0:1, 0:1], d, 0), (shift1, d, 0)], [(d, BF16)], [], n_rows, tile=FWD_ROW_TILE)[0]
    n_proj = N_CHIPS * w_in.shape[2]
    proj_qk = _mm_nn("mm_in_qk", h1, wg_in, b_in, F32, tile=(2048, 512), window=(0, 2 * d), rows_outer=True)
    proj_rest = _mm_nn("mm_in_rest", h1, wg_in, b_in, BF16, tile=(2048, 512), window=(2 * d, n_proj - 2 * d), rows_outer=True)
    w_all = {}
    for n, zone in zip(others, _gather_wait("gather_weights_wait", gather_handle, proj_rest)):
        own = lax.dynamic_slice_in_dim(w_bf[n], my_c * zone.shape[1], zone.shape[1], 0)
        w_all[n] = lax.dynamic_update_slice_in_dim(zone, own[None], dev, 0)
    wg_up = w_all["w_up"].reshape(N_CHIPS, d, w_up.shape[2])
    wg_conv_out = w_all["w_conv_out"].reshape(1, d, d)
    wg_ret_out = w_all["w_ret_out"].reshape(1, vw, d)
    wg_out = w_all["w_out"].reshape(1, d, d)
    wg_down = w_all["w_down"].reshape(1, D_FF, d)
    a1 = _glu_conv_fwd(proj_rest, 4, 5, conv_w, conv_dw_b, n_rows)
    a2 = _rowwise("conv_norm", lambda a, g, b: (_f_conv_norm(a, g, b),), [(a1, d, 0)], [(conv_ln_g, d, 0), (conv_ln_b, d, 0)],
                  [(d, BF16)], [], n_rows, tile=FWD_ROW_TILE)[0]
    y_a = _mm_nn("mm_conv_out", a2, wg_conv_out, None, F32)
    cos_t, sin_t = _rope_tables(positions[0])
    consts = _retention_consts()
    r, states = _retention_fwd(proj_qk, proj_rest, cos_t, sin_t, consts, n_rows)
    ret_rows = [(r, vw, 0), (proj_rest, vw, 1)]
    ret_vecs = [(ret_gn_g, vw, 0), (ret_gn_b, vw, 0)]
    r2 = _rowwise("ret_norm", _per_head(_f_group_norm_gate), ret_rows, ret_vecs, [(vw, BF16)], [], n_rows,
                  tile=FWD_ROW_TILE)[0]
    y_b = _mm_nn("mm_ret_out", r2, wg_ret_out, None, F32)
    merge_rows = [(proj_rest, d, 6), (proj_rest, d, 7), (y_a, d, 0), (y_b, d, 0)]
    m = _rowwise("merge", lambda *a: (_f_merge(*a),), merge_rows, [], [(d, BF16)], [], n_rows, tile=FWD_ROW_TILE)[0]
    t = _mm_nn("mm_out", m, wg_out, None, F32)
    post1_vecs = [(gate1, d, 0), (ln1_g, d, 0), (ln1_b, d, 0), (scale2, d, 0), (shift2, d, 0)]
    x1, h2 = _rowwise("post1", _f_post1, [(xr, d, 0), (t, d, 0)], post1_vecs, [(d, F32), (d, BF16)], [], n_rows,
                      tile=FWD_ROW_TILE)
    pair_up = lambda c: (c % 2) * 2 + c // 2
    paired = lambda a: a.reshape(a.shape[0], N_CHIPS, kf)[:, jnp.array([0, 2, 1, 3])].reshape(a.shape[0], ffw)
    ffn_w_p, ffn_b_p = paired(ffn_w), paired(ffn_dw_b)
    u = _mm_nn("mm_up", h2, wg_up, None, BF16, gmap=pair_up)
    p, u_filtered = _ffn_fwd(u, ffn_w_p, ffn_b_p, n_rows)
    f = _mm_nn("mm_down", p, wg_down, None, F32)

    def loss_rows(x1v, fv, tv, g2v, lg, lb):
        loss, pull = jax.vjp(lambda a, b, c_, e, h: _f_loss(a, b, c_, e, h, tv), x1v, fv, g2v, lg, lb)
        return (*pull(jnp.ones((), F32)), jnp.full((1, 128), loss, F32))

    dx1_a, df, dgate2, dln2_g, dln2_b, loss_v = _rowwise(
        "loss", loss_rows, [(x1, d, 0), (f, d, 0), (target, d, 0)], [(gate2, d, 0), (ln2_g, d, 0), (ln2_b, d, 0)],
        [(d, F32), (d, BF16)], [d, d, d, 128], n_rows, tile=FWD_ROW_TILE)
    loss = lax.psum(loss_v[0, 0], ("x", "y", "c"))

    dp = _mm_nt("mm_down_dx", df, wg_down, BF16)
    gw_down = _mm_tn("mm_down_dw", p, df, 1)
    du, g_ffn_w, g_ffn_b = _ffn_bwd(u, u_filtered, dp, ffn_w_p, n_rows)
    g_ffn_w, g_ffn_b = paired(g_ffn_w), paired(g_ffn_b)
    dh2 = _mm_nt("mm_up_dx", du, wg_up, BF16, gmap=pair_up)
    gw_up = _mm_tn("mm_up_dw", h2, du, N_CHIPS, gmap=pair_up)

    dx_a, dt, dgate1, dln1_g, dln1_b, dscale2, dshift2 = _rowwise(
        "post1_bwd", _vjp_rows(_f_post1, 2, 2), [(xr, d, 0), (t, d, 0), (dx1_a, d, 0), (dh2, d, 0)], post1_vecs,
        [(d, F32), (d, BF16)], [d] * 5, n_rows)
    dm = _mm_nt("mm_out_dx", dt, wg_out, BF16)
    gw_out = _mm_tn("mm_out_dw", m, dt, 1)
    merge_vjp = _vjp_rows(_f_merge, 4, 1)

    def merge_bwd(*a):
        dga, dgb, dya, dyb = merge_vjp(*a)
        return jnp.concatenate([dga, dgb], axis=1), dya, dyb

    dproj, dya, dyb = _rowwise("merge_bwd", merge_bwd, merge_rows + [(dm, d, 0)], [],
                               [(2 * d, BF16, n_proj, 8192 // (2 * d)), (d, BF16), (d, BF16)], [], n_rows, tile=FWD_ROW_TILE)

    da2 = _mm_nt("mm_conv_out_dx", dya, wg_conv_out, BF16)
    gw_conv_out = _mm_tn("mm_conv_out_dw", a2, dya, 1)
    da1, dcl_g, dcl_b = _rowwise("conv_norm_bwd", _vjp_rows(_f_conv_norm, 1, 1), [(a1, d, 0), (da2, d, 0)],
                                 [(conv_ln_g, d, 0), (conv_ln_b, d, 0)], [(d, F32)], [d, d], n_rows, tile=FWD_ROW_TILE)
    dproj, g_conv_w, g_conv_b = _glu_conv_bwd(proj_rest, 4, 5, da1, conv_w, dproj, 6144 // (2 * d), n_rows)

    dr2 = _mm_nt("mm_ret_out_dx", dyb, wg_ret_out, BF16)
    gw_ret_out = _mm_tn("mm_ret_out_dw", r2, dyb, 1)

    def reduce_start(tag, names, fulls):
        shapes = [given[n].shape[1:] for n in names]
        fulls = [g_.reshape(N_CHIPS, r, c_) for g_, (r, c_) in zip(fulls, shapes)]
        theirs = _sibling_swap("grad_swap_" + tag, fulls)
        pair = []
        for n, (r, c_), g_, t_ in zip(names, shapes, fulls, theirs):
            mine = lax.dynamic_slice_in_dim(g_, my_c * (r // 2), r // 2, 1)
            pair.append(_sum_arrays("grad_pair_sum_" + n, [(mine.reshape(-1, c_), 0), (t_.reshape(-1, c_), 0)],
                                    N_CHIPS * (r // 2)).reshape(N_CHIPS, r // 2, c_))
        handle, token = _alltoall_start("grad_alltoall_start_" + tag, pair)
        own = [lax.dynamic_index_in_dim(p_, chip, 0, keepdims=False) for p_ in pair]
        return (names, shapes, handle, own), token[0:1, 0:1]

    def reduce_finish(tag, started, after):
        names, shapes, handle, own = started
        lands = _alltoall_wait("grad_alltoall_wait_" + tag, handle, after)
        return [_sum_arrays("grad_chip_sum_" + n, [(o_, 0)] + [(l_.reshape(-1, c_), k * (r // 2)) for k in range(3)], r // 2)
                for n, (r, c_), o_, l_ in zip(names, shapes, own, lands)]

    late = ("w_up", "w_down", "w_out", "w_conv_out", "w_ret_out")
    started_late, token_late = reduce_start("late", late, [gw_up, gw_down, gw_out, gw_conv_out, gw_ret_out])
    ret_vecs_bwd = [(ret_gn_g + token_late, vw, 0), (ret_gn_b, vw, 0)]
    dr, dproj, dgn_g, dgn_b = _rowwise(
        "ret_norm_bwd", _per_head(_vjp_rows(_f_group_norm_gate, 2, 1)), ret_rows + [(dr2, vw, 0)], ret_vecs_bwd,
        [(vw, BF16), (vw, BF16, n_proj, 4096 // vw)], [vw, vw], n_rows, into=(dproj, 1))
    dproj = _retention_bwd(proj_qk, proj_rest, cos_t, sin_t, states, dr, consts, dproj, n_rows)

    gw_in, gb_in = _mm_tn("mm_in_dw", h1, dproj, N_CHIPS, colsum=True)
    started_in, token_in = reduce_start("in", ("w_in",), [gw_in])
    dh1 = _mm_nt("mm_in_dx", dproj, wg_in + token_in.astype(BF16), BF16)
    mod_bwd = _vjp_rows(_f_modulate, 1, 1)

    def mod1_bwd(xv, dhv, dxav, sv, tv):
        dx, ds, dsh = mod_bwd(xv, dhv, sv, tv)
        return dx + dxav, ds, dsh

    grad_x, dscale1, dshift1 = _rowwise("ln_mod1_bwd", mod1_bwd, [(xr, d, 0), (dh1, d, 0), (dx_a, d, 0)],
                                        [(scale1 + token_in, d, 0), (shift1, d, 0)], [(d, F32)], [d, d], n_rows,
                                        tile=FWD_ROW_TILE)

    dmod = jnp.concatenate([dshift1, dscale1, dgate1, dshift2, dscale2, dgate2], axis=1)
    small_names = ["b_in", "conv_dw_w", "conv_dw_b", "conv_ln_g", "conv_ln_b", "ret_gn_g", "ret_gn_b", "ln1_g", "ln1_b",
                   "ffn_dw_w", "ffn_dw_b", "ln2_g", "ln2_b"]
    small_parts = [gb_in, g_conv_w, g_conv_b, dcl_g, dcl_b, dgn_g, dgn_b, dln1_g, dln1_b, g_ffn_w, g_ffn_b, dln2_g, dln2_b, dmod]
    small_shapes = [a.shape for a in small_parts]
    parts_all = _allgather8("gather_small_grads", [flat_rows(small_parts)])[0]
    part_rows = parts_all.shape[1]
    summed = _sum_arrays("sum_small_grads", [(parts_all.reshape(N_DEV * part_rows, d), k * part_rows) for k in range(N_DEV)],
                         part_rows)
    small_sum = unflatten(summed, small_shapes)
    grads = dict(zip(small_names, small_sum[:-1]))
    grads["b_ada"] = small_sum[-1]
    grads["conv_dw_w"] = lax.dynamic_slice_in_dim(grads["conv_dw_w"], chip * kc, kc, 1)
    grads["ffn_dw_w"] = lax.dynamic_slice_in_dim(grads["ffn_dw_w"], chip * kf, kf, 1)
    o_mod = sum(a.size for a in small_parts[:-1])
    dmod_all = parts_all.reshape(N_DEV, -1)[:, o_mod:o_mod + dmod.shape[1]]
    grads["w_ada"] = _ada_bwd(c_all, lax.dynamic_slice_in_dim(dmod_all, chip * n_ada, n_ada, 1))

    reduced = reduce_finish("late", started_late, grad_x) + reduce_finish("in", started_in, grad_x)
    big_order = late + ("w_in",)
    g_shards = []
    for n, mine_, theirs_ in zip(big_order, reduced, _sibling_exchange("grad_exchange", reduced)):
        both = lax.dynamic_update_slice_in_dim(jnp.stack([theirs_, theirs_]), mine_[None], my_c, 0)
        g_shards.append(both.reshape(given[n].shape[1:]))

    outs = {}
    for n, g_ in zip(big_order, g_shards):
        upd = _adamw("adamw_" + n, given[n][0], g_, given["m_" + n][0], given["v_" + n][0])
        for prefix, val in zip(("grad_", "delta_", "new_m_", "new_v_"), (g_, *upd)):
            outs[prefix + n] = val.reshape(given[n].shape)
    ada = _adamw("adamw_ada", w_ada[0], grads["w_ada"], m_w_ada[0], v_w_ada[0])
    for prefix, val in zip(("grad_", "delta_", "new_m_", "new_v_"), (grads["w_ada"], *ada)):
        outs[prefix + "w_ada"] = val.reshape(w_ada.shape)
    small_all_names = ["b_ada"] + small_names
    small_w_shapes = [given[n].shape for n in small_all_names]
    g_small = flat_rows([grads[n] for n in small_all_names])
    small_upd = _adamw("adamw_small", flat_rows([given[n] for n in small_all_names]), g_small,
                       flat_rows([given["m_" + n] for n in small_all_names]), flat_rows([given["v_" + n] for n in small_all_names]))
    for prefix, packed in zip(("grad_", "delta_", "new_m_", "new_v_"), (g_small, *small_upd)):
        for n, val in zip(small_all_names, unflatten(packed, small_w_shapes)):
            outs[prefix + n] = val

    weights = ["w_ada", "b_ada", "w_in", "b_in", "conv_dw_w", "conv_dw_b", "conv_ln_g", "conv_ln_b", "w_conv_out", "ret_gn_g",
               "ret_gn_b", "w_ret_out", "w_out", "ln1_g", "ln1_b", "w_up", "ffn_dw_w", "ffn_dw_b", "w_down", "ln2_g", "ln2_b"]
    result = [loss, grad_x.reshape(x.shape)]
    for prefix in ("grad_", "delta_", "new_m_", "new_v_"):
        result += [outs[prefix + n] for n in weights]
    return tuple(result)
```
